```python
import jax, jax.numpy as jnp
from jax import lax
import numpy as np

D_MODEL = 1024
BATCH = 32
SEQ = 256
DEPTH = 1
DEC_BATCH = 8
DEC_SEQ = 4096
PAST_LEN = 512

GRID_W = 64
H_MLA = 8
D_NOPE = 64
D_ROPE = 32
D_QK = D_NOPE + D_ROPE
D_V = 64
Q_RANK = 256
KV_RANK = 128
H_RET = 8
D_RET = 64
RET_CHUNK = 128
D_MIX = H_MLA * D_V + H_RET * D_RET
D_RET_ALL = H_RET * D_RET
D_IN = Q_RANK + KV_RANK + D_ROPE + 4 * D_RET_ALL
SPLITS = (Q_RANK, Q_RANK + KV_RANK, Q_RANK + KV_RANK + D_ROPE,
          Q_RANK + KV_RANK + D_ROPE + D_RET_ALL,
          Q_RANK + KV_RANK + D_ROPE + 2 * D_RET_ALL,
          Q_RANK + KV_RANK + D_ROPE + 3 * D_RET_ALL)
N_EXPERTS = 64
TOP_K = 6
D_EXPERT = 256
D_SHARED = 256
ROUTED_SCALE = 2.5
ROPE_BASE = 10000.0
Q_BLOCK = 128
EPS = 1e-6

kernel_name = 'hybrid_mla_retention_moe_diffusion_step'


def rms_norm(x, g):
    xf = x.astype(jnp.float32)
    y = xf * lax.rsqrt(jnp.mean(xf * xf, axis=-1, keepdims=True) + EPS)
    return (y * g.astype(jnp.float32)).astype(x.dtype)


def rotate_axis(x, pos):
    p = x.shape[-1] // 2
    inv = 1.0 / (ROPE_BASE ** (jnp.arange(p, dtype=jnp.float32) / p))
    ang = pos.astype(jnp.float32)[:, None] * inv[None, :]
    cos, sin = jnp.cos(ang), jnp.sin(ang)
    xf = x.astype(jnp.float32)
    x1, x2 = xf[..., :p], xf[..., p:]
    return jnp.concatenate([x1 * cos - x2 * sin, x1 * sin + x2 * cos], axis=-1).astype(x.dtype)


def rope_2d(x, row, col):
    half = x.shape[-1] // 2
    return jnp.concatenate([rotate_axis(x[..., :half], row), rotate_axis(x[..., half:], col)], axis=-1)


def rope_on_rope_dims(t, row, col):
    return jnp.concatenate([t[..., :D_NOPE], rope_2d(t[..., D_NOPE:], row, col)], axis=-1)


def grid_positions(n_tokens):
    rows = n_tokens // GRID_W
    row = jnp.repeat(jnp.arange(rows, dtype=jnp.int32), GRID_W)
    col = jnp.tile(jnp.arange(GRID_W, dtype=jnp.int32), rows)
    return row, col


def adaln(cond, w_ada, b_ada):
    mod = jax.nn.silu(cond) @ w_ada + b_ada
    return [jnp.expand_dims(m, -2) for m in jnp.split(mod, 6, axis=-1)]


def split_heads(t, n_heads):
    B, L, _ = t.shape
    return t.reshape(B, L, n_heads, -1).transpose(0, 2, 1, 3)


def merge_heads(t):
    B, H, L, d = t.shape
    return t.transpose(0, 2, 1, 3).reshape(B, L, H * d)


def mla_queries(cq, q_norm_g, w_uq, q_gain):
    q = split_heads(rms_norm(cq, q_norm_g) @ w_uq, H_MLA)
    return rms_norm(q, q_gain)


def mla_keys_values(ckv_n, kr, w_ukv, k_gain):
    B, L, _ = ckv_n.shape
    kv = split_heads(ckv_n @ w_ukv, H_MLA)
    k_nope, v = kv[..., :D_NOPE], kv[..., D_NOPE:]
    k_rope = jnp.broadcast_to(kr[:, None], (B, H_MLA, L, D_ROPE))
    k = rms_norm(jnp.concatenate([k_nope, k_rope], axis=-1), k_gain)
    return k, v


def softmax_attention(q, k, v):
    B, H, Lq, dk = q.shape
    dv = v.shape[-1]
    nb = Lq // Q_BLOCK
    scale = dk ** -0.5
    kf, vf = k.astype(jnp.float32), v.astype(jnp.float32)
    qb = q.astype(jnp.float32).reshape(B, H, nb, Q_BLOCK, dk).transpose(2, 0, 1, 3, 4)

    def block(qblk):
        p = jax.nn.softmax(jnp.einsum('bhqd,bhkd->bhqk', qblk, kf) * scale, axis=-1)
        return jnp.einsum('bhqk,bhkd->bhqd', p, vf)

    o = lax.map(block, qb)
    return o.transpose(1, 2, 0, 3, 4).reshape(B, H, Lq, dv).astype(v.dtype)


def retention_chunked(q, k, v, log_gamma, s0, include_diag):
    B, H, L, dk = q.shape
    dv = v.shape[-1]
    C = RET_CHUNK
    n = L // C
    qc = q.astype(jnp.float32).reshape(B, H, n, C, dk)
    kc = k.astype(jnp.float32).reshape(B, H, n, C, dk)
    vc = v.astype(jnp.float32).reshape(B, H, n, C, dv)
    lg = log_gamma.astype(jnp.float32)
    idx = jnp.arange(C, dtype=jnp.float32)
    diff = idx[:, None] - idx[None, :]
    mask = (diff >= 0) if include_diag else (diff > 0)
    decay = jnp.where(mask[None], jnp.exp(lg[:, None, None] * jnp.where(mask, diff, 0.0)[None]), 0.0)
    scores = jnp.einsum('bhncd,bhnkd->bhnck', qc, kc) * decay[None, :, None]
    intra = jnp.einsum('bhnck,bhnke->bhnce', scores, vc)
    k_decay = jnp.exp(lg[:, None] * (C - 1 - idx)[None])
    kv_chunk = jnp.einsum('bhnkd,hk,bhnke->nbhde', kc, k_decay, vc)
    chunk_decay = jnp.exp(lg * C)[None, :, None, None]

    def step(S, kv):
        return chunk_decay * S + kv, S

    s_final, s_prev = lax.scan(step, s0.astype(jnp.float32), kv_chunk)
    q_decay = jnp.exp(lg[:, None] * (idx + 1)[None])
    cross = jnp.einsum('bhncd,hc,nbhde->bhnce', qc, q_decay, s_prev)
    return (intra + cross).reshape(B, H, L, dv), s_final


def bidir_retention(q, k, v, lg_f, lg_b, s0_f, s0_b):
    o_f, s_f = retention_chunked(q, k, v, lg_f, s0_f, True)
    o_b, s_b = retention_chunked(jnp.flip(q, 2), jnp.flip(k, 2), jnp.flip(v, 2), lg_b, s0_b, False)
    return (o_f + jnp.flip(o_b, 2)).astype(v.dtype), s_f, s_b


def moe_ffn(h, router_w, router_bias, w_gate, w_up, w_down, sh_gate, sh_up, sh_down):
    B, L, D = h.shape
    t = h.reshape(B * L, D)
    scores = jax.nn.sigmoid((t @ router_w).astype(jnp.float32))
    _, idx = lax.top_k(scores + router_bias.astype(jnp.float32), TOP_K)
    sel = jnp.take_along_axis(scores, idx, axis=-1)
    wts = ROUTED_SCALE * sel / jnp.sum(sel, axis=-1, keepdims=True)
    gates = jnp.einsum('tk,tke->et', wts, jax.nn.one_hot(idx, N_EXPERTS, dtype=jnp.float32)).astype(t.dtype)
    shared = (jax.nn.silu(t @ sh_gate) * (t @ sh_up)) @ sh_down

    def expert(acc, xs):
        wg, wu, wd, g = xs
        return acc + g[:, None] * ((jax.nn.silu(t @ wg) * (t @ wu)) @ wd), None

    y, _ = lax.scan(expert, shared, (w_gate, w_up, w_down, gates))
    return y.reshape(B, L, D)


def trunk_layer(x, cond, lp, ctx):
    B, L, _ = x.shape
    sh1, sc1, g1, sh2, sc2, g2 = adaln(cond, lp['w_ada'], lp['b_ada'])
    h = rms_norm(x, lp['norm1_g']) * (1.0 + sc1) + sh1
    z = h @ lp['w_in']
    cq, ckv, kr, rq, rk, rv, rg = jnp.split(z, SPLITS, axis=-1)
    ckv_n = rms_norm(ckv, lp['mla_kv_norm_g'])
    q = mla_queries(cq, lp['mla_q_norm_g'], lp['w_uq'], lp['mla_q_gain'])
    k, v = mla_keys_values(ckv_n, kr, lp['w_ukv'], lp['mla_k_gain'])
    rq = split_heads(rq, H_RET)
    rk = split_heads(rk, H_RET) * D_RET ** -0.5
    rv = split_heads(rv, H_RET)
    if ctx is None:
        s0_f = jnp.zeros((B, H_RET, D_RET, D_RET), jnp.float32)
        s0_b = jnp.zeros((B, H_RET, D_RET, D_RET), jnp.float32)
    else:
        ckv_c, kr_c, s0_f, s0_b = ctx
        row, col = grid_positions(L)
        q = rope_on_rope_dims(q, row, col)
        k = rope_on_rope_dims(k, row, col)
        rq = rope_2d(rq, row, col)
        rk = rope_2d(rk, row, col)
        k_c, v_c = mla_keys_values(ckv_c, kr_c, lp['w_ukv'], lp['mla_k_gain'])
        k = jnp.concatenate([k, k_c.astype(k.dtype)], axis=2)
        v = jnp.concatenate([v, v_c.astype(v.dtype)], axis=2)
    o_mla = merge_heads(softmax_attention(q, k, v))
    lg_f = -jnp.exp(lp['ret_decay_fwd'].astype(jnp.float32))
    lg_b = -jnp.exp(lp['ret_decay_bwd'].astype(jnp.float32))
    o_ret, s_f, s_b = bidir_retention(rq, rk, rv, lg_f, lg_b, s0_f, s0_b)
    o_ret = merge_heads(rms_norm(o_ret, lp['ret_norm_g'])) * jax.nn.silu(rg)
    x = x + g1 * (jnp.concatenate([o_mla, o_ret], axis=-1) @ lp['w_o'])
    h2 = rms_norm(x, lp['norm2_g']) * (1.0 + sc2) + sh2
    x = x + g2 * moe_ffn(h2, lp['router_w'], lp['router_bias'], lp['exp_w_gate'], lp['exp_w_up'],
                         lp['exp_w_down'], lp['sh_w_gate'], lp['sh_w_up'], lp['sh_w_down'])
    return x, (ckv_n, kr, s_f, s_b)


def setup_inputs(seed: int = 0) -> dict:
    key = jax.random.key(seed)
    ks = jax.random.split(key, 31)

    def nrm(k, shape, s=1.0):
        return s * jax.random.normal(k, shape, jnp.float32)

    decay_base = jnp.asarray(np.log(-np.log1p(-(2.0 ** (-5.0 - np.arange(H_RET))))), jnp.float32)
    d_in_s = D_MODEL ** -0.5
    return {
        'x_prompt': nrm(ks[0], (BATCH, SEQ, D_MODEL)),
        'x_sample': nrm(ks[1], (DEC_BATCH, DEC_SEQ, D_MODEL)),
        'cache_mla_ckv': nrm(ks[2], (DEC_BATCH, DEPTH, PAST_LEN, KV_RANK)),
        'cache_mla_krope': nrm(ks[3], (DEC_BATCH, DEPTH, PAST_LEN, D_ROPE)),
        'state_ret_fwd': nrm(ks[4], (DEC_BATCH, DEPTH, H_RET, D_RET, D_RET), 0.5),
        'state_ret_bwd': nrm(ks[5], (DEC_BATCH, DEPTH, H_RET, D_RET, D_RET), 0.5),
        'c': nrm(ks[6], (DEC_BATCH, D_MODEL)),
        'c_ctx': nrm(ks[7], (D_MODEL,)),
        'w_ada': nrm(ks[8], (DEPTH, D_MODEL, 6 * D_MODEL), 0.5 * d_in_s),
        'b_ada': nrm(ks[9], (DEPTH, 6 * D_MODEL), 0.02),
        'norm1_g': 1.0 + nrm(ks[10], (DEPTH, D_MODEL), 0.1),
        'w_in': nrm(ks[11], (DEPTH, D_MODEL, D_IN), d_in_s),
        'mla_q_norm_g': 1.0 + nrm(ks[12], (DEPTH, Q_RANK), 0.1),
        'w_uq': nrm(ks[13], (DEPTH, Q_RANK, H_MLA * D_QK), Q_RANK ** -0.5),
        'mla_kv_norm_g': 1.0 + nrm(ks[14], (DEPTH, KV_RANK), 0.1),
        'w_ukv': nrm(ks[15], (DEPTH, KV_RANK, H_MLA * (D_NOPE + D_V)), KV_RANK ** -0.5),
        'mla_q_gain': 1.0 + nrm(ks[16], (DEPTH, D_QK), 0.1),
        'mla_k_gain': 1.0 + nrm(ks[17], (DEPTH, D_QK), 0.1),
        'ret_decay_fwd': decay_base[None] + nrm(ks[18], (DEPTH, H_RET), 0.01),
        'ret_decay_bwd': decay_base[None] + nrm(ks[19], (DEPTH, H_RET), 0.01),
        'ret_norm_g': 1.0 + nrm(ks[20], (DEPTH, D_RET), 0.1),
        'w_o': nrm(ks[21], (DEPTH, D_MIX, D_MODEL), D_MIX ** -0.5),
        'norm2_g': 1.0 + nrm(ks[22], (DEPTH, D_MODEL), 0.1),
        'router_w': nrm(ks[23], (DEPTH, D_MODEL, N_EXPERTS), d_in_s),
        'router_bias': nrm(ks[24], (DEPTH, N_EXPERTS), 0.01),
        'exp_w_gate': nrm(ks[25], (DEPTH, N_EXPERTS, D_MODEL, D_EXPERT), d_in_s),
        'exp_w_up': nrm(ks[26], (DEPTH, N_EXPERTS, D_MODEL, D_EXPERT), d_in_s),
        'exp_w_down': nrm(ks[27], (DEPTH, N_EXPERTS, D_EXPERT, D_MODEL), D_EXPERT ** -0.5),
        'sh_w_gate': nrm(ks[28], (DEPTH, D_MODEL, D_SHARED), d_in_s),
        'sh_w_up': nrm(ks[29], (DEPTH, D_MODEL, D_SHARED), d_in_s),
        'sh_w_down': nrm(ks[30], (DEPTH, D_SHARED, D_MODEL), D_SHARED ** -0.5),
    }


def reference(x_prompt, x_sample, cache_mla_ckv, cache_mla_krope, state_ret_fwd, state_ret_bwd,
              c, c_ctx, w_ada, b_ada, norm1_g, w_in, mla_q_norm_g, w_uq, mla_kv_norm_g, w_ukv,
              mla_q_gain, mla_k_gain, ret_decay_fwd, ret_decay_bwd, ret_norm_g, w_o, norm2_g,
              router_w, router_bias, exp_w_gate, exp_w_up, exp_w_down, sh_w_gate, sh_w_up, sh_w_down):
    layers = [dict(w_ada=w_ada[l], b_ada=b_ada[l], norm1_g=norm1_g[l], w_in=w_in[l],
                   mla_q_norm_g=mla_q_norm_g[l], w_uq=w_uq[l], mla_kv_norm_g=mla_kv_norm_g[l],
                   w_ukv=w_ukv[l], mla_q_gain=mla_q_gain[l], mla_k_gain=mla_k_gain[l],
                   ret_decay_fwd=ret_decay_fwd[l], ret_decay_bwd=ret_decay_bwd[l],
                   ret_norm_g=ret_norm_g[l], w_o=w_o[l], norm2_g=norm2_g[l],
                   router_w=router_w[l], router_bias=router_bias[l], exp_w_gate=exp_w_gate[l],
                   exp_w_up=exp_w_up[l], exp_w_down=exp_w_down[l], sh_w_gate=sh_w_gate[l],
                   sh_w_up=sh_w_up[l], sh_w_down=sh_w_down[l])
              for l in range(DEPTH)]

    y_prompt = x_prompt
    ckv_list, kr_list, sf_list, sb_list = [], [], [], []
    for l in range(DEPTH):
        y_prompt, (ckv_l, kr_l, sf_l, sb_l) = trunk_layer(y_prompt, c_ctx, layers[l], None)
        ckv_list.append(ckv_l)
        kr_list.append(kr_l)
        sf_list.append(sf_l)
        sb_list.append(sb_l)

    y_sample = x_sample
    for l in range(DEPTH):
        ctx = (cache_mla_ckv[:, l], cache_mla_krope[:, l], state_ret_fwd[:, l], state_ret_bwd[:, l])
        y_sample, _ = trunk_layer(y_sample, c, layers[l], ctx)

    new_mla_ckv = jnp.stack(ckv_list, axis=1)
    new_mla_krope = jnp.stack(kr_list, axis=1)
    new_ret_fwd = jnp.stack(sf_list, axis=1)
    new_ret_bwd = jnp.stack(sb_list, axis=1)
    return (y_prompt, y_sample, new_mla_ckv, new_mla_krope, new_ret_fwd, new_ret_bwd)
```

```python
import functools
import math

import numpy as np
import jax
import jax.numpy as jnp
from jax import lax
from jax.experimental import pallas as pl
from jax.experimental.pallas import tpu as pltpu

F32 = jnp.float32
BF16 = jnp.bfloat16

D_MODEL = 1024
GRID_W = 64
H_MLA = 8
D_NOPE = 64
D_ROPE = 32
D_QK = D_NOPE + D_ROPE
D_V = 64
Q_RANK = 256
KV_RANK = 128
H_RET = 8
D_RET = 64
RET_CHUNK = 128
D_RET_ALL = H_RET * D_RET
N_EXPERTS = 64
TOP_K = 6
D_EXPERT = 256
D_SHARED = 256
ROUTED_SCALE = 2.5
ROPE_BASE = 10000.0
EPS = 1e-6

LANES = 128
HEAD_SLOT = LANES
D_IN_PAD = Q_RANK + KV_RANK + LANES + 4 * D_RET_ALL
MOD_ROWS = 16
VMEM_LIMIT = 56 * 1024 * 1024


def _cparams(*sem):
    return pltpu.CompilerParams(dimension_semantics=sem, vmem_limit_bytes=VMEM_LIMIT)


def _dot(a, b):
    return jnp.dot(a, b, preferred_element_type=F32)


def _dot_nt(a, b):
    return lax.dot_general(a, b, (((1,), (1,)), ((), ())), preferred_element_type=F32)


def _dot_tn(a, b):
    return lax.dot_general(a, b, (((0,), (0,)), ((), ())), preferred_element_type=F32)


def _rms(x, g):
    return x * lax.rsqrt(jnp.mean(x * x, axis=-1, keepdims=True) + EPS) * g


def _silu(x):
    return x * jax.nn.sigmoid(x)


def _adaln_kernel(c_ref, w_ref, b_ref, o_ref):
    s = _silu(c_ref[...])
    o_ref[...] = _dot(s.astype(BF16), w_ref[...].astype(BF16)) + b_ref[...]


def _adaln(cond, w_ada, b_ada):
    n_out = w_ada.shape[1]
    bn = 1536
    return pl.pallas_call(
        _adaln_kernel,
        grid=(n_out // bn,),
        in_specs=[pl.BlockSpec((MOD_ROWS, D_MODEL), lambda j: (0, 0)),
                  pl.BlockSpec((D_MODEL, bn), lambda j: (0, j)),
                  pl.BlockSpec((1, bn), lambda j: (0, j))],
        out_specs=pl.BlockSpec((MOD_ROWS, bn), lambda j: (0, j)),
        out_shape=jax.ShapeDtypeStruct((MOD_ROWS, n_out), F32),
        compiler_params=_cparams("arbitrary"),
        name="adaln",
    )(cond, w_ada, b_ada.reshape(1, n_out))


def _rope_tables(n_tokens):
    t = np.arange(n_tokens)
    row = (t // GRID_W).astype(np.float64)
    col = (t % GRID_W).astype(np.float64)

    def axis_tables(width, lane0, period):
        p = width // 2
        inv = 1.0 / (ROPE_BASE ** (np.arange(p, dtype=np.float64) / p))
        cos = np.ones((n_tokens, LANES))
        sa = np.zeros((n_tokens, LANES))
        sb = np.zeros((n_tokens, LANES))
        starts = range(lane0, LANES, period) if period else (lane0,)
        for s0 in starts:
            for base, pos in ((s0, row), (s0 + width, col)):
                ang = pos[:, None] * inv[None, :]
                c, s = np.cos(ang), np.sin(ang)
                cos[:, base:base + p] = c
                cos[:, base + p:base + 2 * p] = c
                sa[:, base:base + p] = -s
                sb[:, base + p:base + 2 * p] = s
        return tuple(jnp.asarray(a, F32) for a in (cos, sa, sb))

    qk = axis_tables(D_ROPE // 2, D_NOPE, 0)
    ret = axis_tables(D_RET // 2, 0, D_RET)
    return qk, D_ROPE // 4, ret, D_RET // 4


def _rope(x, tabs, dist):
    cos, sa, sb = tabs
    return x * cos + pltpu.roll(x, LANES - dist, 1) * sa + pltpu.roll(x, dist, 1) * sb


def _kv_heads(ckv_n, krp, wkv_ref, kgain, tabs, dist, k_ref, v_ref):
    kvp = _dot(ckv_n.astype(BF16), wkv_ref[...])
    ss_kr = jnp.sum(krp * krp, axis=-1, keepdims=True)
    for h in range(H_MLA):
        kh = kvp[:, h * HEAD_SLOT:(h + 1) * HEAD_SLOT]
        ms = (jnp.sum(kh * kh, axis=-1, keepdims=True) + ss_kr) * (1.0 / D_QK)
        kh = (kh + krp) * lax.rsqrt(ms + EPS) * kgain
        if tabs is not None:
            kh = _rope(kh, tabs, dist)
        k_ref[:, h * HEAD_SLOT:(h + 1) * HEAD_SLOT] = kh.astype(BF16)
    v_ref[...] = kvp[:, H_MLA * HEAD_SLOT:].astype(BF16)


def _inproj_kernel(rope, *refs):
    (x_ref, mod_ref, n1g_ref, w1_ref, qng_ref, wuq_ref, kvg_ref, wkv_ref, qgain_ref,
     kgain_ref) = refs[:10]
    refs = refs[10:]
    if rope:
        qk_tabs = tuple(r[...] for r in refs[:3])
        ret_tabs = tuple(r[...] for r in refs[3:6])
        refs = refs[6:]
    else:
        qk_tabs = ret_tabs = None
    ckvn_ref, krp_ref, q_ref, k_ref, v_ref, rq_ref, rk_ref, rv_ref, rg_ref = refs
    qk_dist, ret_dist = D_ROPE // 4, D_RET // 4

    mod = mod_ref[0]
    h = _rms(x_ref[...], n1g_ref[...]) * (1.0 + mod[1:2]) + mod[0:1]
    z = _dot(h.astype(BF16), w1_ref[...])
    o = 0
    cq = z[:, o:o + Q_RANK]; o += Q_RANK
    ckv = z[:, o:o + KV_RANK]; o += KV_RANK
    krp = z[:, o:o + LANES]; o += LANES
    rq = z[:, o:o + D_RET_ALL]; o += D_RET_ALL
    rk = z[:, o:o + D_RET_ALL]; o += D_RET_ALL
    rv = z[:, o:o + D_RET_ALL]; o += D_RET_ALL
    rg = z[:, o:o + D_RET_ALL]

    ckv_n = _rms(ckv, kvg_ref[...])
    ckvn_ref[...] = ckv_n
    krp_ref[...] = krp

    q = _dot(_rms(cq, qng_ref[...]).astype(BF16), wuq_ref[...])
    qgain = qgain_ref[...] * (D_QK ** -0.5)
    for hd in range(H_MLA):
        qh = q[:, hd * HEAD_SLOT:(hd + 1) * HEAD_SLOT]
        ms = jnp.sum(qh * qh, axis=-1, keepdims=True) * (1.0 / D_QK)
        qh = qh * lax.rsqrt(ms + EPS) * qgain
        if rope:
            qh = _rope(qh, qk_tabs, qk_dist)
        q_ref[:, hd * HEAD_SLOT:(hd + 1) * HEAD_SLOT] = qh.astype(BF16)

    _kv_heads(ckv_n, krp, wkv_ref, kgain_ref[...], qk_tabs, qk_dist, k_ref, v_ref)

    for t in range(D_RET_ALL // LANES):
        sl = slice(t * LANES, (t + 1) * LANES)
        rq_t, rk_t = rq[:, sl], rk[:, sl] * (D_RET ** -0.5)
        if rope:
            rq_t = _rope(rq_t, ret_tabs, ret_dist)
            rk_t = _rope(rk_t, ret_tabs, ret_dist)
        rq_ref[:, sl] = rq_t.astype(BF16)
        rk_ref[:, sl] = rk_t.astype(BF16)
    rv_ref[...] = rv.astype(BF16)
    rg_ref[...] = _silu(rg).astype(BF16)


def _const_spec(shape):
    return pl.BlockSpec(shape, lambda i: (0,) * len(shape))


def _inproj(x, mod, mod_row0, tokens_per_row, wts, rope_tabs, tm):
    T = x.shape[0]
    bpr = tokens_per_row // tm
    rope = rope_tabs is not None
    tok = lambda w: pl.BlockSpec((tm, w), lambda i: (i, 0))
    in_specs = [tok(D_MODEL),
                pl.BlockSpec((1, 6, D_MODEL), lambda i: (mod_row0 + i // bpr, 0, 0)),
                _const_spec((1, D_MODEL)),
                _const_spec((D_MODEL, D_IN_PAD)),
                _const_spec((1, Q_RANK)),
                _const_spec((Q_RANK, H_MLA * HEAD_SLOT)),
                _const_spec((1, KV_RANK)),
                _const_spec((KV_RANK, H_MLA * (HEAD_SLOT + D_V))),
                _const_spec((1, LANES)),
                _const_spec((1, LANES))]
    args = [x, mod, wts["norm1_g"], wts["w1"], wts["q_norm_g"], wts["w_uq"], wts["kv_norm_g"],
            wts["w_kv"], wts["q_gain"], wts["k_gain"]]
    if rope:
        qk_tabs, ret_tabs = rope_tabs
        nb = qk_tabs[0].shape[0] // tm
        in_specs += [pl.BlockSpec((tm, LANES), lambda i: (i % nb, 0))] * 6
        args += list(qk_tabs) + list(ret_tabs)
    widths = [(KV_RANK, F32), (LANES, F32), (H_MLA * HEAD_SLOT, BF16), (H_MLA * HEAD_SLOT, BF16),
              (H_MLA * D_V, BF16)] + [(D_RET_ALL, BF16)] * 4
    return pl.pallas_call(
        functools.partial(_inproj_kernel, rope),
        grid=(T // tm,),
        in_specs=in_specs,
        out_specs=[tok(w) for w, _ in widths],
        out_shape=[jax.ShapeDtypeStruct((T, w), dt) for w, dt in widths],
        compiler_params=_cparams("parallel"),
        name="inproj_rope" if rope else "inproj",
    )(*args)


def _ctx_kv_kernel(ckv_ref, krp_ref, wkv_ref, kgain_ref, k_ref, v_ref):
    _kv_heads(ckv_ref[...], krp_ref[...], wkv_ref, kgain_ref[...], None, 0, k_ref, v_ref)


def _ctx_kv(ckv, krp, wts, tm):
    T = ckv.shape[0]
    tok = lambda w: pl.BlockSpec((tm, w), lambda i: (i, 0))
    return pl.pallas_call(
        _ctx_kv_kernel,
        grid=(T // tm,),
        in_specs=[tok(KV_RANK), tok(LANES),
                  _const_spec((KV_RANK, H_MLA * (HEAD_SLOT + D_V))), _const_spec((1, LANES))],
        out_specs=[tok(H_MLA * HEAD_SLOT), tok(H_MLA * D_V)],
        out_shape=[jax.ShapeDtypeStruct((T, H_MLA * HEAD_SLOT), BF16),
                   jax.ShapeDtypeStruct((T, H_MLA * D_V), BF16)],
        compiler_params=_cparams("parallel"),
        name="ctx_kv",
    )(ckv, krp, wts["w_kv"], wts["k_gain"])


def _attn_kernel(has_ctx, *refs):
    if has_ctx:
        q_ref, k_ref, v_ref, kc_ref, vc_ref, o_ref = refs
    else:
        q_ref, k_ref, v_ref, o_ref = refs
    outs = []
    for j in range(2):
        hs = slice(j * HEAD_SLOT, (j + 1) * HEAD_SLOT)
        vs = slice(j * D_V, (j + 1) * D_V)
        q = q_ref[0, :, hs]
        s = _dot_nt(q, k_ref[0, :, hs])
        m = jnp.max(s, axis=-1, keepdims=True)
        if has_ctx:
            sc = _dot_nt(q, kc_ref[0, :, hs])
            m = jnp.maximum(m, jnp.max(sc, axis=-1, keepdims=True))
        p = jnp.exp(s - m)
        l = jnp.sum(p, axis=-1, keepdims=True)
        o = _dot(p.astype(BF16), v_ref[0, :, vs])
        if has_ctx:
            pc = jnp.exp(sc - m)
            l = l + jnp.sum(pc, axis=-1, keepdims=True)
            o = o + _dot(pc.astype(BF16), vc_ref[0, :, vs])
        outs.append(o / l)
    o_ref[0] = jnp.concatenate(outs, axis=-1).astype(BF16)


def _attention(q, k, v, kc, vc, tq):
    B, L, _ = q.shape
    has_ctx = kc is not None
    in_specs = [pl.BlockSpec((1, tq, 2 * HEAD_SLOT), lambda b, hp, i: (b, i, hp)),
                pl.BlockSpec((1, L, 2 * HEAD_SLOT), lambda b, hp, i: (b, 0, hp)),
                pl.BlockSpec((1, L, 2 * D_V), lambda b, hp, i: (b, 0, hp))]
    args = [q, k, v]
    if has_ctx:
        Lc = kc.shape[1]
        in_specs += [pl.BlockSpec((1, Lc, 2 * HEAD_SLOT), lambda b, hp, i: (b, 0, hp)),
                     pl.BlockSpec((1, Lc, 2 * D_V), lambda b, hp, i: (b, 0, hp))]
        args += [kc, vc]
    return pl.pallas_call(
        functools.partial(_attn_kernel, has_ctx),
        grid=(B, H_MLA // 2, L // tq),
        in_specs=in_specs,
        out_specs=pl.BlockSpec((1, tq, 2 * D_V), lambda b, hp, i: (b, i, hp)),
        out_shape=jax.ShapeDtypeStruct((B, L, H_MLA * D_V), BF16),
        compiler_params=_cparams("parallel", "parallel", "arbitrary"),
        name="attn_ctx" if has_ctx else "attn",
    )(*args)


def _ret_kernel(has_s0, want_state, n_chunks, *refs):
    rq_ref, rk_ref, rv_ref, rg_ref, decf_ref, decb_ref, g_ref = refs[:7]
    refs = refs[7:]
    if has_s0:
        s0f_ref, s0b_ref = refs[:2]
        refs = refs[2:]
    o_ref = refs[0]
    refs = refs[1:]
    if want_state:
        sf_ref, sb_ref = refs[:2]
        refs = refs[2:]
    of_ref, = refs

    C = RET_CHUNK
    hp = pl.program_id(1)
    r_i = lax.broadcasted_iota(jnp.int32, (C, C), 0)
    c_i = lax.broadcasted_iota(jnp.int32, (C, C), 1)
    diff = (r_i - c_i).astype(F32)
    pos = lax.broadcasted_iota(jnp.int32, (C, 1), 0).astype(F32)
    g = g_ref[...]

    for j in range(2):
        hsl = slice(j * D_RET, (j + 1) * D_RET)
        head = hp * 2 + j
        lgf = -jnp.exp(decf_ref[pl.ds(head, 1), :][:, :1])
        lgb = -jnp.exp(decb_ref[pl.ds(head, 1), :][:, :1])
        decay_f = jnp.where(diff >= 0, jnp.exp(lgf * jnp.maximum(diff, 0.0)), 0.0)
        decay_b = jnp.where(diff < 0, jnp.exp(lgb * jnp.maximum(-diff, 0.0)), 0.0)
        kdec_f, qdec_f, cdec_f = jnp.exp(lgf * (C - 1 - pos)), jnp.exp(lgf * (pos + 1)), jnp.exp(lgf * C)
        kdec_b, qdec_b, cdec_b = jnp.exp(lgb * pos), jnp.exp(lgb * (C - pos)), jnp.exp(lgb * C)

        def chunk_terms(n, S, decay, qdec, kdec):
            rows = pl.ds(pl.multiple_of(n * C, C), C)
            q = rq_ref[0, rows, hsl]
            k = rk_ref[0, rows, hsl]
            v = rv_ref[0, rows, hsl]
            sc = _dot_nt(q, k) * decay
            o = _dot(sc.astype(BF16), v) + _dot((q.astype(F32) * qdec).astype(BF16), S.astype(BF16))
            kv = _dot_tn((k.astype(F32) * kdec).astype(BF16), v)
            return rows, o, kv

        def fwd(n, S):
            rows, o, kv = chunk_terms(n, S, decay_f, qdec_f, kdec_f)
            of_ref[rows, hsl] = o
            return cdec_f * S + kv

        def bwd(i, S):
            n = n_chunks - 1 - i
            rows, o, kv = chunk_terms(n, S, decay_b, qdec_b, kdec_b)
            o = o + of_ref[rows, hsl]
            o = _rms(o, g) * rg_ref[0, rows, hsl].astype(F32)
            o_ref[0, rows, hsl] = o.astype(BF16)
            return cdec_b * S + kv

        S0f = s0f_ref[0, j] if has_s0 else jnp.zeros((D_RET, D_RET), F32)
        S0b = s0b_ref[0, j] if has_s0 else jnp.zeros((D_RET, D_RET), F32)
        Sf = lax.fori_loop(0, n_chunks, fwd, S0f)
        Sb = lax.fori_loop(0, n_chunks, bwd, S0b)
        if want_state:
            sf_ref[0, j] = Sf
            sb_ref[0, j] = Sb


def _retention(rq, rk, rv, rg, dec_f, dec_b, g, s0f, s0b, want_state):
    B, L, _ = rq.shape
    has_s0 = s0f is not None
    seq = pl.BlockSpec((1, L, LANES), lambda b, hp: (b, 0, hp))
    st = pl.BlockSpec((1, 2, D_RET, D_RET), lambda b, hp: (b, hp, 0, 0))
    cst = lambda shape: pl.BlockSpec(shape, lambda b, hp: (0,) * len(shape))
    in_specs = [seq] * 4 + [cst((H_RET, LANES)), cst((H_RET, LANES)), cst((1, D_RET))]
    args = [rq, rk, rv, rg, dec_f, dec_b, g]
    if has_s0:
        in_specs += [st, st]
        args += [s0f, s0b]
    out_specs = [seq]
    out_shape = [jax.ShapeDtypeStruct((B, L, D_RET_ALL), BF16)]
    if want_state:
        out_specs += [st, st]
        out_shape += [jax.ShapeDtypeStruct((B, H_RET, D_RET, D_RET), F32)] * 2
    return pl.pallas_call(
        functools.partial(_ret_kernel, has_s0, want_state, L // RET_CHUNK),
        grid=(B, H_RET // 2),
        in_specs=in_specs,
        out_specs=out_specs,
        out_shape=out_shape,
        scratch_shapes=[pltpu.VMEM((L, LANES), F32)],
        compiler_params=_cparams("parallel", "parallel"),
        name="retention_s0" if has_s0 else "retention",
    )(*args)


def _outproj_kernel(omla_ref, oret_ref, x_ref, mod_ref, wo_ref, n2g_ref, rwh_ref, rwl_ref, rb_ref,
                    x2_ref, h2_ref, gates_ref):
    mod = mod_ref[0]
    half = H_MLA * D_V
    a = _dot(omla_ref[...], wo_ref[:half, :]) + _dot(oret_ref[...], wo_ref[half:, :])
    x2 = x_ref[...] + mod[2:3] * a
    x2_ref[...] = x2
    h2 = _rms(x2, n2g_ref[...]) * (1.0 + mod[4:5]) + mod[3:4]
    h2_hi = h2.astype(BF16)
    h2_ref[...] = h2_hi
    h2_lo = (h2 - h2_hi.astype(F32)).astype(BF16)
    logits = _dot(h2_hi, rwh_ref[...]) + _dot(h2_lo, rwh_ref[...]) + _dot(h2_hi, rwl_ref[...])
    scores = jax.nn.sigmoid(logits)
    lane = lax.broadcasted_iota(jnp.int32, scores.shape, 1)
    cand = scores + rb_ref[...]
    picked = jnp.zeros(scores.shape, jnp.bool_)
    for _ in range(TOP_K):
        mx = jnp.max(cand, axis=-1, keepdims=True)
        first = jnp.min(jnp.where(cand == mx, lane, LANES), axis=-1, keepdims=True)
        hit = lane == first
        picked = jnp.logical_or(picked, hit)
        cand = jnp.where(hit, -jnp.inf, cand)
    sel = jnp.where(picked, scores, 0.0)
    gates_ref[...] = sel * (ROUTED_SCALE / jnp.sum(sel, axis=-1, keepdims=True))


def _outproj(o_mla, o_ret, x, mod, mod_row0, tokens_per_row, wts, tm):
    T = x.shape[0]
    bpr = tokens_per_row // tm
    tok = lambda w: pl.BlockSpec((tm, w), lambda i: (i, 0))
    return pl.pallas_call(
        _outproj_kernel,
        grid=(T // tm,),
        in_specs=[tok(H_MLA * D_V), tok(D_RET_ALL), tok(D_MODEL),
                  pl.BlockSpec((1, 6, D_MODEL), lambda i: (mod_row0 + i // bpr, 0, 0)),
                  _const_spec((D_MODEL, D_MODEL)), _const_spec((1, D_MODEL)),
                  _const_spec((D_MODEL, LANES)), _const_spec((D_MODEL, LANES)), _const_spec((1, LANES))],
        out_specs=[tok(D_MODEL), tok(D_MODEL), tok(LANES)],
        out_shape=[jax.ShapeDtypeStruct((T, D_MODEL), F32),
                   jax.ShapeDtypeStruct((T, D_MODEL), BF16),
                   jax.ShapeDtypeStruct((T, LANES), F32)],
        compiler_params=_cparams("parallel"),
        name="outproj_router",
    )(o_mla, o_ret, x, mod, wts["w_o"], wts["norm2_g"], wts["router_hi"], wts["router_lo"],
      wts["router_bias"])


EXPERTS_PER_STEP = 4


def _moe_kernel(h2_ref, gates_ref, x2_ref, mod_ref, shg_ref, shu_ref, shd_ref, wg_ref, wu_ref, wd_ref,
                y_ref, acc_ref):
    e = pl.program_id(1)
    t = h2_ref[...]

    @pl.when(e == 0)
    def _():
        act = _silu(_dot(t, shg_ref[...])) * _dot(t, shu_ref[...])
        acc_ref[...] = _dot(act.astype(BF16), shd_ref[...])

    gates = gates_ref[...]
    lane = lax.broadcasted_iota(jnp.int32, gates.shape, 1)
    acts = []
    for i in range(EXPERTS_PER_STEP):
        g = jnp.sum(jnp.where(lane == e * EXPERTS_PER_STEP + i, gates, 0.0), axis=-1, keepdims=True)
        act = _silu(_dot(t, wg_ref[i])) * _dot(t, wu_ref[i]) * g
        acts.append(act.astype(BF16))
    wd = wd_ref[...].reshape(EXPERTS_PER_STEP * D_EXPERT, D_MODEL)
    acc_ref[...] += _dot(jnp.concatenate(acts, axis=-1), wd)

    @pl.when(e == pl.num_programs(1) - 1)
    def _():
        y_ref[...] = x2_ref[...] + mod_ref[0][5:6] * acc_ref[...]


def _moe(h2, gates, x2, mod, mod_row0, tokens_per_row, wts, tm):
    T = h2.shape[0]
    bpr = tokens_per_row // tm
    tok = lambda w: pl.BlockSpec((tm, w), lambda i, e: (i, 0))
    cst = lambda shape: pl.BlockSpec(shape, lambda i, e: (0,) * len(shape))
    ex = lambda a, b: pl.BlockSpec((EXPERTS_PER_STEP, a, b), lambda i, e: (e, 0, 0))
    return pl.pallas_call(
        _moe_kernel,
        grid=(T // tm, N_EXPERTS // EXPERTS_PER_STEP),
        in_specs=[tok(D_MODEL), tok(LANES), tok(D_MODEL),
                  pl.BlockSpec((1, 6, D_MODEL), lambda i, e: (mod_row0 + i // bpr, 0, 0)),
                  cst((D_MODEL, D_SHARED)), cst((D_MODEL, D_SHARED)), cst((D_SHARED, D_MODEL)),
                  ex(D_MODEL, D_EXPERT), ex(D_MODEL, D_EXPERT), ex(D_EXPERT, D_MODEL)],
        out_specs=tok(D_MODEL),
        out_shape=jax.ShapeDtypeStruct((T, D_MODEL), F32),
        scratch_shapes=[pltpu.VMEM((tm, D_MODEL), F32)],
        compiler_params=_cparams("parallel", "arbitrary"),
        name="moe",
    )(h2, gates, x2, mod, wts["sh_gate"], wts["sh_up"], wts["sh_down"],
      wts["exp_gate"], wts["exp_up"], wts["exp_down"])


def _pad_heads(w, d_used):
    k = w.shape[0]
    w = w.reshape(k, H_MLA, d_used)
    return jnp.pad(w, ((0, 0), (0, 0), (0, HEAD_SLOT - d_used))).reshape(k, H_MLA * HEAD_SLOT)


def _lane_pad(v, lane0=0):
    return jnp.pad(v, (lane0, LANES - lane0 - v.shape[0])).reshape(1, LANES)


def _prep_weights(l, w_in, norm1_g, mla_q_norm_g, w_uq, mla_kv_norm_g, w_ukv, mla_q_gain, mla_k_gain,
                  w_o, norm2_g, router_w, router_bias, exp_w_gate, exp_w_up, exp_w_down,
                  sh_w_gate, sh_w_up, sh_w_down):
    w = w_in[l]
    a, b, c = Q_RANK, Q_RANK + KV_RANK, Q_RANK + KV_RANK + D_ROPE
    w_kr = jnp.pad(w[:, b:c], ((0, 0), (D_NOPE, LANES - D_NOPE - D_ROPE)))
    w1 = jnp.concatenate([w[:, :b], w_kr, w[:, c:]], axis=1).astype(BF16)
    ukv = w_ukv[l].reshape(KV_RANK, H_MLA, D_NOPE + D_V)
    w_k = _pad_heads(ukv[:, :, :D_NOPE].reshape(KV_RANK, H_MLA * D_NOPE), D_NOPE)
    w_v = ukv[:, :, D_NOPE:].reshape(KV_RANK, H_MLA * D_V)
    rw = jnp.pad(router_w[l], ((0, 0), (0, LANES - N_EXPERTS)))
    rw_hi = rw.astype(BF16)
    rw_lo = (rw - rw_hi.astype(F32)).astype(BF16)
    rb = jnp.concatenate([router_bias[l].astype(F32), jnp.full((LANES - N_EXPERTS,), -jnp.inf, F32)])
    return dict(
        norm1_g=norm1_g[l].reshape(1, D_MODEL), w1=w1,
        q_norm_g=mla_q_norm_g[l].reshape(1, Q_RANK), w_uq=_pad_heads(w_uq[l], D_QK).astype(BF16),
        kv_norm_g=mla_kv_norm_g[l].reshape(1, KV_RANK),
        w_kv=jnp.concatenate([w_k, w_v], axis=1).astype(BF16),
        q_gain=_lane_pad(mla_q_gain[l]), k_gain=_lane_pad(mla_k_gain[l]),
        w_o=w_o[l].astype(BF16), norm2_g=norm2_g[l].reshape(1, D_MODEL),
        router_hi=rw_hi, router_lo=rw_lo, router_bias=rb.reshape(1, LANES),
        exp_gate=exp_w_gate[l].astype(BF16), exp_up=exp_w_up[l].astype(BF16),
        exp_down=exp_w_down[l].astype(BF16),
        sh_gate=sh_w_gate[l].astype(BF16), sh_up=sh_w_up[l].astype(BF16),
        sh_down=sh_w_down[l].astype(BF16))


def _pick(n, prefs):
    for p in prefs:
        if n % p == 0:
            return p
    return n


def _trunk_layer(x, mod, mod_row0, wts, ret_wts, ctx):
    B, L, D = x.shape
    T = B * L
    tokens_per_row = L if ctx is not None else T
    xt = x.reshape(T, D)
    rope_tabs = None
    if ctx is not None:
        qk_tabs, _, ret_tabs, _ = _rope_tables(L)
        rope_tabs = (qk_tabs, ret_tabs)
    tm = _pick(L, (512, 256, 128))
    ckv_n, krp, q, k, v, rq, rk, rv, rg = _inproj(xt, mod, mod_row0, tokens_per_row, wts, rope_tabs, tm)

    seq = lambda a: a.reshape(B, L, a.shape[-1])
    kc = vc = s0f = s0b = None
    if ctx is not None:
        ckv_c, kr_c, s0f, s0b = ctx
        Lc = ckv_c.shape[1]
        krp_c = jnp.pad(kr_c, ((0, 0), (0, 0), (D_NOPE, LANES - D_NOPE - D_ROPE)))
        kc, vc = _ctx_kv(ckv_c.reshape(B * Lc, KV_RANK), krp_c.reshape(B * Lc, LANES), wts,
                         _pick(B * Lc, (512, 256, 128)))
        kc, vc = kc.reshape(B, Lc, -1), vc.reshape(B, Lc, -1)
    o_mla = _attention(seq(q), seq(k), seq(v), kc, vc, _pick(L, (256, 128)))

    dec_f, dec_b, ret_g = ret_wts
    want_state = ctx is None
    ret = _retention(seq(rq), seq(rk), seq(rv), seq(rg), dec_f, dec_b, ret_g, s0f, s0b, want_state)
    o_ret = ret[0]

    x2, h2, gates = _outproj(o_mla.reshape(T, -1), o_ret.reshape(T, -1), xt, mod, mod_row0,
                             tokens_per_row, wts, tm)
    tm_moe = _pick(tokens_per_row, (1024, 512, 256, 128))
    y = _moe(h2, gates, x2, mod, mod_row0, tokens_per_row, wts, tm_moe)
    new = None
    if want_state:
        new = (ckv_n.reshape(B, L, KV_RANK), krp[:, D_NOPE:D_NOPE + D_ROPE].reshape(B, L, D_ROPE),
               ret[1], ret[2])
    return y.reshape(B, L, D), new


def kernel(x_prompt, x_sample, cache_mla_ckv, cache_mla_krope, state_ret_fwd, state_ret_bwd, c, c_ctx,
           w_ada, b_ada, norm1_g, w_in, mla_q_norm_g, w_uq, mla_kv_norm_g, w_ukv, mla_q_gain, mla_k_gain,
           ret_decay_fwd, ret_decay_bwd, ret_norm_g, w_o, norm2_g, router_w, router_bias, exp_w_gate,
           exp_w_up, exp_w_down, sh_w_gate, sh_w_up, sh_w_down):
    depth = w_ada.shape[0]
    n_dec = c.shape[0]
    assert 1 + n_dec <= MOD_ROWS
    cond = jnp.concatenate([c_ctx[None], c, jnp.zeros((MOD_ROWS - 1 - n_dec, D_MODEL), F32)], axis=0)

    y_prompt, y_sample = x_prompt, x_sample
    ckv_l, kr_l, sf_l, sb_l = [], [], [], []
    for l in range(depth):
        wts = _prep_weights(l, w_in, norm1_g, mla_q_norm_g, w_uq, mla_kv_norm_g, w_ukv, mla_q_gain,
                            mla_k_gain, w_o, norm2_g, router_w, router_bias, exp_w_gate, exp_w_up,
                            exp_w_down, sh_w_gate, sh_w_up, sh_w_down)
        ret_wts = (jnp.broadcast_to(ret_decay_fwd[l].astype(F32)[:, None], (H_RET, LANES)),
                   jnp.broadcast_to(ret_decay_bwd[l].astype(F32)[:, None], (H_RET, LANES)),
                   ret_norm_g[l].reshape(1, D_RET))
        mod = _adaln(cond, w_ada[l], b_ada[l]).reshape(MOD_ROWS, 6, D_MODEL)
        y_prompt, new = _trunk_layer(y_prompt, mod, 0, wts, ret_wts, None)
        ckv_l.append(new[0]); kr_l.append(new[1]); sf_l.append(new[2]); sb_l.append(new[3])
        ctx = (cache_mla_ckv[:, l], cache_mla_krope[:, l], state_ret_fwd[:, l], state_ret_bwd[:, l])
        y_sample, _ = _trunk_layer(y_sample, mod, 1, wts, ret_wts, ctx)

    return (y_prompt, y_sample, jnp.stack(ckv_l, axis=1), jnp.stack(kr_l, axis=1),
            jnp.stack(sf_l, axis=1), jnp.stack(sb_l, axis=1))
```

```python
import functools
import math

import numpy as np
import jax
import jax.numpy as jnp
from jax import lax
from jax.experimental import pallas as pl
from jax.experimental.pallas import tpu as pltpu
from jax.experimental.pallas import tpu_sc as plsc

F32 = jnp.float32
BF16 = jnp.bfloat16

D_MODEL = 1024
GRID_W = 64
H_MLA = 8
D_NOPE = 64
D_ROPE = 32
D_QK = D_NOPE + D_ROPE
D_V = 64
Q_RANK = 256
KV_RANK = 128
H_RET = 8
D_RET = 64
RET_CHUNK = 128
D_RET_ALL = H_RET * D_RET
N_EXPERTS = 64
TOP_K = 6
D_EXPERT = 256
D_SHARED = 256
ROUTED_SCALE = 2.5
ROPE_BASE = 10000.0
EPS = 1e-6

LANES = 128
HEAD_SLOT = LANES
D_IN_PAD = Q_RANK + KV_RANK + LANES + 4 * D_RET_ALL
MOD_ROWS = 16
VMEM_LIMIT = 56 * 1024 * 1024


def _cparams(*sem):
    return pltpu.CompilerParams(dimension_semantics=sem, vmem_limit_bytes=VMEM_LIMIT)


def _dot(a, b):
    return jnp.dot(a, b, preferred_element_type=F32)


def _dot_nt(a, b):
    return lax.dot_general(a, b, (((1,), (1,)), ((), ())), preferred_element_type=F32)


def _dot_tn(a, b):
    return lax.dot_general(a, b, (((0,), (0,)), ((), ())), preferred_element_type=F32)


def _rms(x, g):
    return x * lax.rsqrt(jnp.mean(x * x, axis=-1, keepdims=True) + EPS) * g


def _silu(x):
    return x * jax.nn.sigmoid(x)


def _adaln_kernel(c_ref, w_ref, b_ref, o_ref):
    s = _silu(c_ref[...])
    o_ref[...] = _dot(s.astype(BF16), w_ref[...].astype(BF16)) + b_ref[...]


def _adaln(cond, w_ada, b_ada):
    n_out = w_ada.shape[1]
    bn = 1536
    return pl.pallas_call(
        _adaln_kernel,
        grid=(n_out // bn,),
        in_specs=[pl.BlockSpec((MOD_ROWS, D_MODEL), lambda j: (0, 0)),
                  pl.BlockSpec((D_MODEL, bn), lambda j: (0, j)),
                  pl.BlockSpec((1, bn), lambda j: (0, j))],
        out_specs=pl.BlockSpec((MOD_ROWS, bn), lambda j: (0, j)),
        out_shape=jax.ShapeDtypeStruct((MOD_ROWS, n_out), F32),
        compiler_params=_cparams("arbitrary"),
        name="adaln",
    )(cond, w_ada, b_ada.reshape(1, n_out))


def _rope_tables(n_tokens):
    t = np.arange(n_tokens)
    row = (t // GRID_W).astype(np.float64)
    col = (t % GRID_W).astype(np.float64)

    def axis_tables(width, lane0, period):
        p = width // 2
        inv = 1.0 / (ROPE_BASE ** (np.arange(p, dtype=np.float64) / p))
        cos = np.ones((n_tokens, LANES))
        sa = np.zeros((n_tokens, LANES))
        sb = np.zeros((n_tokens, LANES))
        starts = range(lane0, LANES, period) if period else (lane0,)
        for s0 in starts:
            for base, pos in ((s0, row), (s0 + width, col)):
                ang = pos[:, None] * inv[None, :]
                c, s = np.cos(ang), np.sin(ang)
                cos[:, base:base + p] = c
                cos[:, base + p:base + 2 * p] = c
                sa[:, base:base + p] = -s
                sb[:, base + p:base + 2 * p] = s
        return tuple(jnp.asarray(a, F32) for a in (cos, sa, sb))

    qk = axis_tables(D_ROPE // 2, D_NOPE, 0)
    ret = axis_tables(D_RET // 2, 0, D_RET)
    return qk, D_ROPE // 4, ret, D_RET // 4


def _rope(x, tabs, dist):
    cos, sa, sb = tabs
    return x * cos + pltpu.roll(x, LANES - dist, 1) * sa + pltpu.roll(x, dist, 1) * sb


def _kv_heads(ckv_n, krp, wkv_ref, kgain, tabs, dist, k_ref, v_ref):
    kvp = _dot(ckv_n.astype(BF16), wkv_ref[...])
    ss_kr = jnp.sum(krp * krp, axis=-1, keepdims=True)
    for h in range(H_MLA):
        kh = kvp[:, h * HEAD_SLOT:(h + 1) * HEAD_SLOT]
        ms = (jnp.sum(kh * kh, axis=-1, keepdims=True) + ss_kr) * (1.0 / D_QK)
        kh = (kh + krp) * lax.rsqrt(ms + EPS) * kgain
        if tabs is not None:
            kh = _rope(kh, tabs, dist)
        k_ref[:, h * HEAD_SLOT:(h + 1) * HEAD_SLOT] = kh.astype(BF16)
    v_ref[...] = kvp[:, H_MLA * HEAD_SLOT:].astype(BF16)


def _inproj_kernel(rope, *refs):
    (x_ref, mod_ref, n1g_ref, w1_ref, qng_ref, wuq_ref, kvg_ref, wkv_ref, qgain_ref,
     kgain_ref) = refs[:10]
    refs = refs[10:]
    if rope:
        qk_tabs = tuple(r[...] for r in refs[:3])
        ret_tabs = tuple(r[...] for r in refs[3:6])
        refs = refs[6:]
    else:
        qk_tabs = ret_tabs = None
    ckvn_ref, krp_ref, q_ref, k_ref, v_ref, rq_ref, rk_ref, rv_ref, rg_ref = refs
    qk_dist, ret_dist = D_ROPE // 4, D_RET // 4

    mod = mod_ref[0]
    h = _rms(x_ref[...], n1g_ref[...]) * (1.0 + mod[1:2]) + mod[0:1]
    z = _dot(h.astype(BF16), w1_ref[...])
    o = 0
    cq = z[:, o:o + Q_RANK]; o += Q_RANK
    ckv = z[:, o:o + KV_RANK]; o += KV_RANK
    krp = z[:, o:o + LANES]; o += LANES
    rq = z[:, o:o + D_RET_ALL]; o += D_RET_ALL
    rk = z[:, o:o + D_RET_ALL]; o += D_RET_ALL
    rv = z[:, o:o + D_RET_ALL]; o += D_RET_ALL
    rg = z[:, o:o + D_RET_ALL]

    ckv_n = _rms(ckv, kvg_ref[...])
    ckvn_ref[...] = ckv_n
    krp_ref[...] = krp

    q = _dot(_rms(cq, qng_ref[...]).astype(BF16), wuq_ref[...])
    qgain = qgain_ref[...] * (D_QK ** -0.5)
    for hd in range(H_MLA):
        qh = q[:, hd * HEAD_SLOT:(hd + 1) * HEAD_SLOT]
        ms = jnp.sum(qh * qh, axis=-1, keepdims=True) * (1.0 / D_QK)
        qh = qh * lax.rsqrt(ms + EPS) * qgain
        if rope:
            qh = _rope(qh, qk_tabs, qk_dist)
        q_ref[:, hd * HEAD_SLOT:(hd + 1) * HEAD_SLOT] = qh.astype(BF16)

    _kv_heads(ckv_n, krp, wkv_ref, kgain_ref[...], qk_tabs, qk_dist, k_ref, v_ref)

    for t in range(D_RET_ALL // LANES):
        sl = slice(t * LANES, (t + 1) * LANES)
        rq_t, rk_t = rq[:, sl], rk[:, sl] * (D_RET ** -0.5)
        if rope:
            rq_t = _rope(rq_t, ret_tabs, ret_dist)
            rk_t = _rope(rk_t, ret_tabs, ret_dist)
        rq_ref[:, sl] = rq_t.astype(BF16)
        rk_ref[:, sl] = rk_t.astype(BF16)
    rv_ref[...] = rv.astype(BF16)
    rg_ref[...] = _silu(rg).astype(BF16)


def _const_spec(shape):
    return pl.BlockSpec(shape, lambda i: (0,) * len(shape))


def _inproj(x, mod, mod_row0, tokens_per_row, wts, rope_tabs, tm):
    T = x.shape[0]
    bpr = tokens_per_row // tm
    rope = rope_tabs is not None
    tok = lambda w: pl.BlockSpec((tm, w), lambda i: (i, 0))
    in_specs = [tok(D_MODEL),
                pl.BlockSpec((1, 6, D_MODEL), lambda i: (mod_row0 + i // bpr, 0, 0)),
                _const_spec((1, D_MODEL)),
                _const_spec((D_MODEL, D_IN_PAD)),
                _const_spec((1, Q_RANK)),
                _const_spec((Q_RANK, H_MLA * HEAD_SLOT)),
                _const_spec((1, KV_RANK)),
                _const_spec((KV_RANK, H_MLA * (HEAD_SLOT + D_V))),
                _const_spec((1, LANES)),
                _const_spec((1, LANES))]
    args = [x, mod, wts["norm1_g"], wts["w1"], wts["q_norm_g"], wts["w_uq"], wts["kv_norm_g"],
            wts["w_kv"], wts["q_gain"], wts["k_gain"]]
    if rope:
        qk_tabs, ret_tabs = rope_tabs
        nb = qk_tabs[0].shape[0] // tm
        in_specs += [pl.BlockSpec((tm, LANES), lambda i: (i % nb, 0))] * 6
        args += list(qk_tabs) + list(ret_tabs)
    widths = [(KV_RANK, F32), (LANES, F32), (H_MLA * HEAD_SLOT, BF16), (H_MLA * HEAD_SLOT, BF16),
              (H_MLA * D_V, BF16)] + [(D_RET_ALL, BF16)] * 4
    return pl.pallas_call(
        functools.partial(_inproj_kernel, rope),
        grid=(T // tm,),
        in_specs=in_specs,
        out_specs=[tok(w) for w, _ in widths],
        out_shape=[jax.ShapeDtypeStruct((T, w), dt) for w, dt in widths],
        compiler_params=_cparams("parallel"),
        name="inproj_rope" if rope else "inproj",
    )(*args)


def _ctx_kv_kernel(ckv_ref, krp_ref, wkv_ref, kgain_ref, k_ref, v_ref):
    _kv_heads(ckv_ref[...], krp_ref[...], wkv_ref, kgain_ref[...], None, 0, k_ref, v_ref)


def _ctx_kv(ckv, krp, wts, tm):
    T = ckv.shape[0]
    tok = lambda w: pl.BlockSpec((tm, w), lambda i: (i, 0))
    return pl.pallas_call(
        _ctx_kv_kernel,
        grid=(T // tm,),
        in_specs=[tok(KV_RANK), tok(LANES),
                  _const_spec((KV_RANK, H_MLA * (HEAD_SLOT + D_V))), _const_spec((1, LANES))],
        out_specs=[tok(H_MLA * HEAD_SLOT), tok(H_MLA * D_V)],
        out_shape=[jax.ShapeDtypeStruct((T, H_MLA * HEAD_SLOT), BF16),
                   jax.ShapeDtypeStruct((T, H_MLA * D_V), BF16)],
        compiler_params=_cparams("parallel"),
        name="ctx_kv",
    )(ckv, krp, wts["w_kv"], wts["k_gain"])


def _attn_kernel(has_ctx, *refs):
    if has_ctx:
        q_ref, k_ref, v_ref, kc_ref, vc_ref, o_ref = refs
    else:
        q_ref, k_ref, v_ref, o_ref = refs
    outs = []
    for j in range(2):
        hs = slice(j * HEAD_SLOT, (j + 1) * HEAD_SLOT)
        vs = slice(j * D_V, (j + 1) * D_V)
        q = q_ref[0, :, hs]
        s = _dot_nt(q, k_ref[0, :, hs])
        m = jnp.max(s, axis=-1, keepdims=True)
        if has_ctx:
            sc = _dot_nt(q, kc_ref[0, :, hs])
            m = jnp.maximum(m, jnp.max(sc, axis=-1, keepdims=True))
        p = jnp.exp(s - m)
        l = jnp.sum(p, axis=-1, keepdims=True)
        o = _dot(p.astype(BF16), v_ref[0, :, vs])
        if has_ctx:
            pc = jnp.exp(sc - m)
            l = l + jnp.sum(pc, axis=-1, keepdims=True)
            o = o + _dot(pc.astype(BF16), vc_ref[0, :, vs])
        outs.append(o / l)
    o_ref[0] = jnp.concatenate(outs, axis=-1).astype(BF16)


def _attention(q, k, v, kc, vc, tq):
    B, L, _ = q.shape
    has_ctx = kc is not None
    in_specs = [pl.BlockSpec((1, tq, 2 * HEAD_SLOT), lambda b, hp, i: (b, i, hp)),
                pl.BlockSpec((1, L, 2 * HEAD_SLOT), lambda b, hp, i: (b, 0, hp)),
                pl.BlockSpec((1, L, 2 * D_V), lambda b, hp, i: (b, 0, hp))]
    args = [q, k, v]
    if has_ctx:
        Lc = kc.shape[1]
        in_specs += [pl.BlockSpec((1, Lc, 2 * HEAD_SLOT), lambda b, hp, i: (b, 0, hp)),
                     pl.BlockSpec((1, Lc, 2 * D_V), lambda b, hp, i: (b, 0, hp))]
        args += [kc, vc]
    return pl.pallas_call(
        functools.partial(_attn_kernel, has_ctx),
        grid=(B, H_MLA // 2, L // tq),
        in_specs=in_specs,
        out_specs=pl.BlockSpec((1, tq, 2 * D_V), lambda b, hp, i: (b, i, hp)),
        out_shape=jax.ShapeDtypeStruct((B, L, H_MLA * D_V), BF16),
        compiler_params=_cparams("parallel", "parallel", "arbitrary"),
        name="attn_ctx" if has_ctx else "attn",
    )(*args)


def _ret_kernel(has_s0, want_state, n_chunks, *refs):
    rq_ref, rk_ref, rv_ref, rg_ref, decf_ref, decb_ref, g_ref = refs[:7]
    refs = refs[7:]
    if has_s0:
        s0f_ref, s0b_ref = refs[:2]
        refs = refs[2:]
    o_ref = refs[0]
    refs = refs[1:]
    if want_state:
        sf_ref, sb_ref = refs[:2]
        refs = refs[2:]
    kv_ref, sp_ref = refs

    C = RET_CHUNK
    W = 2 * D_RET
    hp = pl.program_id(1)
    lane = lax.broadcasted_iota(jnp.int32, (1, W), 1)
    lane_h1 = lane >= D_RET
    row_h1 = lax.broadcasted_iota(jnp.int32, (W, 1), 0) >= D_RET
    blockdiag = row_h1 == lane_h1
    pos = lax.broadcasted_iota(jnp.int32, (C, 1), 0).astype(F32)
    diff = (lax.broadcasted_iota(jnp.int32, (C, C), 0)
            - lax.broadcasted_iota(jnp.int32, (C, C), 1)).astype(F32)

    lg = {}
    for name, ref in (("f", decf_ref), ("b", decb_ref)):
        a = -jnp.exp(ref[pl.ds(2 * hp, 1), :])
        b = -jnp.exp(ref[pl.ds(2 * hp + 1, 1), :])
        lg[name] = (a[:, :1], b[:, :1], jnp.where(lane_h1, b, a))
    lgf0, lgf1, lgf = lg["f"]
    lgb0, lgb1, lgb = lg["b"]
    qdf, kdf = jnp.exp(lgf * (pos + 1.0)), jnp.exp(lgf * (C - 1.0 - pos))
    qdb, kdb = jnp.exp(lgb * (C - pos)), jnp.exp(lgb * pos)
    row_h1_full = lax.broadcasted_iota(jnp.int32, (W, W), 0) >= D_RET
    cdf = jnp.where(row_h1_full, jnp.exp(lgf1 * C), jnp.exp(lgf0 * C))
    cdb = jnp.where(row_h1_full, jnp.exp(lgb1 * C), jnp.exp(lgb0 * C))

    def intra_decay(lf, lb):
        return jnp.where(diff >= 0, jnp.exp(lf * jnp.maximum(diff, 0.0)), jnp.exp(lb * jnp.maximum(-diff, 0.0)))

    dmat0, dmat1 = intra_decay(lgf0, lgb0), intra_decay(lgf1, lgb1)
    g = g_ref[...]

    def chunk_rows(n):
        return pl.ds(pl.multiple_of(n * C, C), C)

    def phase_a(n, carry):
        rows = chunk_rows(n)
        k = rk_ref[0, rows, :].astype(F32)
        k2 = jnp.concatenate([(k * kdf).astype(BF16), (k * kdb).astype(BF16)], axis=1)
        kv_ref[n] = _dot_tn(k2, rv_ref[0, rows, :])
        return carry

    lax.fori_loop(0, n_chunks, phase_a, 0, unroll=2)

    def init_state(ref):
        if not has_s0:
            return jnp.zeros((W, W), F32)
        z = jnp.zeros((D_RET, D_RET), F32)
        return jnp.concatenate([jnp.concatenate([ref[0, 0], z], axis=1),
                                jnp.concatenate([z, ref[0, 1]], axis=1)], axis=0)

    def scan_f(n, S):
        sp_ref[n, 0:W, :] = S.astype(BF16)
        return cdf * S + jnp.where(blockdiag, kv_ref[n, 0:W, :], 0.0)

    def scan_b(i, S):
        n = n_chunks - 1 - i
        sp_ref[n, W:2 * W, :] = S.astype(BF16)
        return cdb * S + jnp.where(blockdiag, kv_ref[n, W:2 * W, :], 0.0)

    Sf = lax.fori_loop(0, n_chunks, scan_f, init_state(s0f_ref if has_s0 else None))
    Sb = lax.fori_loop(0, n_chunks, scan_b, init_state(s0b_ref if has_s0 else None))
    if want_state:
        for S, ref in ((Sf, sf_ref), (Sb, sb_ref)):
            ref[0, 0] = S[0:D_RET, 0:D_RET]
            ref[0, 1] = S[D_RET:W, D_RET:W]

    def phase_c(n, carry):
        rows = chunk_rows(n)
        q = rq_ref[0, rows, :]
        k = rk_ref[0, rows, :]
        v = rv_ref[0, rows, :]
        zero = jnp.zeros_like(q)
        sc0 = _dot_nt(jnp.where(lane_h1, zero, q), k) * dmat0
        sc1 = _dot_nt(jnp.where(lane_h1, q, zero), k) * dmat1
        p = jnp.concatenate([sc0.astype(BF16), sc1.astype(BF16)], axis=1)
        v_bd = jnp.concatenate([jnp.where(lane_h1, zero, v), jnp.where(lane_h1, v, zero)], axis=0)
        qf = q.astype(F32)
        q2 = jnp.concatenate([(qf * qdf).astype(BF16), (qf * qdb).astype(BF16)], axis=1)
        o = _dot(p, v_bd) + _dot(q2, sp_ref[n])
        o2 = o * o
        ss0 = jnp.sum(jnp.where(lane_h1, 0.0, o2), axis=-1, keepdims=True)
        ss1 = jnp.sum(jnp.where(lane_h1, o2, 0.0), axis=-1, keepdims=True)
        ms = jnp.where(lane_h1, ss1, ss0) * (1.0 / D_RET)
        o = o * lax.rsqrt(ms + EPS) * g * rg_ref[0, rows, :].astype(F32)
        o_ref[0, rows, :] = o.astype(BF16)
        return carry

    lax.fori_loop(0, n_chunks, phase_c, 0, unroll=2)


def _retention(rq, rk, rv, rg, dec_f, dec_b, g, s0f, s0b, want_state):
    B, L, _ = rq.shape
    has_s0 = s0f is not None
    seq = pl.BlockSpec((1, L, LANES), lambda b, hp: (b, 0, hp))
    st = pl.BlockSpec((1, 2, D_RET, D_RET), lambda b, hp: (b, hp, 0, 0))
    cst = lambda shape: pl.BlockSpec(shape, lambda b, hp: (0,) * len(shape))
    in_specs = [seq] * 4 + [cst((H_RET, LANES)), cst((H_RET, LANES)), cst((1, LANES))]
    args = [rq, rk, rv, rg, dec_f, dec_b, g]
    if has_s0:
        in_specs += [st, st]
        args += [s0f, s0b]
    out_specs = [seq]
    out_shape = [jax.ShapeDtypeStruct((B, L, D_RET_ALL), BF16)]
    if want_state:
        out_specs += [st, st]
        out_shape += [jax.ShapeDtypeStruct((B, H_RET, D_RET, D_RET), F32)] * 2
    return pl.pallas_call(
        functools.partial(_ret_kernel, has_s0, want_state, L // RET_CHUNK),
        grid=(B, H_RET // 2),
        in_specs=in_specs,
        out_specs=out_specs,
        out_shape=out_shape,
        scratch_shapes=[pltpu.VMEM((L // RET_CHUNK, 4 * D_RET, LANES), F32),
                        pltpu.VMEM((L // RET_CHUNK, 4 * D_RET, LANES), BF16)],
        compiler_params=_cparams("parallel", "parallel"),
        name="retention_s0" if has_s0 else "retention",
    )(*args)


def _pack_pairs(x):
    w = x.shape[1] // 2
    hi = lax.bitcast_convert_type(x[:, :w].astype(BF16).astype(F32), jnp.uint32)
    lo = lax.bitcast_convert_type(x[:, w:].astype(BF16).astype(F32), jnp.uint32)
    return hi | (lo >> 16)


def _unpack_pairs(p):
    hi = lax.bitcast_convert_type(p & jnp.uint32(0xFFFF0000), F32)
    lo = lax.bitcast_convert_type(p << 16, F32)
    return jnp.concatenate([hi, lo], axis=1)


D_PACKED = D_MODEL // 2


def _outproj_kernel(omla_ref, oret_ref, x_ref, mod_ref, wo_ref, n2g_ref, rwh_ref, rwl_ref, rb_ref,
                    x2_ref, h2p_ref, eid_ref, rnk_ref, wk_ref, cnt_ref, carry_ref):
    @pl.when(pl.program_id(0) == 0)
    def _():
        carry_ref[...] = jnp.zeros_like(carry_ref)

    mod = mod_ref[0]
    half = H_MLA * D_V
    a = _dot(omla_ref[...], wo_ref[:half, :]) + _dot(oret_ref[...], wo_ref[half:, :])
    x2 = x_ref[...] + mod[2:3] * a
    x2_ref[...] = x2
    h2 = _rms(x2, n2g_ref[...]) * (1.0 + mod[4:5]) + mod[3:4]
    h2p_ref[...] = _pack_pairs(h2)
    h2_hi = h2.astype(BF16)
    h2_lo = (h2 - h2_hi.astype(F32)).astype(BF16)
    logits = _dot(h2_hi, rwh_ref[...]) + _dot(h2_lo, rwh_ref[...]) + _dot(h2_hi, rwl_ref[...])
    scores = jax.nn.sigmoid(logits)
    lane = lax.broadcasted_iota(jnp.int32, scores.shape, 1)
    cand = scores + rb_ref[...]
    picked = jnp.zeros(scores.shape, jnp.bool_)
    hits = []
    for _ in range(TOP_K):
        mx = jnp.max(cand, axis=-1, keepdims=True)
        first = jnp.min(jnp.where(cand == mx, lane, LANES), axis=-1, keepdims=True)
        hit = lane == first
        hits.append((first, hit))
        picked = jnp.logical_or(picked, hit)
        cand = jnp.where(hit, -jnp.inf, cand)
    sel = jnp.where(picked, scores, 0.0)
    gates = sel * (ROUTED_SCALE / jnp.sum(sel, axis=-1, keepdims=True))

    tm = scores.shape[0]
    onehot = jnp.where(picked, 1.0, 0.0)
    earlier = (lax.broadcasted_iota(jnp.int32, (tm, tm), 0) > lax.broadcasted_iota(jnp.int32, (tm, tm), 1))
    rank = _dot(jnp.where(earlier, 1.0, 0.0).astype(BF16), onehot.astype(BF16)) + carry_ref[...]
    carry_ref[...] += jnp.sum(onehot, axis=0, keepdims=True)
    cnt_ref[...] = carry_ref[...].astype(jnp.int32)

    eid = jnp.zeros(scores.shape, jnp.int32)
    rnk = jnp.zeros(scores.shape, F32)
    wk = jnp.zeros(scores.shape, F32)
    for k, (first, hit) in enumerate(hits):
        slot = lane == k
        eid = jnp.where(slot, first, eid)
        rnk = jnp.where(slot, jnp.sum(jnp.where(hit, rank, 0.0), axis=-1, keepdims=True), rnk)
        wk = jnp.where(slot, jnp.sum(jnp.where(hit, gates, 0.0), axis=-1, keepdims=True), wk)
    eid_ref[...] = eid
    rnk_ref[...] = rnk.astype(jnp.int32)
    wk_ref[...] = wk


def _outproj(o_mla, o_ret, x, mod, mod_row0, tokens_per_row, wts, tm):
    T = x.shape[0]
    bpr = tokens_per_row // tm
    tok = lambda w: pl.BlockSpec((tm, w), lambda i: (i, 0))
    return pl.pallas_call(
        _outproj_kernel,
        grid=(T // tm,),
        in_specs=[tok(H_MLA * D_V), tok(D_RET_ALL), tok(D_MODEL),
                  pl.BlockSpec((1, 6, D_MODEL), lambda i: (mod_row0 + i // bpr, 0, 0)),
                  _const_spec((D_MODEL, D_MODEL)), _const_spec((1, D_MODEL)),
                  _const_spec((D_MODEL, LANES)), _const_spec((D_MODEL, LANES)), _const_spec((1, LANES))],
        out_specs=[tok(D_MODEL), tok(D_PACKED), tok(LANES), tok(LANES), tok(LANES), _const_spec((1, LANES))],
        out_shape=[jax.ShapeDtypeStruct((T, D_MODEL), F32),
                   jax.ShapeDtypeStruct((T, D_PACKED), jnp.uint32),
                   jax.ShapeDtypeStruct((T, LANES), jnp.int32),
                   jax.ShapeDtypeStruct((T, LANES), jnp.int32),
                   jax.ShapeDtypeStruct((T, LANES), F32),
                   jax.ShapeDtypeStruct((1, LANES), jnp.int32)],
        scratch_shapes=[pltpu.VMEM((1, LANES), F32)],
        compiler_params=_cparams("arbitrary"),
        name="outproj_router",
    )(o_mla, o_ret, x, mod, wts["w_o"], wts["norm2_g"], wts["router_hi"], wts["router_lo"],
      wts["router_bias"])


SC_WORKERS = 32
SC_CHUNK = 64


def _sc_worker_id():
    return lax.axis_index("s") * 2 + lax.axis_index("c")


def _sc_mesh():
    return plsc.VectorSubcoreMesh(core_axis_name="c", subcore_axis_name="s")


def _sc_gather_rows(table, idx):
    N = idx.shape[0]
    W = table.shape[1]
    per_w = N // SC_WORKERS
    n_chunks = per_w // SC_CHUNK
    assert N == SC_WORKERS * n_chunks * SC_CHUNK and n_chunks % 2 == 0

    @functools.partial(
        pl.kernel, mesh=_sc_mesh(), out_type=jax.ShapeDtypeStruct((N, W), table.dtype),
        scratch_types=[pltpu.VMEM((n_chunks, SC_CHUNK), jnp.int32),
                       pltpu.VMEM((2, SC_CHUNK, W), table.dtype),
                       pltpu.SemaphoreType.DMA((2,)), pltpu.SemaphoreType.DMA((2,))],
        name="sc_gather_rows")
    def k(table_hbm, idx_hbm, out_hbm, idx_v, rows_v, gsem, wsem):
        wid = _sc_worker_id()
        base = wid * per_w
        pltpu.sync_copy(idx_hbm.at[wid], idx_v)

        def gather(j, b):
            return pltpu.make_async_copy(table_hbm.at[idx_v.at[j]], rows_v.at[b], gsem.at[b])

        def writeback(j, b):
            off = pl.multiple_of(base + j * SC_CHUNK, SC_CHUNK)
            return pltpu.make_async_copy(rows_v.at[b], out_hbm.at[pl.ds(off, SC_CHUNK)], wsem.at[b])

        gather(0, 0).start()

        @pl.loop(0, n_chunks, step=2)
        def _(j):
            for b in range(2):
                jj = j + b

                @pl.when(jj + 1 < n_chunks)
                def _():
                    @pl.when(jj >= 1)
                    def _():
                        writeback(jj - 1, 1 - b).wait()
                    gather(jj + 1, 1 - b).start()

                gather(jj, b).wait()
                writeback(jj, b).start()

        writeback(n_chunks - 2, 0).wait()
        writeback(n_chunks - 1, 1).wait()

    return k(table, idx.reshape(SC_WORKERS, n_chunks, SC_CHUNK))


def _sc_dispatch_rows(rows, pos, n_out):
    T, W = rows.shape
    K = pos.shape[1]
    per_w = T // SC_WORKERS
    n_chunks = per_w // SC_CHUNK
    assert T == SC_WORKERS * n_chunks * SC_CHUNK and n_chunks % 2 == 0
    idx = pos.reshape(SC_WORKERS, n_chunks, SC_CHUNK, K).transpose(0, 1, 3, 2)

    @functools.partial(
        pl.kernel, mesh=_sc_mesh(), out_type=jax.ShapeDtypeStruct((n_out, W), rows.dtype),
        scratch_types=[pltpu.VMEM((n_chunks, K, SC_CHUNK), jnp.int32),
                       pltpu.VMEM((2, SC_CHUNK, W), rows.dtype),
                       pltpu.SemaphoreType.DMA((2,)), pltpu.SemaphoreType.DMA((2,))],
        name="sc_dispatch_rows")
    def k(rows_hbm, idx_hbm, out_hbm, idx_v, rows_v, lsem, ssem):
        wid = _sc_worker_id()
        base = wid * per_w
        pltpu.sync_copy(idx_hbm.at[wid], idx_v)

        def load(j, b):
            off = pl.multiple_of(base + j * SC_CHUNK, SC_CHUNK)
            return pltpu.make_async_copy(rows_hbm.at[pl.ds(off, SC_CHUNK)], rows_v.at[b], lsem.at[b])

        def scatter(j, b, kk):
            return pltpu.make_async_copy(rows_v.at[b], out_hbm.at[idx_v.at[j, kk]], ssem.at[b])

        load(0, 0).start()

        @pl.loop(0, n_chunks, step=2)
        def _(j):
            for b in range(2):
                jj = j + b

                @pl.when(jj + 1 < n_chunks)
                def _():
                    @pl.when(jj >= 1)
                    def _():
                        for kk in range(K):
                            scatter(jj - 1, 1 - b, kk).wait()
                    load(jj + 1, 1 - b).start()

                load(jj, b).wait()
                for kk in range(K):
                    scatter(jj, b, kk).start()

        for kk in range(K):
            scatter(n_chunks - 2, 0, kk).wait()
        for kk in range(K):
            scatter(n_chunks - 1, 1, kk).wait()

    return k(rows, idx)


ROW_TILE = 512


def _experts_kernel(te_ref, tv_ref, xs_ref, wg_ref, wu_ref, wd_ref, ys_ref):
    del te_ref
    valid = tv_ref[pl.program_id(0)]

    @pl.when(valid > 0)
    def _():
        row = lax.broadcasted_iota(jnp.int32, (ROW_TILE, 1), 0)
        p = jnp.where(row < valid, xs_ref[...], jnp.uint32(0))
        x = _unpack_pairs(p).astype(BF16)
        act = _silu(_dot(x, wg_ref[0])) * _dot(x, wu_ref[0])
        ys_ref[...] = _pack_pairs(_dot(act.astype(BF16), wd_ref[0]))

    @pl.when(valid <= 0)
    def _():
        ys_ref[...] = jnp.zeros_like(ys_ref)


def _experts(xs, tile_expert, tile_valid, wts):
    n_tiles = tile_expert.shape[0]
    ex = lambda a, b: pl.BlockSpec((1, a, b), lambda i, te, tv: (te[i], 0, 0))
    row = pl.BlockSpec((ROW_TILE, D_PACKED), lambda i, te, tv: (i, 0))
    return pl.pallas_call(
        _experts_kernel,
        grid_spec=pltpu.PrefetchScalarGridSpec(
            num_scalar_prefetch=2, grid=(n_tiles,),
            in_specs=[row, ex(D_MODEL, D_EXPERT), ex(D_MODEL, D_EXPERT), ex(D_EXPERT, D_MODEL)],
            out_specs=row),
        out_shape=jax.ShapeDtypeStruct(xs.shape, jnp.uint32),
        compiler_params=_cparams("arbitrary"),
        name="experts",
    )(tile_expert, tile_valid, xs, wts["exp_gate"], wts["exp_up"], wts["exp_down"])


def _combine_kernel(h2p_ref, ys_ref, wk_ref, x2_ref, mod_ref, shg_ref, shu_ref, shd_ref, y_ref):
    t = _unpack_pairs(h2p_ref[...]).astype(BF16)
    act = _silu(_dot(t, shg_ref[...])) * _dot(t, shu_ref[...])
    acc = _dot(act.astype(BF16), shd_ref[...])
    wk = wk_ref[...]
    for k in range(TOP_K):
        acc = acc + wk[:, k:k + 1] * _unpack_pairs(ys_ref[k])
    y_ref[...] = x2_ref[...] + mod_ref[0][5:6] * acc


def _combine(h2p, ys, wk, x2, mod, mod_row0, tokens_per_row, wts, tm):
    T = h2p.shape[0]
    bpr = tokens_per_row // tm
    tok = lambda w: pl.BlockSpec((tm, w), lambda i: (i, 0))
    return pl.pallas_call(
        _combine_kernel,
        grid=(T // tm,),
        in_specs=[tok(D_PACKED), pl.BlockSpec((TOP_K, tm, D_PACKED), lambda i: (0, i, 0)), tok(LANES),
                  tok(D_MODEL), pl.BlockSpec((1, 6, D_MODEL), lambda i: (mod_row0 + i // bpr, 0, 0)),
                  _const_spec((D_MODEL, D_SHARED)), _const_spec((D_MODEL, D_SHARED)),
                  _const_spec((D_SHARED, D_MODEL))],
        out_specs=tok(D_MODEL),
        out_shape=jax.ShapeDtypeStruct((T, D_MODEL), F32),
        compiler_params=_cparams("parallel"),
        name="moe_combine",
    )(h2p, ys, wk, x2, mod, wts["sh_gate"], wts["sh_up"], wts["sh_down"])


def _moe(h2p, eid, rnk, wk, counts, x2, mod, mod_row0, tokens_per_row, wts, tm):
    T = h2p.shape[0]
    n_tiles = (TOP_K * T) // ROW_TILE + N_EXPERTS
    counts = counts[0, :N_EXPERTS]
    padded = ((counts + ROW_TILE - 1) // ROW_TILE) * ROW_TILE
    ends = jnp.cumsum(padded)
    starts = ends - padded
    pos = starts[eid[:, :TOP_K]] + rnk[:, :TOP_K]
    tile_start = jnp.arange(n_tiles, dtype=jnp.int32) * ROW_TILE
    tile_expert = jnp.minimum(jnp.searchsorted(ends, tile_start, side="right"), N_EXPERTS - 1).astype(jnp.int32)
    tile_valid = jnp.clip(counts[tile_expert] - (tile_start - starts[tile_expert]), 0, ROW_TILE).astype(jnp.int32)

    xs = _sc_dispatch_rows(h2p, pos.astype(jnp.int32), n_tiles * ROW_TILE)
    ys = _experts(xs, tile_expert, tile_valid, wts)
    ysel = _sc_gather_rows(ys, pos.T.reshape(TOP_K * T).astype(jnp.int32)).reshape(TOP_K, T, D_PACKED)
    return _combine(h2p, ysel, wk, x2, mod, mod_row0, tokens_per_row, wts, tm)


def _pad_heads(w, d_used):
    k = w.shape[0]
    w = w.reshape(k, H_MLA, d_used)
    return jnp.pad(w, ((0, 0), (0, 0), (0, HEAD_SLOT - d_used))).reshape(k, H_MLA * HEAD_SLOT)


def _lane_pad(v, lane0=0):
    return jnp.pad(v, (lane0, LANES - lane0 - v.shape[0])).reshape(1, LANES)


def _prep_weights(l, w_in, norm1_g, mla_q_norm_g, w_uq, mla_kv_norm_g, w_ukv, mla_q_gain, mla_k_gain,
                  w_o, norm2_g, router_w, router_bias, exp_w_gate, exp_w_up, exp_w_down,
                  sh_w_gate, sh_w_up, sh_w_down):
    w = w_in[l]
    a, b, c = Q_RANK, Q_RANK + KV_RANK, Q_RANK + KV_RANK + D_ROPE
    w_kr = jnp.pad(w[:, b:c], ((0, 0), (D_NOPE, LANES - D_NOPE - D_ROPE)))
    w1 = jnp.concatenate([w[:, :b], w_kr, w[:, c:]], axis=1).astype(BF16)
    ukv = w_ukv[l].reshape(KV_RANK, H_MLA, D_NOPE + D_V)
    w_k = _pad_heads(ukv[:, :, :D_NOPE].reshape(KV_RANK, H_MLA * D_NOPE), D_NOPE)
    w_v = ukv[:, :, D_NOPE:].reshape(KV_RANK, H_MLA * D_V)
    rw = jnp.pad(router_w[l], ((0, 0), (0, LANES - N_EXPERTS)))
    rw_hi = rw.astype(BF16)
    rw_lo = (rw - rw_hi.astype(F32)).astype(BF16)
    rb = jnp.concatenate([router_bias[l].astype(F32), jnp.full((LANES - N_EXPERTS,), -jnp.inf, F32)])
    return dict(
        norm1_g=norm1_g[l].reshape(1, D_MODEL), w1=w1,
        q_norm_g=mla_q_norm_g[l].reshape(1, Q_RANK), w_uq=_pad_heads(w_uq[l], D_QK).astype(BF16),
        kv_norm_g=mla_kv_norm_g[l].reshape(1, KV_RANK),
        w_kv=jnp.concatenate([w_k, w_v], axis=1).astype(BF16),
        q_gain=_lane_pad(mla_q_gain[l]), k_gain=_lane_pad(mla_k_gain[l]),
        w_o=w_o[l].astype(BF16), norm2_g=norm2_g[l].reshape(1, D_MODEL),
        router_hi=rw_hi, router_lo=rw_lo, router_bias=rb.reshape(1, LANES),
        exp_gate=exp_w_gate[l].astype(BF16), exp_up=exp_w_up[l].astype(BF16),
        exp_down=exp_w_down[l].astype(BF16),
        sh_gate=sh_w_gate[l].astype(BF16), sh_up=sh_w_up[l].astype(BF16),
        sh_down=sh_w_down[l].astype(BF16))


def _pick(n, prefs):
    for p in prefs:
        if n % p == 0:
            return p
    return n


def _trunk_layer(x, mod, mod_row0, wts, ret_wts, ctx):
    B, L, D = x.shape
    T = B * L
    tokens_per_row = L if ctx is not None else T
    xt = x.reshape(T, D)
    rope_tabs = None
    if ctx is not None:
        qk_tabs, _, ret_tabs, _ = _rope_tables(L)
        rope_tabs = (qk_tabs, ret_tabs)
    tm = _pick(L, (512, 256, 128))
    ckv_n, krp, q, k, v, rq, rk, rv, rg = _inproj(xt, mod, mod_row0, tokens_per_row, wts, rope_tabs, tm)

    seq = lambda a: a.reshape(B, L, a.shape[-1])
    kc = vc = s0f = s0b = None
    if ctx is not None:
        ckv_c, kr_c, s0f, s0b = ctx
        Lc = ckv_c.shape[1]
        krp_c = jnp.pad(kr_c, ((0, 0), (0, 0), (D_NOPE, LANES - D_NOPE - D_ROPE)))
        kc, vc = _ctx_kv(ckv_c.reshape(B * Lc, KV_RANK), krp_c.reshape(B * Lc, LANES), wts,
                         _pick(B * Lc, (512, 256, 128)))
        kc, vc = kc.reshape(B, Lc, -1), vc.reshape(B, Lc, -1)
    o_mla = _attention(seq(q), seq(k), seq(v), kc, vc, _pick(L, (256, 128)))

    dec_f, dec_b, ret_g = ret_wts
    want_state = ctx is None
    ret = _retention(seq(rq), seq(rk), seq(rv), seq(rg), dec_f, dec_b, ret_g, s0f, s0b, want_state)
    o_ret = ret[0]

    x2, h2p, eid, rnk, wk, counts = _outproj(o_mla.reshape(T, -1), o_ret.reshape(T, -1), xt, mod, mod_row0,
                                             tokens_per_row, wts, tm)
    y = _moe(h2p, eid, rnk, wk, counts, x2, mod, mod_row0, tokens_per_row, wts, tm)
    new = None
    if want_state:
        new = (ckv_n.reshape(B, L, KV_RANK), krp[:, D_NOPE:D_NOPE + D_ROPE].reshape(B, L, D_ROPE),
               ret[1], ret[2])
    return y.reshape(B, L, D), new


def kernel(x_prompt, x_sample, cache_mla_ckv, cache_mla_krope, state_ret_fwd, state_ret_bwd, c, c_ctx,
           w_ada, b_ada, norm1_g, w_in, mla_q_norm_g, w_uq, mla_kv_norm_g, w_ukv, mla_q_gain, mla_k_gain,
           ret_decay_fwd, ret_decay_bwd, ret_norm_g, w_o, norm2_g, router_w, router_bias, exp_w_gate,
           exp_w_up, exp_w_down, sh_w_gate, sh_w_up, sh_w_down):
    depth = w_ada.shape[0]
    n_dec = c.shape[0]
    assert 1 + n_dec <= MOD_ROWS
    cond = jnp.concatenate([c_ctx[None], c, jnp.zeros((MOD_ROWS - 1 - n_dec, D_MODEL), F32)], axis=0)

    y_prompt, y_sample = x_prompt, x_sample
    ckv_l, kr_l, sf_l, sb_l = [], [], [], []
    for l in range(depth):
        wts = _prep_weights(l, w_in, norm1_g, mla_q_norm_g, w_uq, mla_kv_norm_g, w_ukv, mla_q_gain,
                            mla_k_gain, w_o, norm2_g, router_w, router_bias, exp_w_gate, exp_w_up,
                            exp_w_down, sh_w_gate, sh_w_up, sh_w_down)
        ret_wts = (jnp.broadcast_to(ret_decay_fwd[l].astype(F32)[:, None], (H_RET, LANES)),
                   jnp.broadcast_to(ret_decay_bwd[l].astype(F32)[:, None], (H_RET, LANES)),
                   jnp.tile(ret_norm_g[l].reshape(1, D_RET), (1, LANES // D_RET)))
        mod = _adaln(cond, w_ada[l], b_ada[l]).reshape(MOD_ROWS, 6, D_MODEL)
        y_prompt, new = _trunk_layer(y_prompt, mod, 0, wts, ret_wts, None)
        ckv_l.append(new[0]); kr_l.append(new[1]); sf_l.append(new[2]); sb_l.append(new[3])
        ctx = (cache_mla_ckv[:, l], cache_mla_krope[:, l], state_ret_fwd[:, l], state_ret_bwd[:, l])
        y_sample, _ = _trunk_layer(y_sample, mod, 1, wts, ret_wts, ctx)

    return (y_prompt, y_sample, jnp.stack(ckv_l, axis=1), jnp.stack(kr_l, axis=1),
            jnp.stack(sf_l, axis=1), jnp.stack(sb_l, axis=1))
```

```python
import functools
import math

import numpy as np
import jax
import jax.numpy as jnp
from jax import lax
from jax.experimental import pallas as pl
from jax.experimental.pallas import tpu as pltpu
from jax.experimental.pallas import tpu_sc as plsc

F32 = jnp.float32
BF16 = jnp.bfloat16

D_MODEL = 1024
GRID_W = 64
H_MLA = 8
D_NOPE = 64
D_ROPE = 32
D_QK = D_NOPE + D_ROPE
D_V = 64
Q_RANK = 256
KV_RANK = 128
H_RET = 8
D_RET = 64
RET_CHUNK = 128
D_RET_ALL = H_RET * D_RET
N_EXPERTS = 64
TOP_K = 6
D_EXPERT = 256
D_SHARED = 256
ROUTED_SCALE = 2.5
ROPE_BASE = 10000.0
EPS = 1e-6
LOG2_E = math.log2(math.e)

LANES = 128
HEAD_SLOT = LANES
D_IN_PAD = Q_RANK + KV_RANK + LANES + 4 * D_RET_ALL
MOD_ROWS = 16
VMEM_LIMIT = 56 * 1024 * 1024


def _cparams(*sem):
    return pltpu.CompilerParams(dimension_semantics=sem, vmem_limit_bytes=VMEM_LIMIT)


def _dot(a, b):
    return jnp.dot(a, b, preferred_element_type=F32)


def _dot_nt(a, b):
    return lax.dot_general(a, b, (((1,), (1,)), ((), ())), preferred_element_type=F32)


def _dot_tn(a, b):
    return lax.dot_general(a, b, (((0,), (0,)), ((), ())), preferred_element_type=F32)


def _rms(x, g):
    return x * lax.rsqrt(jnp.mean(x * x, axis=-1, keepdims=True) + EPS) * g


def _silu(x):
    return x * jax.nn.sigmoid(x)


def _adaln_kernel(c_ref, w_ref, b_ref, o_ref):
    s = _silu(c_ref[...])
    o_ref[...] = _dot(s.astype(BF16), w_ref[...].astype(BF16)) + b_ref[...]


def _adaln(cond, w_ada, b_ada):
    n_out = w_ada.shape[1]
    bn = 1536
    return pl.pallas_call(
        _adaln_kernel,
        grid=(n_out // bn,),
        in_specs=[pl.BlockSpec((MOD_ROWS, D_MODEL), lambda j: (0, 0)),
                  pl.BlockSpec((D_MODEL, bn), lambda j: (0, j)),
                  pl.BlockSpec((1, bn), lambda j: (0, j))],
        out_specs=pl.BlockSpec((MOD_ROWS, bn), lambda j: (0, j)),
        out_shape=jax.ShapeDtypeStruct((MOD_ROWS, n_out), F32),
        compiler_params=_cparams("arbitrary"),
        name="adaln",
    )(cond, w_ada, b_ada.reshape(1, n_out))


def _rope_tables(n_tokens):
    t = np.arange(n_tokens)
    row = (t // GRID_W).astype(np.float64)
    col = (t % GRID_W).astype(np.float64)

    def axis_tables(width, lane0, period):
        p = width // 2
        inv = 1.0 / (ROPE_BASE ** (np.arange(p, dtype=np.float64) / p))
        cos = np.ones((n_tokens, LANES))
        sa = np.zeros((n_tokens, LANES))
        sb = np.zeros((n_tokens, LANES))
        starts = range(lane0, LANES, period) if period else (lane0,)
        for s0 in starts:
            for base, pos in ((s0, row), (s0 + width, col)):
                ang = pos[:, None] * inv[None, :]
                c, s = np.cos(ang), np.sin(ang)
                cos[:, base:base + p] = c
                cos[:, base + p:base + 2 * p] = c
                sa[:, base:base + p] = -s
                sb[:, base + p:base + 2 * p] = s
        return tuple(jnp.asarray(a, F32) for a in (cos, sa, sb))

    qk = axis_tables(D_ROPE // 2, D_NOPE, 0)
    ret = axis_tables(D_RET // 2, 0, D_RET)
    return qk, D_ROPE // 4, ret, D_RET // 4


def _rope(x, tabs, dist):
    cos, sa, sb = tabs
    return x * cos + pltpu.roll(x, LANES - dist, 1) * sa + pltpu.roll(x, dist, 1) * sb


def _kv_heads(ckv_n, krp, wkv_ref, kgain, tabs, dist, k_ref, v_ref):
    kvp = _dot(ckv_n.astype(BF16), wkv_ref[...])
    ss_kr = jnp.sum(krp * krp, axis=-1, keepdims=True)
    for h in range(H_MLA):
        kh = kvp[:, h * HEAD_SLOT:(h + 1) * HEAD_SLOT]
        ms = (jnp.sum(kh * kh, axis=-1, keepdims=True) + ss_kr) * (1.0 / D_QK)
        kh = (kh + krp) * lax.rsqrt(ms + EPS) * kgain
        if tabs is not None:
            kh = _rope(kh, tabs, dist)
        k_ref[:, h * HEAD_SLOT:(h + 1) * HEAD_SLOT] = kh.astype(BF16)
    v_lane = lax.broadcasted_iota(jnp.int32, (1, H_MLA * HEAD_SLOT), 1) % HEAD_SLOT
    v_ref[...] = (kvp[:, H_MLA * HEAD_SLOT:] + jnp.where(v_lane == D_V, 1.0, 0.0)).astype(BF16)


def _inproj_kernel(rope, *refs):
    (x_ref, mod_ref, n1g_ref, w1_ref, qng_ref, wuq_ref, kvg_ref, wkv_ref, qgain_ref,
     kgain_ref) = refs[:10]
    refs = refs[10:]
    if rope:
        qk_tabs = tuple(r[...] for r in refs[:3])
        ret_tabs = tuple(r[...] for r in refs[3:6])
        refs = refs[6:]
    else:
        qk_tabs = ret_tabs = None
    ckvn_ref, krp_ref, q_ref, k_ref, v_ref, rq_ref, rk_ref, rv_ref, rg_ref = refs
    qk_dist, ret_dist = D_ROPE // 4, D_RET // 4

    mod = mod_ref[0]
    h = _rms(x_ref[...], n1g_ref[...]) * (1.0 + mod[1:2]) + mod[0:1]
    z = _dot(h.astype(BF16), w1_ref[...])
    o = 0
    cq = z[:, o:o + Q_RANK]; o += Q_RANK
    ckv = z[:, o:o + KV_RANK]; o += KV_RANK
    krp = z[:, o:o + LANES]; o += LANES
    rq = z[:, o:o + D_RET_ALL]; o += D_RET_ALL
    rk = z[:, o:o + D_RET_ALL]; o += D_RET_ALL
    rv = z[:, o:o + D_RET_ALL]; o += D_RET_ALL
    rg = z[:, o:o + D_RET_ALL]

    ckv_n = _rms(ckv, kvg_ref[...])
    ckvn_ref[...] = ckv_n
    krp_ref[...] = krp

    q = _dot(_rms(cq, qng_ref[...]).astype(BF16), wuq_ref[...])
    qgain = qgain_ref[...] * (D_QK ** -0.5 * LOG2_E)
    for hd in range(H_MLA):
        qh = q[:, hd * HEAD_SLOT:(hd + 1) * HEAD_SLOT]
        ms = jnp.sum(qh * qh, axis=-1, keepdims=True) * (1.0 / D_QK)
        qh = qh * lax.rsqrt(ms + EPS) * qgain
        if rope:
            qh = _rope(qh, qk_tabs, qk_dist)
        q_ref[:, hd * HEAD_SLOT:(hd + 1) * HEAD_SLOT] = qh.astype(BF16)

    _kv_heads(ckv_n, krp, wkv_ref, kgain_ref[...], qk_tabs, qk_dist, k_ref, v_ref)

    for t in range(D_RET_ALL // LANES):
        sl = slice(t * LANES, (t + 1) * LANES)
        rq_t, rk_t = rq[:, sl], rk[:, sl] * (D_RET ** -0.5)
        if rope:
            rq_t = _rope(rq_t, ret_tabs, ret_dist)
            rk_t = _rope(rk_t, ret_tabs, ret_dist)
        rq_ref[:, sl] = rq_t.astype(BF16)
        rk_ref[:, sl] = rk_t.astype(BF16)
    rv_ref[...] = rv.astype(BF16)
    rg_ref[...] = _silu(rg).astype(BF16)


def _const_spec(shape):
    return pl.BlockSpec(shape, lambda i: (0,) * len(shape))


def _inproj(x, mod, mod_row0, tokens_per_row, wts, rope_tabs, tm):
    T = x.shape[0]
    bpr = tokens_per_row // tm
    rope = rope_tabs is not None
    tok = lambda w: pl.BlockSpec((tm, w), lambda i: (i, 0))
    in_specs = [tok(D_MODEL),
                pl.BlockSpec((1, 6, D_MODEL), lambda i: (mod_row0 + i // bpr, 0, 0)),
                _const_spec((1, D_MODEL)),
                _const_spec((D_MODEL, D_IN_PAD)),
                _const_spec((1, Q_RANK)),
                _const_spec((Q_RANK, H_MLA * HEAD_SLOT)),
                _const_spec((1, KV_RANK)),
                _const_spec((KV_RANK, 2 * H_MLA * HEAD_SLOT)),
                _const_spec((1, LANES)),
                _const_spec((1, LANES))]
    args = [x, mod, wts["norm1_g"], wts["w1"], wts["q_norm_g"], wts["w_uq"], wts["kv_norm_g"],
            wts["w_kv"], wts["q_gain"], wts["k_gain"]]
    if rope:
        qk_tabs, ret_tabs = rope_tabs
        nb = qk_tabs[0].shape[0] // tm
        in_specs += [pl.BlockSpec((tm, LANES), lambda i: (i % nb, 0))] * 6
        args += list(qk_tabs) + list(ret_tabs)
    widths = [(KV_RANK, F32), (LANES, F32), (H_MLA * HEAD_SLOT, BF16), (H_MLA * HEAD_SLOT, BF16),
              (H_MLA * HEAD_SLOT, BF16)] + [(D_RET_ALL, BF16)] * 4
    return pl.pallas_call(
        functools.partial(_inproj_kernel, rope),
        grid=(T // tm,),
        in_specs=in_specs,
        out_specs=[tok(w) for w, _ in widths],
        out_shape=[jax.ShapeDtypeStruct((T, w), dt) for w, dt in widths],
        compiler_params=_cparams("parallel"),
        name="inproj_rope" if rope else "inproj",
    )(*args)


def _ctx_kv_kernel(ckv_ref, krp_ref, wkv_ref, kgain_ref, k_ref, v_ref):
    _kv_heads(ckv_ref[...], krp_ref[...], wkv_ref, kgain_ref[...], None, 0, k_ref, v_ref)


def _ctx_kv(ckv, krp, wts, tm):
    T = ckv.shape[0]
    tok = lambda w: pl.BlockSpec((tm, w), lambda i: (i, 0))
    return pl.pallas_call(
        _ctx_kv_kernel,
        grid=(T // tm,),
        in_specs=[tok(KV_RANK), tok(LANES),
                  _const_spec((KV_RANK, 2 * H_MLA * HEAD_SLOT)), _const_spec((1, LANES))],
        out_specs=[tok(H_MLA * HEAD_SLOT), tok(H_MLA * HEAD_SLOT)],
        out_shape=[jax.ShapeDtypeStruct((T, H_MLA * HEAD_SLOT), BF16),
                   jax.ShapeDtypeStruct((T, H_MLA * HEAD_SLOT), BF16)],
        compiler_params=_cparams("parallel"),
        name="ctx_kv",
    )(ckv, krp, wts["w_kv"], wts["k_gain"])


ATTN_KEY_CHUNK = 512


def _attn_kernel(has_ctx, *refs):
    if has_ctx:
        q_ref, k_ref, v_ref, kc_ref, vc_ref, o_ref = refs
    else:
        q_ref, k_ref, v_ref, o_ref = refs
    sources = [(k_ref, v_ref)] + ([(kc_ref, vc_ref)] if has_ctx else [])
    chunks = []
    for kr, vr in sources:
        n = kr.shape[1]
        kc = _pick(n, (ATTN_KEY_CHUNK, 256, 128))
        chunks += [(kr, vr, slice(c * kc, (c + 1) * kc)) for c in range(n // kc)]
    tq = q_ref.shape[1]
    outs = []
    for j in range(2):
        hs = slice(j * HEAD_SLOT, (j + 1) * HEAD_SLOT)
        q = q_ref[0, :, hs]
        m = jnp.full((tq, 1), -jnp.inf, F32)
        acc = jnp.zeros((tq, HEAD_SLOT), F32)
        for kr, vr, rows in chunks:
            s = _dot_nt(q, kr[0, rows, hs])
            m_new = jnp.maximum(m, jnp.max(s, axis=-1, keepdims=True))
            p = jnp.exp2(s - m_new).astype(BF16)
            acc = acc * jnp.exp2(m - m_new) + _dot(p, vr[0, rows, hs])
            m = m_new
        outs.append(acc[:, :D_V] / acc[:, D_V:D_V + 1])
    o_ref[0] = jnp.concatenate(outs, axis=-1).astype(BF16)


def _attention(q, k, v, kc, vc, tq):
    B, L, _ = q.shape
    has_ctx = kc is not None
    in_specs = [pl.BlockSpec((1, tq, 2 * HEAD_SLOT), lambda b, hp, i: (b, i, hp)),
                pl.BlockSpec((1, L, 2 * HEAD_SLOT), lambda b, hp, i: (b, 0, hp)),
                pl.BlockSpec((1, L, 2 * HEAD_SLOT), lambda b, hp, i: (b, 0, hp))]
    args = [q, k, v]
    n_keys = L
    if has_ctx:
        Lc = kc.shape[1]
        n_keys += Lc
        in_specs += [pl.BlockSpec((1, Lc, 2 * HEAD_SLOT), lambda b, hp, i: (b, 0, hp)),
                     pl.BlockSpec((1, Lc, 2 * HEAD_SLOT), lambda b, hp, i: (b, 0, hp))]
        args += [kc, vc]
    return pl.pallas_call(
        functools.partial(_attn_kernel, has_ctx),
        grid=(B, H_MLA // 2, L // tq),
        in_specs=in_specs,
        out_specs=pl.BlockSpec((1, tq, 2 * D_V), lambda b, hp, i: (b, i, hp)),
        out_shape=jax.ShapeDtypeStruct((B, L, H_MLA * D_V), BF16),
        compiler_params=_cparams("parallel", "parallel", "arbitrary"),
        name="attn_ctx" if has_ctx else "attn",
    )(*args)


def _ret_kernel(has_s0, want_state, n_chunks, *refs):
    rq_ref, rk_ref, rv_ref, rg_ref, decf_ref, decb_ref, g_ref = refs[:7]
    refs = refs[7:]
    if has_s0:
        s0f_ref, s0b_ref = refs[:2]
        refs = refs[2:]
    o_ref = refs[0]
    refs = refs[1:]
    if want_state:
        sf_ref, sb_ref = refs[:2]
        refs = refs[2:]
    kv_ref, sp_ref = refs

    C = RET_CHUNK
    W = 2 * D_RET
    hp = pl.program_id(1)
    lane = lax.broadcasted_iota(jnp.int32, (1, W), 1)
    lane_h1 = lane >= D_RET
    row_h1 = lax.broadcasted_iota(jnp.int32, (W, 1), 0) >= D_RET
    blockdiag = row_h1 == lane_h1
    pos = lax.broadcasted_iota(jnp.int32, (C, 1), 0).astype(F32)
    diff = (lax.broadcasted_iota(jnp.int32, (C, C), 0)
            - lax.broadcasted_iota(jnp.int32, (C, C), 1)).astype(F32)

    lg = {}
    for name, ref in (("f", decf_ref), ("b", decb_ref)):
        a = -jnp.exp(ref[pl.ds(2 * hp, 1), :])
        b = -jnp.exp(ref[pl.ds(2 * hp + 1, 1), :])
        lg[name] = (a[:, :1], b[:, :1], jnp.where(lane_h1, b, a))
    lgf0, lgf1, lgf = lg["f"]
    lgb0, lgb1, lgb = lg["b"]
    qdf, kdf = jnp.exp(lgf * (pos + 1.0)), jnp.exp(lgf * (C - 1.0 - pos))
    qdb, kdb = jnp.exp(lgb * (C - pos)), jnp.exp(lgb * pos)
    row_h1_full = lax.broadcasted_iota(jnp.int32, (W, W), 0) >= D_RET
    cdf = jnp.where(row_h1_full, jnp.exp(lgf1 * C), jnp.exp(lgf0 * C))
    cdb = jnp.where(row_h1_full, jnp.exp(lgb1 * C), jnp.exp(lgb0 * C))

    def intra_decay(lf, lb):
        return jnp.where(diff >= 0, jnp.exp(lf * jnp.maximum(diff, 0.0)), jnp.exp(lb * jnp.maximum(-diff, 0.0)))

    dmat0, dmat1 = intra_decay(lgf0, lgb0), intra_decay(lgf1, lgb1)
    g = g_ref[...]

    def chunk_rows(n):
        return pl.ds(pl.multiple_of(n * C, C), C)

    def phase_a(n, carry):
        rows = chunk_rows(n)
        k = rk_ref[0, rows, :].astype(F32)
        k2 = jnp.concatenate([(k * kdf).astype(BF16), (k * kdb).astype(BF16)], axis=1)
        kv_ref[n] = _dot_tn(k2, rv_ref[0, rows, :])
        return carry

    lax.fori_loop(0, n_chunks, phase_a, 0, unroll=2)

    def init_state(ref):
        if not has_s0:
            return jnp.zeros((W, W), F32)
        z = jnp.zeros((D_RET, D_RET), F32)
        return jnp.concatenate([jnp.concatenate([ref[0, 0], z], axis=1),
                                jnp.concatenate([z, ref[0, 1]], axis=1)], axis=0)

    def scan_f(n, S):
        sp_ref[n, 0:W, :] = S.astype(BF16)
        return cdf * S + jnp.where(blockdiag, kv_ref[n, 0:W, :], 0.0)

    def scan_b(i, S):
        n = n_chunks - 1 - i
        sp_ref[n, W:2 * W, :] = S.astype(BF16)
        return cdb * S + jnp.where(blockdiag, kv_ref[n, W:2 * W, :], 0.0)

    Sf = lax.fori_loop(0, n_chunks, scan_f, init_state(s0f_ref if has_s0 else None))
    Sb = lax.fori_loop(0, n_chunks, scan_b, init_state(s0b_ref if has_s0 else None))
    if want_state:
        for S, ref in ((Sf, sf_ref), (Sb, sb_ref)):
            ref[0, 0] = S[0:D_RET, 0:D_RET]
            ref[0, 1] = S[D_RET:W, D_RET:W]

    def phase_c(n, carry):
        rows = chunk_rows(n)
        q = rq_ref[0, rows, :]
        k = rk_ref[0, rows, :]
        v = rv_ref[0, rows, :]
        zero = jnp.zeros_like(q)
        sc0 = _dot_nt(jnp.where(lane_h1, zero, q), k) * dmat0
        sc1 = _dot_nt(jnp.where(lane_h1, q, zero), k) * dmat1
        p = jnp.concatenate([sc0.astype(BF16), sc1.astype(BF16)], axis=1)
        v_bd = jnp.concatenate([jnp.where(lane_h1, zero, v), jnp.where(lane_h1, v, zero)], axis=0)
        qf = q.astype(F32)
        q2 = jnp.concatenate([(qf * qdf).astype(BF16), (qf * qdb).astype(BF16)], axis=1)
        o = _dot(p, v_bd) + _dot(q2, sp_ref[n])
        o2 = o * o
        ss0 = jnp.sum(jnp.where(lane_h1, 0.0, o2), axis=-1, keepdims=True)
        ss1 = jnp.sum(jnp.where(lane_h1, o2, 0.0), axis=-1, keepdims=True)
        ms = jnp.where(lane_h1, ss1, ss0) * (1.0 / D_RET)
        o = o * lax.rsqrt(ms + EPS) * g * rg_ref[0, rows, :].astype(F32)
        o_ref[0, rows, :] = o.astype(BF16)
        return carry

    lax.fori_loop(0, n_chunks, phase_c, 0, unroll=2)


def _retention(rq, rk, rv, rg, dec_f, dec_b, g, s0f, s0b, want_state):
    B, L, _ = rq.shape
    has_s0 = s0f is not None
    seq = pl.BlockSpec((1, L, LANES), lambda b, hp: (b, 0, hp))
    st = pl.BlockSpec((1, 2, D_RET, D_RET), lambda b, hp: (b, hp, 0, 0))
    cst = lambda shape: pl.BlockSpec(shape, lambda b, hp: (0,) * len(shape))
    in_specs = [seq] * 4 + [cst((H_RET, LANES)), cst((H_RET, LANES)), cst((1, LANES))]
    args = [rq, rk, rv, rg, dec_f, dec_b, g]
    if has_s0:
        in_specs += [st, st]
        args += [s0f, s0b]
    out_specs = [seq]
    out_shape = [jax.ShapeDtypeStruct((B, L, D_RET_ALL), BF16)]
    if want_state:
        out_specs += [st, st]
        out_shape += [jax.ShapeDtypeStruct((B, H_RET, D_RET, D_RET), F32)] * 2
    return pl.pallas_call(
        functools.partial(_ret_kernel, has_s0, want_state, L // RET_CHUNK),
        grid=(B, H_RET // 2),
        in_specs=in_specs,
        out_specs=out_specs,
        out_shape=out_shape,
        scratch_shapes=[pltpu.VMEM((L // RET_CHUNK, 4 * D_RET, LANES), F32),
                        pltpu.VMEM((L // RET_CHUNK, 4 * D_RET, LANES), BF16)],
        compiler_params=_cparams("parallel", "parallel"),
        name="retention_s0" if has_s0 else "retention",
    )(*args)


def _pack_pairs(x):
    w = x.shape[1] // 2
    hi = lax.bitcast_convert_type(x[:, :w].astype(BF16).astype(F32), jnp.uint32)
    lo = lax.bitcast_convert_type(x[:, w:].astype(BF16).astype(F32), jnp.uint32)
    return hi | (lo >> 16)


def _unpack_pairs(p):
    hi = lax.bitcast_convert_type(p & jnp.uint32(0xFFFF0000), F32)
    lo = lax.bitcast_convert_type(p << 16, F32)
    return jnp.concatenate([hi, lo], axis=1)


D_PACKED = D_MODEL // 2


def _outproj_kernel(omla_ref, oret_ref, x_ref, mod_ref, wo_ref, n2g_ref, rwh_ref, rwl_ref, rb_ref,
                    x2_ref, h2p_ref, eid_ref, rnk_ref, wk_ref, cnt_ref, carry_ref):
    @pl.when(pl.program_id(0) == 0)
    def _():
        carry_ref[...] = jnp.zeros_like(carry_ref)

    mod = mod_ref[0]
    half = H_MLA * D_V
    a = _dot(omla_ref[...], wo_ref[:half, :]) + _dot(oret_ref[...], wo_ref[half:, :])
    x2 = x_ref[...] + mod[2:3] * a
    x2_ref[...] = x2
    h2 = _rms(x2, n2g_ref[...]) * (1.0 + mod[4:5]) + mod[3:4]
    h2p_ref[...] = _pack_pairs(h2)
    h2_hi = h2.astype(BF16)
    h2_lo = (h2 - h2_hi.astype(F32)).astype(BF16)
    logits = _dot(h2_hi, rwh_ref[...]) + _dot(h2_lo, rwh_ref[...]) + _dot(h2_hi, rwl_ref[...])
    scores = jax.nn.sigmoid(logits)
    lane = lax.broadcasted_iota(jnp.int32, scores.shape, 1)
    cand = scores + rb_ref[...]
    picked = jnp.zeros(scores.shape, jnp.bool_)
    hits = []
    for _ in range(TOP_K):
        mx = jnp.max(cand, axis=-1, keepdims=True)
        first = jnp.min(jnp.where(cand == mx, lane, LANES), axis=-1, keepdims=True)
        hit = lane == first
        hits.append((first, hit))
        picked = jnp.logical_or(picked, hit)
        cand = jnp.where(hit, -jnp.inf, cand)
    sel = jnp.where(picked, scores, 0.0)
    gates = sel * (ROUTED_SCALE / jnp.sum(sel, axis=-1, keepdims=True))

    tm = scores.shape[0]
    onehot = jnp.where(picked, 1.0, 0.0)
    earlier = (lax.broadcasted_iota(jnp.int32, (tm, tm), 0) > lax.broadcasted_iota(jnp.int32, (tm, tm), 1))
    rank = _dot(jnp.where(earlier, 1.0, 0.0).astype(BF16), onehot.astype(BF16)) + carry_ref[...]
    carry_ref[...] += jnp.sum(onehot, axis=0, keepdims=True)
    cnt_ref[...] = carry_ref[...].astype(jnp.int32)

    eid = jnp.zeros(scores.shape, jnp.int32)
    rnk = jnp.zeros(scores.shape, F32)
    wk = jnp.zeros(scores.shape, F32)
    for k, (first, hit) in enumerate(hits):
        slot = lane == k
        eid = jnp.where(slot, first, eid)
        rnk = jnp.where(slot, jnp.sum(jnp.where(hit, rank, 0.0), axis=-1, keepdims=True), rnk)
        wk = jnp.where(slot, jnp.sum(jnp.where(hit, gates, 0.0), axis=-1, keepdims=True), wk)
    eid_ref[...] = eid
    rnk_ref[...] = rnk.astype(jnp.int32)
    wk_ref[...] = wk


def _outproj(o_mla, o_ret, x, mod, mod_row0, tokens_per_row, wts, tm):
    T = x.shape[0]
    bpr = tokens_per_row // tm
    tok = lambda w: pl.BlockSpec((tm, w), lambda i: (i, 0))
    return pl.pallas_call(
        _outproj_kernel,
        grid=(T // tm,),
        in_specs=[tok(H_MLA * D_V), tok(D_RET_ALL), tok(D_MODEL),
                  pl.BlockSpec((1, 6, D_MODEL), lambda i: (mod_row0 + i // bpr, 0, 0)),
                  _const_spec((D_MODEL, D_MODEL)), _const_spec((1, D_MODEL)),
                  _const_spec((D_MODEL, LANES)), _const_spec((D_MODEL, LANES)), _const_spec((1, LANES))],
        out_specs=[tok(D_MODEL), tok(D_PACKED), tok(LANES), tok(LANES), tok(LANES), _const_spec((1, LANES))],
        out_shape=[jax.ShapeDtypeStruct((T, D_MODEL), F32),
                   jax.ShapeDtypeStruct((T, D_PACKED), jnp.uint32),
                   jax.ShapeDtypeStruct((T, LANES), jnp.int32),
                   jax.ShapeDtypeStruct((T, LANES), jnp.int32),
                   jax.ShapeDtypeStruct((T, LANES), F32),
                   jax.ShapeDtypeStruct((1, LANES), jnp.int32)],
        scratch_shapes=[pltpu.VMEM((1, LANES), F32)],
        compiler_params=_cparams("arbitrary"),
        name="outproj_router",
    )(o_mla, o_ret, x, mod, wts["w_o"], wts["norm2_g"], wts["router_hi"], wts["router_lo"],
      wts["router_bias"])


def _plan_kernel(eid_ref, rnk_ref, tab_ref, pos_ref):
    eid = eid_ref[...]
    rnk = rnk_ref[...]
    tab = tab_ref[...]
    lane = lax.broadcasted_iota(jnp.int32, eid.shape, 1)
    pos = jnp.zeros(eid.shape, jnp.int32)
    for k in range(TOP_K):
        onehot = jnp.where(lane == eid[:, k:k + 1], 1.0, 0.0).astype(BF16)
        d = _dot(onehot, tab)
        start = d[:, 0:1] * 65536.0 + d[:, 1:2] * 256.0 + d[:, 2:3]
        pos = jnp.where(lane == k, start.astype(jnp.int32) + rnk, pos)
    pos_ref[...] = pos


def _plan(eid, rnk, starts, tm):
    T = eid.shape[0]
    digits = jnp.stack([starts >> 16, (starts >> 8) & 255, starts & 255], axis=1).astype(BF16)
    tab = jnp.zeros((LANES, LANES), BF16).at[:N_EXPERTS, :3].set(digits)
    tok = pl.BlockSpec((tm, LANES), lambda i: (i, 0))
    return pl.pallas_call(
        _plan_kernel,
        grid=(T // tm,),
        in_specs=[tok, tok, _const_spec((LANES, LANES))],
        out_specs=tok,
        out_shape=jax.ShapeDtypeStruct((T, LANES), jnp.int32),
        compiler_params=_cparams("parallel"),
        name="moe_plan",
    )(eid, rnk, tab)


SC_WORKERS = 32
SC_CHUNK = 64


def _sc_worker_id():
    return lax.axis_index("s") * 2 + lax.axis_index("c")


def _sc_mesh():
    return plsc.VectorSubcoreMesh(core_axis_name="c", subcore_axis_name="s")


def _sc_gather_rows(table, idx):
    N = idx.shape[0]
    W = table.shape[1]
    per_w = N // SC_WORKERS
    n_chunks = per_w // SC_CHUNK
    assert N == SC_WORKERS * n_chunks * SC_CHUNK and n_chunks % 2 == 0

    @functools.partial(
        pl.kernel, mesh=_sc_mesh(), out_type=jax.ShapeDtypeStruct((N, W), table.dtype),
        scratch_types=[pltpu.VMEM((n_chunks, SC_CHUNK), jnp.int32),
                       pltpu.VMEM((2, SC_CHUNK, W), table.dtype),
                       pltpu.SemaphoreType.DMA((2,)), pltpu.SemaphoreType.DMA((2,))],
        name="sc_gather_rows")
    def k(table_hbm, idx_hbm, out_hbm, idx_v, rows_v, gsem, wsem):
        wid = _sc_worker_id()
        base = wid * per_w
        pltpu.sync_copy(idx_hbm.at[wid], idx_v)

        def gather(j, b):
            return pltpu.make_async_copy(table_hbm.at[idx_v.at[j]], rows_v.at[b], gsem.at[b])

        def writeback(j, b):
            off = pl.multiple_of(base + j * SC_CHUNK, SC_CHUNK)
            return pltpu.make_async_copy(rows_v.at[b], out_hbm.at[pl.ds(off, SC_CHUNK)], wsem.at[b])

        gather(0, 0).start()

        @pl.loop(0, n_chunks, step=2)
        def _(j):
            for b in range(2):
                jj = j + b

                @pl.when(jj + 1 < n_chunks)
                def _():
                    @pl.when(jj >= 1)
                    def _():
                        writeback(jj - 1, 1 - b).wait()
                    gather(jj + 1, 1 - b).start()

                gather(jj, b).wait()
                writeback(jj, b).start()

        writeback(n_chunks - 2, 0).wait()
        writeback(n_chunks - 1, 1).wait()

    return k(table, idx.reshape(SC_WORKERS, n_chunks, SC_CHUNK))


def _sc_dispatch_rows(rows, pos, n_out):
    T, W = rows.shape
    K = pos.shape[1]
    per_w = T // SC_WORKERS
    n_chunks = per_w // SC_CHUNK
    assert T == SC_WORKERS * n_chunks * SC_CHUNK and n_chunks % 2 == 0
    idx = pos.reshape(SC_WORKERS, n_chunks, SC_CHUNK, K).transpose(0, 1, 3, 2)

    @functools.partial(
        pl.kernel, mesh=_sc_mesh(), out_type=jax.ShapeDtypeStruct((n_out, W), rows.dtype),
        scratch_types=[pltpu.VMEM((n_chunks, K, SC_CHUNK), jnp.int32),
                       pltpu.VMEM((2, SC_CHUNK, W), rows.dtype),
                       pltpu.SemaphoreType.DMA((2,)), pltpu.SemaphoreType.DMA((2,))],
        name="sc_dispatch_rows")
    def k(rows_hbm, idx_hbm, out_hbm, idx_v, rows_v, lsem, ssem):
        wid = _sc_worker_id()
        base = wid * per_w
        pltpu.sync_copy(idx_hbm.at[wid], idx_v)

        def load(j, b):
            off = pl.multiple_of(base + j * SC_CHUNK, SC_CHUNK)
            return pltpu.make_async_copy(rows_hbm.at[pl.ds(off, SC_CHUNK)], rows_v.at[b], lsem.at[b])

        def scatter(j, b, kk):
            return pltpu.make_async_copy(rows_v.at[b], out_hbm.at[idx_v.at[j, kk]], ssem.at[b])

        load(0, 0).start()

        @pl.loop(0, n_chunks, step=2)
        def _(j):
            for b in range(2):
                jj = j + b

                @pl.when(jj + 1 < n_chunks)
                def _():
                    @pl.when(jj >= 1)
                    def _():
                        for kk in range(K):
                            scatter(jj - 1, 1 - b, kk).wait()
                    load(jj + 1, 1 - b).start()

                load(jj, b).wait()
                for kk in range(K):
                    scatter(jj, b, kk).start()

        for kk in range(K):
            scatter(n_chunks - 2, 0, kk).wait()
        for kk in range(K):
            scatter(n_chunks - 1, 1, kk).wait()

    return k(rows, idx)


ROW_TILE = 512


def _experts_kernel(te_ref, tv_ref, xs_ref, wg_ref, wu_ref, wd_ref, ys_ref):
    del te_ref
    valid = tv_ref[pl.program_id(0)]

    @pl.when(valid > 0)
    def _():
        row = lax.broadcasted_iota(jnp.int32, (ROW_TILE, 1), 0)
        p = jnp.where(row < valid, xs_ref[...], jnp.uint32(0))
        x = _unpack_pairs(p).astype(BF16)
        act = _silu(_dot(x, wg_ref[0])) * _dot(x, wu_ref[0])
        ys_ref[...] = _pack_pairs(_dot(act.astype(BF16), wd_ref[0]))

    @pl.when(valid <= 0)
    def _():
        ys_ref[...] = jnp.zeros_like(ys_ref)


def _experts(xs, tile_expert, tile_valid, wts):
    n_tiles = tile_expert.shape[0]
    ex = lambda a, b: pl.BlockSpec((1, a, b), lambda i, te, tv: (te[i], 0, 0))
    row = pl.BlockSpec((ROW_TILE, D_PACKED), lambda i, te, tv: (i, 0))
    return pl.pallas_call(
        _experts_kernel,
        grid_spec=pltpu.PrefetchScalarGridSpec(
            num_scalar_prefetch=2, grid=(n_tiles,),
            in_specs=[row, ex(D_MODEL, D_EXPERT), ex(D_MODEL, D_EXPERT), ex(D_EXPERT, D_MODEL)],
            out_specs=row),
        out_shape=jax.ShapeDtypeStruct(xs.shape, jnp.uint32),
        compiler_params=_cparams("arbitrary"),
        name="experts",
    )(tile_expert, tile_valid, xs, wts["exp_gate"], wts["exp_up"], wts["exp_down"])


def _combine_kernel(h2p_ref, ys_ref, wk_ref, x2_ref, mod_ref, shg_ref, shu_ref, shd_ref, y_ref):
    t = _unpack_pairs(h2p_ref[...]).astype(BF16)
    act = _silu(_dot(t, shg_ref[...])) * _dot(t, shu_ref[...])
    acc = _dot(act.astype(BF16), shd_ref[...])
    wk = wk_ref[...]
    for k in range(TOP_K):
        acc = acc + wk[:, k:k + 1] * _unpack_pairs(ys_ref[k])
    y_ref[...] = x2_ref[...] + mod_ref[0][5:6] * acc


def _combine(h2p, ys, wk, x2, mod, mod_row0, tokens_per_row, wts, tm):
    T = h2p.shape[0]
    bpr = tokens_per_row // tm
    tok = lambda w: pl.BlockSpec((tm, w), lambda i: (i, 0))
    return pl.pallas_call(
        _combine_kernel,
        grid=(T // tm,),
        in_specs=[tok(D_PACKED), pl.BlockSpec((TOP_K, tm, D_PACKED), lambda i: (0, i, 0)), tok(LANES),
                  tok(D_MODEL), pl.BlockSpec((1, 6, D_MODEL), lambda i: (mod_row0 + i // bpr, 0, 0)),
                  _const_spec((D_MODEL, D_SHARED)), _const_spec((D_MODEL, D_SHARED)),
                  _const_spec((D_SHARED, D_MODEL))],
        out_specs=tok(D_MODEL),
        out_shape=jax.ShapeDtypeStruct((T, D_MODEL), F32),
        compiler_params=_cparams("parallel"),
        name="moe_combine",
    )(h2p, ys, wk, x2, mod, wts["sh_gate"], wts["sh_up"], wts["sh_down"])


def _moe(h2p, eid, rnk, wk, counts, x2, mod, mod_row0, tokens_per_row, wts, tm):
    T = h2p.shape[0]
    n_tiles = (TOP_K * T) // ROW_TILE + N_EXPERTS
    counts = counts[0, :N_EXPERTS]
    padded = ((counts + ROW_TILE - 1) // ROW_TILE) * ROW_TILE
    ends = jnp.cumsum(padded)
    starts = ends - padded
    pos = _plan(eid, rnk, starts, tm)[:, :TOP_K]
    tile_start = jnp.arange(n_tiles, dtype=jnp.int32) * ROW_TILE
    tile_expert = jnp.minimum(jnp.sum(tile_start[:, None] >= ends[None, :], axis=1), N_EXPERTS - 1).astype(jnp.int32)
    tile_valid = jnp.clip(counts[tile_expert] - (tile_start - starts[tile_expert]), 0, ROW_TILE).astype(jnp.int32)

    xs = _sc_dispatch_rows(h2p, pos.astype(jnp.int32), n_tiles * ROW_TILE)
    ys = _experts(xs, tile_expert, tile_valid, wts)
    ysel = _sc_gather_rows(ys, pos.T.reshape(TOP_K * T).astype(jnp.int32)).reshape(TOP_K, T, D_PACKED)
    return _combine(h2p, ysel, wk, x2, mod, mod_row0, tokens_per_row, wts, tm)


def _pad_heads(w, d_used):
    k = w.shape[0]
    w = w.reshape(k, H_MLA, d_used)
    return jnp.pad(w, ((0, 0), (0, 0), (0, HEAD_SLOT - d_used))).reshape(k, H_MLA * HEAD_SLOT)


def _lane_pad(v, lane0=0):
    return jnp.pad(v, (lane0, LANES - lane0 - v.shape[0])).reshape(1, LANES)


def _prep_weights(l, w_in, norm1_g, mla_q_norm_g, w_uq, mla_kv_norm_g, w_ukv, mla_q_gain, mla_k_gain,
                  w_o, norm2_g, router_w, router_bias, exp_w_gate, exp_w_up, exp_w_down,
                  sh_w_gate, sh_w_up, sh_w_down):
    w = w_in[l]
    a, b, c = Q_RANK, Q_RANK + KV_RANK, Q_RANK + KV_RANK + D_ROPE
    w_kr = jnp.pad(w[:, b:c], ((0, 0), (D_NOPE, LANES - D_NOPE - D_ROPE)))
    w1 = jnp.concatenate([w[:, :b], w_kr, w[:, c:]], axis=1).astype(BF16)
    ukv = w_ukv[l].reshape(KV_RANK, H_MLA, D_NOPE + D_V)
    w_k = _pad_heads(ukv[:, :, :D_NOPE].reshape(KV_RANK, H_MLA * D_NOPE), D_NOPE)
    w_v = _pad_heads(ukv[:, :, D_NOPE:].reshape(KV_RANK, H_MLA * D_V), D_V)
    rw = jnp.pad(router_w[l], ((0, 0), (0, LANES - N_EXPERTS)))
    rw_hi = rw.astype(BF16)
    rw_lo = (rw - rw_hi.astype(F32)).astype(BF16)
    rb = jnp.concatenate([router_bias[l].astype(F32), jnp.full((LANES - N_EXPERTS,), -jnp.inf, F32)])
    return dict(
        norm1_g=norm1_g[l].reshape(1, D_MODEL), w1=w1,
        q_norm_g=mla_q_norm_g[l].reshape(1, Q_RANK), w_uq=_pad_heads(w_uq[l], D_QK).astype(BF16),
        kv_norm_g=mla_kv_norm_g[l].reshape(1, KV_RANK),
        w_kv=jnp.concatenate([w_k, w_v], axis=1).astype(BF16),
        q_gain=_lane_pad(mla_q_gain[l]), k_gain=_lane_pad(mla_k_gain[l]),
        w_o=w_o[l].astype(BF16), norm2_g=norm2_g[l].reshape(1, D_MODEL),
        router_hi=rw_hi, router_lo=rw_lo, router_bias=rb.reshape(1, LANES),
        exp_gate=exp_w_gate[l].astype(BF16), exp_up=exp_w_up[l].astype(BF16),
        exp_down=exp_w_down[l].astype(BF16),
        sh_gate=sh_w_gate[l].astype(BF16), sh_up=sh_w_up[l].astype(BF16),
        sh_down=sh_w_down[l].astype(BF16))


def _pick(n, prefs):
    for p in prefs:
        if n % p == 0:
            return p
    return n


def _trunk_layer(x, mod, mod_row0, wts, ret_wts, ctx):
    B, L, D = x.shape
    T = B * L
    tokens_per_row = L if ctx is not None else T
    xt = x.reshape(T, D)
    rope_tabs = None
    if ctx is not None:
        qk_tabs, _, ret_tabs, _ = _rope_tables(L)
        rope_tabs = (qk_tabs, ret_tabs)
    tm = _pick(L, (512, 256, 128))
    ckv_n, krp, q, k, v, rq, rk, rv, rg = _inproj(xt, mod, mod_row0, tokens_per_row, wts, rope_tabs, tm)

    seq = lambda a: a.reshape(B, L, a.shape[-1])
    kc = vc = s0f = s0b = None
    if ctx is not None:
        ckv_c, kr_c, s0f, s0b = ctx
        Lc = ckv_c.shape[1]
        krp_c = jnp.pad(kr_c, ((0, 0), (0, 0), (D_NOPE, LANES - D_NOPE - D_ROPE)))
        kc, vc = _ctx_kv(ckv_c.reshape(B * Lc, KV_RANK), krp_c.reshape(B * Lc, LANES), wts,
                         _pick(B * Lc, (512, 256, 128)))
        kc, vc = kc.reshape(B, Lc, -1), vc.reshape(B, Lc, -1)
    o_mla = _attention(seq(q), seq(k), seq(v), kc, vc, _pick(L, (512, 256, 128)))

    dec_f, dec_b, ret_g = ret_wts
    want_state = ctx is None
    ret = _retention(seq(rq), seq(rk), seq(rv), seq(rg), dec_f, dec_b, ret_g, s0f, s0b, want_state)
    o_ret = ret[0]

    x2, h2p, eid, rnk, wk, counts = _outproj(o_mla.reshape(T, -1), o_ret.reshape(T, -1), xt, mod, mod_row0,
                                             tokens_per_row, wts, tm)
    y = _moe(h2p, eid, rnk, wk, counts, x2, mod, mod_row0, tokens_per_row, wts, tm)
    new = None
    if want_state:
        new = (ckv_n.reshape(B, L, KV_RANK), krp[:, D_NOPE:D_NOPE + D_ROPE].reshape(B, L, D_ROPE),
               ret[1], ret[2])
    return y.reshape(B, L, D), new


def kernel(x_prompt, x_sample, cache_mla_ckv, cache_mla_krope, state_ret_fwd, state_ret_bwd, c, c_ctx,
           w_ada, b_ada, norm1_g, w_in, mla_q_norm_g, w_uq, mla_kv_norm_g, w_ukv, mla_q_gain, mla_k_gain,
           ret_decay_fwd, ret_decay_bwd, ret_norm_g, w_o, norm2_g, router_w, router_bias, exp_w_gate,
           exp_w_up, exp_w_down, sh_w_gate, sh_w_up, sh_w_down):
    depth = w_ada.shape[0]
    n_dec = c.shape[0]
    assert 1 + n_dec <= MOD_ROWS
    cond = jnp.concatenate([c_ctx[None], c, jnp.zeros((MOD_ROWS - 1 - n_dec, D_MODEL), F32)], axis=0)

    y_prompt, y_sample = x_prompt, x_sample
    ckv_l, kr_l, sf_l, sb_l = [], [], [], []
    for l in range(depth):
        wts = _prep_weights(l, w_in, norm1_g, mla_q_norm_g, w_uq, mla_kv_norm_g, w_ukv, mla_q_gain,
                            mla_k_gain, w_o, norm2_g, router_w, router_bias, exp_w_gate, exp_w_up,
                            exp_w_down, sh_w_gate, sh_w_up, sh_w_down)
        ret_wts = (jnp.broadcast_to(ret_decay_fwd[l].astype(F32)[:, None], (H_RET, LANES)),
                   jnp.broadcast_to(ret_decay_bwd[l].astype(F32)[:, None], (H_RET, LANES)),
                   jnp.tile(ret_norm_g[l].reshape(1, D_RET), (1, LANES // D_RET)))
        mod = _adaln(cond, w_ada[l], b_ada[l]).reshape(MOD_ROWS, 6, D_MODEL)
        y_prompt, new = _trunk_layer(y_prompt, mod, 0, wts, ret_wts, None)
        ckv_l.append(new[0]); kr_l.append(new[1]); sf_l.append(new[2]); sb_l.append(new[3])
        ctx = (cache_mla_ckv[:, l], cache_mla_krope[:, l], state_ret_fwd[:, l], state_ret_bwd[:, l])
        y_sample, _ = _trunk_layer(y_sample, mod, 1, wts, ret_wts, ctx)

    return (y_prompt, y_sample, jnp.stack(ckv_l, axis=1), jnp.stack(kr_l, axis=1),
            jnp.stack(sf_l, axis=1), jnp.stack(sb_l, axis=1))
```

```python
import functools
import math

import numpy as np
import jax
import jax.numpy as jnp
from jax import lax
from jax.experimental import pallas as pl
from jax.experimental.pallas import tpu as pltpu
from jax.experimental.pallas import tpu_sc as plsc

F32 = jnp.float32
BF16 = jnp.bfloat16

D_MODEL = 1024
GRID_W = 64
H_MLA = 8
D_NOPE = 64
D_ROPE = 32
D_QK = D_NOPE + D_ROPE
D_V = 64
Q_RANK = 256
KV_RANK = 128
H_RET = 8
D_RET = 64
RET_CHUNK = 128
D_RET_ALL = H_RET * D_RET
N_EXPERTS = 64
TOP_K = 6
D_EXPERT = 256
D_SHARED = 256
ROUTED_SCALE = 2.5
ROPE_BASE = 10000.0
EPS = 1e-6
LOG2_E = math.log2(math.e)

LANES = 128
HEAD_SLOT = LANES
D_IN_PAD = Q_RANK + KV_RANK + LANES + 4 * D_RET_ALL
MOD_ROWS = 16
VMEM_LIMIT = 56 * 1024 * 1024


def _cparams(*sem):
    return pltpu.CompilerParams(dimension_semantics=sem, vmem_limit_bytes=VMEM_LIMIT)


def _dot(a, b):
    return jnp.dot(a, b, preferred_element_type=F32)


def _dot_nt(a, b):
    return lax.dot_general(a, b, (((1,), (1,)), ((), ())), preferred_element_type=F32)


def _dot_tn(a, b):
    return lax.dot_general(a, b, (((0,), (0,)), ((), ())), preferred_element_type=F32)


def _rms(x, g):
    return x * lax.rsqrt(jnp.mean(x * x, axis=-1, keepdims=True) + EPS) * g


def _silu(x):
    return x * jax.nn.sigmoid(x)


def _adaln_kernel(c_ref, w_ref, b_ref, o_ref):
    s = _silu(c_ref[...])
    o_ref[...] = _dot(s.astype(BF16), w_ref[...].astype(BF16)) + b_ref[...]


def _adaln(cond, w_ada, b_ada):
    n_out = w_ada.shape[1]
    bn = 1536
    return pl.pallas_call(
        _adaln_kernel,
        grid=(n_out // bn,),
        in_specs=[pl.BlockSpec((MOD_ROWS, D_MODEL), lambda j: (0, 0)),
                  pl.BlockSpec((D_MODEL, bn), lambda j: (0, j)),
                  pl.BlockSpec((1, bn), lambda j: (0, j))],
        out_specs=pl.BlockSpec((MOD_ROWS, bn), lambda j: (0, j)),
        out_shape=jax.ShapeDtypeStruct((MOD_ROWS, n_out), F32),
        compiler_params=_cparams("arbitrary"),
        name="adaln",
    )(cond, w_ada, b_ada.reshape(1, n_out))


def _swap_partner(n_dims, half_pair):
    j = np.arange(n_dims)
    return np.where((j % (2 * half_pair)) < half_pair, j + half_pair, j - half_pair)


def _rope_tables(n_tokens):
    t = np.arange(n_tokens)
    row = (t // GRID_W).astype(np.float64)
    col = (t % GRID_W).astype(np.float64)

    def axis_tables(width, lane0, period, fill):
        p = width // 2
        inv = 1.0 / (ROPE_BASE ** (np.arange(p, dtype=np.float64) / p))
        cos = np.full((n_tokens, LANES), fill)
        sin = np.zeros((n_tokens, LANES))
        starts = range(lane0, LANES, period) if period else (lane0,)
        for s0 in starts:
            for base, pos in ((s0, row), (s0 + width, col)):
                ang = pos[:, None] * inv[None, :]
                c, sn = np.cos(ang), np.sin(ang)
                cos[:, base:base + p] = c
                cos[:, base + p:base + 2 * p] = c
                sin[:, base:base + p] = -sn
                sin[:, base + p:base + 2 * p] = sn
        return cos, sin

    qk_cos, qk_sin = axis_tables(D_ROPE // 2, D_NOPE, 0, 1.0)
    ret_cos, ret_sin = axis_tables(D_RET // 2, 0, D_RET, 1.0)
    as_f32 = lambda *a: tuple(jnp.asarray(x, F32) for x in a)
    return as_f32(qk_cos, qk_sin), as_f32(ret_cos, ret_sin)


def _head_sums(x):
    r = lax.broadcasted_iota(jnp.int32, (2 * HEAD_SLOT, 2 * HEAD_SLOT), 0) // HEAD_SLOT
    c = lax.broadcasted_iota(jnp.int32, (2 * HEAD_SLOT, 2 * HEAD_SLOT), 1) // HEAD_SLOT
    ones_bd = jnp.where(r == c, 1.0, 0.0).astype(BF16)
    xb = x.astype(BF16)
    w = 2 * HEAD_SLOT
    return jnp.concatenate([_dot(xb[:, g * w:(g + 1) * w], ones_bd) for g in range(x.shape[1] // w)], axis=1)


def _tile_heads(v):
    return jnp.concatenate([v] * H_MLA, axis=1)


def _kv_heads(ckv_n, krp, krp_sw, wkv_ref, kgain, kgain_sw, tabs, k_ref, v_ref):
    kvp = _dot(ckv_n.astype(BF16), wkv_ref[...])
    kn = kvp[:, :H_MLA * HEAD_SLOT]
    ms = _head_sums(kn * kn + _tile_heads(krp * krp)) * (1.0 / D_QK)
    r = lax.rsqrt(ms + EPS)
    k = (kn + _tile_heads(krp)) * r * _tile_heads(kgain)
    if tabs is not None:
        cos, sin = tabs
        k = k * _tile_heads(cos) + r * _tile_heads(krp_sw * kgain_sw * sin)
    k_ref[...] = k.astype(BF16)
    v_lane = lax.broadcasted_iota(jnp.int32, (1, H_MLA * HEAD_SLOT), 1) % HEAD_SLOT
    v_ref[...] = (kvp[:, H_MLA * HEAD_SLOT:] + jnp.where(v_lane == D_V, 1.0, 0.0)).astype(BF16)


def _inproj_kernel(rope, *refs):
    (x_ref, mod_ref, n1g_ref, w1_ref, qng_ref, wuq_ref, kvg_ref, wkv_ref, qgain_ref,
     kgain_ref) = refs[:10]
    refs = refs[10:]
    if rope:
        qk_tabs = tuple(r[...] for r in refs[:2])
        ret_cos, ret_sin = (r[...] for r in refs[2:4])
        refs = refs[4:]
    else:
        qk_tabs = None
    ckvn_ref, krp_ref, q_ref, k_ref, v_ref, rq_ref, rk_ref, rv_ref, rg_ref = refs

    mod = mod_ref[0]
    h = _rms(x_ref[...], n1g_ref[...]) * (1.0 + mod[1:2]) + mod[0:1]
    z = _dot(h.astype(BF16), w1_ref[...])
    o = 0
    cq = z[:, o:o + Q_RANK]; o += Q_RANK
    ckv = z[:, o:o + KV_RANK]; o += KV_RANK
    krp = z[:, o:o + LANES]; o += LANES
    rq = z[:, o:o + D_RET_ALL]; o += D_RET_ALL
    rk = z[:, o:o + D_RET_ALL]; o += D_RET_ALL
    rv = z[:, o:o + D_RET_ALL]; o += D_RET_ALL
    rg = z[:, o:o + D_RET_ALL]; o += D_RET_ALL
    krp_sw = None
    if rope:
        krp_sw = z[:, o:o + LANES]; o += LANES
        rq_sw = z[:, o:o + D_RET_ALL]; o += D_RET_ALL
        rk_sw = z[:, o:o + D_RET_ALL]

    ckv_n = _rms(ckv, kvg_ref[...])
    ckvn_ref[...] = ckv_n
    krp_ref[...] = krp

    qn = _rms(cq, qng_ref[...]).astype(BF16)
    scale = D_QK ** -0.5 * LOG2_E
    q = _dot(qn, wuq_ref[:, :H_MLA * HEAD_SLOT])
    r = lax.rsqrt(_head_sums(q * q) * (1.0 / D_QK) + EPS)
    q = q * r * _tile_heads(qgain_ref[0:1] * scale)
    if rope:
        cos, sin = qk_tabs
        q_sw = _dot(qn, wuq_ref[:, H_MLA * HEAD_SLOT:])
        q = q * _tile_heads(cos) + q_sw * r * _tile_heads(qgain_ref[1:2] * scale * sin)
    q_ref[...] = q.astype(BF16)

    _kv_heads(ckv_n, krp, krp_sw, wkv_ref, kgain_ref[0:1], kgain_ref[1:2], qk_tabs, k_ref, v_ref)

    for t in range(D_RET_ALL // LANES):
        sl = slice(t * LANES, (t + 1) * LANES)
        rq_t, rk_t = rq[:, sl], rk[:, sl]
        if rope:
            rq_t = rq_t * ret_cos + rq_sw[:, sl] * ret_sin
            rk_t = rk_t * ret_cos + rk_sw[:, sl] * ret_sin
        rq_ref[:, sl] = rq_t.astype(BF16)
        rk_ref[:, sl] = (rk_t * (D_RET ** -0.5)).astype(BF16)
    rv_ref[...] = rv.astype(BF16)
    rg_ref[...] = _silu(rg).astype(BF16)


def _const_spec(shape):
    return pl.BlockSpec(shape, lambda i: (0,) * len(shape))


def _inproj(x, mod, mod_row0, tokens_per_row, wts, rope_tabs, tm):
    T = x.shape[0]
    bpr = tokens_per_row // tm
    rope = rope_tabs is not None
    tok = lambda w: pl.BlockSpec((tm, w), lambda i: (i, 0))
    in_specs = [tok(D_MODEL),
                pl.BlockSpec((1, 6, D_MODEL), lambda i: (mod_row0 + i // bpr, 0, 0)),
                _const_spec((1, D_MODEL)),
                _const_spec((D_MODEL, D_IN_PAD + (LANES + 2 * D_RET_ALL if rope else 0))),
                _const_spec((1, Q_RANK)),
                _const_spec((Q_RANK, 2 * H_MLA * HEAD_SLOT)),
                _const_spec((1, KV_RANK)),
                _const_spec((KV_RANK, 2 * H_MLA * HEAD_SLOT)),
                _const_spec((2, LANES)),
                _const_spec((2, LANES))]
    args = [x, mod, wts["norm1_g"], wts["w1_rope" if rope else "w1"], wts["q_norm_g"], wts["w_uq"],
            wts["kv_norm_g"], wts["w_kv"], wts["q_gain"], wts["k_gain"]]
    if rope:
        qk_tabs, ret_tabs = rope_tabs
        nb = qk_tabs[0].shape[0] // tm
        in_specs += [pl.BlockSpec((tm, LANES), lambda i: (i % nb, 0))] * 4
        args += list(qk_tabs) + list(ret_tabs)
    widths = [(KV_RANK, F32), (LANES, F32), (H_MLA * HEAD_SLOT, BF16), (H_MLA * HEAD_SLOT, BF16),
              (H_MLA * HEAD_SLOT, BF16)] + [(D_RET_ALL, BF16)] * 4
    return pl.pallas_call(
        functools.partial(_inproj_kernel, rope),
        grid=(T // tm,),
        in_specs=in_specs,
        out_specs=[tok(w) for w, _ in widths],
        out_shape=[jax.ShapeDtypeStruct((T, w), dt) for w, dt in widths],
        compiler_params=_cparams("parallel"),
        name="inproj_rope" if rope else "inproj",
    )(*args)


def _ctx_kv_kernel(ckv_ref, krp_ref, wkv_ref, kgain_ref, k_ref, v_ref):
    _kv_heads(ckv_ref[...], krp_ref[...], None, wkv_ref, kgain_ref[0:1], None, None, k_ref, v_ref)


def _ctx_kv(ckv, krp, wts, tm):
    T = ckv.shape[0]
    tok = lambda w: pl.BlockSpec((tm, w), lambda i: (i, 0))
    return pl.pallas_call(
        _ctx_kv_kernel,
        grid=(T // tm,),
        in_specs=[tok(KV_RANK), tok(LANES),
                  _const_spec((KV_RANK, 2 * H_MLA * HEAD_SLOT)), _const_spec((2, LANES))],
        out_specs=[tok(H_MLA * HEAD_SLOT), tok(H_MLA * HEAD_SLOT)],
        out_shape=[jax.ShapeDtypeStruct((T, H_MLA * HEAD_SLOT), BF16),
                   jax.ShapeDtypeStruct((T, H_MLA * HEAD_SLOT), BF16)],
        compiler_params=_cparams("parallel"),
        name="ctx_kv",
    )(ckv, krp, wts["w_kv"], wts["k_gain"])


ATTN_KEY_CHUNK = 512


def _attn_kernel(has_ctx, *refs):
    if has_ctx:
        q_ref, k_ref, v_ref, kc_ref, vc_ref, o_ref = refs
    else:
        q_ref, k_ref, v_ref, o_ref = refs
    sources = [(k_ref, v_ref)] + ([(kc_ref, vc_ref)] if has_ctx else [])
    chunks = []
    for kr, vr in sources:
        n = kr.shape[1]
        kc = _pick(n, (ATTN_KEY_CHUNK, 256, 128))
        chunks += [(kr, vr, slice(c * kc, (c + 1) * kc)) for c in range(n // kc)]
    tq = q_ref.shape[1]
    outs = []
    for j in range(2):
        hs = slice(j * HEAD_SLOT, (j + 1) * HEAD_SLOT)
        q = q_ref[0, :, hs]
        m = jnp.full((tq, 1), -jnp.inf, F32)
        acc = jnp.zeros((tq, HEAD_SLOT), F32)
        for kr, vr, rows in chunks:
            s = _dot_nt(q, kr[0, rows, hs])
            m_new = jnp.maximum(m, jnp.max(s, axis=-1, keepdims=True))
            p = jnp.exp2(s - m_new).astype(BF16)
            acc = acc * jnp.exp2(m - m_new) + _dot(p, vr[0, rows, hs])
            m = m_new
        outs.append(acc[:, :D_V] / acc[:, D_V:D_V + 1])
    o_ref[0] = jnp.concatenate(outs, axis=-1).astype(BF16)


def _attention(q, k, v, kc, vc, tq):
    B, L, _ = q.shape
    has_ctx = kc is not None
    in_specs = [pl.BlockSpec((1, tq, 2 * HEAD_SLOT), lambda b, hp, i: (b, i, hp)),
                pl.BlockSpec((1, L, 2 * HEAD_SLOT), lambda b, hp, i: (b, 0, hp)),
                pl.BlockSpec((1, L, 2 * HEAD_SLOT), lambda b, hp, i: (b, 0, hp))]
    args = [q, k, v]
    if has_ctx:
        Lc = kc.shape[1]
        in_specs += [pl.BlockSpec((1, Lc, 2 * HEAD_SLOT), lambda b, hp, i: (b, 0, hp)),
                     pl.BlockSpec((1, Lc, 2 * HEAD_SLOT), lambda b, hp, i: (b, 0, hp))]
        args += [kc, vc]
    return pl.pallas_call(
        functools.partial(_attn_kernel, has_ctx),
        grid=(B, H_MLA // 2, L // tq),
        in_specs=in_specs,
        out_specs=pl.BlockSpec((1, tq, 2 * D_V), lambda b, hp, i: (b, i, hp)),
        out_shape=jax.ShapeDtypeStruct((B, L, H_MLA * D_V), BF16),
        compiler_params=_cparams("parallel", "parallel", "arbitrary"),
        name="attn_ctx" if has_ctx else "attn",
    )(*args)


def _ret_kernel(has_s0, want_state, n_chunks, *refs):
    rq_ref, rk_ref, rv_ref, rg_ref, decf_ref, decb_ref, g_ref = refs[:7]
    refs = refs[7:]
    if has_s0:
        s0f_ref, s0b_ref = refs[:2]
        refs = refs[2:]
    o_ref = refs[0]
    refs = refs[1:]
    if want_state:
        sf_ref, sb_ref = refs[:2]
        refs = refs[2:]
    kv_ref, sp_ref = refs

    C = RET_CHUNK
    W = 2 * D_RET
    hp = pl.program_id(1)
    lane = lax.broadcasted_iota(jnp.int32, (1, W), 1)
    lane_h1 = lane >= D_RET
    row_h1 = lax.broadcasted_iota(jnp.int32, (W, 1), 0) >= D_RET
    blockdiag = row_h1 == lane_h1
    pos = lax.broadcasted_iota(jnp.int32, (C, 1), 0).astype(F32)
    diff = (lax.broadcasted_iota(jnp.int32, (C, C), 0)
            - lax.broadcasted_iota(jnp.int32, (C, C), 1)).astype(F32)

    lg = {}
    for name, ref in (("f", decf_ref), ("b", decb_ref)):
        a = -jnp.exp(ref[pl.ds(2 * hp, 1), :])
        b = -jnp.exp(ref[pl.ds(2 * hp + 1, 1), :])
        lg[name] = (a[:, :1], b[:, :1], jnp.where(lane_h1, b, a))
    lgf0, lgf1, lgf = lg["f"]
    lgb0, lgb1, lgb = lg["b"]
    qdf, kdf = jnp.exp(lgf * (pos + 1.0)), jnp.exp(lgf * (C - 1.0 - pos))
    qdb, kdb = jnp.exp(lgb * (C - pos)), jnp.exp(lgb * pos)
    row_h1_full = lax.broadcasted_iota(jnp.int32, (W, W), 0) >= D_RET
    cdf = jnp.where(row_h1_full, jnp.exp(lgf1 * C), jnp.exp(lgf0 * C))
    cdb = jnp.where(row_h1_full, jnp.exp(lgb1 * C), jnp.exp(lgb0 * C))

    def intra_decay(lf, lb):
        return jnp.where(diff >= 0, jnp.exp(lf * jnp.maximum(diff, 0.0)), jnp.exp(lb * jnp.maximum(-diff, 0.0)))

    dmat0, dmat1 = intra_decay(lgf0, lgb0), intra_decay(lgf1, lgb1)
    g = g_ref[...]

    def chunk_rows(n):
        return pl.ds(pl.multiple_of(n * C, C), C)

    def phase_a(n, carry):
        rows = chunk_rows(n)
        k = rk_ref[0, rows, :].astype(F32)
        k2 = jnp.concatenate([(k * kdf).astype(BF16), (k * kdb).astype(BF16)], axis=1)
        kv_ref[n] = _dot_tn(k2, rv_ref[0, rows, :])
        return carry

    lax.fori_loop(0, n_chunks, phase_a, 0, unroll=2)

    def init_state(ref):
        if not has_s0:
            return jnp.zeros((W, W), F32)
        z = jnp.zeros((D_RET, D_RET), F32)
        return jnp.concatenate([jnp.concatenate([ref[0, 0], z], axis=1),
                                jnp.concatenate([z, ref[0, 1]], axis=1)], axis=0)

    def scan_f(n, S):
        sp_ref[n, 0:W, :] = S.astype(BF16)
        return cdf * S + jnp.where(blockdiag, kv_ref[n, 0:W, :], 0.0)

    def scan_b(i, S):
        n = n_chunks - 1 - i
        sp_ref[n, W:2 * W, :] = S.astype(BF16)
        return cdb * S + jnp.where(blockdiag, kv_ref[n, W:2 * W, :], 0.0)

    Sf = lax.fori_loop(0, n_chunks, scan_f, init_state(s0f_ref if has_s0 else None))
    Sb = lax.fori_loop(0, n_chunks, scan_b, init_state(s0b_ref if has_s0 else None))
    if want_state:
        for S, ref in ((Sf, sf_ref), (Sb, sb_ref)):
            ref[0, 0] = S[0:D_RET, 0:D_RET]
            ref[0, 1] = S[D_RET:W, D_RET:W]

    def phase_c(n, carry):
        rows = chunk_rows(n)
        q = rq_ref[0, rows, :]
        k = rk_ref[0, rows, :]
        v = rv_ref[0, rows, :]
        zero = jnp.zeros_like(q)
        sc0 = _dot_nt(jnp.where(lane_h1, zero, q), k) * dmat0
        sc1 = _dot_nt(jnp.where(lane_h1, q, zero), k) * dmat1
        p = jnp.concatenate([sc0.astype(BF16), sc1.astype(BF16)], axis=1)
        v_bd = jnp.concatenate([jnp.where(lane_h1, zero, v), jnp.where(lane_h1, v, zero)], axis=0)
        qf = q.astype(F32)
        q2 = jnp.concatenate([(qf * qdf).astype(BF16), (qf * qdb).astype(BF16)], axis=1)
        o = _dot(p, v_bd) + _dot(q2, sp_ref[n])
        o2 = o * o
        ss0 = jnp.sum(jnp.where(lane_h1, 0.0, o2), axis=-1, keepdims=True)
        ss1 = jnp.sum(jnp.where(lane_h1, o2, 0.0), axis=-1, keepdims=True)
        ms = jnp.where(lane_h1, ss1, ss0) * (1.0 / D_RET)
        o = o * lax.rsqrt(ms + EPS) * g * rg_ref[0, rows, :].astype(F32)
        o_ref[0, rows, :] = o.astype(BF16)
        return carry

    lax.fori_loop(0, n_chunks, phase_c, 0, unroll=2)


def _retention(rq, rk, rv, rg, dec_f, dec_b, g, s0f, s0b, want_state):
    B, L, _ = rq.shape
    has_s0 = s0f is not None
    seq = pl.BlockSpec((1, L, LANES), lambda b, hp: (b, 0, hp))
    st = pl.BlockSpec((1, 2, D_RET, D_RET), lambda b, hp: (b, hp, 0, 0))
    cst = lambda shape: pl.BlockSpec(shape, lambda b, hp: (0,) * len(shape))
    in_specs = [seq] * 4 + [cst((H_RET, LANES)), cst((H_RET, LANES)), cst((1, LANES))]
    args = [rq, rk, rv, rg, dec_f, dec_b, g]
    if has_s0:
        in_specs += [st, st]
        args += [s0f, s0b]
    out_specs = [seq]
    out_shape = [jax.ShapeDtypeStruct((B, L, D_RET_ALL), BF16)]
    if want_state:
        out_specs += [st, st]
        out_shape += [jax.ShapeDtypeStruct((B, H_RET, D_RET, D_RET), F32)] * 2
    return pl.pallas_call(
        functools.partial(_ret_kernel, has_s0, want_state, L // RET_CHUNK),
        grid=(B, H_RET // 2),
        in_specs=in_specs,
        out_specs=out_specs,
        out_shape=out_shape,
        scratch_shapes=[pltpu.VMEM((L // RET_CHUNK, 4 * D_RET, LANES), F32),
                        pltpu.VMEM((L // RET_CHUNK, 4 * D_RET, LANES), BF16)],
        compiler_params=_cparams("parallel", "parallel"),
        name="retention_s0" if has_s0 else "retention",
    )(*args)


def _pack_pairs(x):
    w = x.shape[1] // 2
    hi = lax.bitcast_convert_type(x[:, :w].astype(BF16).astype(F32), jnp.uint32)
    lo = lax.bitcast_convert_type(x[:, w:].astype(BF16).astype(F32), jnp.uint32)
    return hi | (lo >> 16)


def _unpack_pairs(p):
    hi = lax.bitcast_convert_type(p & jnp.uint32(0xFFFF0000), F32)
    lo = lax.bitcast_convert_type(p << 16, F32)
    return jnp.concatenate([hi, lo], axis=1)


D_PACKED = D_MODEL // 2


def _outproj_kernel(omla_ref, oret_ref, x_ref, mod_ref, wo_ref, n2g_ref, rwh_ref, rwl_ref, rb_ref,
                    x2_ref, h2p_ref, eid_ref, rnk_ref, wk_ref, cnt_ref, carry_ref):
    @pl.when(pl.program_id(0) == 0)
    def _():
        carry_ref[...] = jnp.zeros_like(carry_ref)

    mod = mod_ref[0]
    half = H_MLA * D_V
    a = _dot(omla_ref[...], wo_ref[:half, :]) + _dot(oret_ref[...], wo_ref[half:, :])
    x2 = x_ref[...] + mod[2:3] * a
    x2_ref[...] = x2
    h2 = _rms(x2, n2g_ref[...]) * (1.0 + mod[4:5]) + mod[3:4]
    h2p_ref[...] = _pack_pairs(h2)
    h2_hi = h2.astype(BF16)
    h2_lo = (h2 - h2_hi.astype(F32)).astype(BF16)
    logits = _dot(h2_hi, rwh_ref[...]) + _dot(h2_lo, rwh_ref[...]) + _dot(h2_hi, rwl_ref[...])
    scores = jax.nn.sigmoid(logits)
    lane = lax.broadcasted_iota(jnp.int32, scores.shape, 1)
    cand = scores + rb_ref[...]
    picked = jnp.zeros(scores.shape, jnp.bool_)
    hits = []
    for _ in range(TOP_K):
        mx = jnp.max(cand, axis=-1, keepdims=True)
        first = jnp.min(jnp.where(cand == mx, lane, LANES), axis=-1, keepdims=True)
        hit = lane == first
        hits.append((first, hit))
        picked = jnp.logical_or(picked, hit)
        cand = jnp.where(hit, -jnp.inf, cand)
    sel = jnp.where(picked, scores, 0.0)
    gates = sel * (ROUTED_SCALE / jnp.sum(sel, axis=-1, keepdims=True))

    tm = scores.shape[0]
    onehot = jnp.where(picked, 1.0, 0.0)
    earlier = (lax.broadcasted_iota(jnp.int32, (tm, tm), 0) > lax.broadcasted_iota(jnp.int32, (tm, tm), 1))
    rank = _dot(jnp.where(earlier, 1.0, 0.0).astype(BF16), onehot.astype(BF16)) + carry_ref[...]
    carry_ref[...] += jnp.sum(onehot, axis=0, keepdims=True)
    cnt_ref[...] = carry_ref[...].astype(jnp.int32)

    eid = jnp.zeros(scores.shape, jnp.int32)
    rnk = jnp.zeros(scores.shape, F32)
    wk = jnp.zeros(scores.shape, F32)
    for k, (first, hit) in enumerate(hits):
        slot = lane == k
        eid = jnp.where(slot, first, eid)
        rnk = jnp.where(slot, jnp.sum(jnp.where(hit, rank, 0.0), axis=-1, keepdims=True), rnk)
        wk = jnp.where(slot, jnp.sum(jnp.where(hit, gates, 0.0), axis=-1, keepdims=True), wk)
    eid_ref[...] = eid
    rnk_ref[...] = rnk.astype(jnp.int32)
    wk_ref[...] = wk


def _outproj(o_mla, o_ret, x, mod, mod_row0, tokens_per_row, wts, tm):
    T = x.shape[0]
    bpr = tokens_per_row // tm
    tok = lambda w: pl.BlockSpec((tm, w), lambda i: (i, 0))
    return pl.pallas_call(
        _outproj_kernel,
        grid=(T // tm,),
        in_specs=[tok(H_MLA * D_V), tok(D_RET_ALL), tok(D_MODEL),
                  pl.BlockSpec((1, 6, D_MODEL), lambda i: (mod_row0 + i // bpr, 0, 0)),
                  _const_spec((D_MODEL, D_MODEL)), _const_spec((1, D_MODEL)),
                  _const_spec((D_MODEL, LANES)), _const_spec((D_MODEL, LANES)), _const_spec((1, LANES))],
        out_specs=[tok(D_MODEL), tok(D_PACKED), tok(LANES), tok(LANES), tok(LANES), _const_spec((1, LANES))],
        out_shape=[jax.ShapeDtypeStruct((T, D_MODEL), F32),
                   jax.ShapeDtypeStruct((T, D_PACKED), jnp.uint32),
                   jax.ShapeDtypeStruct((T, LANES), jnp.int32),
                   jax.ShapeDtypeStruct((T, LANES), jnp.int32),
                   jax.ShapeDtypeStruct((T, LANES), F32),
                   jax.ShapeDtypeStruct((1, LANES), jnp.int32)],
        scratch_shapes=[pltpu.VMEM((1, LANES), F32)],
        compiler_params=_cparams("arbitrary"),
        name="outproj_router",
    )(o_mla, o_ret, x, mod, wts["w_o"], wts["norm2_g"], wts["router_hi"], wts["router_lo"],
      wts["router_bias"])


def _plan_kernel(eid_ref, rnk_ref, tab_ref, pos_ref):
    eid = eid_ref[...]
    rnk = rnk_ref[...]
    tab = tab_ref[...]
    lane = lax.broadcasted_iota(jnp.int32, eid.shape, 1)
    pos = jnp.zeros(eid.shape, jnp.int32)
    for k in range(TOP_K):
        onehot = jnp.where(lane == eid[:, k:k + 1], 1.0, 0.0).astype(BF16)
        d = _dot(onehot, tab)
        start = d[:, 0:1] * 65536.0 + d[:, 1:2] * 256.0 + d[:, 2:3]
        pos = jnp.where(lane == k, start.astype(jnp.int32) + rnk, pos)
    pos_ref[...] = pos


def _plan(eid, rnk, starts, tm):
    T = eid.shape[0]
    digits = jnp.stack([starts >> 16, (starts >> 8) & 255, starts & 255], axis=1).astype(BF16)
    tab = jnp.zeros((LANES, LANES), BF16).at[:N_EXPERTS, :3].set(digits)
    tok = pl.BlockSpec((tm, LANES), lambda i: (i, 0))
    return pl.pallas_call(
        _plan_kernel,
        grid=(T // tm,),
        in_specs=[tok, tok, _const_spec((LANES, LANES))],
        out_specs=tok,
        out_shape=jax.ShapeDtypeStruct((T, LANES), jnp.int32),
        compiler_params=_cparams("parallel"),
        name="moe_plan",
    )(eid, rnk, tab)


SC_WORKERS = 32
SC_CHUNK = 64


def _sc_worker_id():
    return lax.axis_index("s") * 2 + lax.axis_index("c")


def _sc_mesh():
    return plsc.VectorSubcoreMesh(core_axis_name="c", subcore_axis_name="s")


def _sc_gather_rows(table, idx):
    N = idx.shape[0]
    W = table.shape[1]
    per_w = N // SC_WORKERS
    n_chunks = per_w // SC_CHUNK
    assert N == SC_WORKERS * n_chunks * SC_CHUNK and n_chunks % 2 == 0

    @functools.partial(
        pl.kernel, mesh=_sc_mesh(), out_type=jax.ShapeDtypeStruct((N, W), table.dtype),
        scratch_types=[pltpu.VMEM((n_chunks, SC_CHUNK), jnp.int32),
                       pltpu.VMEM((2, SC_CHUNK, W), table.dtype),
                       pltpu.SemaphoreType.DMA((2,)), pltpu.SemaphoreType.DMA((2,))],
        name="sc_gather_rows")
    def k(table_hbm, idx_hbm, out_hbm, idx_v, rows_v, gsem, wsem):
        wid = _sc_worker_id()
        base = wid * per_w
        pltpu.sync_copy(idx_hbm.at[wid], idx_v)

        def gather(j, b):
            return pltpu.make_async_copy(table_hbm.at[idx_v.at[j]], rows_v.at[b], gsem.at[b])

        def writeback(j, b):
            off = pl.multiple_of(base + j * SC_CHUNK, SC_CHUNK)
            return pltpu.make_async_copy(rows_v.at[b], out_hbm.at[pl.ds(off, SC_CHUNK)], wsem.at[b])

        gather(0, 0).start()

        @pl.loop(0, n_chunks, step=2)
        def _(j):
            for b in range(2):
                jj = j + b

                @pl.when(jj + 1 < n_chunks)
                def _():
                    @pl.when(jj >= 1)
                    def _():
                        writeback(jj - 1, 1 - b).wait()
                    gather(jj + 1, 1 - b).start()

                gather(jj, b).wait()
                writeback(jj, b).start()

        writeback(n_chunks - 2, 0).wait()
        writeback(n_chunks - 1, 1).wait()

    return k(table, idx.reshape(SC_WORKERS, n_chunks, SC_CHUNK))


def _sc_dispatch_rows(rows_list, pos_list, n_out):
    W = rows_list[0].shape[1]
    K = pos_list[0].shape[1]
    n_src = len(rows_list)
    plan = []
    idx_list = []
    for rows, pos in zip(rows_list, pos_list):
        T = rows.shape[0]
        per_w = T // SC_WORKERS
        n_chunks = per_w // SC_CHUNK
        assert T == SC_WORKERS * n_chunks * SC_CHUNK and n_chunks % 2 == 0
        plan.append((per_w, n_chunks))
        idx_list.append(pos.reshape(SC_WORKERS, n_chunks, SC_CHUNK, K).transpose(0, 1, 3, 2))
    max_chunks = max(n for _, n in plan)

    @functools.partial(
        pl.kernel, mesh=_sc_mesh(), out_type=jax.ShapeDtypeStruct((n_out, W), rows_list[0].dtype),
        scratch_types=[pltpu.VMEM((max_chunks, K, SC_CHUNK), jnp.int32),
                       pltpu.VMEM((2, SC_CHUNK, W), rows_list[0].dtype),
                       pltpu.SemaphoreType.DMA((2,)), pltpu.SemaphoreType.DMA((2,))],
        name="sc_dispatch_rows")
    def k(*refs):
        rows_refs, idx_refs = refs[:n_src], refs[n_src:2 * n_src]
        out_hbm, idx_v, rows_v, lsem, ssem = refs[2 * n_src:]
        wid = _sc_worker_id()
        for rows_hbm, idx_hbm, (per_w, n_chunks) in zip(rows_refs, idx_refs, plan):
            base = wid * per_w
            pltpu.sync_copy(idx_hbm.at[wid], idx_v.at[pl.ds(0, n_chunks)])

            def load(j, b):
                off = pl.multiple_of(base + j * SC_CHUNK, SC_CHUNK)
                return pltpu.make_async_copy(rows_hbm.at[pl.ds(off, SC_CHUNK)], rows_v.at[b], lsem.at[b])

            def scatter(j, b, kk):
                return pltpu.make_async_copy(rows_v.at[b], out_hbm.at[idx_v.at[j, kk]], ssem.at[b])

            load(0, 0).start()

            @pl.loop(0, n_chunks, step=2)
            def _(j):
                for b in range(2):
                    jj = j + b

                    @pl.when(jj + 1 < n_chunks)
                    def _():
                        @pl.when(jj >= 1)
                        def _():
                            for kk in range(K):
                                scatter(jj - 1, 1 - b, kk).wait()
                        load(jj + 1, 1 - b).start()

                    load(jj, b).wait()
                    for kk in range(K):
                        scatter(jj, b, kk).start()

            for kk in range(K):
                scatter(n_chunks - 2, 0, kk).wait()
            for kk in range(K):
                scatter(n_chunks - 1, 1, kk).wait()

    return k(*rows_list, *idx_list)


ROW_TILE = 512


def _experts_kernel(te_ref, tv_ref, xs_ref, wg_ref, wu_ref, wd_ref, ys_ref, wg_s, wu_s, wd_s):
    i = pl.program_id(0)
    valid = tv_ref[i]

    @pl.when(jnp.logical_or(i == 0, te_ref[i] != te_ref[jnp.maximum(i - 1, 0)]))
    def _():
        wg_s[...] = wg_ref[0].astype(BF16)
        wu_s[...] = wu_ref[0].astype(BF16)
        wd_s[...] = wd_ref[0].astype(BF16)

    @pl.when(valid > 0)
    def _():
        row = lax.broadcasted_iota(jnp.int32, (ROW_TILE, 1), 0)
        p = jnp.where(row < valid, xs_ref[...], jnp.uint32(0))
        x = _unpack_pairs(p).astype(BF16)
        act = _silu(_dot(x, wg_s[...])) * _dot(x, wu_s[...])
        ys_ref[...] = _pack_pairs(_dot(act.astype(BF16), wd_s[...]))

    @pl.when(valid <= 0)
    def _():
        ys_ref[...] = jnp.zeros_like(ys_ref)


def _experts(xs, tile_expert, tile_valid, wts):
    n_tiles = tile_expert.shape[0]
    ex = lambda a, b: pl.BlockSpec((1, a, b), lambda i, te, tv: (te[i], 0, 0))
    row = pl.BlockSpec((ROW_TILE, D_PACKED), lambda i, te, tv: (i, 0))
    return pl.pallas_call(
        _experts_kernel,
        grid_spec=pltpu.PrefetchScalarGridSpec(
            num_scalar_prefetch=2, grid=(n_tiles,),
            in_specs=[row, ex(D_MODEL, D_EXPERT), ex(D_MODEL, D_EXPERT), ex(D_EXPERT, D_MODEL)],
            out_specs=row,
            scratch_shapes=[pltpu.VMEM((D_MODEL, D_EXPERT), BF16), pltpu.VMEM((D_MODEL, D_EXPERT), BF16),
                            pltpu.VMEM((D_EXPERT, D_MODEL), BF16)]),
        out_shape=jax.ShapeDtypeStruct(xs.shape, jnp.uint32),
        compiler_params=_cparams("arbitrary"),
        name="experts",
    )(tile_expert, tile_valid, xs, wts["exp_gate"], wts["exp_up"], wts["exp_down"])


def _combine_kernel(h2p_ref, ys_ref, wk_ref, x2_ref, mod_ref, shg_ref, shu_ref, shd_ref, y_ref):
    t = _unpack_pairs(h2p_ref[...]).astype(BF16)
    act = _silu(_dot(t, shg_ref[...])) * _dot(t, shu_ref[...])
    acc = _dot(act.astype(BF16), shd_ref[...])
    wk = wk_ref[...]
    for k in range(TOP_K):
        acc = acc + wk[:, k:k + 1] * _unpack_pairs(ys_ref[k])
    y_ref[...] = x2_ref[...] + mod_ref[0][5:6] * acc


def _combine(h2p, ys, ys_row0, wk, x2, mod, mod_row0, tokens_per_row, wts, tm):
    T = h2p.shape[0]
    bpr = tokens_per_row // tm
    blk0 = ys_row0 // tm
    tok = lambda w: pl.BlockSpec((tm, w), lambda i: (i, 0))
    return pl.pallas_call(
        _combine_kernel,
        grid=(T // tm,),
        in_specs=[tok(D_PACKED), pl.BlockSpec((TOP_K, tm, D_PACKED), lambda i: (0, blk0 + i, 0)), tok(LANES),
                  tok(D_MODEL), pl.BlockSpec((1, 6, D_MODEL), lambda i: (mod_row0 + i // bpr, 0, 0)),
                  _const_spec((D_MODEL, D_SHARED)), _const_spec((D_MODEL, D_SHARED)),
                  _const_spec((D_SHARED, D_MODEL))],
        out_specs=tok(D_MODEL),
        out_shape=jax.ShapeDtypeStruct((T, D_MODEL), F32),
        compiler_params=_cparams("parallel"),
        name="moe_combine",
    )(h2p, ys, wk, x2, mod, wts["sh_gate"], wts["sh_up"], wts["sh_down"])


def _moe(halves, mod, wts):
    routed = [h["routed"] for h in halves]
    sizes = [r[1].shape[0] for r in routed]
    n_tiles = (TOP_K * sum(sizes)) // ROW_TILE + N_EXPERTS
    path_counts = [r[5][0, :N_EXPERTS] for r in routed]
    counts = sum(path_counts)
    padded = ((counts + ROW_TILE - 1) // ROW_TILE) * ROW_TILE
    ends = jnp.cumsum(padded)
    starts = ends - padded
    tile_start = jnp.arange(n_tiles, dtype=jnp.int32) * ROW_TILE
    tile_expert = jnp.minimum(jnp.sum(tile_start[:, None] >= ends[None, :], axis=1), N_EXPERTS - 1).astype(jnp.int32)
    tile_valid = jnp.clip(counts[tile_expert] - (tile_start - starts[tile_expert]), 0, ROW_TILE).astype(jnp.int32)

    pos, first = [], starts
    for h, r, c in zip(halves, routed, path_counts):
        pos.append(_plan(r[2], r[3], first, h["tm"])[:, :TOP_K])
        first = first + c
    xs = _sc_dispatch_rows([r[1] for r in routed], pos, n_tiles * ROW_TILE)
    ys = _experts(xs, tile_expert, tile_valid, wts)
    ysel = _sc_gather_rows(ys, jnp.concatenate(pos, axis=0).T.reshape(-1)).reshape(TOP_K, sum(sizes), D_PACKED)
    outs, off = [], 0
    for h, r in zip(halves, routed):
        y = _combine(r[1], ysel, off, r[4], r[0], mod, h["mod_row0"], h["tokens_per_row"], wts, h["tm"])
        outs.append(y.reshape(h["shape"]))
        off += r[1].shape[0]
    return outs


def _pad_heads(w, d_used):
    k = w.shape[0]
    w = w.reshape(k, H_MLA, d_used)
    return jnp.pad(w, ((0, 0), (0, 0), (0, HEAD_SLOT - d_used))).reshape(k, H_MLA * HEAD_SLOT)


def _lane_pad(v):
    return jnp.pad(v, (0, LANES - v.shape[0])).reshape(1, LANES)


def _rot_partners(w):
    rot = w[..., w.shape[-1] - D_ROPE:]
    return jnp.concatenate([jnp.zeros_like(w[..., :w.shape[-1] - D_ROPE]),
                            rot[..., _swap_partner(D_ROPE, D_ROPE // 4)]], axis=-1)


def _prep_weights(l, w_in, norm1_g, mla_q_norm_g, w_uq, mla_kv_norm_g, w_ukv, mla_q_gain, mla_k_gain,
                  w_o, norm2_g, router_w, router_bias, exp_w_gate, exp_w_up, exp_w_down,
                  sh_w_gate, sh_w_up, sh_w_down):
    w = w_in[l]
    a, b, c = Q_RANK, Q_RANK + KV_RANK, Q_RANK + KV_RANK + D_ROPE
    kr_slot = lambda wk: jnp.pad(wk, ((0, 0), (D_NOPE, LANES - D_QK)))
    w1 = jnp.concatenate([w[:, :b], kr_slot(w[:, b:c]), w[:, c:]], axis=1).astype(BF16)
    ret_sw = np.concatenate([h * D_RET + _swap_partner(D_RET, D_RET // 4) for h in range(H_RET)])
    w_rq, w_rk = w[:, c:c + D_RET_ALL], w[:, c + D_RET_ALL:c + 2 * D_RET_ALL]
    w1_rope = jnp.concatenate([w1, kr_slot(_rot_partners(w[:, b:c])).astype(BF16),
                               w_rq[:, ret_sw].astype(BF16), w_rk[:, ret_sw].astype(BF16)], axis=1)
    uq = w_uq[l].reshape(Q_RANK, H_MLA, D_QK)
    pad_slots = lambda u: jnp.pad(u, ((0, 0), (0, 0), (0, HEAD_SLOT - D_QK))).reshape(Q_RANK, H_MLA * HEAD_SLOT)
    gains = lambda g: jnp.concatenate([_lane_pad(g), _lane_pad(_rot_partners(g))], axis=0)
    ukv = w_ukv[l].reshape(KV_RANK, H_MLA, D_NOPE + D_V)
    w_k = _pad_heads(ukv[:, :, :D_NOPE].reshape(KV_RANK, H_MLA * D_NOPE), D_NOPE)
    w_v = _pad_heads(ukv[:, :, D_NOPE:].reshape(KV_RANK, H_MLA * D_V), D_V)
    rw = jnp.pad(router_w[l], ((0, 0), (0, LANES - N_EXPERTS)))
    rw_hi = rw.astype(BF16)
    rw_lo = (rw - rw_hi.astype(F32)).astype(BF16)
    rb = jnp.concatenate([router_bias[l].astype(F32), jnp.full((LANES - N_EXPERTS,), -jnp.inf, F32)])
    return dict(
        norm1_g=norm1_g[l].reshape(1, D_MODEL), w1=w1, w1_rope=w1_rope,
        q_norm_g=mla_q_norm_g[l].reshape(1, Q_RANK),
        w_uq=jnp.concatenate([pad_slots(uq), pad_slots(_rot_partners(uq))], axis=1).astype(BF16),
        kv_norm_g=mla_kv_norm_g[l].reshape(1, KV_RANK),
        w_kv=jnp.concatenate([w_k, w_v], axis=1).astype(BF16),
        q_gain=gains(mla_q_gain[l]), k_gain=gains(mla_k_gain[l]),
        w_o=w_o[l].astype(BF16), norm2_g=norm2_g[l].reshape(1, D_MODEL),
        router_hi=rw_hi, router_lo=rw_lo, router_bias=rb.reshape(1, LANES),
        exp_gate=exp_w_gate[l], exp_up=exp_w_up[l], exp_down=exp_w_down[l],
        sh_gate=sh_w_gate[l].astype(BF16), sh_up=sh_w_up[l].astype(BF16),
        sh_down=sh_w_down[l].astype(BF16))


def _pick(n, prefs):
    for p in prefs:
        if n % p == 0:
            return p
    return n


def _mixer_half(x, mod, mod_row0, wts, ret_wts, ctx):
    B, L, D = x.shape
    T = B * L
    tokens_per_row = L if ctx is not None else T
    xt = x.reshape(T, D)
    rope_tabs = None
    if ctx is not None:
        rope_tabs = _rope_tables(L)
    tm = _pick(L, (512, 256, 128))
    ckv_n, krp, q, k, v, rq, rk, rv, rg = _inproj(xt, mod, mod_row0, tokens_per_row, wts, rope_tabs, tm)

    seq = lambda a: a.reshape(B, L, a.shape[-1])
    kc = vc = s0f = s0b = None
    if ctx is not None:
        ckv_c, kr_c, s0f, s0b = ctx
        Lc = ckv_c.shape[1]
        krp_c = jnp.pad(kr_c, ((0, 0), (0, 0), (D_NOPE, LANES - D_NOPE - D_ROPE)))
        kc, vc = _ctx_kv(ckv_c.reshape(B * Lc, KV_RANK), krp_c.reshape(B * Lc, LANES), wts,
                         _pick(B * Lc, (512, 256, 128)))
        kc, vc = kc.reshape(B, Lc, -1), vc.reshape(B, Lc, -1)
    o_mla = _attention(seq(q), seq(k), seq(v), kc, vc, _pick(L, (512, 256, 128)))

    dec_f, dec_b, ret_g = ret_wts
    want_state = ctx is None
    ret = _retention(seq(rq), seq(rk), seq(rv), seq(rg), dec_f, dec_b, ret_g, s0f, s0b, want_state)
    o_ret = ret[0]

    routed = _outproj(o_mla.reshape(T, -1), o_ret.reshape(T, -1), xt, mod, mod_row0, tokens_per_row, wts, tm)
    new = None
    if want_state:
        new = (ckv_n.reshape(B, L, KV_RANK), krp[:, D_NOPE:D_NOPE + D_ROPE].reshape(B, L, D_ROPE),
               ret[1], ret[2])
    return dict(routed=routed, mod_row0=mod_row0, tokens_per_row=tokens_per_row, tm=tm, shape=(B, L, D)), new


def kernel(x_prompt, x_sample, cache_mla_ckv, cache_mla_krope, state_ret_fwd, state_ret_bwd, c, c_ctx,
           w_ada, b_ada, norm1_g, w_in, mla_q_norm_g, w_uq, mla_kv_norm_g, w_ukv, mla_q_gain, mla_k_gain,
           ret_decay_fwd, ret_decay_bwd, ret_norm_g, w_o, norm2_g, router_w, router_bias, exp_w_gate,
           exp_w_up, exp_w_down, sh_w_gate, sh_w_up, sh_w_down):
    depth = w_ada.shape[0]
    n_dec = c.shape[0]
    assert 1 + n_dec <= MOD_ROWS
    cond = jnp.concatenate([c_ctx[None], c, jnp.zeros((MOD_ROWS - 1 - n_dec, D_MODEL), F32)], axis=0)

    y_prompt, y_sample = x_prompt, x_sample
    ckv_l, kr_l, sf_l, sb_l = [], [], [], []
    for l in range(depth):
        wts = _prep_weights(l, w_in, norm1_g, mla_q_norm_g, w_uq, mla_kv_norm_g, w_ukv, mla_q_gain,
                            mla_k_gain, w_o, norm2_g, router_w, router_bias, exp_w_gate, exp_w_up,
                            exp_w_down, sh_w_gate, sh_w_up, sh_w_down)
        ret_wts = (jnp.broadcast_to(ret_decay_fwd[l].astype(F32)[:, None], (H_RET, LANES)),
                   jnp.broadcast_to(ret_decay_bwd[l].astype(F32)[:, None], (H_RET, LANES)),
                   jnp.tile(ret_norm_g[l].reshape(1, D_RET), (1, LANES // D_RET)))
        mod = _adaln(cond, w_ada[l], b_ada[l]).reshape(MOD_ROWS, 6, D_MODEL)
        half_p, new = _mixer_half(y_prompt, mod, 0, wts, ret_wts, None)
        ckv_l.append(new[0]); kr_l.append(new[1]); sf_l.append(new[2]); sb_l.append(new[3])
        ctx = (cache_mla_ckv[:, l], cache_mla_krope[:, l], state_ret_fwd[:, l], state_ret_bwd[:, l])
        half_s, _ = _mixer_half(y_sample, mod, 1, wts, ret_wts, ctx)
        y_prompt, y_sample = _moe([half_p, half_s], mod, wts)

    return (y_prompt, y_sample, jnp.stack(ckv_l, axis=1), jnp.stack(kr_l, axis=1),
            jnp.stack(sf_l, axis=1), jnp.stack(sb_l, axis=1))
```

```python
import functools
import math

import numpy as np
import jax
import jax.numpy as jnp
from jax import lax
from jax.experimental import pallas as pl
from jax.experimental.pallas import tpu as pltpu
from jax.experimental.pallas import tpu_sc as plsc

F32 = jnp.float32
BF16 = jnp.bfloat16

D_MODEL = 1024
GRID_W = 64
H_MLA = 8
D_NOPE = 64
D_ROPE = 32
D_QK = D_NOPE + D_ROPE
D_V = 64
Q_RANK = 256
KV_RANK = 128
H_RET = 8
D_RET = 64
RET_CHUNK = 128
D_RET_ALL = H_RET * D_RET
N_EXPERTS = 64
TOP_K = 6
D_EXPERT = 256
D_SHARED = 256
ROUTED_SCALE = 2.5
ROPE_BASE = 10000.0
EPS = 1e-6
LOG2_E = math.log2(math.e)

LANES = 128
HEAD_SLOT = LANES
D_IN_PAD = Q_RANK + KV_RANK + LANES + 4 * D_RET_ALL
MOD_ROWS = 16
VMEM_LIMIT = 56 * 1024 * 1024


def _cparams(*sem):
    return pltpu.CompilerParams(dimension_semantics=sem, vmem_limit_bytes=VMEM_LIMIT)


def _dot(a, b):
    return jnp.dot(a, b, preferred_element_type=F32)


def _dot_nt(a, b):
    return lax.dot_general(a, b, (((1,), (1,)), ((), ())), preferred_element_type=F32)


def _dot_tn(a, b):
    return lax.dot_general(a, b, (((0,), (0,)), ((), ())), preferred_element_type=F32)


def _rms(x, g):
    return x * lax.rsqrt(jnp.mean(x * x, axis=-1, keepdims=True) + EPS) * g


def _silu(x):
    return x * jax.nn.sigmoid(x)


def _adaln_kernel(c_ref, w_ref, b_ref, o_ref):
    s = _silu(c_ref[...])
    o_ref[...] = _dot(s.astype(BF16), w_ref[...].astype(BF16)) + b_ref[...]


def _adaln(cond, w_ada, b_ada):
    n_out = w_ada.shape[1]
    bn = 1536
    return pl.pallas_call(
        _adaln_kernel,
        grid=(n_out // bn,),
        in_specs=[pl.BlockSpec((MOD_ROWS, D_MODEL), lambda j: (0, 0)),
                  pl.BlockSpec((D_MODEL, bn), lambda j: (0, j)),
                  pl.BlockSpec((1, bn), lambda j: (0, j))],
        out_specs=pl.BlockSpec((MOD_ROWS, bn), lambda j: (0, j)),
        out_shape=jax.ShapeDtypeStruct((MOD_ROWS, n_out), F32),
        compiler_params=_cparams("arbitrary"),
        name="adaln",
    )(cond, w_ada, b_ada.reshape(1, n_out))


def _swap_partner(n_dims, half_pair):
    j = np.arange(n_dims)
    return np.where((j % (2 * half_pair)) < half_pair, j + half_pair, j - half_pair)


def _rope_tables(n_tokens):
    t = np.arange(n_tokens)
    row = (t // GRID_W).astype(np.float64)
    col = (t % GRID_W).astype(np.float64)

    def axis_tables(width, lane0, period, fill):
        p = width // 2
        inv = 1.0 / (ROPE_BASE ** (np.arange(p, dtype=np.float64) / p))
        cos = np.full((n_tokens, LANES), fill)
        sin = np.zeros((n_tokens, LANES))
        starts = range(lane0, LANES, period) if period else (lane0,)
        for s0 in starts:
            for base, pos in ((s0, row), (s0 + width, col)):
                ang = pos[:, None] * inv[None, :]
                c, sn = np.cos(ang), np.sin(ang)
                cos[:, base:base + p] = c
                cos[:, base + p:base + 2 * p] = c
                sin[:, base:base + p] = -sn
                sin[:, base + p:base + 2 * p] = sn
        return cos, sin

    qk_cos, qk_sin = axis_tables(D_ROPE // 2, D_NOPE, 0, 1.0)
    ret_cos, ret_sin = axis_tables(D_RET // 2, 0, D_RET, 1.0)
    as_f32 = lambda *a: tuple(jnp.asarray(x, F32) for x in a)
    return as_f32(qk_cos, qk_sin), as_f32(ret_cos, ret_sin)


def _head_sums(x):
    r = lax.broadcasted_iota(jnp.int32, (2 * HEAD_SLOT, 2 * HEAD_SLOT), 0) // HEAD_SLOT
    c = lax.broadcasted_iota(jnp.int32, (2 * HEAD_SLOT, 2 * HEAD_SLOT), 1) // HEAD_SLOT
    ones_bd = jnp.where(r == c, 1.0, 0.0).astype(BF16)
    xb = x.astype(BF16)
    w = 2 * HEAD_SLOT
    return jnp.concatenate([_dot(xb[:, g * w:(g + 1) * w], ones_bd) for g in range(x.shape[1] // w)], axis=1)


def _tile_heads(v):
    return jnp.concatenate([v] * H_MLA, axis=1)


def _kv_heads(ckv_n, krp, krp_sw, wkv_ref, kgain, kgain_sw, tabs, k_ref, v_ref):
    kvp = _dot(ckv_n.astype(BF16), wkv_ref[...])
    kn = kvp[:, :H_MLA * HEAD_SLOT]
    ms = _head_sums(kn * kn + _tile_heads(krp * krp)) * (1.0 / D_QK)
    r = lax.rsqrt(ms + EPS)
    k = (kn + _tile_heads(krp)) * r * _tile_heads(kgain)
    if tabs is not None:
        cos, sin = tabs
        k = k * _tile_heads(cos) + r * _tile_heads(krp_sw * kgain_sw * sin)
    k_ref[...] = k.astype(BF16)
    v_lane = lax.broadcasted_iota(jnp.int32, (1, H_MLA * HEAD_SLOT), 1) % HEAD_SLOT
    v_ref[...] = (kvp[:, H_MLA * HEAD_SLOT:] + jnp.where(v_lane == D_V, 1.0, 0.0)).astype(BF16)


def _inproj_kernel(rope, *refs):
    (x_ref, mod_ref, n1g_ref, w1_ref, qng_ref, wuq_ref, kvg_ref, wkv_ref, qgain_ref,
     kgain_ref) = refs[:10]
    refs = refs[10:]
    if rope:
        qk_tabs = tuple(r[...] for r in refs[:2])
        ret_cos, ret_sin = (r[...] for r in refs[2:4])
        refs = refs[4:]
    else:
        qk_tabs = None
    ckvn_ref, krp_ref, q_ref, k_ref, v_ref, rq_ref, rk_ref, rv_ref, rg_ref = refs

    mod = mod_ref[0]
    h = _rms(x_ref[...], n1g_ref[...]) * (1.0 + mod[1:2]) + mod[0:1]
    z = _dot(h.astype(BF16), w1_ref[...])
    o = 0
    cq = z[:, o:o + Q_RANK]; o += Q_RANK
    ckv = z[:, o:o + KV_RANK]; o += KV_RANK
    krp = z[:, o:o + LANES]; o += LANES
    rq = z[:, o:o + D_RET_ALL]; o += D_RET_ALL
    rk = z[:, o:o + D_RET_ALL]; o += D_RET_ALL
    rv = z[:, o:o + D_RET_ALL]; o += D_RET_ALL
    rg = z[:, o:o + D_RET_ALL]; o += D_RET_ALL
    krp_sw = None
    if rope:
        krp_sw = z[:, o:o + LANES]; o += LANES
        rq_sw = z[:, o:o + D_RET_ALL]; o += D_RET_ALL
        rk_sw = z[:, o:o + D_RET_ALL]

    ckv_n = _rms(ckv, kvg_ref[...])
    ckvn_ref[...] = ckv_n
    krp_ref[...] = krp

    qn = _rms(cq, qng_ref[...]).astype(BF16)
    scale = D_QK ** -0.5 * LOG2_E
    q = _dot(qn, wuq_ref[:, :H_MLA * HEAD_SLOT])
    r = lax.rsqrt(_head_sums(q * q) * (1.0 / D_QK) + EPS)
    q = q * r * _tile_heads(qgain_ref[0:1] * scale)
    if rope:
        cos, sin = qk_tabs
        q_sw = _dot(qn, wuq_ref[:, H_MLA * HEAD_SLOT:])
        q = q * _tile_heads(cos) + q_sw * r * _tile_heads(qgain_ref[1:2] * scale * sin)
    q_ref[...] = q.astype(BF16)

    _kv_heads(ckv_n, krp, krp_sw, wkv_ref, kgain_ref[0:1], kgain_ref[1:2], qk_tabs, k_ref, v_ref)

    for t in range(D_RET_ALL // LANES):
        sl = slice(t * LANES, (t + 1) * LANES)
        rq_t, rk_t = rq[:, sl], rk[:, sl]
        if rope:
            rq_t = rq_t * ret_cos + rq_sw[:, sl] * ret_sin
            rk_t = rk_t * ret_cos + rk_sw[:, sl] * ret_sin
        rq_ref[:, sl] = rq_t.astype(BF16)
        rk_ref[:, sl] = (rk_t * (D_RET ** -0.5)).astype(BF16)
    rv_ref[...] = rv.astype(BF16)
    rg_ref[...] = _silu(rg).astype(BF16)


def _const_spec(shape):
    return pl.BlockSpec(shape, lambda i: (0,) * len(shape))


def _inproj(x, mod, mod_row0, tokens_per_row, wts, rope_tabs, tm):
    T = x.shape[0]
    bpr = tokens_per_row // tm
    rope = rope_tabs is not None
    tok = lambda w: pl.BlockSpec((tm, w), lambda i: (i, 0))
    in_specs = [tok(D_MODEL),
                pl.BlockSpec((1, 6, D_MODEL), lambda i: (mod_row0 + i // bpr, 0, 0)),
                _const_spec((1, D_MODEL)),
                _const_spec((D_MODEL, D_IN_PAD + (LANES + 2 * D_RET_ALL if rope else 0))),
                _const_spec((1, Q_RANK)),
                _const_spec((Q_RANK, 2 * H_MLA * HEAD_SLOT)),
                _const_spec((1, KV_RANK)),
                _const_spec((KV_RANK, 2 * H_MLA * HEAD_SLOT)),
                _const_spec((2, LANES)),
                _const_spec((2, LANES))]
    args = [x, mod, wts["norm1_g"], wts["w1_rope" if rope else "w1"], wts["q_norm_g"], wts["w_uq"],
            wts["kv_norm_g"], wts["w_kv"], wts["q_gain"], wts["k_gain"]]
    if rope:
        qk_tabs, ret_tabs = rope_tabs
        nb = qk_tabs[0].shape[0] // tm
        in_specs += [pl.BlockSpec((tm, LANES), lambda i: (i % nb, 0))] * 4
        args += list(qk_tabs) + list(ret_tabs)
    widths = [(KV_RANK, F32), (LANES, F32), (H_MLA * HEAD_SLOT, BF16), (H_MLA * HEAD_SLOT, BF16),
              (H_MLA * HEAD_SLOT, BF16)] + [(D_RET_ALL, BF16)] * 4
    return pl.pallas_call(
        functools.partial(_inproj_kernel, rope),
        grid=(T // tm,),
        in_specs=in_specs,
        out_specs=[tok(w) for w, _ in widths],
        out_shape=[jax.ShapeDtypeStruct((T, w), dt) for w, dt in widths],
        compiler_params=_cparams("parallel"),
        name="inproj_rope" if rope else "inproj",
    )(*args)


def _ctx_kv_kernel(ckv_ref, krp_ref, wkv_ref, kgain_ref, k_ref, v_ref):
    _kv_heads(ckv_ref[...], krp_ref[...], None, wkv_ref, kgain_ref[0:1], None, None, k_ref, v_ref)


def _ctx_kv(ckv, krp, wts, tm):
    T = ckv.shape[0]
    tok = lambda w: pl.BlockSpec((tm, w), lambda i: (i, 0))
    return pl.pallas_call(
        _ctx_kv_kernel,
        grid=(T // tm,),
        in_specs=[tok(KV_RANK), tok(LANES),
                  _const_spec((KV_RANK, 2 * H_MLA * HEAD_SLOT)), _const_spec((2, LANES))],
        out_specs=[tok(H_MLA * HEAD_SLOT), tok(H_MLA * HEAD_SLOT)],
        out_shape=[jax.ShapeDtypeStruct((T, H_MLA * HEAD_SLOT), BF16),
                   jax.ShapeDtypeStruct((T, H_MLA * HEAD_SLOT), BF16)],
        compiler_params=_cparams("parallel"),
        name="ctx_kv",
    )(ckv, krp, wts["w_kv"], wts["k_gain"])


ATTN_KEY_CHUNK = 2048


def _attn_kernel(has_ctx, *refs):
    if has_ctx:
        q_ref, k_ref, v_ref, kc_ref, vc_ref, o_ref = refs
    else:
        q_ref, k_ref, v_ref, o_ref = refs
    sources = [(k_ref, v_ref)] + ([(kc_ref, vc_ref)] if has_ctx else [])
    chunks = []
    for kr, vr in sources:
        n = kr.shape[1]
        kc = _pick(n, (ATTN_KEY_CHUNK, 256, 128))
        chunks += [(kr, vr, slice(c * kc, (c + 1) * kc)) for c in range(n // kc)]
    tq = q_ref.shape[1]
    outs = []
    for j in range(2):
        hs = slice(j * HEAD_SLOT, (j + 1) * HEAD_SLOT)
        q = q_ref[0, :, hs]
        m = jnp.full((tq, 1), -jnp.inf, F32)
        acc = jnp.zeros((tq, HEAD_SLOT), F32)
        for kr, vr, rows in chunks:
            s = _dot_nt(q, kr[0, rows, hs])
            m_new = jnp.maximum(m, jnp.max(s, axis=-1, keepdims=True))
            p = jnp.exp2(s - m_new).astype(BF16)
            acc = acc * jnp.exp2(m - m_new) + _dot(p, vr[0, rows, hs])
            m = m_new
        outs.append(acc[:, :D_V] / acc[:, D_V:D_V + 1])
    o_ref[0] = jnp.concatenate(outs, axis=-1).astype(BF16)


def _attention(q, k, v, kc, vc, tq):
    B, L, _ = q.shape
    has_ctx = kc is not None
    in_specs = [pl.BlockSpec((1, tq, 2 * HEAD_SLOT), lambda b, hp, i: (b, i, hp)),
                pl.BlockSpec((1, L, 2 * HEAD_SLOT), lambda b, hp, i: (b, 0, hp)),
                pl.BlockSpec((1, L, 2 * HEAD_SLOT), lambda b, hp, i: (b, 0, hp))]
    args = [q, k, v]
    if has_ctx:
        Lc = kc.shape[1]
        in_specs += [pl.BlockSpec((1, Lc, 2 * HEAD_SLOT), lambda b, hp, i: (b, 0, hp)),
                     pl.BlockSpec((1, Lc, 2 * HEAD_SLOT), lambda b, hp, i: (b, 0, hp))]
        args += [kc, vc]
    return pl.pallas_call(
        functools.partial(_attn_kernel, has_ctx),
        grid=(B, H_MLA // 2, L // tq),
        in_specs=in_specs,
        out_specs=pl.BlockSpec((1, tq, 2 * D_V), lambda b, hp, i: (b, i, hp)),
        out_shape=jax.ShapeDtypeStruct((B, L, H_MLA * D_V), BF16),
        compiler_params=_cparams("parallel", "parallel", "arbitrary"),
        name="attn_ctx" if has_ctx else "attn",
    )(*args)


def _ret_kernel(has_s0, want_state, n_chunks, *refs):
    rq_ref, rk_ref, rv_ref, rg_ref, decf_ref, decb_ref, g_ref = refs[:7]
    refs = refs[7:]
    if has_s0:
        s0f_ref, s0b_ref = refs[:2]
        refs = refs[2:]
    o_ref = refs[0]
    refs = refs[1:]
    if want_state:
        sf_ref, sb_ref = refs[:2]
        refs = refs[2:]
    kv_ref, sp_ref = refs

    C = RET_CHUNK
    W = 2 * D_RET
    hp = pl.program_id(1)
    lane = lax.broadcasted_iota(jnp.int32, (1, W), 1)
    lane_h1 = lane >= D_RET
    row_h1 = lax.broadcasted_iota(jnp.int32, (W, 1), 0) >= D_RET
    blockdiag = row_h1 == lane_h1
    pos = lax.broadcasted_iota(jnp.int32, (C, 1), 0).astype(F32)
    diff = (lax.broadcasted_iota(jnp.int32, (C, C), 0)
            - lax.broadcasted_iota(jnp.int32, (C, C), 1)).astype(F32)

    lg = {}
    for name, ref in (("f", decf_ref), ("b", decb_ref)):
        a = -jnp.exp(ref[pl.ds(2 * hp, 1), :])
        b = -jnp.exp(ref[pl.ds(2 * hp + 1, 1), :])
        lg[name] = (a[:, :1], b[:, :1], jnp.where(lane_h1, b, a))
    lgf0, lgf1, lgf = lg["f"]
    lgb0, lgb1, lgb = lg["b"]
    qdf, kdf = jnp.exp(lgf * (pos + 1.0)), jnp.exp(lgf * (C - 1.0 - pos))
    qdb, kdb = jnp.exp(lgb * (C - pos)), jnp.exp(lgb * pos)
    row_h1_full = lax.broadcasted_iota(jnp.int32, (W, W), 0) >= D_RET
    cdf = jnp.where(row_h1_full, jnp.exp(lgf1 * C), jnp.exp(lgf0 * C))
    cdb = jnp.where(row_h1_full, jnp.exp(lgb1 * C), jnp.exp(lgb0 * C))

    def intra_decay(lf, lb):
        return jnp.where(diff >= 0, jnp.exp(lf * jnp.maximum(diff, 0.0)), jnp.exp(lb * jnp.maximum(-diff, 0.0)))

    dmat0, dmat1 = intra_decay(lgf0, lgb0), intra_decay(lgf1, lgb1)
    g = g_ref[...]

    def chunk_rows(n):
        return pl.ds(pl.multiple_of(n * C, C), C)

    def phase_a(n, carry):
        rows = chunk_rows(n)
        k = rk_ref[0, rows, :].astype(F32)
        k2 = jnp.concatenate([(k * kdf).astype(BF16), (k * kdb).astype(BF16)], axis=1)
        kv_ref[n] = _dot_tn(k2, rv_ref[0, rows, :])
        return carry

    lax.fori_loop(0, n_chunks, phase_a, 0, unroll=2)

    def init_state(ref):
        if not has_s0:
            return jnp.zeros((W, W), F32)
        z = jnp.zeros((D_RET, D_RET), F32)
        return jnp.concatenate([jnp.concatenate([ref[0, 0], z], axis=1),
                                jnp.concatenate([z, ref[0, 1]], axis=1)], axis=0)

    def scan_f(n, S):
        sp_ref[n, 0:W, :] = S.astype(BF16)
        return cdf * S + jnp.where(blockdiag, kv_ref[n, 0:W, :], 0.0)

    def scan_b(i, S):
        n = n_chunks - 1 - i
        sp_ref[n, W:2 * W, :] = S.astype(BF16)
        return cdb * S + jnp.where(blockdiag, kv_ref[n, W:2 * W, :], 0.0)

    Sf = lax.fori_loop(0, n_chunks, scan_f, init_state(s0f_ref if has_s0 else None))
    Sb = lax.fori_loop(0, n_chunks, scan_b, init_state(s0b_ref if has_s0 else None))
    if want_state:
        for S, ref in ((Sf, sf_ref), (Sb, sb_ref)):
            ref[0, 0] = S[0:D_RET, 0:D_RET]
            ref[0, 1] = S[D_RET:W, D_RET:W]

    def phase_c(n, carry):
        rows = chunk_rows(n)
        q = rq_ref[0, rows, :]
        k = rk_ref[0, rows, :]
        v = rv_ref[0, rows, :]
        zero = jnp.zeros_like(q)
        sc0 = _dot_nt(jnp.where(lane_h1, zero, q), k) * dmat0
        sc1 = _dot_nt(jnp.where(lane_h1, q, zero), k) * dmat1
        p = jnp.concatenate([sc0.astype(BF16), sc1.astype(BF16)], axis=1)
        v_bd = jnp.concatenate([jnp.where(lane_h1, zero, v), jnp.where(lane_h1, v, zero)], axis=0)
        qf = q.astype(F32)
        q2 = jnp.concatenate([(qf * qdf).astype(BF16), (qf * qdb).astype(BF16)], axis=1)
        o = _dot(p, v_bd) + _dot(q2, sp_ref[n])
        o2 = o * o
        ss0 = jnp.sum(jnp.where(lane_h1, 0.0, o2), axis=-1, keepdims=True)
        ss1 = jnp.sum(jnp.where(lane_h1, o2, 0.0), axis=-1, keepdims=True)
        ms = jnp.where(lane_h1, ss1, ss0) * (1.0 / D_RET)
        o = o * lax.rsqrt(ms + EPS) * g * rg_ref[0, rows, :].astype(F32)
        o_ref[0, rows, :] = o.astype(BF16)
        return carry

    lax.fori_loop(0, n_chunks, phase_c, 0, unroll=2)


def _retention(rq, rk, rv, rg, dec_f, dec_b, g, s0f, s0b, want_state):
    B, L, _ = rq.shape
    has_s0 = s0f is not None
    seq = pl.BlockSpec((1, L, LANES), lambda b, hp: (b, 0, hp))
    st = pl.BlockSpec((1, 2, D_RET, D_RET), lambda b, hp: (b, hp, 0, 0))
    cst = lambda shape: pl.BlockSpec(shape, lambda b, hp: (0,) * len(shape))
    in_specs = [seq] * 4 + [cst((H_RET, LANES)), cst((H_RET, LANES)), cst((1, LANES))]
    args = [rq, rk, rv, rg, dec_f, dec_b, g]
    if has_s0:
        in_specs += [st, st]
        args += [s0f, s0b]
    out_specs = [seq]
    out_shape = [jax.ShapeDtypeStruct((B, L, D_RET_ALL), BF16)]
    if want_state:
        out_specs += [st, st]
        out_shape += [jax.ShapeDtypeStruct((B, H_RET, D_RET, D_RET), F32)] * 2
    return pl.pallas_call(
        functools.partial(_ret_kernel, has_s0, want_state, L // RET_CHUNK),
        grid=(B, H_RET // 2),
        in_specs=in_specs,
        out_specs=out_specs,
        out_shape=out_shape,
        scratch_shapes=[pltpu.VMEM((L // RET_CHUNK, 4 * D_RET, LANES), F32),
                        pltpu.VMEM((L // RET_CHUNK, 4 * D_RET, LANES), BF16)],
        compiler_params=_cparams("parallel", "parallel"),
        name="retention_s0" if has_s0 else "retention",
    )(*args)


def _pack_pairs(x):
    w = x.shape[1] // 2
    hi = lax.bitcast_convert_type(x[:, :w].astype(BF16).astype(F32), jnp.uint32)
    lo = lax.bitcast_convert_type(x[:, w:].astype(BF16).astype(F32), jnp.uint32)
    return hi | (lo >> 16)


def _unpack_pairs(p):
    hi = lax.bitcast_convert_type(p & jnp.uint32(0xFFFF0000), F32)
    lo = lax.bitcast_convert_type(p << 16, F32)
    return jnp.concatenate([hi, lo], axis=1)


D_PACKED = D_MODEL // 2


def _outproj_kernel(omla_ref, oret_ref, x_ref, mod_ref, wo_ref, n2g_ref, rwh_ref, rwl_ref, rb_ref,
                    x2_ref, h2p_ref, eid_ref, rnk_ref, wk_ref, cnt_ref, carry_ref):
    @pl.when(pl.program_id(0) == 0)
    def _():
        carry_ref[...] = jnp.zeros_like(carry_ref)

    mod = mod_ref[0]
    half = H_MLA * D_V
    a = _dot(omla_ref[...], wo_ref[:half, :]) + _dot(oret_ref[...], wo_ref[half:, :])
    x2 = x_ref[...] + mod[2:3] * a
    x2_ref[...] = x2
    h2 = _rms(x2, n2g_ref[...]) * (1.0 + mod[4:5]) + mod[3:4]
    h2p_ref[...] = _pack_pairs(h2)
    h2_hi = h2.astype(BF16)
    h2_lo = (h2 - h2_hi.astype(F32)).astype(BF16)
    logits = _dot(h2_hi, rwh_ref[...]) + _dot(h2_lo, rwh_ref[...]) + _dot(h2_hi, rwl_ref[...])
    scores = jax.nn.sigmoid(logits)
    lane = lax.broadcasted_iota(jnp.int32, scores.shape, 1)
    cand = scores + rb_ref[...]
    picked = jnp.zeros(scores.shape, jnp.bool_)
    hits = []
    for _ in range(TOP_K):
        mx = jnp.max(cand, axis=-1, keepdims=True)
        first = jnp.min(jnp.where(cand == mx, lane, LANES), axis=-1, keepdims=True)
        hit = lane == first
        hits.append((first, hit))
        picked = jnp.logical_or(picked, hit)
        cand = jnp.where(hit, -jnp.inf, cand)
    sel = jnp.where(picked, scores, 0.0)
    gates = sel * (ROUTED_SCALE / jnp.sum(sel, axis=-1, keepdims=True))

    tm = scores.shape[0]
    onehot = jnp.where(picked, 1.0, 0.0)
    earlier = (lax.broadcasted_iota(jnp.int32, (tm, tm), 0) > lax.broadcasted_iota(jnp.int32, (tm, tm), 1))
    rank = _dot(jnp.where(earlier, 1.0, 0.0).astype(BF16), onehot.astype(BF16)) + carry_ref[...]
    carry_ref[...] += jnp.sum(onehot, axis=0, keepdims=True)
    cnt_ref[...] = carry_ref[...].astype(jnp.int32)

    eid = jnp.zeros(scores.shape, jnp.int32)
    rnk = jnp.zeros(scores.shape, F32)
    wk = jnp.zeros(scores.shape, F32)
    for k, (first, hit) in enumerate(hits):
        slot = lane == k
        eid = jnp.where(slot, first, eid)
        rnk = jnp.where(slot, jnp.sum(jnp.where(hit, rank, 0.0), axis=-1, keepdims=True), rnk)
        wk = jnp.where(slot, jnp.sum(jnp.where(hit, gates, 0.0), axis=-1, keepdims=True), wk)
    eid_ref[...] = eid
    rnk_ref[...] = rnk.astype(jnp.int32)
    wk_ref[...] = wk


def _outproj(o_mla, o_ret, x, mod, mod_row0, tokens_per_row, wts, tm):
    T = x.shape[0]
    bpr = tokens_per_row // tm
    tok = lambda w: pl.BlockSpec((tm, w), lambda i: (i, 0))
    return pl.pallas_call(
        _outproj_kernel,
        grid=(T // tm,),
        in_specs=[tok(H_MLA * D_V), tok(D_RET_ALL), tok(D_MODEL),
                  pl.BlockSpec((1, 6, D_MODEL), lambda i: (mod_row0 + i // bpr, 0, 0)),
                  _const_spec((D_MODEL, D_MODEL)), _const_spec((1, D_MODEL)),
                  _const_spec((D_MODEL, LANES)), _const_spec((D_MODEL, LANES)), _const_spec((1, LANES))],
        out_specs=[tok(D_MODEL), tok(D_PACKED), tok(LANES), tok(LANES), tok(LANES), _const_spec((1, LANES))],
        out_shape=[jax.ShapeDtypeStruct((T, D_MODEL), F32),
                   jax.ShapeDtypeStruct((T, D_PACKED), jnp.uint32),
                   jax.ShapeDtypeStruct((T, LANES), jnp.int32),
                   jax.ShapeDtypeStruct((T, LANES), jnp.int32),
                   jax.ShapeDtypeStruct((T, LANES), F32),
                   jax.ShapeDtypeStruct((1, LANES), jnp.int32)],
        scratch_shapes=[pltpu.VMEM((1, LANES), F32)],
        compiler_params=_cparams("arbitrary"),
        name="outproj_router",
    )(o_mla, o_ret, x, mod, wts["w_o"], wts["norm2_g"], wts["router_hi"], wts["router_lo"],
      wts["router_bias"])


def _plan_kernel(eid_ref, rnk_ref, tab_ref, pos_ref):
    eid = eid_ref[...]
    rnk = rnk_ref[...]
    tab = tab_ref[...]
    lane = lax.broadcasted_iota(jnp.int32, eid.shape, 1)
    pos = jnp.zeros(eid.shape, jnp.int32)
    for k in range(TOP_K):
        onehot = jnp.where(lane == eid[:, k:k + 1], 1.0, 0.0).astype(BF16)
        d = _dot(onehot, tab)
        start = d[:, 0:1] * 65536.0 + d[:, 1:2] * 256.0 + d[:, 2:3]
        pos = jnp.where(lane == k, start.astype(jnp.int32) + rnk, pos)
    pos_ref[...] = pos


def _plan(eid, rnk, starts, tm):
    T = eid.shape[0]
    digits = jnp.stack([starts >> 16, (starts >> 8) & 255, starts & 255], axis=1).astype(BF16)
    tab = jnp.zeros((LANES, LANES), BF16).at[:N_EXPERTS, :3].set(digits)
    tok = pl.BlockSpec((tm, LANES), lambda i: (i, 0))
    return pl.pallas_call(
        _plan_kernel,
        grid=(T // tm,),
        in_specs=[tok, tok, _const_spec((LANES, LANES))],
        out_specs=tok,
        out_shape=jax.ShapeDtypeStruct((T, LANES), jnp.int32),
        compiler_params=_cparams("parallel"),
        name="moe_plan",
    )(eid, rnk, tab)


SC_WORKERS = 32
SC_CHUNK = 64


def _sc_worker_id():
    return lax.axis_index("s") * 2 + lax.axis_index("c")


def _sc_mesh():
    return plsc.VectorSubcoreMesh(core_axis_name="c", subcore_axis_name="s")


def _sc_gather_rows(table, idx):
    N = idx.shape[0]
    W = table.shape[1]
    per_w = N // SC_WORKERS
    n_chunks = per_w // SC_CHUNK
    assert N == SC_WORKERS * n_chunks * SC_CHUNK and n_chunks % 2 == 0

    @functools.partial(
        pl.kernel, mesh=_sc_mesh(), out_type=jax.ShapeDtypeStruct((N, W), table.dtype),
        scratch_types=[pltpu.VMEM((n_chunks, SC_CHUNK), jnp.int32),
                       pltpu.VMEM((2, SC_CHUNK, W), table.dtype),
                       pltpu.SemaphoreType.DMA((2,)), pltpu.SemaphoreType.DMA((2,))],
        name="sc_gather_rows")
    def k(table_hbm, idx_hbm, out_hbm, idx_v, rows_v, gsem, wsem):
        wid = _sc_worker_id()
        base = wid * per_w
        pltpu.sync_copy(idx_hbm.at[wid], idx_v)

        def gather(j, b):
            return pltpu.make_async_copy(table_hbm.at[idx_v.at[j]], rows_v.at[b], gsem.at[b])

        def writeback(j, b):
            off = pl.multiple_of(base + j * SC_CHUNK, SC_CHUNK)
            return pltpu.make_async_copy(rows_v.at[b], out_hbm.at[pl.ds(off, SC_CHUNK)], wsem.at[b])

        gather(0, 0).start()

        @pl.loop(0, n_chunks, step=2)
        def _(j):
            for b in range(2):
                jj = j + b

                @pl.when(jj + 1 < n_chunks)
                def _():
                    @pl.when(jj >= 1)
                    def _():
                        writeback(jj - 1, 1 - b).wait()
                    gather(jj + 1, 1 - b).start()

                gather(jj, b).wait()
                writeback(jj, b).start()

        writeback(n_chunks - 2, 0).wait()
        writeback(n_chunks - 1, 1).wait()

    return k(table, idx.reshape(SC_WORKERS, n_chunks, SC_CHUNK))


def _sc_dispatch_rows(rows_list, pos_list, n_out):
    W = rows_list[0].shape[1]
    K = pos_list[0].shape[1]
    n_src = len(rows_list)
    plan = []
    idx_list = []
    for rows, pos in zip(rows_list, pos_list):
        T = rows.shape[0]
        per_w = T // SC_WORKERS
        n_chunks = per_w // SC_CHUNK
        assert T == SC_WORKERS * n_chunks * SC_CHUNK and n_chunks % 2 == 0
        plan.append((per_w, n_chunks))
        idx_list.append(pos.reshape(SC_WORKERS, n_chunks, SC_CHUNK, K).transpose(0, 1, 3, 2))
    max_chunks = max(n for _, n in plan)

    @functools.partial(
        pl.kernel, mesh=_sc_mesh(), out_type=jax.ShapeDtypeStruct((n_out, W), rows_list[0].dtype),
        scratch_types=[pltpu.VMEM((max_chunks, K, SC_CHUNK), jnp.int32),
                       pltpu.VMEM((2, SC_CHUNK, W), rows_list[0].dtype),
                       pltpu.SemaphoreType.DMA((2,)), pltpu.SemaphoreType.DMA((2,))],
        name="sc_dispatch_rows")
    def k(*refs):
        rows_refs, idx_refs = refs[:n_src], refs[n_src:2 * n_src]
        out_hbm, idx_v, rows_v, lsem, ssem = refs[2 * n_src:]
        wid = _sc_worker_id()
        for rows_hbm, idx_hbm, (per_w, n_chunks) in zip(rows_refs, idx_refs, plan):
            base = wid * per_w
            pltpu.sync_copy(idx_hbm.at[wid], idx_v.at[pl.ds(0, n_chunks)])

            def load(j, b):
                off = pl.multiple_of(base + j * SC_CHUNK, SC_CHUNK)
                return pltpu.make_async_copy(rows_hbm.at[pl.ds(off, SC_CHUNK)], rows_v.at[b], lsem.at[b])

            def scatter(j, b, kk):
                return pltpu.make_async_copy(rows_v.at[b], out_hbm.at[idx_v.at[j, kk]], ssem.at[b])

            load(0, 0).start()

            @pl.loop(0, n_chunks, step=2)
            def _(j):
                for b in range(2):
                    jj = j + b

                    @pl.when(jj + 1 < n_chunks)
                    def _():
                        @pl.when(jj >= 1)
                        def _():
                            for kk in range(K):
                                scatter(jj - 1, 1 - b, kk).wait()
                        load(jj + 1, 1 - b).start()

                    load(jj, b).wait()
                    for kk in range(K):
                        scatter(jj, b, kk).start()

            for kk in range(K):
                scatter(n_chunks - 2, 0, kk).wait()
            for kk in range(K):
                scatter(n_chunks - 1, 1, kk).wait()

    return k(*rows_list, *idx_list)


ROW_TILE = 512
MOE_GATHER_GROUP = 8192


def _experts_kernel(te_ref, tv_ref, xs_ref, wg_ref, wu_ref, wd_ref, ys_ref, wg_s, wu_s, wd_s):
    i = pl.program_id(0)
    valid = tv_ref[i]

    @pl.when(jnp.logical_or(i == 0, te_ref[i] != te_ref[jnp.maximum(i - 1, 0)]))
    def _():
        wg_s[...] = wg_ref[0].astype(BF16)
        wu_s[...] = wu_ref[0].astype(BF16)
        wd_s[...] = wd_ref[0].astype(BF16)

    @pl.when(valid > 0)
    def _():
        row = lax.broadcasted_iota(jnp.int32, (ROW_TILE, 1), 0)
        p = jnp.where(row < valid, xs_ref[...], jnp.uint32(0))
        x = _unpack_pairs(p).astype(BF16)
        act = _silu(_dot(x, wg_s[...])) * _dot(x, wu_s[...])
        ys_ref[...] = _pack_pairs(_dot(act.astype(BF16), wd_s[...]))

    @pl.when(valid <= 0)
    def _():
        ys_ref[...] = jnp.zeros_like(ys_ref)


def _experts(xs, tile_expert, tile_valid, wts):
    n_tiles = tile_expert.shape[0]
    ex = lambda a, b: pl.BlockSpec((1, a, b), lambda i, te, tv: (te[i], 0, 0))
    row = pl.BlockSpec((ROW_TILE, D_PACKED), lambda i, te, tv: (i, 0))
    return pl.pallas_call(
        _experts_kernel,
        grid_spec=pltpu.PrefetchScalarGridSpec(
            num_scalar_prefetch=2, grid=(n_tiles,),
            in_specs=[row, ex(D_MODEL, D_EXPERT), ex(D_MODEL, D_EXPERT), ex(D_EXPERT, D_MODEL)],
            out_specs=row,
            scratch_shapes=[pltpu.VMEM((D_MODEL, D_EXPERT), BF16), pltpu.VMEM((D_MODEL, D_EXPERT), BF16),
                            pltpu.VMEM((D_EXPERT, D_MODEL), BF16)]),
        out_shape=jax.ShapeDtypeStruct(xs.shape, jnp.uint32),
        compiler_params=_cparams("arbitrary"),
        name="experts",
    )(tile_expert, tile_valid, xs, wts["exp_gate"], wts["exp_up"], wts["exp_down"])


def _combine_kernel(h2p_ref, ys_ref, wk_ref, x2_ref, mod_ref, shg_ref, shu_ref, shd_ref, *rest):
    y_ref = rest[-1]
    t = _unpack_pairs(h2p_ref[...]).astype(BF16)
    act = _silu(_dot(t, shg_ref[...])) * _dot(t, shu_ref[...])
    acc = _dot(act.astype(BF16), shd_ref[...])
    wk = wk_ref[...]
    for k in range(TOP_K):
        acc = acc + wk[:, k:k + 1] * _unpack_pairs(ys_ref[k])
    y_ref[...] = x2_ref[...] + mod_ref[0][5:6] * acc


def _combine(h2p, ys, row0, wk, x2, y_prev, mod, mod_row0, tokens_per_row, wts, tm):
    T = h2p.shape[0]
    G = ys.shape[1]
    bpr = tokens_per_row // tm
    blk0 = row0 // tm
    tok = lambda w: pl.BlockSpec((tm, w), lambda i: (blk0 + i, 0))
    in_specs = [tok(D_PACKED), pl.BlockSpec((TOP_K, tm, D_PACKED), lambda i: (0, i, 0)), tok(LANES),
                tok(D_MODEL), pl.BlockSpec((1, 6, D_MODEL), lambda i: (mod_row0 + (blk0 + i) // bpr, 0, 0)),
                _const_spec((D_MODEL, D_SHARED)), _const_spec((D_MODEL, D_SHARED)),
                _const_spec((D_SHARED, D_MODEL))]
    args = [h2p, ys, wk, x2, mod, wts["sh_gate"], wts["sh_up"], wts["sh_down"]]
    aliases = {}
    if y_prev is not None:
        aliases = {len(args): 0}
        in_specs.append(pl.BlockSpec(memory_space=pl.ANY))
        args.append(y_prev)
    return pl.pallas_call(
        _combine_kernel,
        grid=(G // tm,),
        in_specs=in_specs,
        out_specs=tok(D_MODEL),
        out_shape=jax.ShapeDtypeStruct((T, D_MODEL), F32),
        input_output_aliases=aliases,
        compiler_params=_cparams("parallel"),
        name="moe_combine",
    )(*args)


def _moe(halves, mod, wts):
    routed = [h["routed"] for h in halves]
    sizes = [r[1].shape[0] for r in routed]
    n_tiles = (TOP_K * sum(sizes)) // ROW_TILE + N_EXPERTS
    path_counts = [r[5][0, :N_EXPERTS] for r in routed]
    counts = sum(path_counts)
    padded = ((counts + ROW_TILE - 1) // ROW_TILE) * ROW_TILE
    ends = jnp.cumsum(padded)
    starts = ends - padded
    tile_start = jnp.arange(n_tiles, dtype=jnp.int32) * ROW_TILE
    tile_expert = jnp.minimum(jnp.sum(tile_start[:, None] >= ends[None, :], axis=1), N_EXPERTS - 1).astype(jnp.int32)
    tile_valid = jnp.clip(counts[tile_expert] - (tile_start - starts[tile_expert]), 0, ROW_TILE).astype(jnp.int32)

    pos, first = [], starts
    for h, r, c in zip(halves, routed, path_counts):
        pos.append(_plan(r[2], r[3], first, h["tm"])[:, :TOP_K])
        first = first + c
    xs = _sc_dispatch_rows([r[1] for r in routed], pos, n_tiles * ROW_TILE)
    ys = _experts(xs, tile_expert, tile_valid, wts)
    outs = []
    for h, r, p in zip(halves, routed, pos):
        T = r[1].shape[0]
        G = MOE_GATHER_GROUP if T % MOE_GATHER_GROUP == 0 else T
        y = None
        for row0 in range(0, T, G):
            ysel = _sc_gather_rows(ys, p[row0:row0 + G].T.reshape(-1)).reshape(TOP_K, G, D_PACKED)
            y = _combine(r[1], ysel, row0, r[4], r[0], y, mod, h["mod_row0"], h["tokens_per_row"], wts, h["tm"])
        outs.append(y.reshape(h["shape"]))
    return outs


def _pad_heads(w, d_used):
    k = w.shape[0]
    w = w.reshape(k, H_MLA, d_used)
    return jnp.pad(w, ((0, 0), (0, 0), (0, HEAD_SLOT - d_used))).reshape(k, H_MLA * HEAD_SLOT)


def _lane_pad(v):
    return jnp.pad(v, (0, LANES - v.shape[0])).reshape(1, LANES)


def _rot_partners(w):
    rot = w[..., w.shape[-1] - D_ROPE:]
    return jnp.concatenate([jnp.zeros_like(w[..., :w.shape[-1] - D_ROPE]),
                            rot[..., _swap_partner(D_ROPE, D_ROPE // 4)]], axis=-1)


def _prep_weights(l, w_in, norm1_g, mla_q_norm_g, w_uq, mla_kv_norm_g, w_ukv, mla_q_gain, mla_k_gain,
                  w_o, norm2_g, router_w, router_bias, exp_w_gate, exp_w_up, exp_w_down,
                  sh_w_gate, sh_w_up, sh_w_down):
    w = w_in[l]
    a, b, c = Q_RANK, Q_RANK + KV_RANK, Q_RANK + KV_RANK + D_ROPE
    kr_slot = lambda wk: jnp.pad(wk, ((0, 0), (D_NOPE, LANES - D_QK)))
    w1 = jnp.concatenate([w[:, :b], kr_slot(w[:, b:c]), w[:, c:]], axis=1).astype(BF16)
    ret_sw = np.concatenate([h * D_RET + _swap_partner(D_RET, D_RET // 4) for h in range(H_RET)])
    w_rq, w_rk = w[:, c:c + D_RET_ALL], w[:, c + D_RET_ALL:c + 2 * D_RET_ALL]
    w1_rope = jnp.concatenate([w1, kr_slot(_rot_partners(w[:, b:c])).astype(BF16),
                               w_rq[:, ret_sw].astype(BF16), w_rk[:, ret_sw].astype(BF16)], axis=1)
    uq = w_uq[l].reshape(Q_RANK, H_MLA, D_QK)
    pad_slots = lambda u: jnp.pad(u, ((0, 0), (0, 0), (0, HEAD_SLOT - D_QK))).reshape(Q_RANK, H_MLA * HEAD_SLOT)
    gains = lambda g: jnp.concatenate([_lane_pad(g), _lane_pad(_rot_partners(g))], axis=0)
    ukv = w_ukv[l].reshape(KV_RANK, H_MLA, D_NOPE + D_V)
    w_k = _pad_heads(ukv[:, :, :D_NOPE].reshape(KV_RANK, H_MLA * D_NOPE), D_NOPE)
    w_v = _pad_heads(ukv[:, :, D_NOPE:].reshape(KV_RANK, H_MLA * D_V), D_V)
    rw = jnp.pad(router_w[l], ((0, 0), (0, LANES - N_EXPERTS)))
    rw_hi = rw.astype(BF16)
    rw_lo = (rw - rw_hi.astype(F32)).astype(BF16)
    rb = jnp.concatenate([router_bias[l].astype(F32), jnp.full((LANES - N_EXPERTS,), -jnp.inf, F32)])
    return dict(
        norm1_g=norm1_g[l].reshape(1, D_MODEL), w1=w1, w1_rope=w1_rope,
        q_norm_g=mla_q_norm_g[l].reshape(1, Q_RANK),
        w_uq=jnp.concatenate([pad_slots(uq), pad_slots(_rot_partners(uq))], axis=1).astype(BF16),
        kv_norm_g=mla_kv_norm_g[l].reshape(1, KV_RANK),
        w_kv=jnp.concatenate([w_k, w_v], axis=1).astype(BF16),
        q_gain=gains(mla_q_gain[l]), k_gain=gains(mla_k_gain[l]),
        w_o=w_o[l].astype(BF16), norm2_g=norm2_g[l].reshape(1, D_MODEL),
        router_hi=rw_hi, router_lo=rw_lo, router_bias=rb.reshape(1, LANES),
        exp_gate=exp_w_gate[l], exp_up=exp_w_up[l], exp_down=exp_w_down[l],
        sh_gate=sh_w_gate[l].astype(BF16), sh_up=sh_w_up[l].astype(BF16),
        sh_down=sh_w_down[l].astype(BF16))


def _pick(n, prefs):
    for p in prefs:
        if n % p == 0:
            return p
    return n


def _mixer_half(x, mod, mod_row0, wts, ret_wts, ctx):
    B, L, D = x.shape
    T = B * L
    tokens_per_row = L if ctx is not None else T
    xt = x.reshape(T, D)
    rope_tabs = None
    if ctx is not None:
        rope_tabs = _rope_tables(L)
    tm = _pick(L, (512, 256, 128))
    ckv_n, krp, q, k, v, rq, rk, rv, rg = _inproj(xt, mod, mod_row0, tokens_per_row, wts, rope_tabs, tm)

    seq = lambda a: a.reshape(B, L, a.shape[-1])
    kc = vc = s0f = s0b = None
    if ctx is not None:
        ckv_c, kr_c, s0f, s0b = ctx
        Lc = ckv_c.shape[1]
        krp_c = jnp.pad(kr_c, ((0, 0), (0, 0), (D_NOPE, LANES - D_NOPE - D_ROPE)))
        kc, vc = _ctx_kv(ckv_c.reshape(B * Lc, KV_RANK), krp_c.reshape(B * Lc, LANES), wts,
                         _pick(B * Lc, (512, 256, 128)))
        kc, vc = kc.reshape(B, Lc, -1), vc.reshape(B, Lc, -1)
    o_mla = _attention(seq(q), seq(k), seq(v), kc, vc, _pick(L, (1024, 512, 256, 128)))

    dec_f, dec_b, ret_g = ret_wts
    want_state = ctx is None
    ret = _retention(seq(rq), seq(rk), seq(rv), seq(rg), dec_f, dec_b, ret_g, s0f, s0b, want_state)
    o_ret = ret[0]

    routed = _outproj(o_mla.reshape(T, -1), o_ret.reshape(T, -1), xt, mod, mod_row0, tokens_per_row, wts, tm)
    new = None
    if want_state:
        new = (ckv_n.reshape(B, L, KV_RANK), krp[:, D_NOPE:D_NOPE + D_ROPE].reshape(B, L, D_ROPE),
               ret[1], ret[2])
    return dict(routed=routed, mod_row0=mod_row0, tokens_per_row=tokens_per_row, tm=tm, shape=(B, L, D)), new


def kernel(x_prompt, x_sample, cache_mla_ckv, cache_mla_krope, state_ret_fwd, state_ret_bwd, c, c_ctx,
           w_ada, b_ada, norm1_g, w_in, mla_q_norm_g, w_uq, mla_kv_norm_g, w_ukv, mla_q_gain, mla_k_gain,
           ret_decay_fwd, ret_decay_bwd, ret_norm_g, w_o, norm2_g, router_w, router_bias, exp_w_gate,
           exp_w_up, exp_w_down, sh_w_gate, sh_w_up, sh_w_down):
    depth = w_ada.shape[0]
    n_dec = c.shape[0]
    assert 1 + n_dec <= MOD_ROWS
    cond = jnp.concatenate([c_ctx[None], c, jnp.zeros((MOD_ROWS - 1 - n_dec, D_MODEL), F32)], axis=0)

    y_prompt, y_sample = x_prompt, x_sample
    ckv_l, kr_l, sf_l, sb_l = [], [], [], []
    for l in range(depth):
        wts = _prep_weights(l, w_in, norm1_g, mla_q_norm_g, w_uq, mla_kv_norm_g, w_ukv, mla_q_gain,
                            mla_k_gain, w_o, norm2_g, router_w, router_bias, exp_w_gate, exp_w_up,
                            exp_w_down, sh_w_gate, sh_w_up, sh_w_down)
        ret_wts = (jnp.broadcast_to(ret_decay_fwd[l].astype(F32)[:, None], (H_RET, LANES)),
                   jnp.broadcast_to(ret_decay_bwd[l].astype(F32)[:, None], (H_RET, LANES)),
                   jnp.tile(ret_norm_g[l].reshape(1, D_RET), (1, LANES // D_RET)))
        mod = _adaln(cond, w_ada[l], b_ada[l]).reshape(MOD_ROWS, 6, D_MODEL)
        half_p, new = _mixer_half(y_prompt, mod, 0, wts, ret_wts, None)
        ckv_l.append(new[0]); kr_l.append(new[1]); sf_l.append(new[2]); sb_l.append(new[3])
        ctx = (cache_mla_ckv[:, l], cache_mla_krope[:, l], state_ret_fwd[:, l], state_ret_bwd[:, l])
        half_s, _ = _mixer_half(y_sample, mod, 1, wts, ret_wts, ctx)
        y_prompt, y_sample = _moe([half_p, half_s], mod, wts)

    return (y_prompt, y_sample, jnp.stack(ckv_l, axis=1), jnp.stack(kr_l, axis=1),
            jnp.stack(sf_l, axis=1), jnp.stack(sb_l, axis=1))
```

```python
import functools
import math

import numpy as np
import jax
import jax.numpy as jnp
from jax import lax
from jax.experimental import pallas as pl
from jax.experimental.pallas import tpu as pltpu
from jax.experimental.pallas import tpu_sc as plsc

F32 = jnp.float32
BF16 = jnp.bfloat16

D_MODEL = 1024
GRID_W = 64
H_MLA = 8
D_NOPE = 64
D_ROPE = 32
D_QK = D_NOPE + D_ROPE
D_V = 64
Q_RANK = 256
KV_RANK = 128
H_RET = 8
D_RET = 64
RET_CHUNK = 128
D_RET_ALL = H_RET * D_RET
N_EXPERTS = 64
TOP_K = 6
D_EXPERT = 256
D_SHARED = 256
ROUTED_SCALE = 2.5
ROPE_BASE = 10000.0
EPS = 1e-6
LOG2_E = math.log2(math.e)

LANES = 128
HEAD_SLOT = LANES
D_IN_PAD = Q_RANK + KV_RANK + LANES + 4 * D_RET_ALL
MOD_ROWS = 16
VMEM_LIMIT = 56 * 1024 * 1024


def _cparams(*sem):
    return pltpu.CompilerParams(dimension_semantics=sem, vmem_limit_bytes=VMEM_LIMIT)


def _dot(a, b):
    return jnp.dot(a, b, preferred_element_type=F32)


def _dot_nt(a, b):
    return lax.dot_general(a, b, (((1,), (1,)), ((), ())), preferred_element_type=F32)


def _dot_tn(a, b):
    return lax.dot_general(a, b, (((0,), (0,)), ((), ())), preferred_element_type=F32)


def _rms(x, g):
    return x * lax.rsqrt(jnp.mean(x * x, axis=-1, keepdims=True) + EPS) * g


def _silu(x):
    return x * jax.nn.sigmoid(x)


def _adaln_kernel(c_ref, w_ref, b_ref, o_ref):
    s = _silu(c_ref[...])
    o_ref[...] = _dot(s.astype(BF16), w_ref[...].astype(BF16)) + b_ref[...]


def _adaln(cond, w_ada, b_ada):
    n_out = w_ada.shape[1]
    bn = 1536
    return pl.pallas_call(
        _adaln_kernel,
        grid=(n_out // bn,),
        in_specs=[pl.BlockSpec((MOD_ROWS, D_MODEL), lambda j: (0, 0)),
                  pl.BlockSpec((D_MODEL, bn), lambda j: (0, j)),
                  pl.BlockSpec((1, bn), lambda j: (0, j))],
        out_specs=pl.BlockSpec((MOD_ROWS, bn), lambda j: (0, j)),
        out_shape=jax.ShapeDtypeStruct((MOD_ROWS, n_out), F32),
        compiler_params=_cparams("arbitrary"),
        name="adaln",
    )(cond, w_ada, b_ada.reshape(1, n_out))


def _swap_partner(n_dims, half_pair):
    j = np.arange(n_dims)
    return np.where((j % (2 * half_pair)) < half_pair, j + half_pair, j - half_pair)


def _rope_tables(n_tokens):
    t = np.arange(n_tokens)
    row = (t // GRID_W).astype(np.float64)
    col = (t % GRID_W).astype(np.float64)

    def axis_tables(width, lane0, period, fill):
        p = width // 2
        inv = 1.0 / (ROPE_BASE ** (np.arange(p, dtype=np.float64) / p))
        cos = np.full((n_tokens, LANES), fill)
        sin = np.zeros((n_tokens, LANES))
        starts = range(lane0, LANES, period) if period else (lane0,)
        for s0 in starts:
            for base, pos in ((s0, row), (s0 + width, col)):
                ang = pos[:, None] * inv[None, :]
                c, sn = np.cos(ang), np.sin(ang)
                cos[:, base:base + p] = c
                cos[:, base + p:base + 2 * p] = c
                sin[:, base:base + p] = -sn
                sin[:, base + p:base + 2 * p] = sn
        return cos, sin

    qk_cos, qk_sin = axis_tables(D_ROPE // 2, D_NOPE, 0, 1.0)
    ret_cos, ret_sin = axis_tables(D_RET // 2, 0, D_RET, 1.0)
    as_f32 = lambda *a: tuple(jnp.asarray(x, F32) for x in a)
    return as_f32(qk_cos, qk_sin), as_f32(ret_cos, ret_sin)


def _head_sums(x):
    r = lax.broadcasted_iota(jnp.int32, (2 * HEAD_SLOT, 2 * HEAD_SLOT), 0) // HEAD_SLOT
    c = lax.broadcasted_iota(jnp.int32, (2 * HEAD_SLOT, 2 * HEAD_SLOT), 1) // HEAD_SLOT
    ones_bd = jnp.where(r == c, 1.0, 0.0).astype(BF16)
    xb = x.astype(BF16)
    w = 2 * HEAD_SLOT
    return jnp.concatenate([_dot(xb[:, g * w:(g + 1) * w], ones_bd) for g in range(x.shape[1] // w)], axis=1)


def _tile_heads(v):
    return jnp.concatenate([v] * H_MLA, axis=1)


def _kv_heads(ckv_n, krp, krp_sw, wkv_ref, kgain, kgain_sw, tabs, k_ref, v_ref):
    kvp = _dot(ckv_n.astype(BF16), wkv_ref[...])
    kn = kvp[:, :H_MLA * HEAD_SLOT]
    ms = _head_sums(kn * kn + _tile_heads(krp * krp)) * (1.0 / D_QK)
    r = lax.rsqrt(ms + EPS)
    k = (kn + _tile_heads(krp)) * r * _tile_heads(kgain)
    if tabs is not None:
        cos, sin = tabs
        k = k * _tile_heads(cos) + r * _tile_heads(krp_sw * kgain_sw * sin)
    k_ref[...] = k.astype(BF16)
    v_lane = lax.broadcasted_iota(jnp.int32, (1, H_MLA * HEAD_SLOT), 1) % HEAD_SLOT
    v_ref[...] = (kvp[:, H_MLA * HEAD_SLOT:] + jnp.where(v_lane == D_V, 1.0, 0.0)).astype(BF16)


def _inproj_kernel(rope, *refs):
    (x_ref, mod_ref, n1g_ref, w1_ref, qng_ref, wuq_ref, kvg_ref, wkv_ref, qgain_ref,
     kgain_ref) = refs[:10]
    refs = refs[10:]
    if rope:
        qk_tabs = tuple(r[...] for r in refs[:2])
        ret_cos, ret_sin = (r[...] for r in refs[2:4])
        refs = refs[4:]
    else:
        qk_tabs = None
    ckvn_ref, krp_ref, q_ref, k_ref, v_ref, rq_ref, rk_ref, rv_ref, rg_ref = refs

    mod = mod_ref[0]
    h = _rms(x_ref[...], n1g_ref[...]) * (1.0 + mod[1:2]) + mod[0:1]
    z = _dot(h.astype(BF16), w1_ref[...])
    o = 0
    cq = z[:, o:o + Q_RANK]; o += Q_RANK
    ckv = z[:, o:o + KV_RANK]; o += KV_RANK
    krp = z[:, o:o + LANES]; o += LANES
    rq = z[:, o:o + D_RET_ALL]; o += D_RET_ALL
    rk = z[:, o:o + D_RET_ALL]; o += D_RET_ALL
    rv = z[:, o:o + D_RET_ALL]; o += D_RET_ALL
    rg = z[:, o:o + D_RET_ALL]; o += D_RET_ALL
    krp_sw = None
    if rope:
        krp_sw = z[:, o:o + LANES]; o += LANES
        rq_sw = z[:, o:o + D_RET_ALL]; o += D_RET_ALL
        rk_sw = z[:, o:o + D_RET_ALL]

    ckv_n = _rms(ckv, kvg_ref[...])
    ckvn_ref[...] = ckv_n
    krp_ref[...] = krp

    qn = _rms(cq, qng_ref[...]).astype(BF16)
    scale = D_QK ** -0.5 * LOG2_E
    q = _dot(qn, wuq_ref[:, :H_MLA * HEAD_SLOT])
    r = lax.rsqrt(_head_sums(q * q) * (1.0 / D_QK) + EPS)
    q = q * r * _tile_heads(qgain_ref[0:1] * scale)
    if rope:
        cos, sin = qk_tabs
        q_sw = _dot(qn, wuq_ref[:, H_MLA * HEAD_SLOT:])
        q = q * _tile_heads(cos) + q_sw * r * _tile_heads(qgain_ref[1:2] * scale * sin)
    q_ref[...] = q.astype(BF16)

    _kv_heads(ckv_n, krp, krp_sw, wkv_ref, kgain_ref[0:1], kgain_ref[1:2], qk_tabs, k_ref, v_ref)

    for t in range(D_RET_ALL // LANES):
        sl = slice(t * LANES, (t + 1) * LANES)
        rq_t, rk_t = rq[:, sl], rk[:, sl]
        if rope:
            rq_t = rq_t * ret_cos + rq_sw[:, sl] * ret_sin
            rk_t = rk_t * ret_cos + rk_sw[:, sl] * ret_sin
        rq_ref[:, sl] = rq_t.astype(BF16)
        rk_ref[:, sl] = (rk_t * (D_RET ** -0.5)).astype(BF16)
    rv_ref[...] = rv.astype(BF16)
    rg_ref[...] = _silu(rg).astype(BF16)


def _const_spec(shape):
    return pl.BlockSpec(shape, lambda i: (0,) * len(shape))


def _inproj(x, mod, mod_row0, tokens_per_row, wts, rope_tabs, tm):
    T = x.shape[0]
    bpr = tokens_per_row // tm
    rope = rope_tabs is not None
    tok = lambda w: pl.BlockSpec((tm, w), lambda i: (i, 0))
    in_specs = [tok(D_MODEL),
                pl.BlockSpec((1, 6, D_MODEL), lambda i: (mod_row0 + i // bpr, 0, 0)),
                _const_spec((1, D_MODEL)),
                _const_spec((D_MODEL, D_IN_PAD + (LANES + 2 * D_RET_ALL if rope else 0))),
                _const_spec((1, Q_RANK)),
                _const_spec((Q_RANK, 2 * H_MLA * HEAD_SLOT)),
                _const_spec((1, KV_RANK)),
                _const_spec((KV_RANK, 2 * H_MLA * HEAD_SLOT)),
                _const_spec((2, LANES)),
                _const_spec((2, LANES))]
    args = [x, mod, wts["norm1_g"], wts["w1_rope" if rope else "w1"], wts["q_norm_g"], wts["w_uq"],
            wts["kv_norm_g"], wts["w_kv"], wts["q_gain"], wts["k_gain"]]
    if rope:
        qk_tabs, ret_tabs = rope_tabs
        nb = qk_tabs[0].shape[0] // tm
        in_specs += [pl.BlockSpec((tm, LANES), lambda i: (i % nb, 0))] * 4
        args += list(qk_tabs) + list(ret_tabs)
    widths = [(KV_RANK, F32), (LANES, F32), (H_MLA * HEAD_SLOT, BF16), (H_MLA * HEAD_SLOT, BF16),
              (H_MLA * HEAD_SLOT, BF16)] + [(D_RET_ALL, BF16)] * 4
    return pl.pallas_call(
        functools.partial(_inproj_kernel, rope),
        grid=(T // tm,),
        in_specs=in_specs,
        out_specs=[tok(w) for w, _ in widths],
        out_shape=[jax.ShapeDtypeStruct((T, w), dt) for w, dt in widths],
        compiler_params=_cparams("parallel"),
        name="inproj_rope" if rope else "inproj",
    )(*args)


def _ctx_kv_kernel(ckv_ref, krp_ref, wkv_ref, kgain_ref, k_ref, v_ref):
    _kv_heads(ckv_ref[...], krp_ref[...], None, wkv_ref, kgain_ref[0:1], None, None, k_ref, v_ref)


def _ctx_kv(ckv, krp, wts, tm):
    T = ckv.shape[0]
    tok = lambda w: pl.BlockSpec((tm, w), lambda i: (i, 0))
    return pl.pallas_call(
        _ctx_kv_kernel,
        grid=(T // tm,),
        in_specs=[tok(KV_RANK), tok(LANES),
                  _const_spec((KV_RANK, 2 * H_MLA * HEAD_SLOT)), _const_spec((2, LANES))],
        out_specs=[tok(H_MLA * HEAD_SLOT), tok(H_MLA * HEAD_SLOT)],
        out_shape=[jax.ShapeDtypeStruct((T, H_MLA * HEAD_SLOT), BF16),
                   jax.ShapeDtypeStruct((T, H_MLA * HEAD_SLOT), BF16)],
        compiler_params=_cparams("parallel"),
        name="ctx_kv",
    )(ckv, krp, wts["w_kv"], wts["k_gain"])


ATTN_KEY_CHUNK = 2048


def _attn_kernel(has_ctx, *refs):
    if has_ctx:
        q_ref, k_ref, v_ref, kc_ref, vc_ref, o_ref = refs
    else:
        q_ref, k_ref, v_ref, o_ref = refs
    sources = [(k_ref, v_ref)] + ([(kc_ref, vc_ref)] if has_ctx else [])
    chunks = []
    for kr, vr in sources:
        n = kr.shape[1]
        kc = _pick(n, (ATTN_KEY_CHUNK, 256, 128))
        chunks += [(kr, vr, slice(c * kc, (c + 1) * kc)) for c in range(n // kc)]
    tq = q_ref.shape[1]
    outs = []
    for j in range(2):
        hs = slice(j * HEAD_SLOT, (j + 1) * HEAD_SLOT)
        q = q_ref[0, :, hs]
        m = jnp.full((tq, 1), -jnp.inf, F32)
        acc = jnp.zeros((tq, HEAD_SLOT), F32)
        for kr, vr, rows in chunks:
            s = _dot_nt(q, kr[0, rows, hs])
            m_new = jnp.maximum(m, jnp.max(s, axis=-1, keepdims=True))
            p = jnp.exp2(s - m_new).astype(BF16)
            acc = acc * jnp.exp2(m - m_new) + _dot(p, vr[0, rows, hs])
            m = m_new
        outs.append(acc[:, :D_V] / acc[:, D_V:D_V + 1])
    o_ref[0] = jnp.concatenate(outs, axis=-1).astype(BF16)


def _attention(q, k, v, kc, vc, tq):
    B, L, _ = q.shape
    has_ctx = kc is not None
    in_specs = [pl.BlockSpec((1, tq, 2 * HEAD_SLOT), lambda b, hp, i: (b, i, hp)),
                pl.BlockSpec((1, L, 2 * HEAD_SLOT), lambda b, hp, i: (b, 0, hp)),
                pl.BlockSpec((1, L, 2 * HEAD_SLOT), lambda b, hp, i: (b, 0, hp))]
    args = [q, k, v]
    if has_ctx:
        Lc = kc.shape[1]
        in_specs += [pl.BlockSpec((1, Lc, 2 * HEAD_SLOT), lambda b, hp, i: (b, 0, hp)),
                     pl.BlockSpec((1, Lc, 2 * HEAD_SLOT), lambda b, hp, i: (b, 0, hp))]
        args += [kc, vc]
    return pl.pallas_call(
        functools.partial(_attn_kernel, has_ctx),
        grid=(B, H_MLA // 2, L // tq),
        in_specs=in_specs,
        out_specs=pl.BlockSpec((1, tq, 2 * D_V), lambda b, hp, i: (b, i, hp)),
        out_shape=jax.ShapeDtypeStruct((B, L, H_MLA * D_V), BF16),
        compiler_params=_cparams("parallel", "parallel", "arbitrary"),
        name="attn_ctx" if has_ctx else "attn",
    )(*args)


RET_UNROLL = 8


def _ret_kernel(has_s0, want_state, n_chunks, *refs):
    rq_ref, rk_ref, rv_ref, rg_ref, decf_ref, decb_ref, g_ref = refs[:7]
    refs = refs[7:]
    if has_s0:
        s0f_ref, s0b_ref = refs[:2]
        refs = refs[2:]
    o_ref = refs[0]
    refs = refs[1:]
    if want_state:
        sf_ref, sb_ref = refs[:2]
        refs = refs[2:]
    kv_ref, sp_ref = refs

    C = RET_CHUNK
    W = 2 * D_RET
    hp = pl.program_id(1)
    lane = lax.broadcasted_iota(jnp.int32, (1, W), 1)
    lane_h1 = lane >= D_RET
    row_h1 = lax.broadcasted_iota(jnp.int32, (W, 1), 0) >= D_RET
    blockdiag = row_h1 == lane_h1
    pos = lax.broadcasted_iota(jnp.int32, (C, 1), 0).astype(F32)
    diff = (lax.broadcasted_iota(jnp.int32, (C, C), 0)
            - lax.broadcasted_iota(jnp.int32, (C, C), 1)).astype(F32)

    lg = {}
    for name, ref in (("f", decf_ref), ("b", decb_ref)):
        a = -jnp.exp(ref[pl.ds(2 * hp, 1), :])
        b = -jnp.exp(ref[pl.ds(2 * hp + 1, 1), :])
        lg[name] = (a[:, :1], b[:, :1], jnp.where(lane_h1, b, a))
    lgf0, lgf1, lgf = lg["f"]
    lgb0, lgb1, lgb = lg["b"]
    qdf, kdf = jnp.exp(lgf * (pos + 1.0)), jnp.exp(lgf * (C - 1.0 - pos))
    qdb, kdb = jnp.exp(lgb * (C - pos)), jnp.exp(lgb * pos)
    row_h1_full = lax.broadcasted_iota(jnp.int32, (W, W), 0) >= D_RET
    cdf = jnp.where(row_h1_full, jnp.exp(lgf1 * C), jnp.exp(lgf0 * C))
    cdb = jnp.where(row_h1_full, jnp.exp(lgb1 * C), jnp.exp(lgb0 * C))

    def intra_decay(lf, lb):
        return jnp.where(diff >= 0, jnp.exp(lf * jnp.maximum(diff, 0.0)), jnp.exp(lb * jnp.maximum(-diff, 0.0)))

    dmat0, dmat1 = intra_decay(lgf0, lgb0), intra_decay(lgf1, lgb1)
    g = g_ref[...]

    def chunk_rows(n):
        return pl.ds(pl.multiple_of(n * C, C), C)

    def phase_a(n, carry):
        rows = chunk_rows(n)
        k = rk_ref[0, rows, :].astype(F32)
        k2 = jnp.concatenate([(k * kdf).astype(BF16), (k * kdb).astype(BF16)], axis=1)
        kv_ref[n] = _dot_tn(k2, rv_ref[0, rows, :])
        return carry

    unroll = min(RET_UNROLL, n_chunks)
    lax.fori_loop(0, n_chunks, phase_a, 0, unroll=unroll)

    def init_state(ref):
        if not has_s0:
            return jnp.zeros((W, W), F32)
        z = jnp.zeros((D_RET, D_RET), F32)
        return jnp.concatenate([jnp.concatenate([ref[0, 0], z], axis=1),
                                jnp.concatenate([z, ref[0, 1]], axis=1)], axis=0)

    def scan_f(n, S):
        sp_ref[n, 0:W, :] = S.astype(BF16)
        return cdf * S + jnp.where(blockdiag, kv_ref[n, 0:W, :], 0.0)

    def scan_b(i, S):
        n = n_chunks - 1 - i
        sp_ref[n, W:2 * W, :] = S.astype(BF16)
        return cdb * S + jnp.where(blockdiag, kv_ref[n, W:2 * W, :], 0.0)

    Sf = lax.fori_loop(0, n_chunks, scan_f, init_state(s0f_ref if has_s0 else None))
    Sb = lax.fori_loop(0, n_chunks, scan_b, init_state(s0b_ref if has_s0 else None))
    if want_state:
        for S, ref in ((Sf, sf_ref), (Sb, sb_ref)):
            ref[0, 0] = S[0:D_RET, 0:D_RET]
            ref[0, 1] = S[D_RET:W, D_RET:W]

    def phase_c(n, carry):
        rows = chunk_rows(n)
        q = rq_ref[0, rows, :]
        k = rk_ref[0, rows, :]
        v = rv_ref[0, rows, :]
        zero = jnp.zeros_like(q)
        sc0 = _dot_nt(jnp.where(lane_h1, zero, q), k) * dmat0
        sc1 = _dot_nt(jnp.where(lane_h1, q, zero), k) * dmat1
        p = jnp.concatenate([sc0.astype(BF16), sc1.astype(BF16)], axis=1)
        v_bd = jnp.concatenate([jnp.where(lane_h1, zero, v), jnp.where(lane_h1, v, zero)], axis=0)
        qf = q.astype(F32)
        q2 = jnp.concatenate([(qf * qdf).astype(BF16), (qf * qdb).astype(BF16)], axis=1)
        o = _dot(p, v_bd) + _dot(q2, sp_ref[n])
        o2 = o * o
        ss0 = jnp.sum(jnp.where(lane_h1, 0.0, o2), axis=-1, keepdims=True)
        ss1 = jnp.sum(jnp.where(lane_h1, o2, 0.0), axis=-1, keepdims=True)
        ms = jnp.where(lane_h1, ss1, ss0) * (1.0 / D_RET)
        o = o * lax.rsqrt(ms + EPS) * g * rg_ref[0, rows, :].astype(F32)
        o_ref[0, rows, :] = o.astype(BF16)
        return carry

    lax.fori_loop(0, n_chunks, phase_c, 0, unroll=unroll)


def _retention(rq, rk, rv, rg, dec_f, dec_b, g, s0f, s0b, want_state):
    B, L, _ = rq.shape
    has_s0 = s0f is not None
    seq = pl.BlockSpec((1, L, LANES), lambda b, hp: (b, 0, hp))
    st = pl.BlockSpec((1, 2, D_RET, D_RET), lambda b, hp: (b, hp, 0, 0))
    cst = lambda shape: pl.BlockSpec(shape, lambda b, hp: (0,) * len(shape))
    in_specs = [seq] * 4 + [cst((H_RET, LANES)), cst((H_RET, LANES)), cst((1, LANES))]
    args = [rq, rk, rv, rg, dec_f, dec_b, g]
    if has_s0:
        in_specs += [st, st]
        args += [s0f, s0b]
    out_specs = [seq]
    out_shape = [jax.ShapeDtypeStruct((B, L, D_RET_ALL), BF16)]
    if want_state:
        out_specs += [st, st]
        out_shape += [jax.ShapeDtypeStruct((B, H_RET, D_RET, D_RET), F32)] * 2
    return pl.pallas_call(
        functools.partial(_ret_kernel, has_s0, want_state, L // RET_CHUNK),
        grid=(B, H_RET // 2),
        in_specs=in_specs,
        out_specs=out_specs,
        out_shape=out_shape,
        scratch_shapes=[pltpu.VMEM((L // RET_CHUNK, 4 * D_RET, LANES), F32),
                        pltpu.VMEM((L // RET_CHUNK, 4 * D_RET, LANES), BF16)],
        compiler_params=_cparams("parallel", "parallel"),
        name="retention_s0" if has_s0 else "retention",
    )(*args)


def _pack_pairs(x):
    w = x.shape[1] // 2
    hi = lax.bitcast_convert_type(x[:, :w].astype(BF16).astype(F32), jnp.uint32)
    lo = lax.bitcast_convert_type(x[:, w:].astype(BF16).astype(F32), jnp.uint32)
    return hi | (lo >> 16)


def _unpack_pairs(p):
    hi = lax.bitcast_convert_type(p & jnp.uint32(0xFFFF0000), F32)
    lo = lax.bitcast_convert_type(p << 16, F32)
    return jnp.concatenate([hi, lo], axis=1)


D_PACKED = D_MODEL // 2


def _outproj_kernel(omla_ref, oret_ref, x_ref, mod_ref, wo_ref, n2g_ref, rwh_ref, rwl_ref, rb_ref, tri_ref,
                    x2_ref, h2p_ref, code_ref, rank_ref, wk_ref, cnt_ref, carry_ref):
    @pl.when(pl.program_id(0) == 0)
    def _():
        carry_ref[...] = jnp.zeros_like(carry_ref)

    mod = mod_ref[0]
    half = H_MLA * D_V
    a = _dot(omla_ref[...], wo_ref[:half, :]) + _dot(oret_ref[...], wo_ref[half:, :])
    x2 = x_ref[...] + mod[2:3] * a
    x2_ref[...] = x2
    h2 = _rms(x2, n2g_ref[...]) * (1.0 + mod[4:5]) + mod[3:4]
    h2p_ref[...] = _pack_pairs(h2)
    h2_hi = h2.astype(BF16)
    h2_lo = (h2 - h2_hi.astype(F32)).astype(BF16)
    logits = _dot(h2_hi, rwh_ref[...]) + _dot(h2_lo, rwh_ref[...]) + _dot(h2_hi, rwl_ref[...])
    scores = jax.nn.sigmoid(logits)
    lane = lax.broadcasted_iota(jnp.int32, scores.shape, 1)
    cand = scores + rb_ref[...]
    picked = jnp.zeros(scores.shape, jnp.bool_)
    hits = []
    for _ in range(TOP_K):
        mx = jnp.max(cand, axis=-1, keepdims=True)
        first = jnp.min(jnp.where(cand == mx, lane, LANES), axis=-1, keepdims=True)
        hit = lane == first
        hits.append(hit)
        picked = jnp.logical_or(picked, hit)
        cand = jnp.where(hit, -jnp.inf, cand)
    sel = jnp.where(picked, scores, 0.0)
    gates = sel * (ROUTED_SCALE / jnp.sum(sel, axis=-1, keepdims=True))

    onehot = jnp.where(picked, 1.0, 0.0)
    rank_ref[...] = _dot(tri_ref[...], onehot.astype(BF16)) + carry_ref[...]
    carry_ref[...] += jnp.sum(onehot, axis=0, keepdims=True)
    cnt_ref[...] = carry_ref[...].astype(jnp.int32)

    code = jnp.zeros(scores.shape, jnp.int32)
    wk = jnp.zeros(scores.shape, F32)
    for k, hit in enumerate(hits):
        code = jnp.where(hit, k + 1, code)
        wk = jnp.where(lane == k, jnp.sum(jnp.where(hit, gates, 0.0), axis=-1, keepdims=True), wk)
    code_ref[...] = code
    wk_ref[...] = wk


def _outproj(o_mla, o_ret, x, mod, mod_row0, tokens_per_row, wts, tm):
    T = x.shape[0]
    bpr = tokens_per_row // tm
    tok = lambda w: pl.BlockSpec((tm, w), lambda i: (i, 0))
    tri = jnp.tril(jnp.ones((tm, tm), BF16), -1)
    return pl.pallas_call(
        _outproj_kernel,
        grid=(T // tm,),
        in_specs=[tok(H_MLA * D_V), tok(D_RET_ALL), tok(D_MODEL),
                  pl.BlockSpec((1, 6, D_MODEL), lambda i: (mod_row0 + i // bpr, 0, 0)),
                  _const_spec((D_MODEL, D_MODEL)), _const_spec((1, D_MODEL)),
                  _const_spec((D_MODEL, LANES)), _const_spec((D_MODEL, LANES)), _const_spec((1, LANES)),
                  _const_spec((tm, tm))],
        out_specs=[tok(D_MODEL), tok(D_PACKED), tok(LANES), tok(LANES), tok(LANES), _const_spec((1, LANES))],
        out_shape=[jax.ShapeDtypeStruct((T, D_MODEL), F32),
                   jax.ShapeDtypeStruct((T, D_PACKED), jnp.uint32),
                   jax.ShapeDtypeStruct((T, LANES), jnp.int32),
                   jax.ShapeDtypeStruct((T, LANES), F32),
                   jax.ShapeDtypeStruct((T, LANES), F32),
                   jax.ShapeDtypeStruct((1, LANES), jnp.int32)],
        scratch_shapes=[pltpu.VMEM((1, LANES), F32)],
        compiler_params=_cparams("arbitrary"),
        name="outproj_router",
    )(o_mla, o_ret, x, mod, wts["w_o"], wts["norm2_g"], wts["router_hi"], wts["router_lo"],
      wts["router_bias"], tri)


def _plan_kernel(code_ref, rank_ref, first_ref, pos_ref):
    code = code_ref[...]
    row = rank_ref[...] + first_ref[...]
    lane = lax.broadcasted_iota(jnp.int32, code.shape, 1)
    pos = jnp.zeros(code.shape, F32)
    for k in range(TOP_K):
        pos = jnp.where(lane == k, jnp.sum(jnp.where(code == k + 1, row, 0.0), axis=-1, keepdims=True), pos)
    pos_ref[...] = pos.astype(jnp.int32)


def _plan(code, rank, first_rows, tm):
    T = code.shape[0]
    first = jnp.pad(first_rows.astype(F32), (0, LANES - N_EXPERTS)).reshape(1, LANES)
    tok = pl.BlockSpec((tm, LANES), lambda i: (i, 0))
    return pl.pallas_call(
        _plan_kernel,
        grid=(T // tm,),
        in_specs=[tok, tok, _const_spec((1, LANES))],
        out_specs=tok,
        out_shape=jax.ShapeDtypeStruct((T, LANES), jnp.int32),
        compiler_params=_cparams("parallel"),
        name="moe_plan",
    )(code, rank, first)


SC_WORKERS = 32
SC_CHUNK = 64


def _sc_worker_id():
    return lax.axis_index("s") * 2 + lax.axis_index("c")


def _sc_mesh():
    return plsc.VectorSubcoreMesh(core_axis_name="c", subcore_axis_name="s")


def _sc_gather_rows(table, idx):
    N = idx.shape[0]
    W = table.shape[1]
    per_w = N // SC_WORKERS
    n_chunks = per_w // SC_CHUNK
    assert N == SC_WORKERS * n_chunks * SC_CHUNK and n_chunks % 2 == 0

    @functools.partial(
        pl.kernel, mesh=_sc_mesh(), out_type=jax.ShapeDtypeStruct((N, W), table.dtype),
        scratch_types=[pltpu.VMEM((n_chunks, SC_CHUNK), jnp.int32),
                       pltpu.VMEM((2, SC_CHUNK, W), table.dtype),
                       pltpu.SemaphoreType.DMA((2,)), pltpu.SemaphoreType.DMA((2,))],
        name="sc_gather_rows")
    def k(table_hbm, idx_hbm, out_hbm, idx_v, rows_v, gsem, wsem):
        wid = _sc_worker_id()
        base = wid * per_w
        pltpu.sync_copy(idx_hbm.at[wid], idx_v)

        def gather(j, b):
            return pltpu.make_async_copy(table_hbm.at[idx_v.at[j]], rows_v.at[b], gsem.at[b])

        def writeback(j, b):
            off = pl.multiple_of(base + j * SC_CHUNK, SC_CHUNK)
            return pltpu.make_async_copy(rows_v.at[b], out_hbm.at[pl.ds(off, SC_CHUNK)], wsem.at[b])

        gather(0, 0).start()

        @pl.loop(0, n_chunks, step=2)
        def _(j):
            for b in range(2):
                jj = j + b

                @pl.when(jj + 1 < n_chunks)
                def _():
                    @pl.when(jj >= 1)
                    def _():
                        writeback(jj - 1, 1 - b).wait()
                    gather(jj + 1, 1 - b).start()

                gather(jj, b).wait()
                writeback(jj, b).start()

        writeback(n_chunks - 2, 0).wait()
        writeback(n_chunks - 1, 1).wait()

    return k(table, idx.reshape(SC_WORKERS, n_chunks, SC_CHUNK))


def _sc_dispatch_rows(rows_list, pos_list, n_out):
    W = rows_list[0].shape[1]
    K = pos_list[0].shape[1]
    n_src = len(rows_list)
    plan = []
    idx_list = []
    for rows, pos in zip(rows_list, pos_list):
        T = rows.shape[0]
        per_w = T // SC_WORKERS
        n_chunks = per_w // SC_CHUNK
        assert T == SC_WORKERS * n_chunks * SC_CHUNK and n_chunks % 2 == 0
        plan.append((per_w, n_chunks))
        idx_list.append(pos.reshape(SC_WORKERS, n_chunks, SC_CHUNK, K).transpose(0, 1, 3, 2))
    max_chunks = max(n for _, n in plan)

    @functools.partial(
        pl.kernel, mesh=_sc_mesh(), out_type=jax.ShapeDtypeStruct((n_out, W), rows_list[0].dtype),
        scratch_types=[pltpu.VMEM((max_chunks, K, SC_CHUNK), jnp.int32),
                       pltpu.VMEM((2, SC_CHUNK, W), rows_list[0].dtype),
                       pltpu.SemaphoreType.DMA((2,)), pltpu.SemaphoreType.DMA((2,))],
        name="sc_dispatch_rows")
    def k(*refs):
        rows_refs, idx_refs = refs[:n_src], refs[n_src:2 * n_src]
        out_hbm, idx_v, rows_v, lsem, ssem = refs[2 * n_src:]
        wid = _sc_worker_id()
        for rows_hbm, idx_hbm, (per_w, n_chunks) in zip(rows_refs, idx_refs, plan):
            base = wid * per_w
            pltpu.sync_copy(idx_hbm.at[wid], idx_v.at[pl.ds(0, n_chunks)])

            def load(j, b):
                off = pl.multiple_of(base + j * SC_CHUNK, SC_CHUNK)
                return pltpu.make_async_copy(rows_hbm.at[pl.ds(off, SC_CHUNK)], rows_v.at[b], lsem.at[b])

            def scatter(j, b, kk):
                return pltpu.make_async_copy(rows_v.at[b], out_hbm.at[idx_v.at[j, kk]], ssem.at[b])

            load(0, 0).start()

            @pl.loop(0, n_chunks, step=2)
            def _(j):
                for b in range(2):
                    jj = j + b

                    @pl.when(jj + 1 < n_chunks)
                    def _():
                        @pl.when(jj >= 1)
                        def _():
                            for kk in range(K):
                                scatter(jj - 1, 1 - b, kk).wait()
                        load(jj + 1, 1 - b).start()

                    load(jj, b).wait()
                    for kk in range(K):
                        scatter(jj, b, kk).start()

            for kk in range(K):
                scatter(n_chunks - 2, 0, kk).wait()
            for kk in range(K):
                scatter(n_chunks - 1, 1, kk).wait()

    return k(*rows_list, *idx_list)


ROW_TILE = 512
MOE_GATHER_GROUP = 8192


def _experts_kernel(te_ref, tv_ref, xs_ref, wg_ref, wu_ref, wd_ref, ys_ref, wg_s, wu_s, wd_s):
    i = pl.program_id(0)
    valid = tv_ref[i]

    @pl.when(jnp.logical_or(i == 0, te_ref[i] != te_ref[jnp.maximum(i - 1, 0)]))
    def _():
        wg_s[...] = wg_ref[0].astype(BF16)
        wu_s[...] = wu_ref[0].astype(BF16)
        wd_s[...] = wd_ref[0].astype(BF16)

    @pl.when(valid > 0)
    def _():
        row = lax.broadcasted_iota(jnp.int32, (ROW_TILE, 1), 0)
        p = jnp.where(row < valid, xs_ref[...], jnp.uint32(0))
        x = _unpack_pairs(p).astype(BF16)
        act = _silu(_dot(x, wg_s[...])) * _dot(x, wu_s[...])
        ys_ref[...] = _pack_pairs(_dot(act.astype(BF16), wd_s[...]))

    @pl.when(valid <= 0)
    def _():
        ys_ref[...] = jnp.zeros_like(ys_ref)


def _experts(xs, tile_expert, tile_valid, wts):
    n_tiles = tile_expert.shape[0]
    ex = lambda a, b: pl.BlockSpec((1, a, b), lambda i, te, tv: (te[i], 0, 0))
    row = pl.BlockSpec((ROW_TILE, D_PACKED), lambda i, te, tv: (i, 0))
    return pl.pallas_call(
        _experts_kernel,
        grid_spec=pltpu.PrefetchScalarGridSpec(
            num_scalar_prefetch=2, grid=(n_tiles,),
            in_specs=[row, ex(D_MODEL, D_EXPERT), ex(D_MODEL, D_EXPERT), ex(D_EXPERT, D_MODEL)],
            out_specs=row,
            scratch_shapes=[pltpu.VMEM((D_MODEL, D_EXPERT), BF16), pltpu.VMEM((D_MODEL, D_EXPERT), BF16),
                            pltpu.VMEM((D_EXPERT, D_MODEL), BF16)]),
        out_shape=jax.ShapeDtypeStruct(xs.shape, jnp.uint32),
        compiler_params=_cparams("arbitrary"),
        name="experts",
    )(tile_expert, tile_valid, xs, wts["exp_gate"], wts["exp_up"], wts["exp_down"])


def _combine_kernel(h2p_ref, ys_ref, wk_ref, x2_ref, mod_ref, shg_ref, shu_ref, shd_ref, *rest):
    y_ref = rest[-1]
    t = _unpack_pairs(h2p_ref[...]).astype(BF16)
    act = _silu(_dot(t, shg_ref[...])) * _dot(t, shu_ref[...])
    acc = _dot(act.astype(BF16), shd_ref[...])
    wk = wk_ref[...]
    for k in range(TOP_K):
        acc = acc + wk[:, k:k + 1] * _unpack_pairs(ys_ref[k])
    y_ref[...] = x2_ref[...] + mod_ref[0][5:6] * acc


def _combine(h2p, ys, row0, wk, x2, y_prev, mod, mod_row0, tokens_per_row, wts, tm):
    T = h2p.shape[0]
    G = ys.shape[1]
    bpr = tokens_per_row // tm
    blk0 = row0 // tm
    tok = lambda w: pl.BlockSpec((tm, w), lambda i: (blk0 + i, 0))
    in_specs = [tok(D_PACKED), pl.BlockSpec((TOP_K, tm, D_PACKED), lambda i: (0, i, 0)), tok(LANES),
                tok(D_MODEL), pl.BlockSpec((1, 6, D_MODEL), lambda i: (mod_row0 + (blk0 + i) // bpr, 0, 0)),
                _const_spec((D_MODEL, D_SHARED)), _const_spec((D_MODEL, D_SHARED)),
                _const_spec((D_SHARED, D_MODEL))]
    args = [h2p, ys, wk, x2, mod, wts["sh_gate"], wts["sh_up"], wts["sh_down"]]
    aliases = {}
    if y_prev is not None:
        aliases = {len(args): 0}
        in_specs.append(pl.BlockSpec(memory_space=pl.ANY))
        args.append(y_prev)
    return pl.pallas_call(
        _combine_kernel,
        grid=(G // tm,),
        in_specs=in_specs,
        out_specs=tok(D_MODEL),
        out_shape=jax.ShapeDtypeStruct((T, D_MODEL), F32),
        input_output_aliases=aliases,
        compiler_params=_cparams("parallel"),
        name="moe_combine",
    )(*args)


def _moe(halves, mod, wts):
    routed = [h["routed"] for h in halves]
    sizes = [r[1].shape[0] for r in routed]
    n_tiles = (TOP_K * sum(sizes)) // ROW_TILE + N_EXPERTS
    path_counts = [r[5][0, :N_EXPERTS] for r in routed]
    counts = sum(path_counts)
    padded = ((counts + ROW_TILE - 1) // ROW_TILE) * ROW_TILE
    ends = jnp.cumsum(padded)
    starts = ends - padded
    tile_start = jnp.arange(n_tiles, dtype=jnp.int32) * ROW_TILE
    tile_expert = jnp.minimum(jnp.sum(tile_start[:, None] >= ends[None, :], axis=1), N_EXPERTS - 1).astype(jnp.int32)
    tile_valid = jnp.clip(counts[tile_expert] - (tile_start - starts[tile_expert]), 0, ROW_TILE).astype(jnp.int32)

    pos, first = [], starts
    for h, r, c in zip(halves, routed, path_counts):
        pos.append(_plan(r[2], r[3], first, h["tm"])[:, :TOP_K])
        first = first + c
    xs = _sc_dispatch_rows([r[1] for r in routed], pos, n_tiles * ROW_TILE)
    ys = _experts(xs, tile_expert, tile_valid, wts)
    outs = []
    for h, r, p in zip(halves, routed, pos):
        T = r[1].shape[0]
        G = MOE_GATHER_GROUP if T % MOE_GATHER_GROUP == 0 else T
        y = None
        for row0 in range(0, T, G):
            ysel = _sc_gather_rows(ys, p[row0:row0 + G].T.reshape(-1)).reshape(TOP_K, G, D_PACKED)
            y = _combine(r[1], ysel, row0, r[4], r[0], y, mod, h["mod_row0"], h["tokens_per_row"], wts, h["tm"])
        outs.append(y.reshape(h["shape"]))
    return outs


def _pad_heads(w, d_used):
    k = w.shape[0]
    w = w.reshape(k, H_MLA, d_used)
    return jnp.pad(w, ((0, 0), (0, 0), (0, HEAD_SLOT - d_used))).reshape(k, H_MLA * HEAD_SLOT)


def _lane_pad(v):
    return jnp.pad(v, (0, LANES - v.shape[0])).reshape(1, LANES)


def _rot_partners(w):
    rot = w[..., w.shape[-1] - D_ROPE:]
    return jnp.concatenate([jnp.zeros_like(w[..., :w.shape[-1] - D_ROPE]),
                            rot[..., _swap_partner(D_ROPE, D_ROPE // 4)]], axis=-1)


def _prep_weights(l, w_in, norm1_g, mla_q_norm_g, w_uq, mla_kv_norm_g, w_ukv, mla_q_gain, mla_k_gain,
                  w_o, norm2_g, router_w, router_bias, exp_w_gate, exp_w_up, exp_w_down,
                  sh_w_gate, sh_w_up, sh_w_down):
    w = w_in[l]
    a, b, c = Q_RANK, Q_RANK + KV_RANK, Q_RANK + KV_RANK + D_ROPE
    kr_slot = lambda wk: jnp.pad(wk, ((0, 0), (D_NOPE, LANES - D_QK)))
    w1 = jnp.concatenate([w[:, :b], kr_slot(w[:, b:c]), w[:, c:]], axis=1).astype(BF16)
    ret_sw = np.concatenate([h * D_RET + _swap_partner(D_RET, D_RET // 4) for h in range(H_RET)])
    w_rq, w_rk = w[:, c:c + D_RET_ALL], w[:, c + D_RET_ALL:c + 2 * D_RET_ALL]
    w1_rope = jnp.concatenate([w1, kr_slot(_rot_partners(w[:, b:c])).astype(BF16),
                               w_rq[:, ret_sw].astype(BF16), w_rk[:, ret_sw].astype(BF16)], axis=1)
    uq = w_uq[l].reshape(Q_RANK, H_MLA, D_QK)
    pad_slots = lambda u: jnp.pad(u, ((0, 0), (0, 0), (0, HEAD_SLOT - D_QK))).reshape(Q_RANK, H_MLA * HEAD_SLOT)
    gains = lambda g: jnp.concatenate([_lane_pad(g), _lane_pad(_rot_partners(g))], axis=0)
    ukv = w_ukv[l].reshape(KV_RANK, H_MLA, D_NOPE + D_V)
    w_k = _pad_heads(ukv[:, :, :D_NOPE].reshape(KV_RANK, H_MLA * D_NOPE), D_NOPE)
    w_v = _pad_heads(ukv[:, :, D_NOPE:].reshape(KV_RANK, H_MLA * D_V), D_V)
    rw = jnp.pad(router_w[l], ((0, 0), (0, LANES - N_EXPERTS)))
    rw_hi = rw.astype(BF16)
    rw_lo = (rw - rw_hi.astype(F32)).astype(BF16)
    rb = jnp.concatenate([router_bias[l].astype(F32), jnp.full((LANES - N_EXPERTS,), -jnp.inf, F32)])
    return dict(
        norm1_g=norm1_g[l].reshape(1, D_MODEL), w1=w1, w1_rope=w1_rope,
        q_norm_g=mla_q_norm_g[l].reshape(1, Q_RANK),
        w_uq=jnp.concatenate([pad_slots(uq), pad_slots(_rot_partners(uq))], axis=1).astype(BF16),
        kv_norm_g=mla_kv_norm_g[l].reshape(1, KV_RANK),
        w_kv=jnp.concatenate([w_k, w_v], axis=1).astype(BF16),
        q_gain=gains(mla_q_gain[l]), k_gain=gains(mla_k_gain[l]),
        w_o=w_o[l].astype(BF16), norm2_g=norm2_g[l].reshape(1, D_MODEL),
        router_hi=rw_hi, router_lo=rw_lo, router_bias=rb.reshape(1, LANES),
        exp_gate=exp_w_gate[l], exp_up=exp_w_up[l], exp_down=exp_w_down[l],
        sh_gate=sh_w_gate[l].astype(BF16), sh_up=sh_w_up[l].astype(BF16),
        sh_down=sh_w_down[l].astype(BF16))


def _pick(n, prefs):
    for p in prefs:
        if n % p == 0:
            return p
    return n


def _mixer_half(x, mod, mod_row0, wts, ret_wts, ctx):
    B, L, D = x.shape
    T = B * L
    tokens_per_row = L if ctx is not None else T
    xt = x.reshape(T, D)
    rope_tabs = None
    if ctx is not None:
        rope_tabs = _rope_tables(L)
    tm = _pick(L, (512, 256, 128))
    ckv_n, krp, q, k, v, rq, rk, rv, rg = _inproj(xt, mod, mod_row0, tokens_per_row, wts, rope_tabs, tm)

    seq = lambda a: a.reshape(B, L, a.shape[-1])
    kc = vc = s0f = s0b = None
    if ctx is not None:
        ckv_c, kr_c, s0f, s0b = ctx
        Lc = ckv_c.shape[1]
        krp_c = jnp.pad(kr_c, ((0, 0), (0, 0), (D_NOPE, LANES - D_NOPE - D_ROPE)))
        kc, vc = _ctx_kv(ckv_c.reshape(B * Lc, KV_RANK), krp_c.reshape(B * Lc, LANES), wts,
                         _pick(B * Lc, (512, 256, 128)))
        kc, vc = kc.reshape(B, Lc, -1), vc.reshape(B, Lc, -1)
    o_mla = _attention(seq(q), seq(k), seq(v), kc, vc, _pick(L, (1024, 512, 256, 128)))

    dec_f, dec_b, ret_g = ret_wts
    want_state = ctx is None
    ret = _retention(seq(rq), seq(rk), seq(rv), seq(rg), dec_f, dec_b, ret_g, s0f, s0b, want_state)
    o_ret = ret[0]

    routed = _outproj(o_mla.reshape(T, -1), o_ret.reshape(T, -1), xt, mod, mod_row0, tokens_per_row, wts, tm)
    new = None
    if want_state:
        new = (ckv_n.reshape(B, L, KV_RANK), krp[:, D_NOPE:D_NOPE + D_ROPE].reshape(B, L, D_ROPE),
               ret[1], ret[2])
    return dict(routed=routed, mod_row0=mod_row0, tokens_per_row=tokens_per_row, tm=tm, shape=(B, L, D)), new


def kernel(x_prompt, x_sample, cache_mla_ckv, cache_mla_krope, state_ret_fwd, state_ret_bwd, c, c_ctx,
           w_ada, b_ada, norm1_g, w_in, mla_q_norm_g, w_uq, mla_kv_norm_g, w_ukv, mla_q_gain, mla_k_gain,
           ret_decay_fwd, ret_decay_bwd, ret_norm_g, w_o, norm2_g, router_w, router_bias, exp_w_gate,
           exp_w_up, exp_w_down, sh_w_gate, sh_w_up, sh_w_down):
    depth = w_ada.shape[0]
    n_dec = c.shape[0]
    assert 1 + n_dec <= MOD_ROWS
    cond = jnp.concatenate([c_ctx[None], c, jnp.zeros((MOD_ROWS - 1 - n_dec, D_MODEL), F32)], axis=0)

    y_prompt, y_sample = x_prompt, x_sample
    ckv_l, kr_l, sf_l, sb_l = [], [], [], []
    for l in range(depth):
        wts = _prep_weights(l, w_in, norm1_g, mla_q_norm_g, w_uq, mla_kv_norm_g, w_ukv, mla_q_gain,
                            mla_k_gain, w_o, norm2_g, router_w, router_bias, exp_w_gate, exp_w_up,
                            exp_w_down, sh_w_gate, sh_w_up, sh_w_down)
        ret_wts = (jnp.broadcast_to(ret_decay_fwd[l].astype(F32)[:, None], (H_RET, LANES)),
                   jnp.broadcast_to(ret_decay_bwd[l].astype(F32)[:, None], (H_RET, LANES)),
                   jnp.tile(ret_norm_g[l].reshape(1, D_RET), (1, LANES // D_RET)))
        mod = _adaln(cond, w_ada[l], b_ada[l]).reshape(MOD_ROWS, 6, D_MODEL)
        half_p, new = _mixer_half(y_prompt, mod, 0, wts, ret_wts, None)
        ckv_l.append(new[0]); kr_l.append(new[1]); sf_l.append(new[2]); sb_l.append(new[3])
        ctx = (cache_mla_ckv[:, l], cache_mla_krope[:, l], state_ret_fwd[:, l], state_ret_bwd[:, l])
        half_s, _ = _mixer_half(y_sample, mod, 1, wts, ret_wts, ctx)
        y_prompt, y_sample = _moe([half_p, half_s], mod, wts)

    return (y_prompt, y_sample, jnp.stack(ckv_l, axis=1), jnp.stack(kr_l, axis=1),
            jnp.stack(sf_l, axis=1), jnp.stack(sb_l, axis=1))
```

```python
import functools
import math

import numpy as np
import jax
import jax.numpy as jnp
from jax import lax
from jax.experimental import pallas as pl
from jax.experimental.pallas import tpu as pltpu
from jax.experimental.pallas import tpu_sc as plsc

F32 = jnp.float32
BF16 = jnp.bfloat16

D_MODEL = 1024
GRID_W = 64
H_MLA = 8
D_NOPE = 64
D_ROPE = 32
D_QK = D_NOPE + D_ROPE
D_V = 64
Q_RANK = 256
KV_RANK = 128
H_RET = 8
D_RET = 64
RET_CHUNK = 128
D_RET_ALL = H_RET * D_RET
N_EXPERTS = 64
TOP_K = 6
D_EXPERT = 256
D_SHARED = 256
ROUTED_SCALE = 2.5
ROPE_BASE = 10000.0
EPS = 1e-6
LOG2_E = math.log2(math.e)

LANES = 128
HEAD_SLOT = LANES
D_IN_PAD = Q_RANK + KV_RANK + LANES + 4 * D_RET_ALL
MOD_ROWS = 16
VMEM_LIMIT = 56 * 1024 * 1024


def _cparams(*sem):
    return pltpu.CompilerParams(dimension_semantics=sem, vmem_limit_bytes=VMEM_LIMIT)


def _dot(a, b):
    return jnp.dot(a, b, preferred_element_type=F32)


def _dot_nt(a, b):
    return lax.dot_general(a, b, (((1,), (1,)), ((), ())), preferred_element_type=F32)


def _dot_tn(a, b):
    return lax.dot_general(a, b, (((0,), (0,)), ((), ())), preferred_element_type=F32)


def _rms(x, g):
    return x * lax.rsqrt(jnp.mean(x * x, axis=-1, keepdims=True) + EPS) * g


def _silu(x):
    return x * jax.nn.sigmoid(x)


def _adaln_kernel(c_ref, w_ref, b_ref, o_ref):
    s = _silu(c_ref[...])
    o_ref[...] = _dot(s.astype(BF16), w_ref[...].astype(BF16)) + b_ref[...]


def _adaln(cond, w_ada, b_ada):
    n_out = w_ada.shape[1]
    bn = 1536
    return pl.pallas_call(
        _adaln_kernel,
        grid=(n_out // bn,),
        in_specs=[pl.BlockSpec((MOD_ROWS, D_MODEL), lambda j: (0, 0)),
                  pl.BlockSpec((D_MODEL, bn), lambda j: (0, j)),
                  pl.BlockSpec((1, bn), lambda j: (0, j))],
        out_specs=pl.BlockSpec((MOD_ROWS, bn), lambda j: (0, j)),
        out_shape=jax.ShapeDtypeStruct((MOD_ROWS, n_out), F32),
        compiler_params=_cparams("arbitrary"),
        name="adaln",
    )(cond, w_ada, b_ada.reshape(1, n_out))


def _swap_halves(w, half_pair):
    g = w.reshape(*w.shape[:-1], w.shape[-1] // (2 * half_pair), 2, half_pair)
    return g[..., ::-1, :].reshape(w.shape)


def _rope_tables(n_tokens):
    t = np.arange(n_tokens)
    row = (t // GRID_W).astype(np.float64)
    col = (t % GRID_W).astype(np.float64)

    def axis_tables(width, lane0, period, fill):
        p = width // 2
        inv = 1.0 / (ROPE_BASE ** (np.arange(p, dtype=np.float64) / p))
        cos = np.full((n_tokens, LANES), fill)
        sin = np.zeros((n_tokens, LANES))
        starts = range(lane0, LANES, period) if period else (lane0,)
        for s0 in starts:
            for base, pos in ((s0, row), (s0 + width, col)):
                ang = pos[:, None] * inv[None, :]
                c, sn = np.cos(ang), np.sin(ang)
                cos[:, base:base + p] = c
                cos[:, base + p:base + 2 * p] = c
                sin[:, base:base + p] = -sn
                sin[:, base + p:base + 2 * p] = sn
        return cos, sin

    qk_cos, qk_sin = axis_tables(D_ROPE // 2, D_NOPE, 0, 1.0)
    ret_cos, ret_sin = axis_tables(D_RET // 2, 0, D_RET, 1.0)
    as_f32 = lambda *a: tuple(jnp.asarray(x, F32) for x in a)
    return as_f32(qk_cos, qk_sin), as_f32(ret_cos, ret_sin)


def _head_sums(x):
    r = lax.broadcasted_iota(jnp.int32, (2 * HEAD_SLOT, 2 * HEAD_SLOT), 0) // HEAD_SLOT
    c = lax.broadcasted_iota(jnp.int32, (2 * HEAD_SLOT, 2 * HEAD_SLOT), 1) // HEAD_SLOT
    ones_bd = jnp.where(r == c, 1.0, 0.0).astype(BF16)
    xb = x.astype(BF16)
    w = 2 * HEAD_SLOT
    return jnp.concatenate([_dot(xb[:, g * w:(g + 1) * w], ones_bd) for g in range(x.shape[1] // w)], axis=1)


def _tile_heads(v):
    return jnp.concatenate([v] * H_MLA, axis=1)


def _kv_heads(ckv_n, krp, krp_sw, wkv_ref, kgain, kgain_sw, tabs, k_ref, v_ref):
    kvp = _dot(ckv_n.astype(BF16), wkv_ref[...])
    kn = kvp[:, :H_MLA * HEAD_SLOT]
    ms = _head_sums(kn * kn + _tile_heads(krp * krp)) * (1.0 / D_QK)
    r = lax.rsqrt(ms + EPS)
    k = (kn + _tile_heads(krp)) * r * _tile_heads(kgain)
    if tabs is not None:
        cos, sin = tabs
        k = k * _tile_heads(cos) + r * _tile_heads(krp_sw * kgain_sw * sin)
    k_ref[...] = k.astype(BF16)
    v_lane = lax.broadcasted_iota(jnp.int32, (1, H_MLA * HEAD_SLOT), 1) % HEAD_SLOT
    v_ref[...] = (kvp[:, H_MLA * HEAD_SLOT:] + jnp.where(v_lane == D_V, 1.0, 0.0)).astype(BF16)


def _inproj_kernel(rope, *refs):
    (x_ref, mod_ref, n1g_ref, w1_ref, qng_ref, wuq_ref, kvg_ref, wkv_ref, qgain_ref,
     kgain_ref) = refs[:10]
    refs = refs[10:]
    if rope:
        qk_tabs = tuple(r[...] for r in refs[:2])
        ret_cos, ret_sin = (r[...] for r in refs[2:4])
        refs = refs[4:]
    else:
        qk_tabs = None
    ckvn_ref, krp_ref, q_ref, k_ref, v_ref, rq_ref, rk_ref, rv_ref, rg_ref = refs

    mod = mod_ref[0]
    h = _rms(x_ref[...], n1g_ref[...]) * (1.0 + mod[1:2]) + mod[0:1]
    z = _dot(h.astype(BF16), w1_ref[...])
    o = 0
    cq = z[:, o:o + Q_RANK]; o += Q_RANK
    ckv = z[:, o:o + KV_RANK]; o += KV_RANK
    krp = z[:, o:o + LANES]; o += LANES
    rq = z[:, o:o + D_RET_ALL]; o += D_RET_ALL
    rk = z[:, o:o + D_RET_ALL]; o += D_RET_ALL
    rv = z[:, o:o + D_RET_ALL]; o += D_RET_ALL
    rg = z[:, o:o + D_RET_ALL]; o += D_RET_ALL
    krp_sw = None
    if rope:
        krp_sw = z[:, o:o + LANES]; o += LANES
        rq_sw = z[:, o:o + D_RET_ALL]; o += D_RET_ALL
        rk_sw = z[:, o:o + D_RET_ALL]

    ckv_n = _rms(ckv, kvg_ref[...])
    ckvn_ref[...] = ckv_n
    krp_ref[...] = krp

    qn = _rms(cq, qng_ref[...]).astype(BF16)
    scale = D_QK ** -0.5 * LOG2_E
    q = _dot(qn, wuq_ref[:, :H_MLA * HEAD_SLOT])
    r = lax.rsqrt(_head_sums(q * q) * (1.0 / D_QK) + EPS)
    q = q * r * _tile_heads(qgain_ref[0:1] * scale)
    if rope:
        cos, sin = qk_tabs
        q_sw = _dot(qn, wuq_ref[:, H_MLA * HEAD_SLOT:])
        q = q * _tile_heads(cos) + q_sw * r * _tile_heads(qgain_ref[1:2] * scale * sin)
    q_ref[...] = q.astype(BF16)

    _kv_heads(ckv_n, krp, krp_sw, wkv_ref, kgain_ref[0:1], kgain_ref[1:2], qk_tabs, k_ref, v_ref)

    for t in range(D_RET_ALL // LANES):
        sl = slice(t * LANES, (t + 1) * LANES)
        rq_t, rk_t = rq[:, sl], rk[:, sl]
        if rope:
            rq_t = rq_t * ret_cos + rq_sw[:, sl] * ret_sin
            rk_t = rk_t * ret_cos + rk_sw[:, sl] * ret_sin
        rq_ref[:, sl] = rq_t.astype(BF16)
        rk_ref[:, sl] = (rk_t * (D_RET ** -0.5)).astype(BF16)
    rv_ref[...] = rv.astype(BF16)
    rg_ref[...] = _silu(rg).astype(BF16)


def _const_spec(shape):
    return pl.BlockSpec(shape, lambda i: (0,) * len(shape))


def _inproj(x, mod, mod_row0, tokens_per_row, wts, rope_tabs, tm):
    T = x.shape[0]
    bpr = tokens_per_row // tm
    rope = rope_tabs is not None
    tok = lambda w: pl.BlockSpec((tm, w), lambda i: (i, 0))
    in_specs = [tok(D_MODEL),
                pl.BlockSpec((1, 6, D_MODEL), lambda i: (mod_row0 + i // bpr, 0, 0)),
                _const_spec((1, D_MODEL)),
                _const_spec((D_MODEL, D_IN_PAD + (LANES + 2 * D_RET_ALL if rope else 0))),
                _const_spec((1, Q_RANK)),
                _const_spec((Q_RANK, 2 * H_MLA * HEAD_SLOT)),
                _const_spec((1, KV_RANK)),
                _const_spec((KV_RANK, 2 * H_MLA * HEAD_SLOT)),
                _const_spec((2, LANES)),
                _const_spec((2, LANES))]
    args = [x, mod, wts["norm1_g"], wts["w1_rope" if rope else "w1"], wts["q_norm_g"], wts["w_uq"],
            wts["kv_norm_g"], wts["w_kv"], wts["q_gain"], wts["k_gain"]]
    if rope:
        qk_tabs, ret_tabs = rope_tabs
        nb = qk_tabs[0].shape[0] // tm
        in_specs += [pl.BlockSpec((tm, LANES), lambda i: (i % nb, 0))] * 4
        args += list(qk_tabs) + list(ret_tabs)
    widths = [(KV_RANK, F32), (LANES, F32), (H_MLA * HEAD_SLOT, BF16), (H_MLA * HEAD_SLOT, BF16),
              (H_MLA * HEAD_SLOT, BF16)] + [(D_RET_ALL, BF16)] * 4
    return pl.pallas_call(
        functools.partial(_inproj_kernel, rope),
        grid=(T // tm,),
        in_specs=in_specs,
        out_specs=[tok(w) for w, _ in widths],
        out_shape=[jax.ShapeDtypeStruct((T, w), dt) for w, dt in widths],
        compiler_params=_cparams("parallel"),
        name="inproj_rope" if rope else "inproj",
    )(*args)


def _ctx_kv_kernel(ckv_ref, krp_ref, wkv_ref, kgain_ref, k_ref, v_ref):
    _kv_heads(ckv_ref[...], krp_ref[...], None, wkv_ref, kgain_ref[0:1], None, None, k_ref, v_ref)


def _ctx_kv(ckv, krp, wts, tm):
    T = ckv.shape[0]
    tok = lambda w: pl.BlockSpec((tm, w), lambda i: (i, 0))
    return pl.pallas_call(
        _ctx_kv_kernel,
        grid=(T // tm,),
        in_specs=[tok(KV_RANK), tok(LANES),
                  _const_spec((KV_RANK, 2 * H_MLA * HEAD_SLOT)), _const_spec((2, LANES))],
        out_specs=[tok(H_MLA * HEAD_SLOT), tok(H_MLA * HEAD_SLOT)],
        out_shape=[jax.ShapeDtypeStruct((T, H_MLA * HEAD_SLOT), BF16),
                   jax.ShapeDtypeStruct((T, H_MLA * HEAD_SLOT), BF16)],
        compiler_params=_cparams("parallel"),
        name="ctx_kv",
    )(ckv, krp, wts["w_kv"], wts["k_gain"])


ATTN_KEY_CHUNK = 2048


def _attn_kernel(has_ctx, *refs):
    if has_ctx:
        q_ref, k_ref, v_ref, kc_ref, vc_ref, o_ref = refs
    else:
        q_ref, k_ref, v_ref, o_ref = refs
    sources = [(k_ref, v_ref)] + ([(kc_ref, vc_ref)] if has_ctx else [])
    chunks = []
    for kr, vr in sources:
        n = kr.shape[1]
        kc = _pick(n, (ATTN_KEY_CHUNK, 256, 128))
        chunks += [(kr, vr, slice(c * kc, (c + 1) * kc)) for c in range(n // kc)]
    tq = q_ref.shape[1]
    outs = []
    for j in range(2):
        hs = slice(j * HEAD_SLOT, (j + 1) * HEAD_SLOT)
        q = q_ref[0, :, hs]
        m = jnp.full((tq, 1), -jnp.inf, F32)
        acc = jnp.zeros((tq, HEAD_SLOT), F32)
        for kr, vr, rows in chunks:
            s = _dot_nt(q, kr[0, rows, hs])
            m_new = jnp.maximum(m, jnp.max(s, axis=-1, keepdims=True))
            p = jnp.exp2(s - m_new).astype(BF16)
            acc = acc * jnp.exp2(m - m_new) + _dot(p, vr[0, rows, hs])
            m = m_new
        outs.append(acc[:, :D_V] / acc[:, D_V:D_V + 1])
    o_ref[0] = jnp.concatenate(outs, axis=-1).astype(BF16)


def _attention(q, k, v, kc, vc, tq):
    B, L, _ = q.shape
    has_ctx = kc is not None
    in_specs = [pl.BlockSpec((1, tq, 2 * HEAD_SLOT), lambda b, hp, i: (b, i, hp)),
                pl.BlockSpec((1, L, 2 * HEAD_SLOT), lambda b, hp, i: (b, 0, hp)),
                pl.BlockSpec((1, L, 2 * HEAD_SLOT), lambda b, hp, i: (b, 0, hp))]
    args = [q, k, v]
    if has_ctx:
        Lc = kc.shape[1]
        in_specs += [pl.BlockSpec((1, Lc, 2 * HEAD_SLOT), lambda b, hp, i: (b, 0, hp)),
                     pl.BlockSpec((1, Lc, 2 * HEAD_SLOT), lambda b, hp, i: (b, 0, hp))]
        args += [kc, vc]
    return pl.pallas_call(
        functools.partial(_attn_kernel, has_ctx),
        grid=(B, H_MLA // 2, L // tq),
        in_specs=in_specs,
        out_specs=pl.BlockSpec((1, tq, 2 * D_V), lambda b, hp, i: (b, i, hp)),
        out_shape=jax.ShapeDtypeStruct((B, L, H_MLA * D_V), BF16),
        compiler_params=_cparams("parallel", "parallel", "arbitrary"),
        name="attn_ctx" if has_ctx else "attn",
    )(*args)


RET_UNROLL = 8


def _ret_kernel(has_s0, want_state, n_chunks, *refs):
    rq_ref, rk_ref, rv_ref, rg_ref, decf_ref, decb_ref, g_ref = refs[:7]
    refs = refs[7:]
    if has_s0:
        s0f_ref, s0b_ref = refs[:2]
        refs = refs[2:]
    o_ref = refs[0]
    refs = refs[1:]
    if want_state:
        sf_ref, sb_ref = refs[:2]
        refs = refs[2:]
    kv_ref, sp_ref = refs

    C = RET_CHUNK
    W = 2 * D_RET
    hp = pl.program_id(1)
    lane = lax.broadcasted_iota(jnp.int32, (1, W), 1)
    lane_h1 = lane >= D_RET
    row_h1 = lax.broadcasted_iota(jnp.int32, (W, 1), 0) >= D_RET
    blockdiag = row_h1 == lane_h1
    pos = lax.broadcasted_iota(jnp.int32, (C, 1), 0).astype(F32)
    diff = (lax.broadcasted_iota(jnp.int32, (C, C), 0)
            - lax.broadcasted_iota(jnp.int32, (C, C), 1)).astype(F32)

    lg = {}
    for name, ref in (("f", decf_ref), ("b", decb_ref)):
        a = -jnp.exp(ref[pl.ds(2 * hp, 1), :])
        b = -jnp.exp(ref[pl.ds(2 * hp + 1, 1), :])
        lg[name] = (a[:, :1], b[:, :1], jnp.where(lane_h1, b, a))
    lgf0, lgf1, lgf = lg["f"]
    lgb0, lgb1, lgb = lg["b"]
    qdf, kdf = jnp.exp(lgf * (pos + 1.0)), jnp.exp(lgf * (C - 1.0 - pos))
    qdb, kdb = jnp.exp(lgb * (C - pos)), jnp.exp(lgb * pos)
    row_h1_full = lax.broadcasted_iota(jnp.int32, (W, W), 0) >= D_RET
    cdf = jnp.where(row_h1_full, jnp.exp(lgf1 * C), jnp.exp(lgf0 * C))
    cdb = jnp.where(row_h1_full, jnp.exp(lgb1 * C), jnp.exp(lgb0 * C))

    def intra_decay(lf, lb):
        return jnp.where(diff >= 0, jnp.exp(lf * jnp.maximum(diff, 0.0)), jnp.exp(lb * jnp.maximum(-diff, 0.0)))

    dmat0, dmat1 = intra_decay(lgf0, lgb0), intra_decay(lgf1, lgb1)
    g = g_ref[...]

    def chunk_rows(n):
        return pl.ds(pl.multiple_of(n * C, C), C)

    def phase_a(n, carry):
        rows = chunk_rows(n)
        k = rk_ref[0, rows, :].astype(F32)
        k2 = jnp.concatenate([(k * kdf).astype(BF16), (k * kdb).astype(BF16)], axis=1)
        kv_ref[n] = _dot_tn(k2, rv_ref[0, rows, :])
        return carry

    unroll = min(RET_UNROLL, n_chunks)
    lax.fori_loop(0, n_chunks, phase_a, 0, unroll=unroll)

    def init_state(ref):
        if not has_s0:
            return jnp.zeros((W, W), F32)
        z = jnp.zeros((D_RET, D_RET), F32)
        return jnp.concatenate([jnp.concatenate([ref[0, 0], z], axis=1),
                                jnp.concatenate([z, ref[0, 1]], axis=1)], axis=0)

    def scan_f(n, S):
        sp_ref[n, 0:W, :] = S.astype(BF16)
        return cdf * S + jnp.where(blockdiag, kv_ref[n, 0:W, :], 0.0)

    def scan_b(i, S):
        n = n_chunks - 1 - i
        sp_ref[n, W:2 * W, :] = S.astype(BF16)
        return cdb * S + jnp.where(blockdiag, kv_ref[n, W:2 * W, :], 0.0)

    Sf = lax.fori_loop(0, n_chunks, scan_f, init_state(s0f_ref if has_s0 else None))
    Sb = lax.fori_loop(0, n_chunks, scan_b, init_state(s0b_ref if has_s0 else None))
    if want_state:
        for S, ref in ((Sf, sf_ref), (Sb, sb_ref)):
            ref[0, 0] = S[0:D_RET, 0:D_RET]
            ref[0, 1] = S[D_RET:W, D_RET:W]

    def phase_c(n, carry):
        rows = chunk_rows(n)
        q = rq_ref[0, rows, :]
        k = rk_ref[0, rows, :]
        v = rv_ref[0, rows, :]
        zero = jnp.zeros_like(q)
        sc0 = _dot_nt(jnp.where(lane_h1, zero, q), k) * dmat0
        sc1 = _dot_nt(jnp.where(lane_h1, q, zero), k) * dmat1
        p = jnp.concatenate([sc0.astype(BF16), sc1.astype(BF16)], axis=1)
        v_bd = jnp.concatenate([jnp.where(lane_h1, zero, v), jnp.where(lane_h1, v, zero)], axis=0)
        qf = q.astype(F32)
        q2 = jnp.concatenate([(qf * qdf).astype(BF16), (qf * qdb).astype(BF16)], axis=1)
        o = _dot(p, v_bd) + _dot(q2, sp_ref[n])
        o2 = o * o
        ss0 = jnp.sum(jnp.where(lane_h1, 0.0, o2), axis=-1, keepdims=True)
        ss1 = jnp.sum(jnp.where(lane_h1, o2, 0.0), axis=-1, keepdims=True)
        ms = jnp.where(lane_h1, ss1, ss0) * (1.0 / D_RET)
        o = o * lax.rsqrt(ms + EPS) * g * rg_ref[0, rows, :].astype(F32)
        o_ref[0, rows, :] = o.astype(BF16)
        return carry

    lax.fori_loop(0, n_chunks, phase_c, 0, unroll=unroll)


def _retention(rq, rk, rv, rg, dec_f, dec_b, g, s0f, s0b, want_state):
    B, L, _ = rq.shape
    has_s0 = s0f is not None
    seq = pl.BlockSpec((1, L, LANES), lambda b, hp: (b, 0, hp))
    st = pl.BlockSpec((1, 2, D_RET, D_RET), lambda b, hp: (b, hp, 0, 0))
    cst = lambda shape: pl.BlockSpec(shape, lambda b, hp: (0,) * len(shape))
    in_specs = [seq] * 4 + [cst((H_RET, LANES)), cst((H_RET, LANES)), cst((1, LANES))]
    args = [rq, rk, rv, rg, dec_f, dec_b, g]
    if has_s0:
        in_specs += [st, st]
        args += [s0f, s0b]
    out_specs = [seq]
    out_shape = [jax.ShapeDtypeStruct((B, L, D_RET_ALL), BF16)]
    if want_state:
        out_specs += [st, st]
        out_shape += [jax.ShapeDtypeStruct((B, H_RET, D_RET, D_RET), F32)] * 2
    return pl.pallas_call(
        functools.partial(_ret_kernel, has_s0, want_state, L // RET_CHUNK),
        grid=(B, H_RET // 2),
        in_specs=in_specs,
        out_specs=out_specs,
        out_shape=out_shape,
        scratch_shapes=[pltpu.VMEM((L // RET_CHUNK, 4 * D_RET, LANES), F32),
                        pltpu.VMEM((L // RET_CHUNK, 4 * D_RET, LANES), BF16)],
        compiler_params=_cparams("parallel", "parallel"),
        name="retention_s0" if has_s0 else "retention",
    )(*args)


def _pack_pairs(x):
    w = x.shape[1] // 2
    hi = lax.bitcast_convert_type(x[:, :w].astype(BF16).astype(F32), jnp.uint32)
    lo = lax.bitcast_convert_type(x[:, w:].astype(BF16).astype(F32), jnp.uint32)
    return hi | (lo >> 16)


def _unpack_pairs(p):
    hi = lax.bitcast_convert_type(p & jnp.uint32(0xFFFF0000), F32)
    lo = lax.bitcast_convert_type(p << 16, F32)
    return jnp.concatenate([hi, lo], axis=1)


D_PACKED = D_MODEL // 2


def _outproj_kernel(omla_ref, oret_ref, x_ref, mod_ref, wo_ref, n2g_ref, rwh_ref, rwl_ref, rb_ref, tri_ref,
                    x2_ref, h2p_ref, code_ref, rank_ref, wk_ref, cnt_ref):
    @pl.when(pl.program_id(0) == 0)
    def _():
        cnt_ref[...] = jnp.zeros_like(cnt_ref)

    mod = mod_ref[0]
    half = H_MLA * D_V
    a = _dot(omla_ref[...], wo_ref[:half, :]) + _dot(oret_ref[...], wo_ref[half:, :])
    x2 = x_ref[...] + mod[2:3] * a
    x2_ref[...] = x2
    h2 = _rms(x2, n2g_ref[...]) * (1.0 + mod[4:5]) + mod[3:4]
    h2p_ref[...] = _pack_pairs(h2)
    h2_hi = h2.astype(BF16)
    h2_lo = (h2 - h2_hi.astype(F32)).astype(BF16)
    rw_hi = rwh_ref[...]
    logits = _dot_nt(rw_hi, h2_hi) + _dot_nt(rw_hi, h2_lo) + _dot_nt(rwl_ref[...], h2_hi)
    scores = jax.nn.sigmoid(logits)
    tm = scores.shape[1]
    expert = lax.broadcasted_iota(jnp.int32, scores.shape, 0)
    cand = scores + rb_ref[...]
    code = jnp.zeros(scores.shape, jnp.int32)
    for k in range(TOP_K):
        mx = jnp.max(cand, axis=0, keepdims=True)
        first = jnp.min(jnp.where(cand == mx, expert, LANES), axis=0, keepdims=True)
        hit = expert == first
        code = jnp.where(hit, k + 1, code)
        cand = jnp.where(hit, -jnp.inf, cand)
    picked = code > 0
    sel = jnp.where(picked, scores, 0.0)
    gates = sel * (ROUTED_SCALE / jnp.sum(sel, axis=0, keepdims=True))
    code_ref[...] = code

    onehot = jnp.where(picked, 1.0, 0.0)
    carry = cnt_ref[...].astype(F32)
    rank = _dot(onehot.astype(BF16), tri_ref[...])
    rank_ref[...] = rank + jnp.concatenate([carry] * (tm // LANES), axis=1)
    cnt_ref[...] = (carry + jnp.sum(onehot, axis=1, keepdims=True)).astype(jnp.int32)

    row = lax.broadcasted_iota(jnp.int32, scores.shape, 0)
    wk_t = jnp.zeros(scores.shape, F32)
    for k in range(TOP_K):
        w = jnp.sum(jnp.where(code == k + 1, gates, 0.0), axis=0, keepdims=True)
        wk_t = jnp.where(row == k, w, wk_t)
    wk_ref[...] = wk_t.T


def _outproj(o_mla, o_ret, x, mod, mod_row0, tokens_per_row, wts, tm):
    T = x.shape[0]
    bpr = tokens_per_row // tm
    tok = lambda w: pl.BlockSpec((tm, w), lambda i: (i, 0))
    emaj = pl.BlockSpec((LANES, tm), lambda i: (0, i))
    tri = jnp.triu(jnp.ones((tm, tm), BF16), 1)
    bias = jnp.broadcast_to(wts["router_bias"].reshape(LANES, 1), (LANES, tm))
    return pl.pallas_call(
        _outproj_kernel,
        grid=(T // tm,),
        in_specs=[tok(H_MLA * D_V), tok(D_RET_ALL), tok(D_MODEL),
                  pl.BlockSpec((1, 6, D_MODEL), lambda i: (mod_row0 + i // bpr, 0, 0)),
                  _const_spec((D_MODEL, D_MODEL)), _const_spec((1, D_MODEL)),
                  _const_spec((LANES, D_MODEL)), _const_spec((LANES, D_MODEL)), _const_spec((LANES, tm)),
                  _const_spec((tm, tm))],
        out_specs=[tok(D_MODEL), tok(D_PACKED), emaj, emaj, tok(LANES), _const_spec((LANES, LANES))],
        out_shape=[jax.ShapeDtypeStruct((T, D_MODEL), F32),
                   jax.ShapeDtypeStruct((T, D_PACKED), jnp.uint32),
                   jax.ShapeDtypeStruct((LANES, T), jnp.int32),
                   jax.ShapeDtypeStruct((LANES, T), F32),
                   jax.ShapeDtypeStruct((T, LANES), F32),
                   jax.ShapeDtypeStruct((LANES, LANES), jnp.int32)],
        compiler_params=_cparams("arbitrary"),
        name="outproj_router",
    )(o_mla, o_ret, x, mod, wts["w_o"], wts["norm2_g"], wts["router_hi"], wts["router_lo"], bias, tri)


def _plan_kernel(code_ref, rank_ref, first_ref, pos_ref):
    code = code_ref[...]
    tm = code.shape[1]
    row = rank_ref[...] + jnp.concatenate([first_ref[...]] * (tm // LANES), axis=1)
    krow = lax.broadcasted_iota(jnp.int32, pos_ref.shape, 0)
    pos = jnp.zeros(pos_ref.shape, F32)
    for k in range(TOP_K):
        pos = jnp.where(krow == k, jnp.sum(jnp.where(code == k + 1, row, 0.0), axis=0, keepdims=True), pos)
    pos_ref[...] = pos.astype(jnp.int32)


def _plan(code, rank, first_rows, tm):
    T = code.shape[1]
    first = jnp.broadcast_to(jnp.pad(first_rows.astype(F32), (0, LANES - N_EXPERTS)).reshape(LANES, 1),
                             (LANES, LANES))
    emaj = pl.BlockSpec((LANES, tm), lambda i: (0, i))
    return pl.pallas_call(
        _plan_kernel,
        grid=(T // tm,),
        in_specs=[emaj, emaj, _const_spec((LANES, LANES))],
        out_specs=pl.BlockSpec((8, tm), lambda i: (0, i)),
        out_shape=jax.ShapeDtypeStruct((8, T), jnp.int32),
        compiler_params=_cparams("parallel"),
        name="moe_plan",
    )(code, rank, first)


SC_WORKERS = 32
SC_CHUNK = 64


def _sc_worker_id():
    return lax.axis_index("s") * 2 + lax.axis_index("c")


def _sc_mesh():
    return plsc.VectorSubcoreMesh(core_axis_name="c", subcore_axis_name="s")


def _sc_gather_rows(table, idx):
    N = idx.shape[0]
    W = table.shape[1]
    per_w = N // SC_WORKERS
    n_chunks = per_w // SC_CHUNK
    assert N == SC_WORKERS * n_chunks * SC_CHUNK and n_chunks % 2 == 0

    @functools.partial(
        pl.kernel, mesh=_sc_mesh(), out_type=jax.ShapeDtypeStruct((N, W), table.dtype),
        scratch_types=[pltpu.VMEM((n_chunks, SC_CHUNK), jnp.int32),
                       pltpu.VMEM((2, SC_CHUNK, W), table.dtype),
                       pltpu.SemaphoreType.DMA((2,)), pltpu.SemaphoreType.DMA((2,))],
        name="sc_gather_rows")
    def k(table_hbm, idx_hbm, out_hbm, idx_v, rows_v, gsem, wsem):
        wid = _sc_worker_id()
        base = wid * per_w
        pltpu.sync_copy(idx_hbm.at[wid], idx_v)

        def gather(j, b):
            return pltpu.make_async_copy(table_hbm.at[idx_v.at[j]], rows_v.at[b], gsem.at[b])

        def writeback(j, b):
            off = pl.multiple_of(base + j * SC_CHUNK, SC_CHUNK)
            return pltpu.make_async_copy(rows_v.at[b], out_hbm.at[pl.ds(off, SC_CHUNK)], wsem.at[b])

        gather(0, 0).start()

        @pl.loop(0, n_chunks, step=2)
        def _(j):
            for b in range(2):
                jj = j + b

                @pl.when(jj + 1 < n_chunks)
                def _():
                    @pl.when(jj >= 1)
                    def _():
                        writeback(jj - 1, 1 - b).wait()
                    gather(jj + 1, 1 - b).start()

                gather(jj, b).wait()
                writeback(jj, b).start()

        writeback(n_chunks - 2, 0).wait()
        writeback(n_chunks - 1, 1).wait()

    return k(table, idx.reshape(SC_WORKERS, n_chunks, SC_CHUNK))


def _sc_dispatch_rows(rows_list, pos_list, n_out):
    W = rows_list[0].shape[1]
    K = pos_list[0].shape[0]
    n_src = len(rows_list)
    plan = []
    idx_list = []
    for rows, pos in zip(rows_list, pos_list):
        T = rows.shape[0]
        per_w = T // SC_WORKERS
        n_chunks = per_w // SC_CHUNK
        assert T == SC_WORKERS * n_chunks * SC_CHUNK and n_chunks % 2 == 0
        plan.append((per_w, n_chunks))
        idx_list.append(pos.reshape(K, SC_WORKERS, n_chunks, SC_CHUNK).transpose(1, 2, 0, 3))
    max_chunks = max(n for _, n in plan)

    @functools.partial(
        pl.kernel, mesh=_sc_mesh(), out_type=jax.ShapeDtypeStruct((n_out, W), rows_list[0].dtype),
        scratch_types=[pltpu.VMEM((max_chunks, K, SC_CHUNK), jnp.int32),
                       pltpu.VMEM((2, SC_CHUNK, W), rows_list[0].dtype),
                       pltpu.SemaphoreType.DMA((2,)), pltpu.SemaphoreType.DMA((2,))],
        name="sc_dispatch_rows")
    def k(*refs):
        rows_refs, idx_refs = refs[:n_src], refs[n_src:2 * n_src]
        out_hbm, idx_v, rows_v, lsem, ssem = refs[2 * n_src:]
        wid = _sc_worker_id()
        for rows_hbm, idx_hbm, (per_w, n_chunks) in zip(rows_refs, idx_refs, plan):
            base = wid * per_w
            pltpu.sync_copy(idx_hbm.at[wid], idx_v.at[pl.ds(0, n_chunks)])

            def load(j, b):
                off = pl.multiple_of(base + j * SC_CHUNK, SC_CHUNK)
                return pltpu.make_async_copy(rows_hbm.at[pl.ds(off, SC_CHUNK)], rows_v.at[b], lsem.at[b])

            def scatter(j, b, kk):
                return pltpu.make_async_copy(rows_v.at[b], out_hbm.at[idx_v.at[j, kk]], ssem.at[b])

            load(0, 0).start()

            @pl.loop(0, n_chunks, step=2)
            def _(j):
                for b in range(2):
                    jj = j + b

                    @pl.when(jj + 1 < n_chunks)
                    def _():
                        @pl.when(jj >= 1)
                        def _():
                            for kk in range(K):
                                scatter(jj - 1, 1 - b, kk).wait()
                        load(jj + 1, 1 - b).start()

                    load(jj, b).wait()
                    for kk in range(K):
                        scatter(jj, b, kk).start()

            for kk in range(K):
                scatter(n_chunks - 2, 0, kk).wait()
            for kk in range(K):
                scatter(n_chunks - 1, 1, kk).wait()

    return k(*rows_list, *idx_list)


ROW_TILE = 512
MOE_GATHER_GROUP = 8192


def _experts_kernel(te_ref, tv_ref, xs_ref, wg_ref, wu_ref, wd_ref, ys_ref, wg_s, wu_s, wd_s):
    i = pl.program_id(0)
    valid = tv_ref[i]

    @pl.when(jnp.logical_or(i == 0, te_ref[i] != te_ref[jnp.maximum(i - 1, 0)]))
    def _():
        wg_s[...] = wg_ref[0].astype(BF16)
        wu_s[...] = wu_ref[0].astype(BF16)
        wd_s[...] = wd_ref[0].astype(BF16)

    @pl.when(valid > 0)
    def _():
        row = lax.broadcasted_iota(jnp.int32, (ROW_TILE, 1), 0)
        p = jnp.where(row < valid, xs_ref[...], jnp.uint32(0))
        x = _unpack_pairs(p).astype(BF16)
        act = _silu(_dot(x, wg_s[...])) * _dot(x, wu_s[...])
        ys_ref[...] = _pack_pairs(_dot(act.astype(BF16), wd_s[...]))

    @pl.when(valid <= 0)
    def _():
        ys_ref[...] = jnp.zeros_like(ys_ref)


def _experts(xs, tile_expert, tile_valid, wts):
    n_tiles = tile_expert.shape[0]
    ex = lambda a, b: pl.BlockSpec((1, a, b), lambda i, te, tv: (te[i], 0, 0))
    row = pl.BlockSpec((ROW_TILE, D_PACKED), lambda i, te, tv: (i, 0))
    return pl.pallas_call(
        _experts_kernel,
        grid_spec=pltpu.PrefetchScalarGridSpec(
            num_scalar_prefetch=2, grid=(n_tiles,),
            in_specs=[row, ex(D_MODEL, D_EXPERT), ex(D_MODEL, D_EXPERT), ex(D_EXPERT, D_MODEL)],
            out_specs=row,
            scratch_shapes=[pltpu.VMEM((D_MODEL, D_EXPERT), BF16), pltpu.VMEM((D_MODEL, D_EXPERT), BF16),
                            pltpu.VMEM((D_EXPERT, D_MODEL), BF16)]),
        out_shape=jax.ShapeDtypeStruct(xs.shape, jnp.uint32),
        compiler_params=_cparams("arbitrary"),
        name="experts",
    )(tile_expert, tile_valid, xs, wts["exp_gate"], wts["exp_up"], wts["exp_down"])


def _combine_kernel(h2p_ref, ys_ref, wk_ref, x2_ref, mod_ref, shg_ref, shu_ref, shd_ref, *rest):
    y_ref = rest[-1]
    t = _unpack_pairs(h2p_ref[...]).astype(BF16)
    act = _silu(_dot(t, shg_ref[...])) * _dot(t, shu_ref[...])
    acc = _dot(act.astype(BF16), shd_ref[...])
    wk = wk_ref[...]
    for k in range(TOP_K):
        acc = acc + wk[:, k:k + 1] * _unpack_pairs(ys_ref[k])
    y_ref[...] = x2_ref[...] + mod_ref[0][5:6] * acc


def _combine(h2p, ys, row0, wk, x2, y_prev, mod, mod_row0, tokens_per_row, wts, tm):
    T = h2p.shape[0]
    G = ys.shape[1]
    bpr = tokens_per_row // tm
    blk0 = row0 // tm
    tok = lambda w: pl.BlockSpec((tm, w), lambda i: (blk0 + i, 0))
    in_specs = [tok(D_PACKED), pl.BlockSpec((TOP_K, tm, D_PACKED), lambda i: (0, i, 0)), tok(LANES),
                tok(D_MODEL), pl.BlockSpec((1, 6, D_MODEL), lambda i: (mod_row0 + (blk0 + i) // bpr, 0, 0)),
                _const_spec((D_MODEL, D_SHARED)), _const_spec((D_MODEL, D_SHARED)),
                _const_spec((D_SHARED, D_MODEL))]
    args = [h2p, ys, wk, x2, mod, wts["sh_gate"], wts["sh_up"], wts["sh_down"]]
    aliases = {}
    if y_prev is not None:
        aliases = {len(args): 0}
        in_specs.append(pl.BlockSpec(memory_space=pl.ANY))
        args.append(y_prev)
    return pl.pallas_call(
        _combine_kernel,
        grid=(G // tm,),
        in_specs=in_specs,
        out_specs=tok(D_MODEL),
        out_shape=jax.ShapeDtypeStruct((T, D_MODEL), F32),
        input_output_aliases=aliases,
        compiler_params=_cparams("parallel"),
        name="moe_combine",
    )(*args)


def _moe(halves, mod, wts):
    routed = [h["routed"] for h in halves]
    sizes = [r[1].shape[0] for r in routed]
    n_tiles = (TOP_K * sum(sizes)) // ROW_TILE + N_EXPERTS
    path_counts = [r[5][:N_EXPERTS, 0] for r in routed]
    counts = sum(path_counts)
    padded = ((counts + ROW_TILE - 1) // ROW_TILE) * ROW_TILE
    ends = jnp.cumsum(padded)
    starts = ends - padded
    tile_start = jnp.arange(n_tiles, dtype=jnp.int32) * ROW_TILE
    tile_expert = jnp.minimum(jnp.sum(tile_start[:, None] >= ends[None, :], axis=1), N_EXPERTS - 1).astype(jnp.int32)
    tile_valid = jnp.clip(counts[tile_expert] - (tile_start - starts[tile_expert]), 0, ROW_TILE).astype(jnp.int32)

    pos, first = [], starts
    for h, r, c in zip(halves, routed, path_counts):
        pos.append(_plan(r[2], r[3], first, h["tm"])[:TOP_K])
        first = first + c
    xs = _sc_dispatch_rows([r[1] for r in routed], pos, n_tiles * ROW_TILE)
    ys = _experts(xs, tile_expert, tile_valid, wts)
    outs = []
    for h, r, p in zip(halves, routed, pos):
        T = r[1].shape[0]
        G = MOE_GATHER_GROUP if T % MOE_GATHER_GROUP == 0 else T
        y = None
        for row0 in range(0, T, G):
            ysel = _sc_gather_rows(ys, p[:, row0:row0 + G].reshape(-1)).reshape(TOP_K, G, D_PACKED)
            y = _combine(r[1], ysel, row0, r[4], r[0], y, mod, h["mod_row0"], h["tokens_per_row"], wts, h["tm"])
        outs.append(y.reshape(h["shape"]))
    return outs


def _pad_heads(w, d_used):
    k = w.shape[0]
    w = w.reshape(k, H_MLA, d_used)
    return jnp.pad(w, ((0, 0), (0, 0), (0, HEAD_SLOT - d_used))).reshape(k, H_MLA * HEAD_SLOT)


def _lane_pad(v):
    return jnp.pad(v, (0, LANES - v.shape[0])).reshape(1, LANES)


def _rot_partners(w):
    rot = w[..., w.shape[-1] - D_ROPE:]
    return jnp.concatenate([jnp.zeros_like(w[..., :w.shape[-1] - D_ROPE]),
                            _swap_halves(rot, D_ROPE // 4)], axis=-1)


def _prep_weights(l, w_in, norm1_g, mla_q_norm_g, w_uq, mla_kv_norm_g, w_ukv, mla_q_gain, mla_k_gain,
                  w_o, norm2_g, router_w, router_bias, exp_w_gate, exp_w_up, exp_w_down,
                  sh_w_gate, sh_w_up, sh_w_down):
    w = w_in[l]
    a, b, c = Q_RANK, Q_RANK + KV_RANK, Q_RANK + KV_RANK + D_ROPE
    kr_slot = lambda wk: jnp.pad(wk, ((0, 0), (D_NOPE, LANES - D_QK)))
    w1 = jnp.concatenate([w[:, :b], kr_slot(w[:, b:c]), w[:, c:]], axis=1).astype(BF16)
    w_rq, w_rk = w[:, c:c + D_RET_ALL], w[:, c + D_RET_ALL:c + 2 * D_RET_ALL]
    w1_rope = jnp.concatenate([w1, kr_slot(_rot_partners(w[:, b:c])).astype(BF16),
                               _swap_halves(w_rq, D_RET // 4).astype(BF16),
                               _swap_halves(w_rk, D_RET // 4).astype(BF16)], axis=1)
    uq = w_uq[l].reshape(Q_RANK, H_MLA, D_QK)
    pad_slots = lambda u: jnp.pad(u, ((0, 0), (0, 0), (0, HEAD_SLOT - D_QK))).reshape(Q_RANK, H_MLA * HEAD_SLOT)
    gains = lambda g: jnp.concatenate([_lane_pad(g), _lane_pad(_rot_partners(g))], axis=0)
    ukv = w_ukv[l].reshape(KV_RANK, H_MLA, D_NOPE + D_V)
    w_k = _pad_heads(ukv[:, :, :D_NOPE].reshape(KV_RANK, H_MLA * D_NOPE), D_NOPE)
    w_v = _pad_heads(ukv[:, :, D_NOPE:].reshape(KV_RANK, H_MLA * D_V), D_V)
    rw = jnp.pad(router_w[l].T, ((0, LANES - N_EXPERTS), (0, 0)))
    rw_hi = rw.astype(BF16)
    rw_lo = (rw - rw_hi.astype(F32)).astype(BF16)
    rb = jnp.concatenate([router_bias[l].astype(F32), jnp.full((LANES - N_EXPERTS,), -jnp.inf, F32)])
    return dict(
        norm1_g=norm1_g[l].reshape(1, D_MODEL), w1=w1, w1_rope=w1_rope,
        q_norm_g=mla_q_norm_g[l].reshape(1, Q_RANK),
        w_uq=jnp.concatenate([pad_slots(uq), pad_slots(_rot_partners(uq))], axis=1).astype(BF16),
        kv_norm_g=mla_kv_norm_g[l].reshape(1, KV_RANK),
        w_kv=jnp.concatenate([w_k, w_v], axis=1).astype(BF16),
        q_gain=gains(mla_q_gain[l]), k_gain=gains(mla_k_gain[l]),
        w_o=w_o[l].astype(BF16), norm2_g=norm2_g[l].reshape(1, D_MODEL),
        router_hi=rw_hi, router_lo=rw_lo, router_bias=rb.reshape(1, LANES),
        exp_gate=exp_w_gate[l], exp_up=exp_w_up[l], exp_down=exp_w_down[l],
        sh_gate=sh_w_gate[l].astype(BF16), sh_up=sh_w_up[l].astype(BF16),
        sh_down=sh_w_down[l].astype(BF16))


def _pick(n, prefs):
    for p in prefs:
        if n % p == 0:
            return p
    return n


def _mixer_half(x, mod, mod_row0, wts, ret_wts, ctx):
    B, L, D = x.shape
    T = B * L
    tokens_per_row = L if ctx is not None else T
    xt = x.reshape(T, D)
    rope_tabs = None
    if ctx is not None:
        rope_tabs = _rope_tables(L)
    tm = _pick(L, (512, 256, 128))
    ckv_n, krp, q, k, v, rq, rk, rv, rg = _inproj(xt, mod, mod_row0, tokens_per_row, wts, rope_tabs, tm)

    seq = lambda a: a.reshape(B, L, a.shape[-1])
    kc = vc = s0f = s0b = None
    if ctx is not None:
        ckv_c, kr_c, s0f, s0b = ctx
        Lc = ckv_c.shape[1]
        krp_c = jnp.pad(kr_c, ((0, 0), (0, 0), (D_NOPE, LANES - D_NOPE - D_ROPE)))
        kc, vc = _ctx_kv(ckv_c.reshape(B * Lc, KV_RANK), krp_c.reshape(B * Lc, LANES), wts,
                         _pick(B * Lc, (512, 256, 128)))
        kc, vc = kc.reshape(B, Lc, -1), vc.reshape(B, Lc, -1)
    o_mla = _attention(seq(q), seq(k), seq(v), kc, vc, _pick(L, (1024, 512, 256, 128)))

    dec_f, dec_b, ret_g = ret_wts
    want_state = ctx is None
    ret = _retention(seq(rq), seq(rk), seq(rv), seq(rg), dec_f, dec_b, ret_g, s0f, s0b, want_state)
    o_ret = ret[0]

    routed = _outproj(o_mla.reshape(T, -1), o_ret.reshape(T, -1), xt, mod, mod_row0, tokens_per_row, wts, tm)
    new = None
    if want_state:
        new = (ckv_n.reshape(B, L, KV_RANK), krp[:, D_NOPE:D_NOPE + D_ROPE].reshape(B, L, D_ROPE),
               ret[1], ret[2])
    return dict(routed=routed, mod_row0=mod_row0, tokens_per_row=tokens_per_row, tm=tm, shape=(B, L, D)), new


def kernel(x_prompt, x_sample, cache_mla_ckv, cache_mla_krope, state_ret_fwd, state_ret_bwd, c, c_ctx,
           w_ada, b_ada, norm1_g, w_in, mla_q_norm_g, w_uq, mla_kv_norm_g, w_ukv, mla_q_gain, mla_k_gain,
           ret_decay_fwd, ret_decay_bwd, ret_norm_g, w_o, norm2_g, router_w, router_bias, exp_w_gate,
           exp_w_up, exp_w_down, sh_w_gate, sh_w_up, sh_w_down):
    depth = w_ada.shape[0]
    n_dec = c.shape[0]
    assert 1 + n_dec <= MOD_ROWS
    cond = jnp.concatenate([c_ctx[None], c, jnp.zeros((MOD_ROWS - 1 - n_dec, D_MODEL), F32)], axis=0)

    y_prompt, y_sample = x_prompt, x_sample
    ckv_l, kr_l, sf_l, sb_l = [], [], [], []
    for l in range(depth):
        wts = _prep_weights(l, w_in, norm1_g, mla_q_norm_g, w_uq, mla_kv_norm_g, w_ukv, mla_q_gain,
                            mla_k_gain, w_o, norm2_g, router_w, router_bias, exp_w_gate, exp_w_up,
                            exp_w_down, sh_w_gate, sh_w_up, sh_w_down)
        ret_wts = (jnp.broadcast_to(ret_decay_fwd[l].astype(F32)[:, None], (H_RET, LANES)),
                   jnp.broadcast_to(ret_decay_bwd[l].astype(F32)[:, None], (H_RET, LANES)),
                   jnp.tile(ret_norm_g[l].reshape(1, D_RET), (1, LANES // D_RET)))
        mod = _adaln(cond, w_ada[l], b_ada[l]).reshape(MOD_ROWS, 6, D_MODEL)
        half_p, new = _mixer_half(y_prompt, mod, 0, wts, ret_wts, None)
        ckv_l.append(new[0]); kr_l.append(new[1]); sf_l.append(new[2]); sb_l.append(new[3])
        ctx = (cache_mla_ckv[:, l], cache_mla_krope[:, l], state_ret_fwd[:, l], state_ret_bwd[:, l])
        half_s, _ = _mixer_half(y_sample, mod, 1, wts, ret_wts, ctx)
        y_prompt, y_sample = _moe([half_p, half_s], mod, wts)

    return (y_prompt, y_sample, jnp.stack(ckv_l, axis=1), jnp.stack(kr_l, axis=1),
            jnp.stack(sf_l, axis=1), jnp.stack(sb_l, axis=1))
```

```python
import functools
import math

import numpy as np
import jax
import jax.numpy as jnp
from jax import lax
from jax.experimental import pallas as pl
from jax.experimental.pallas import tpu as pltpu
from jax.experimental.pallas import tpu_sc as plsc

F32 = jnp.float32
BF16 = jnp.bfloat16

D_MODEL = 1024
GRID_W = 64
H_MLA = 8
D_NOPE = 64
D_ROPE = 32
D_QK = D_NOPE + D_ROPE
D_V = 64
Q_RANK = 256
KV_RANK = 128
H_RET = 8
D_RET = 64
RET_CHUNK = 128
D_RET_ALL = H_RET * D_RET
N_EXPERTS = 64
TOP_K = 6
D_EXPERT = 256
D_SHARED = 256
ROUTED_SCALE = 2.5
ROPE_BASE = 10000.0
EPS = 1e-6
LOG2_E = math.log2(math.e)

LANES = 128
HEAD_SLOT = LANES
D_IN_PAD = Q_RANK + KV_RANK + LANES + 4 * D_RET_ALL
MOD_ROWS = 16
VMEM_LIMIT = 56 * 1024 * 1024


def _cparams(*sem):
    return pltpu.CompilerParams(dimension_semantics=sem, vmem_limit_bytes=VMEM_LIMIT)


def _dot(a, b):
    return jnp.dot(a, b, preferred_element_type=F32)


def _dot_nt(a, b):
    return lax.dot_general(a, b, (((1,), (1,)), ((), ())), preferred_element_type=F32)


def _dot_tn(a, b):
    return lax.dot_general(a, b, (((0,), (0,)), ((), ())), preferred_element_type=F32)


def _rms(x, g):
    return x * lax.rsqrt(jnp.mean(x * x, axis=-1, keepdims=True) + EPS) * g


def _silu(x):
    return x * jax.nn.sigmoid(x)


def _adaln_kernel(c_ref, w_ref, b_ref, o_ref):
    s = _silu(c_ref[...])
    o_ref[...] = _dot(s.astype(BF16), w_ref[...].astype(BF16)) + b_ref[...]


def _adaln(cond, w_ada, b_ada):
    n_out = w_ada.shape[1]
    bn = 1536
    return pl.pallas_call(
        _adaln_kernel,
        grid=(n_out // bn,),
        in_specs=[pl.BlockSpec((MOD_ROWS, D_MODEL), lambda j: (0, 0)),
                  pl.BlockSpec((D_MODEL, bn), lambda j: (0, j)),
                  pl.BlockSpec((1, bn), lambda j: (0, j))],
        out_specs=pl.BlockSpec((MOD_ROWS, bn), lambda j: (0, j)),
        out_shape=jax.ShapeDtypeStruct((MOD_ROWS, n_out), F32),
        compiler_params=_cparams("arbitrary"),
        name="adaln",
    )(cond, w_ada, b_ada.reshape(1, n_out))


def _swap_halves(w, half_pair):
    g = w.reshape(*w.shape[:-1], w.shape[-1] // (2 * half_pair), 2, half_pair)
    return g[..., ::-1, :].reshape(w.shape)


def _rope_tables(n_tokens):
    t = np.arange(n_tokens)
    row = (t // GRID_W).astype(np.float64)
    col = (t % GRID_W).astype(np.float64)

    def axis_tables(width, lane0, period, fill):
        p = width // 2
        inv = 1.0 / (ROPE_BASE ** (np.arange(p, dtype=np.float64) / p))
        cos = np.full((n_tokens, LANES), fill)
        sin = np.zeros((n_tokens, LANES))
        starts = range(lane0, LANES, period) if period else (lane0,)
        for s0 in starts:
            for base, pos in ((s0, row), (s0 + width, col)):
                ang = pos[:, None] * inv[None, :]
                c, sn = np.cos(ang), np.sin(ang)
                cos[:, base:base + p] = c
                cos[:, base + p:base + 2 * p] = c
                sin[:, base:base + p] = -sn
                sin[:, base + p:base + 2 * p] = sn
        return cos, sin

    qk_cos, qk_sin = axis_tables(D_ROPE // 2, D_NOPE, 0, 1.0)
    ret_cos, ret_sin = axis_tables(D_RET // 2, 0, D_RET, 1.0)
    as_f32 = lambda *a: tuple(jnp.asarray(x, F32) for x in a)
    return as_f32(qk_cos, qk_sin), as_f32(ret_cos, ret_sin)


def _head_sums(x):
    r = lax.broadcasted_iota(jnp.int32, (2 * HEAD_SLOT, 2 * HEAD_SLOT), 0) // HEAD_SLOT
    c = lax.broadcasted_iota(jnp.int32, (2 * HEAD_SLOT, 2 * HEAD_SLOT), 1) // HEAD_SLOT
    ones_bd = jnp.where(r == c, 1.0, 0.0).astype(BF16)
    xb = x.astype(BF16)
    w = 2 * HEAD_SLOT
    return jnp.concatenate([_dot(xb[:, g * w:(g + 1) * w], ones_bd) for g in range(x.shape[1] // w)], axis=1)


def _tile_heads(v):
    return jnp.concatenate([v] * H_MLA, axis=1)


def _kv_heads(ckv_n, krp, krp_sw, wkv_ref, kgain, kgain_sw, tabs, k_ref, v_ref):
    kvp = _dot(ckv_n.astype(BF16), wkv_ref[...])
    kn = kvp[:, :H_MLA * HEAD_SLOT]
    ms = _head_sums(kn * kn + _tile_heads(krp * krp)) * (1.0 / D_QK)
    r = lax.rsqrt(ms + EPS)
    k = (kn + _tile_heads(krp)) * r * _tile_heads(kgain)
    if tabs is not None:
        cos, sin = tabs
        k = k * _tile_heads(cos) + r * _tile_heads(krp_sw * kgain_sw * sin)
    k_ref[...] = k.astype(BF16)
    v_lane = lax.broadcasted_iota(jnp.int32, (1, H_MLA * HEAD_SLOT), 1) % HEAD_SLOT
    v_ref[...] = (kvp[:, H_MLA * HEAD_SLOT:] + jnp.where(v_lane == D_V, 1.0, 0.0)).astype(BF16)


def _inproj_kernel(rope, *refs):
    (x_ref, mod_ref, n1g_ref, w1_ref, qng_ref, wuq_ref, kvg_ref, wkv_ref, qgain_ref,
     kgain_ref) = refs[:10]
    refs = refs[10:]
    if rope:
        qk_tabs = tuple(r[...] for r in refs[:2])
        ret_cos, ret_sin = (r[...] for r in refs[2:4])
        refs = refs[4:]
    else:
        qk_tabs = None
    ckvn_ref, krp_ref, q_ref, k_ref, v_ref, rq_ref, rk_ref, rv_ref, rg_ref = refs

    mod = mod_ref[0]
    h = _rms(x_ref[...], n1g_ref[...]) * (1.0 + mod[1:2]) + mod[0:1]
    z = _dot(h.astype(BF16), w1_ref[...])
    o = 0
    cq = z[:, o:o + Q_RANK]; o += Q_RANK
    ckv = z[:, o:o + KV_RANK]; o += KV_RANK
    krp = z[:, o:o + LANES]; o += LANES
    rq = z[:, o:o + D_RET_ALL]; o += D_RET_ALL
    rk = z[:, o:o + D_RET_ALL]; o += D_RET_ALL
    rv = z[:, o:o + D_RET_ALL]; o += D_RET_ALL
    rg = z[:, o:o + D_RET_ALL]; o += D_RET_ALL
    krp_sw = None
    if rope:
        krp_sw = z[:, o:o + LANES]; o += LANES
        rq_sw = z[:, o:o + D_RET_ALL]; o += D_RET_ALL
        rk_sw = z[:, o:o + D_RET_ALL]

    ckv_n = _rms(ckv, kvg_ref[...])
    ckvn_ref[...] = ckv_n
    krp_ref[...] = krp

    qn = _rms(cq, qng_ref[...]).astype(BF16)
    scale = D_QK ** -0.5 * LOG2_E
    q = _dot(qn, wuq_ref[:, :H_MLA * HEAD_SLOT])
    r = lax.rsqrt(_head_sums(q * q) * (1.0 / D_QK) + EPS)
    q = q * r * _tile_heads(qgain_ref[0:1] * scale)
    if rope:
        cos, sin = qk_tabs
        q_sw = _dot(qn, wuq_ref[:, H_MLA * HEAD_SLOT:])
        q = q * _tile_heads(cos) + q_sw * r * _tile_heads(qgain_ref[1:2] * scale * sin)
    q_ref[...] = q.astype(BF16)

    _kv_heads(ckv_n, krp, krp_sw, wkv_ref, kgain_ref[0:1], kgain_ref[1:2], qk_tabs, k_ref, v_ref)

    for t in range(D_RET_ALL // LANES):
        sl = slice(t * LANES, (t + 1) * LANES)
        rq_t, rk_t = rq[:, sl], rk[:, sl]
        if rope:
            rq_t = rq_t * ret_cos + rq_sw[:, sl] * ret_sin
            rk_t = rk_t * ret_cos + rk_sw[:, sl] * ret_sin
        rq_ref[:, sl] = rq_t.astype(BF16)
        rk_ref[:, sl] = (rk_t * (D_RET ** -0.5)).astype(BF16)
    rv_ref[...] = rv.astype(BF16)
    rg_ref[...] = _silu(rg).astype(BF16)


def _const_spec(shape):
    return pl.BlockSpec(shape, lambda i: (0,) * len(shape))


def _inproj(x, mod, mod_row0, tokens_per_row, wts, rope_tabs, tm):
    T = x.shape[0]
    bpr = tokens_per_row // tm
    rope = rope_tabs is not None
    tok = lambda w: pl.BlockSpec((tm, w), lambda i: (i, 0))
    in_specs = [tok(D_MODEL),
                pl.BlockSpec((1, 6, D_MODEL), lambda i: (mod_row0 + i // bpr, 0, 0)),
                _const_spec((1, D_MODEL)),
                _const_spec((D_MODEL, D_IN_PAD + (LANES + 2 * D_RET_ALL if rope else 0))),
                _const_spec((1, Q_RANK)),
                _const_spec((Q_RANK, 2 * H_MLA * HEAD_SLOT)),
                _const_spec((1, KV_RANK)),
                _const_spec((KV_RANK, 2 * H_MLA * HEAD_SLOT)),
                _const_spec((2, LANES)),
                _const_spec((2, LANES))]
    args = [x, mod, wts["norm1_g"], wts["w1_rope" if rope else "w1"], wts["q_norm_g"], wts["w_uq"],
            wts["kv_norm_g"], wts["w_kv"], wts["q_gain"], wts["k_gain"]]
    if rope:
        qk_tabs, ret_tabs = rope_tabs
        nb = qk_tabs[0].shape[0] // tm
        in_specs += [pl.BlockSpec((tm, LANES), lambda i: (i % nb, 0))] * 4
        args += list(qk_tabs) + list(ret_tabs)
    widths = [(KV_RANK, F32), (LANES, F32), (H_MLA * HEAD_SLOT, BF16), (H_MLA * HEAD_SLOT, BF16),
              (H_MLA * HEAD_SLOT, BF16)] + [(D_RET_ALL, BF16)] * 4
    return pl.pallas_call(
        functools.partial(_inproj_kernel, rope),
        grid=(T // tm,),
        in_specs=in_specs,
        out_specs=[tok(w) for w, _ in widths],
        out_shape=[jax.ShapeDtypeStruct((T, w), dt) for w, dt in widths],
        compiler_params=_cparams("parallel"),
        name="inproj_rope" if rope else "inproj",
    )(*args)


def _ctx_kv_kernel(ckv_ref, krp_ref, wkv_ref, kgain_ref, k_ref, v_ref):
    _kv_heads(ckv_ref[...], krp_ref[...], None, wkv_ref, kgain_ref[0:1], None, None, k_ref, v_ref)


def _ctx_kv(ckv, krp, wts, tm):
    T = ckv.shape[0]
    tok = lambda w: pl.BlockSpec((tm, w), lambda i: (i, 0))
    return pl.pallas_call(
        _ctx_kv_kernel,
        grid=(T // tm,),
        in_specs=[tok(KV_RANK), tok(LANES),
                  _const_spec((KV_RANK, 2 * H_MLA * HEAD_SLOT)), _const_spec((2, LANES))],
        out_specs=[tok(H_MLA * HEAD_SLOT), tok(H_MLA * HEAD_SLOT)],
        out_shape=[jax.ShapeDtypeStruct((T, H_MLA * HEAD_SLOT), BF16),
                   jax.ShapeDtypeStruct((T, H_MLA * HEAD_SLOT), BF16)],
        compiler_params=_cparams("parallel"),
        name="ctx_kv",
    )(ckv, krp, wts["w_kv"], wts["k_gain"])


ATTN_KEY_CHUNK = 2048


def _attn_kernel(has_ctx, *refs):
    if has_ctx:
        q_ref, k_ref, v_ref, kc_ref, vc_ref, o_ref = refs
    else:
        q_ref, k_ref, v_ref, o_ref = refs
    sources = [(k_ref, v_ref)] + ([(kc_ref, vc_ref)] if has_ctx else [])
    chunks = []
    for kr, vr in sources:
        n = kr.shape[1]
        kc = _pick(n, (ATTN_KEY_CHUNK, 256, 128))
        chunks += [(kr, vr, slice(c * kc, (c + 1) * kc)) for c in range(n // kc)]
    tq = q_ref.shape[1]
    outs = []
    for j in range(2):
        hs = slice(j * HEAD_SLOT, (j + 1) * HEAD_SLOT)
        q = q_ref[0, :, hs]
        m = jnp.full((tq, 1), -jnp.inf, F32)
        acc = jnp.zeros((tq, HEAD_SLOT), F32)
        for kr, vr, rows in chunks:
            s = _dot_nt(q, kr[0, rows, hs])
            m_new = jnp.maximum(m, jnp.max(s, axis=-1, keepdims=True))
            p = jnp.exp2(s - m_new).astype(BF16)
            acc = acc * jnp.exp2(m - m_new) + _dot(p, vr[0, rows, hs])
            m = m_new
        outs.append(acc[:, :D_V] / acc[:, D_V:D_V + 1])
    o_ref[0] = jnp.concatenate(outs, axis=-1).astype(BF16)


def _attention(q, k, v, kc, vc, tq):
    B, L, _ = q.shape
    has_ctx = kc is not None
    in_specs = [pl.BlockSpec((1, tq, 2 * HEAD_SLOT), lambda b, hp, i: (b, i, hp)),
                pl.BlockSpec((1, L, 2 * HEAD_SLOT), lambda b, hp, i: (b, 0, hp)),
                pl.BlockSpec((1, L, 2 * HEAD_SLOT), lambda b, hp, i: (b, 0, hp))]
    args = [q, k, v]
    if has_ctx:
        Lc = kc.shape[1]
        in_specs += [pl.BlockSpec((1, Lc, 2 * HEAD_SLOT), lambda b, hp, i: (b, 0, hp)),
                     pl.BlockSpec((1, Lc, 2 * HEAD_SLOT), lambda b, hp, i: (b, 0, hp))]
        args += [kc, vc]
    return pl.pallas_call(
        functools.partial(_attn_kernel, has_ctx),
        grid=(B, H_MLA // 2, L // tq),
        in_specs=in_specs,
        out_specs=pl.BlockSpec((1, tq, 2 * D_V), lambda b, hp, i: (b, i, hp)),
        out_shape=jax.ShapeDtypeStruct((B, L, H_MLA * D_V), BF16),
        compiler_params=_cparams("parallel", "parallel", "arbitrary"),
        name="attn_ctx" if has_ctx else "attn",
    )(*args)


RET_UNROLL = 8
RET_SHORT_SEQ = 1024


def _ret_pair(has_s0, want_state, n_chunks, pair, hp, *refs):
    rq_ref, rk_ref, rv_ref, rg_ref, decf_ref, decb_ref, g_ref = refs[:7]
    refs = refs[7:]
    if has_s0:
        s0f_ref, s0b_ref = refs[:2]
        refs = refs[2:]
    o_ref = refs[0]
    refs = refs[1:]
    if want_state:
        sf_ref, sb_ref = refs[:2]
        refs = refs[2:]
    kv_ref, sp_ref = refs

    C = RET_CHUNK
    W = 2 * D_RET
    ls = slice(pair * W, (pair + 1) * W)
    lane = lax.broadcasted_iota(jnp.int32, (1, W), 1)
    lane_h1 = lane >= D_RET
    row_h1 = lax.broadcasted_iota(jnp.int32, (W, 1), 0) >= D_RET
    blockdiag = row_h1 == lane_h1
    pos = lax.broadcasted_iota(jnp.int32, (C, 1), 0).astype(F32)
    diff = (lax.broadcasted_iota(jnp.int32, (C, C), 0)
            - lax.broadcasted_iota(jnp.int32, (C, C), 1)).astype(F32)

    lg = {}
    for name, ref in (("f", decf_ref), ("b", decb_ref)):
        a = -jnp.exp(ref[pl.ds(2 * hp, 1), :])
        b = -jnp.exp(ref[pl.ds(2 * hp + 1, 1), :])
        lg[name] = (a[:, :1], b[:, :1], jnp.where(lane_h1, b, a))
    lgf0, lgf1, lgf = lg["f"]
    lgb0, lgb1, lgb = lg["b"]
    qdf, kdf = jnp.exp(lgf * (pos + 1.0)), jnp.exp(lgf * (C - 1.0 - pos))
    qdb, kdb = jnp.exp(lgb * (C - pos)), jnp.exp(lgb * pos)
    row_h1_full = lax.broadcasted_iota(jnp.int32, (W, W), 0) >= D_RET
    cdf = jnp.where(row_h1_full, jnp.exp(lgf1 * C), jnp.exp(lgf0 * C))
    cdb = jnp.where(row_h1_full, jnp.exp(lgb1 * C), jnp.exp(lgb0 * C))

    def intra_decay(lf, lb):
        return jnp.where(diff >= 0, jnp.exp(lf * jnp.maximum(diff, 0.0)), jnp.exp(lb * jnp.maximum(-diff, 0.0)))

    dmat0, dmat1 = intra_decay(lgf0, lgb0), intra_decay(lgf1, lgb1)
    g = g_ref[...]

    def chunk_rows(n):
        return pl.ds(pl.multiple_of(n * C, C), C)

    def phase_a(n, carry):
        rows = chunk_rows(n)
        k = rk_ref[0, rows, ls].astype(F32)
        k2 = jnp.concatenate([(k * kdf).astype(BF16), (k * kdb).astype(BF16)], axis=1)
        kv_ref[n] = _dot_tn(k2, rv_ref[0, rows, ls])
        return carry

    unroll = min(RET_UNROLL, n_chunks)
    lax.fori_loop(0, n_chunks, phase_a, 0, unroll=unroll)

    def init_state(ref):
        if not has_s0:
            return jnp.zeros((W, W), F32)
        z = jnp.zeros((D_RET, D_RET), F32)
        return jnp.concatenate([jnp.concatenate([ref[0, 2 * pair], z], axis=1),
                                jnp.concatenate([z, ref[0, 2 * pair + 1]], axis=1)], axis=0)

    def scan_f(n, S):
        sp_ref[n, 0:W, :] = S.astype(BF16)
        return cdf * S + jnp.where(blockdiag, kv_ref[n, 0:W, :], 0.0)

    def scan_b(i, S):
        n = n_chunks - 1 - i
        sp_ref[n, W:2 * W, :] = S.astype(BF16)
        return cdb * S + jnp.where(blockdiag, kv_ref[n, W:2 * W, :], 0.0)

    Sf = lax.fori_loop(0, n_chunks, scan_f, init_state(s0f_ref if has_s0 else None))
    Sb = lax.fori_loop(0, n_chunks, scan_b, init_state(s0b_ref if has_s0 else None))
    if want_state:
        for S, ref in ((Sf, sf_ref), (Sb, sb_ref)):
            ref[0, 2 * pair] = S[0:D_RET, 0:D_RET]
            ref[0, 2 * pair + 1] = S[D_RET:W, D_RET:W]

    def phase_c(n, carry):
        rows = chunk_rows(n)
        q = rq_ref[0, rows, ls]
        k = rk_ref[0, rows, ls]
        v = rv_ref[0, rows, ls]
        zero = jnp.zeros_like(q)
        sc0 = _dot_nt(jnp.where(lane_h1, zero, q), k) * dmat0
        sc1 = _dot_nt(jnp.where(lane_h1, q, zero), k) * dmat1
        p = jnp.concatenate([sc0.astype(BF16), sc1.astype(BF16)], axis=1)
        v_bd = jnp.concatenate([jnp.where(lane_h1, zero, v), jnp.where(lane_h1, v, zero)], axis=0)
        qf = q.astype(F32)
        q2 = jnp.concatenate([(qf * qdf).astype(BF16), (qf * qdb).astype(BF16)], axis=1)
        o = _dot(p, v_bd) + _dot(q2, sp_ref[n])
        o2 = o * o
        ss0 = jnp.sum(jnp.where(lane_h1, 0.0, o2), axis=-1, keepdims=True)
        ss1 = jnp.sum(jnp.where(lane_h1, o2, 0.0), axis=-1, keepdims=True)
        ms = jnp.where(lane_h1, ss1, ss0) * (1.0 / D_RET)
        o = o * lax.rsqrt(ms + EPS) * g * rg_ref[0, rows, ls].astype(F32)
        o_ref[0, rows, ls] = o.astype(BF16)
        return carry

    lax.fori_loop(0, n_chunks, phase_c, 0, unroll=unroll)


def _ret_kernel(has_s0, want_state, n_chunks, pairs, *refs):
    for pair in range(pairs):
        _ret_pair(has_s0, want_state, n_chunks, pair, pl.program_id(1) * pairs + pair, *refs)


def _retention(rq, rk, rv, rg, dec_f, dec_b, g, s0f, s0b, want_state):
    B, L, _ = rq.shape
    has_s0 = s0f is not None
    pairs = H_RET // 2 if L <= RET_SHORT_SEQ else 1
    seq = pl.BlockSpec((1, L, pairs * LANES), lambda b, hp: (b, 0, hp))
    st = pl.BlockSpec((1, 2 * pairs, D_RET, D_RET), lambda b, hp: (b, hp, 0, 0))
    cst = lambda shape: pl.BlockSpec(shape, lambda b, hp: (0,) * len(shape))
    in_specs = [seq] * 4 + [cst((H_RET, LANES)), cst((H_RET, LANES)), cst((1, LANES))]
    args = [rq, rk, rv, rg, dec_f, dec_b, g]
    if has_s0:
        in_specs += [st, st]
        args += [s0f, s0b]
    out_specs = [seq]
    out_shape = [jax.ShapeDtypeStruct((B, L, D_RET_ALL), BF16)]
    if want_state:
        out_specs += [st, st]
        out_shape += [jax.ShapeDtypeStruct((B, H_RET, D_RET, D_RET), F32)] * 2
    return pl.pallas_call(
        functools.partial(_ret_kernel, has_s0, want_state, L // RET_CHUNK, pairs),
        grid=(B, H_RET // 2 // pairs),
        in_specs=in_specs,
        out_specs=out_specs,
        out_shape=out_shape,
        scratch_shapes=[pltpu.VMEM((L // RET_CHUNK, 4 * D_RET, LANES), F32),
                        pltpu.VMEM((L // RET_CHUNK, 4 * D_RET, LANES), BF16)],
        compiler_params=_cparams("parallel", "parallel"),
        name="retention_s0" if has_s0 else "retention",
    )(*args)


def _pack_pairs(x):
    w = x.shape[1] // 2
    hi = lax.bitcast_convert_type(x[:, :w].astype(BF16).astype(F32), jnp.uint32)
    lo = lax.bitcast_convert_type(x[:, w:].astype(BF16).astype(F32), jnp.uint32)
    return hi | (lo >> 16)


def _unpack_pairs(p):
    hi = lax.bitcast_convert_type(p & jnp.uint32(0xFFFF0000), F32)
    lo = lax.bitcast_convert_type(p << 16, F32)
    return jnp.concatenate([hi, lo], axis=1)


D_PACKED = D_MODEL // 2


def _outproj_kernel(omla_ref, oret_ref, x_ref, mod_ref, wo_ref, n2g_ref, rwh_ref, rwl_ref, rb_ref, tri_ref,
                    x2_ref, h2p_ref, code_ref, rank_ref, wk_ref, cnt_ref):
    @pl.when(pl.program_id(0) == 0)
    def _():
        cnt_ref[...] = jnp.zeros_like(cnt_ref)

    mod = mod_ref[0]
    half = H_MLA * D_V
    a = _dot(omla_ref[...], wo_ref[:half, :]) + _dot(oret_ref[...], wo_ref[half:, :])
    x2 = x_ref[...] + mod[2:3] * a
    x2_ref[...] = x2
    h2 = _rms(x2, n2g_ref[...]) * (1.0 + mod[4:5]) + mod[3:4]
    h2p_ref[...] = _pack_pairs(h2)
    h2_hi = h2.astype(BF16)
    h2_lo = (h2 - h2_hi.astype(F32)).astype(BF16)
    rw_hi = rwh_ref[...]
    logits = _dot_nt(rw_hi, h2_hi) + _dot_nt(rw_hi, h2_lo) + _dot_nt(rwl_ref[...], h2_hi)
    scores = jax.nn.sigmoid(logits)
    tm = scores.shape[1]
    expert = lax.broadcasted_iota(jnp.int32, scores.shape, 0)
    cand = scores + rb_ref[...]
    code = jnp.zeros(scores.shape, jnp.int32)
    for k in range(TOP_K):
        mx = jnp.max(cand, axis=0, keepdims=True)
        first = jnp.min(jnp.where(cand == mx, expert, LANES), axis=0, keepdims=True)
        hit = expert == first
        code = jnp.where(hit, k + 1, code)
        cand = jnp.where(hit, -jnp.inf, cand)
    picked = code > 0
    sel = jnp.where(picked, scores, 0.0)
    gates = sel * (ROUTED_SCALE / jnp.sum(sel, axis=0, keepdims=True))
    code_ref[...] = code

    onehot = jnp.where(picked, 1.0, 0.0)
    carry = cnt_ref[...].astype(F32)
    rank = _dot(onehot.astype(BF16), tri_ref[...])
    rank_ref[...] = rank + jnp.concatenate([carry] * (tm // LANES), axis=1)
    cnt_ref[...] = (carry + jnp.sum(onehot, axis=1, keepdims=True)).astype(jnp.int32)

    row = lax.broadcasted_iota(jnp.int32, scores.shape, 0)
    wk_t = jnp.zeros(scores.shape, F32)
    for k in range(TOP_K):
        w = jnp.sum(jnp.where(code == k + 1, gates, 0.0), axis=0, keepdims=True)
        wk_t = jnp.where(row == k, w, wk_t)
    wk_ref[...] = wk_t.T


def _outproj(o_mla, o_ret, x, mod, mod_row0, tokens_per_row, wts, tm):
    T = x.shape[0]
    bpr = tokens_per_row // tm
    tok = lambda w: pl.BlockSpec((tm, w), lambda i: (i, 0))
    emaj = pl.BlockSpec((LANES, tm), lambda i: (0, i))
    tri = jnp.triu(jnp.ones((tm, tm), BF16), 1)
    bias = jnp.broadcast_to(wts["router_bias"].reshape(LANES, 1), (LANES, tm))
    return pl.pallas_call(
        _outproj_kernel,
        grid=(T // tm,),
        in_specs=[tok(H_MLA * D_V), tok(D_RET_ALL), tok(D_MODEL),
                  pl.BlockSpec((1, 6, D_MODEL), lambda i: (mod_row0 + i // bpr, 0, 0)),
                  _const_spec((D_MODEL, D_MODEL)), _const_spec((1, D_MODEL)),
                  _const_spec((LANES, D_MODEL)), _const_spec((LANES, D_MODEL)), _const_spec((LANES, tm)),
                  _const_spec((tm, tm))],
        out_specs=[tok(D_MODEL), tok(D_PACKED), emaj, emaj, tok(LANES), _const_spec((LANES, LANES))],
        out_shape=[jax.ShapeDtypeStruct((T, D_MODEL), F32),
                   jax.ShapeDtypeStruct((T, D_PACKED), jnp.uint32),
                   jax.ShapeDtypeStruct((LANES, T), jnp.int32),
                   jax.ShapeDtypeStruct((LANES, T), F32),
                   jax.ShapeDtypeStruct((T, LANES), F32),
                   jax.ShapeDtypeStruct((LANES, LANES), jnp.int32)],
        compiler_params=_cparams("arbitrary"),
        name="outproj_router",
    )(o_mla, o_ret, x, mod, wts["w_o"], wts["norm2_g"], wts["router_hi"], wts["router_lo"], bias, tri)


def _plan_kernel(code_ref, rank_ref, first_ref, pos_ref):
    code = code_ref[...]
    tm = code.shape[1]
    row = rank_ref[...] + jnp.concatenate([first_ref[...]] * (tm // LANES), axis=1)
    krow = lax.broadcasted_iota(jnp.int32, pos_ref.shape, 0)
    pos = jnp.zeros(pos_ref.shape, F32)
    for k in range(TOP_K):
        pos = jnp.where(krow == k, jnp.sum(jnp.where(code == k + 1, row, 0.0), axis=0, keepdims=True), pos)
    pos_ref[...] = pos.astype(jnp.int32)


def _plan(code, rank, first_rows, tm):
    T = code.shape[1]
    first = jnp.broadcast_to(jnp.pad(first_rows.astype(F32), (0, LANES - N_EXPERTS)).reshape(LANES, 1),
                             (LANES, LANES))
    emaj = pl.BlockSpec((LANES, tm), lambda i: (0, i))
    return pl.pallas_call(
        _plan_kernel,
        grid=(T // tm,),
        in_specs=[emaj, emaj, _const_spec((LANES, LANES))],
        out_specs=pl.BlockSpec((8, tm), lambda i: (0, i)),
        out_shape=jax.ShapeDtypeStruct((8, T), jnp.int32),
        compiler_params=_cparams("parallel"),
        name="moe_plan",
    )(code, rank, first)


SC_WORKERS = 32
SC_CHUNK = 64


def _sc_worker_id():
    return lax.axis_index("s") * 2 + lax.axis_index("c")


def _sc_mesh():
    return plsc.VectorSubcoreMesh(core_axis_name="c", subcore_axis_name="s")


def _sc_gather_rows(table, idx):
    N = idx.shape[0]
    W = table.shape[1]
    per_w = N // SC_WORKERS
    n_chunks = per_w // SC_CHUNK
    assert N == SC_WORKERS * n_chunks * SC_CHUNK and n_chunks % 2 == 0

    @functools.partial(
        pl.kernel, mesh=_sc_mesh(), out_type=jax.ShapeDtypeStruct((N, W), table.dtype),
        scratch_types=[pltpu.VMEM((n_chunks, SC_CHUNK), jnp.int32),
                       pltpu.VMEM((2, SC_CHUNK, W), table.dtype),
                       pltpu.SemaphoreType.DMA((2,)), pltpu.SemaphoreType.DMA((2,))],
        name="sc_gather_rows")
    def k(table_hbm, idx_hbm, out_hbm, idx_v, rows_v, gsem, wsem):
        wid = _sc_worker_id()
        base = wid * per_w
        pltpu.sync_copy(idx_hbm.at[wid], idx_v)

        def gather(j, b):
            return pltpu.make_async_copy(table_hbm.at[idx_v.at[j]], rows_v.at[b], gsem.at[b])

        def writeback(j, b):
            off = pl.multiple_of(base + j * SC_CHUNK, SC_CHUNK)
            return pltpu.make_async_copy(rows_v.at[b], out_hbm.at[pl.ds(off, SC_CHUNK)], wsem.at[b])

        gather(0, 0).start()

        @pl.loop(0, n_chunks, step=2)
        def _(j):
            for b in range(2):
                jj = j + b

                @pl.when(jj + 1 < n_chunks)
                def _():
                    @pl.when(jj >= 1)
                    def _():
                        writeback(jj - 1, 1 - b).wait()
                    gather(jj + 1, 1 - b).start()

                gather(jj, b).wait()
                writeback(jj, b).start()

        writeback(n_chunks - 2, 0).wait()
        writeback(n_chunks - 1, 1).wait()

    return k(table, idx.reshape(SC_WORKERS, n_chunks, SC_CHUNK))


def _sc_dispatch_rows(rows_list, pos_list, n_out):
    W = rows_list[0].shape[1]
    K = pos_list[0].shape[0]
    n_src = len(rows_list)
    plan = []
    idx_list = []
    for rows, pos in zip(rows_list, pos_list):
        T = rows.shape[0]
        per_w = T // SC_WORKERS
        n_chunks = per_w // SC_CHUNK
        assert T == SC_WORKERS * n_chunks * SC_CHUNK and n_chunks % 2 == 0
        plan.append((per_w, n_chunks))
        idx_list.append(pos.reshape(K, SC_WORKERS, n_chunks, SC_CHUNK).transpose(1, 2, 0, 3))
    max_chunks = max(n for _, n in plan)

    @functools.partial(
        pl.kernel, mesh=_sc_mesh(), out_type=jax.ShapeDtypeStruct((n_out, W), rows_list[0].dtype),
        scratch_types=[pltpu.VMEM((max_chunks, K, SC_CHUNK), jnp.int32),
                       pltpu.VMEM((2, SC_CHUNK, W), rows_list[0].dtype),
                       pltpu.SemaphoreType.DMA((2,)), pltpu.SemaphoreType.DMA((2,))],
        name="sc_dispatch_rows")
    def k(*refs):
        rows_refs, idx_refs = refs[:n_src], refs[n_src:2 * n_src]
        out_hbm, idx_v, rows_v, lsem, ssem = refs[2 * n_src:]
        wid = _sc_worker_id()
        for rows_hbm, idx_hbm, (per_w, n_chunks) in zip(rows_refs, idx_refs, plan):
            base = wid * per_w
            pltpu.sync_copy(idx_hbm.at[wid], idx_v.at[pl.ds(0, n_chunks)])

            def load(j, b):
                off = pl.multiple_of(base + j * SC_CHUNK, SC_CHUNK)
                return pltpu.make_async_copy(rows_hbm.at[pl.ds(off, SC_CHUNK)], rows_v.at[b], lsem.at[b])

            def scatter(j, b, kk):
                return pltpu.make_async_copy(rows_v.at[b], out_hbm.at[idx_v.at[j, kk]], ssem.at[b])

            load(0, 0).start()

            @pl.loop(0, n_chunks, step=2)
            def _(j):
                for b in range(2):
                    jj = j + b

                    @pl.when(jj + 1 < n_chunks)
                    def _():
                        @pl.when(jj >= 1)
                        def _():
                            for kk in range(K):
                                scatter(jj - 1, 1 - b, kk).wait()
                        load(jj + 1, 1 - b).start()

                    load(jj, b).wait()
                    for kk in range(K):
                        scatter(jj, b, kk).start()

            for kk in range(K):
                scatter(n_chunks - 2, 0, kk).wait()
            for kk in range(K):
                scatter(n_chunks - 1, 1, kk).wait()

    return k(*rows_list, *idx_list)


ROW_TILE = 1024
MOE_GATHER_GROUP = 8192


def _experts_kernel(te_ref, tv_ref, xs_ref, wg_ref, wu_ref, wd_ref, ys_ref, wg_s, wu_s, wd_s):
    i = pl.program_id(0)
    valid = tv_ref[i]

    @pl.when(jnp.logical_or(i == 0, te_ref[i] != te_ref[jnp.maximum(i - 1, 0)]))
    def _():
        wg_s[...] = wg_ref[0].astype(BF16)
        wu_s[...] = wu_ref[0].astype(BF16)
        wd_s[...] = wd_ref[0].astype(BF16)

    @pl.when(valid > 0)
    def _():
        row = lax.broadcasted_iota(jnp.int32, (ROW_TILE, 1), 0)
        p = jnp.where(row < valid, xs_ref[...], jnp.uint32(0))
        x = _unpack_pairs(p).astype(BF16)
        act = _silu(_dot(x, wg_s[...])) * _dot(x, wu_s[...])
        ys_ref[...] = _pack_pairs(_dot(act.astype(BF16), wd_s[...]))

    @pl.when(valid <= 0)
    def _():
        ys_ref[...] = jnp.zeros_like(ys_ref)


def _experts(xs, tile_expert, tile_valid, wts):
    n_tiles = tile_expert.shape[0]
    ex = lambda a, b: pl.BlockSpec((1, a, b), lambda i, te, tv: (te[i], 0, 0))
    row = pl.BlockSpec((ROW_TILE, D_PACKED), lambda i, te, tv: (i, 0))
    return pl.pallas_call(
        _experts_kernel,
        grid_spec=pltpu.PrefetchScalarGridSpec(
            num_scalar_prefetch=2, grid=(n_tiles,),
            in_specs=[row, ex(D_MODEL, D_EXPERT), ex(D_MODEL, D_EXPERT), ex(D_EXPERT, D_MODEL)],
            out_specs=row,
            scratch_shapes=[pltpu.VMEM((D_MODEL, D_EXPERT), BF16), pltpu.VMEM((D_MODEL, D_EXPERT), BF16),
                            pltpu.VMEM((D_EXPERT, D_MODEL), BF16)]),
        out_shape=jax.ShapeDtypeStruct(xs.shape, jnp.uint32),
        compiler_params=_cparams("arbitrary"),
        name="experts",
    )(tile_expert, tile_valid, xs, wts["exp_gate"], wts["exp_up"], wts["exp_down"])


def _combine_kernel(h2p_ref, ys_ref, wk_ref, x2_ref, mod_ref, shg_ref, shu_ref, shd_ref, *rest):
    y_ref = rest[-1]
    t = _unpack_pairs(h2p_ref[...]).astype(BF16)
    act = _silu(_dot(t, shg_ref[...])) * _dot(t, shu_ref[...])
    acc = _dot(act.astype(BF16), shd_ref[...])
    wk = wk_ref[...]
    for k in range(TOP_K):
        acc = acc + wk[:, k:k + 1] * _unpack_pairs(ys_ref[k])
    y_ref[...] = x2_ref[...] + mod_ref[0][5:6] * acc


def _combine(h2p, ys, row0, wk, x2, y_prev, mod, mod_row0, tokens_per_row, wts, tm):
    T = h2p.shape[0]
    G = ys.shape[1]
    bpr = tokens_per_row // tm
    blk0 = row0 // tm
    tok = lambda w: pl.BlockSpec((tm, w), lambda i: (blk0 + i, 0))
    in_specs = [tok(D_PACKED), pl.BlockSpec((TOP_K, tm, D_PACKED), lambda i: (0, i, 0)), tok(LANES),
                tok(D_MODEL), pl.BlockSpec((1, 6, D_MODEL), lambda i: (mod_row0 + (blk0 + i) // bpr, 0, 0)),
                _const_spec((D_MODEL, D_SHARED)), _const_spec((D_MODEL, D_SHARED)),
                _const_spec((D_SHARED, D_MODEL))]
    args = [h2p, ys, wk, x2, mod, wts["sh_gate"], wts["sh_up"], wts["sh_down"]]
    aliases = {}
    if y_prev is not None:
        aliases = {len(args): 0}
        in_specs.append(pl.BlockSpec(memory_space=pl.ANY))
        args.append(y_prev)
    return pl.pallas_call(
        _combine_kernel,
        grid=(G // tm,),
        in_specs=in_specs,
        out_specs=tok(D_MODEL),
        out_shape=jax.ShapeDtypeStruct((T, D_MODEL), F32),
        input_output_aliases=aliases,
        compiler_params=_cparams("parallel"),
        name="moe_combine",
    )(*args)


def _moe(halves, mod, wts):
    routed = [h["routed"] for h in halves]
    sizes = [r[1].shape[0] for r in routed]
    n_tiles = (TOP_K * sum(sizes)) // ROW_TILE + N_EXPERTS
    path_counts = [r[5][:N_EXPERTS, 0] for r in routed]
    counts = sum(path_counts)
    padded = ((counts + ROW_TILE - 1) // ROW_TILE) * ROW_TILE
    ends = jnp.cumsum(padded)
    starts = ends - padded
    tile_start = jnp.arange(n_tiles, dtype=jnp.int32) * ROW_TILE
    tile_expert = jnp.minimum(jnp.sum(tile_start[:, None] >= ends[None, :], axis=1), N_EXPERTS - 1).astype(jnp.int32)
    overlap = (jnp.minimum(tile_start[:, None] + ROW_TILE, (starts + counts)[None, :])
               - jnp.maximum(tile_start[:, None], starts[None, :]))
    tile_valid = jnp.sum(jnp.maximum(overlap, 0), axis=1).astype(jnp.int32)

    pos, first = [], starts
    for h, r, c in zip(halves, routed, path_counts):
        pos.append(_plan(r[2], r[3], first, h["tm"])[:TOP_K])
        first = first + c
    xs = _sc_dispatch_rows([r[1] for r in routed], pos, n_tiles * ROW_TILE)
    ys = _experts(xs, tile_expert, tile_valid, wts)
    outs = []
    for h, r, p in zip(halves, routed, pos):
        T = r[1].shape[0]
        G = MOE_GATHER_GROUP if T % MOE_GATHER_GROUP == 0 else T
        y = None
        for row0 in range(0, T, G):
            ysel = _sc_gather_rows(ys, p[:, row0:row0 + G].reshape(-1)).reshape(TOP_K, G, D_PACKED)
            y = _combine(r[1], ysel, row0, r[4], r[0], y, mod, h["mod_row0"], h["tokens_per_row"], wts, h["tm"])
        outs.append(y.reshape(h["shape"]))
    return outs


def _pad_heads(w, d_used):
    k = w.shape[0]
    w = w.reshape(k, H_MLA, d_used)
    return jnp.pad(w, ((0, 0), (0, 0), (0, HEAD_SLOT - d_used))).reshape(k, H_MLA * HEAD_SLOT)


def _lane_pad(v):
    return jnp.pad(v, (0, LANES - v.shape[0])).reshape(1, LANES)


def _rot_partners(w):
    rot = w[..., w.shape[-1] - D_ROPE:]
    return jnp.concatenate([jnp.zeros_like(w[..., :w.shape[-1] - D_ROPE]),
                            _swap_halves(rot, D_ROPE // 4)], axis=-1)


def _prep_weights(l, w_in, norm1_g, mla_q_norm_g, w_uq, mla_kv_norm_g, w_ukv, mla_q_gain, mla_k_gain,
                  w_o, norm2_g, router_w, router_bias, exp_w_gate, exp_w_up, exp_w_down,
                  sh_w_gate, sh_w_up, sh_w_down):
    w = w_in[l]
    a, b, c = Q_RANK, Q_RANK + KV_RANK, Q_RANK + KV_RANK + D_ROPE
    kr_slot = lambda wk: jnp.pad(wk, ((0, 0), (D_NOPE, LANES - D_QK)))
    w1 = jnp.concatenate([w[:, :b], kr_slot(w[:, b:c]), w[:, c:]], axis=1).astype(BF16)
    w_rq, w_rk = w[:, c:c + D_RET_ALL], w[:, c + D_RET_ALL:c + 2 * D_RET_ALL]
    w1_rope = jnp.concatenate([w1, kr_slot(_rot_partners(w[:, b:c])).astype(BF16),
                               _swap_halves(w_rq, D_RET // 4).astype(BF16),
                               _swap_halves(w_rk, D_RET // 4).astype(BF16)], axis=1)
    uq = w_uq[l].reshape(Q_RANK, H_MLA, D_QK)
    pad_slots = lambda u: jnp.pad(u, ((0, 0), (0, 0), (0, HEAD_SLOT - D_QK))).reshape(Q_RANK, H_MLA * HEAD_SLOT)
    gains = lambda g: jnp.concatenate([_lane_pad(g), _lane_pad(_rot_partners(g))], axis=0)
    ukv = w_ukv[l].reshape(KV_RANK, H_MLA, D_NOPE + D_V)
    w_k = _pad_heads(ukv[:, :, :D_NOPE].reshape(KV_RANK, H_MLA * D_NOPE), D_NOPE)
    w_v = _pad_heads(ukv[:, :, D_NOPE:].reshape(KV_RANK, H_MLA * D_V), D_V)
    rw = jnp.pad(router_w[l].T, ((0, LANES - N_EXPERTS), (0, 0)))
    rw_hi = rw.astype(BF16)
    rw_lo = (rw - rw_hi.astype(F32)).astype(BF16)
    rb = jnp.concatenate([router_bias[l].astype(F32), jnp.full((LANES - N_EXPERTS,), -jnp.inf, F32)])
    return dict(
        norm1_g=norm1_g[l].reshape(1, D_MODEL), w1=w1, w1_rope=w1_rope,
        q_norm_g=mla_q_norm_g[l].reshape(1, Q_RANK),
        w_uq=jnp.concatenate([pad_slots(uq), pad_slots(_rot_partners(uq))], axis=1).astype(BF16),
        kv_norm_g=mla_kv_norm_g[l].reshape(1, KV_RANK),
        w_kv=jnp.concatenate([w_k, w_v], axis=1).astype(BF16),
        q_gain=gains(mla_q_gain[l]), k_gain=gains(mla_k_gain[l]),
        w_o=w_o[l].astype(BF16), norm2_g=norm2_g[l].reshape(1, D_MODEL),
        router_hi=rw_hi, router_lo=rw_lo, router_bias=rb.reshape(1, LANES),
        exp_gate=exp_w_gate[l], exp_up=exp_w_up[l], exp_down=exp_w_down[l],
        sh_gate=sh_w_gate[l].astype(BF16), sh_up=sh_w_up[l].astype(BF16),
        sh_down=sh_w_down[l].astype(BF16))


def _pick(n, prefs):
    for p in prefs:
        if n % p == 0:
            return p
    return n


def _mixer_half(x, mod, mod_row0, wts, ret_wts, ctx):
    B, L, D = x.shape
    T = B * L
    tokens_per_row = L if ctx is not None else T
    xt = x.reshape(T, D)
    rope_tabs = None
    if ctx is not None:
        rope_tabs = _rope_tables(L)
    tm = _pick(L, (512, 256, 128))
    ckv_n, krp, q, k, v, rq, rk, rv, rg = _inproj(xt, mod, mod_row0, tokens_per_row, wts, rope_tabs, tm)

    seq = lambda a: a.reshape(B, L, a.shape[-1])
    kc = vc = s0f = s0b = None
    if ctx is not None:
        ckv_c, kr_c, s0f, s0b = ctx
        Lc = ckv_c.shape[1]
        krp_c = jnp.pad(kr_c, ((0, 0), (0, 0), (D_NOPE, LANES - D_NOPE - D_ROPE)))
        kc, vc = _ctx_kv(ckv_c.reshape(B * Lc, KV_RANK), krp_c.reshape(B * Lc, LANES), wts,
                         _pick(B * Lc, (512, 256, 128)))
        kc, vc = kc.reshape(B, Lc, -1), vc.reshape(B, Lc, -1)
    o_mla = _attention(seq(q), seq(k), seq(v), kc, vc, _pick(L, (1024, 512, 256, 128)))

    dec_f, dec_b, ret_g = ret_wts
    want_state = ctx is None
    ret = _retention(seq(rq), seq(rk), seq(rv), seq(rg), dec_f, dec_b, ret_g, s0f, s0b, want_state)
    o_ret = ret[0]

    routed = _outproj(o_mla.reshape(T, -1), o_ret.reshape(T, -1), xt, mod, mod_row0, tokens_per_row, wts, tm)
    new = None
    if want_state:
        new = (ckv_n.reshape(B, L, KV_RANK), krp[:, D_NOPE:D_NOPE + D_ROPE].reshape(B, L, D_ROPE),
               ret[1], ret[2])
    return dict(routed=routed, mod_row0=mod_row0, tokens_per_row=tokens_per_row, tm=tm, shape=(B, L, D)), new


def kernel(x_prompt, x_sample, cache_mla_ckv, cache_mla_krope, state_ret_fwd, state_ret_bwd, c, c_ctx,
           w_ada, b_ada, norm1_g, w_in, mla_q_norm_g, w_uq, mla_kv_norm_g, w_ukv, mla_q_gain, mla_k_gain,
           ret_decay_fwd, ret_decay_bwd, ret_norm_g, w_o, norm2_g, router_w, router_bias, exp_w_gate,
           exp_w_up, exp_w_down, sh_w_gate, sh_w_up, sh_w_down):
    depth = w_ada.shape[0]
    n_dec = c.shape[0]
    assert 1 + n_dec <= MOD_ROWS
    cond = jnp.concatenate([c_ctx[None], c, jnp.zeros((MOD_ROWS - 1 - n_dec, D_MODEL), F32)], axis=0)

    y_prompt, y_sample = x_prompt, x_sample
    ckv_l, kr_l, sf_l, sb_l = [], [], [], []
    for l in range(depth):
        wts = _prep_weights(l, w_in, norm1_g, mla_q_norm_g, w_uq, mla_kv_norm_g, w_ukv, mla_q_gain,
                            mla_k_gain, w_o, norm2_g, router_w, router_bias, exp_w_gate, exp_w_up,
                            exp_w_down, sh_w_gate, sh_w_up, sh_w_down)
        ret_wts = (jnp.broadcast_to(ret_decay_fwd[l].astype(F32)[:, None], (H_RET, LANES)),
                   jnp.broadcast_to(ret_decay_bwd[l].astype(F32)[:, None], (H_RET, LANES)),
                   jnp.tile(ret_norm_g[l].reshape(1, D_RET), (1, LANES // D_RET)))
        mod = _adaln(cond, w_ada[l], b_ada[l]).reshape(MOD_ROWS, 6, D_MODEL)
        half_p, new = _mixer_half(y_prompt, mod, 0, wts, ret_wts, None)
        ckv_l.append(new[0]); kr_l.append(new[1]); sf_l.append(new[2]); sb_l.append(new[3])
        ctx = (cache_mla_ckv[:, l], cache_mla_krope[:, l], state_ret_fwd[:, l], state_ret_bwd[:, l])
        half_s, _ = _mixer_half(y_sample, mod, 1, wts, ret_wts, ctx)
        y_prompt, y_sample = _moe([half_p, half_s], mod, wts)

    return (y_prompt, y_sample, jnp.stack(ckv_l, axis=1), jnp.stack(kr_l, axis=1),
            jnp.stack(sf_l, axis=1), jnp.stack(sb_l, axis=1))
```

```python
import functools
import math

import numpy as np
import jax
import jax.numpy as jnp
from jax import lax
from jax.experimental import pallas as pl
from jax.experimental.pallas import tpu as pltpu
from jax.experimental.pallas import tpu_sc as plsc

F32 = jnp.float32
BF16 = jnp.bfloat16

D_MODEL = 1024
GRID_W = 64
H_MLA = 8
D_NOPE = 64
D_ROPE = 32
D_QK = D_NOPE + D_ROPE
D_V = 64
Q_RANK = 256
KV_RANK = 128
H_RET = 8
D_RET = 64
RET_CHUNK = 128
D_RET_ALL = H_RET * D_RET
N_EXPERTS = 64
TOP_K = 6
D_EXPERT = 256
D_SHARED = 256
ROUTED_SCALE = 2.5
ROPE_BASE = 10000.0
EPS = 1e-6
LOG2_E = math.log2(math.e)

LANES = 128
HEAD_SLOT = LANES
D_IN_PAD = Q_RANK + KV_RANK + LANES + 4 * D_RET_ALL
MOD_ROWS = 16
VMEM_LIMIT = 56 * 1024 * 1024


def _cparams(*sem):
    return pltpu.CompilerParams(dimension_semantics=sem, vmem_limit_bytes=VMEM_LIMIT)


def _dot(a, b):
    return jnp.dot(a, b, preferred_element_type=F32)


def _dot_nt(a, b):
    return lax.dot_general(a, b, (((1,), (1,)), ((), ())), preferred_element_type=F32)


def _dot_tn(a, b):
    return lax.dot_general(a, b, (((0,), (0,)), ((), ())), preferred_element_type=F32)


def _rms(x, g):
    return x * lax.rsqrt(jnp.mean(x * x, axis=-1, keepdims=True) + EPS) * g


def _silu(x):
    return x * jax.nn.sigmoid(x)


def _adaln_kernel(c_ref, w_ref, b_ref, o_ref):
    s = _silu(c_ref[...])
    o_ref[...] = _dot(s.astype(BF16), w_ref[...].astype(BF16)) + b_ref[...]


def _adaln(cond, w_ada, b_ada):
    n_out = w_ada.shape[1]
    bn = 1536
    return pl.pallas_call(
        _adaln_kernel,
        grid=(n_out // bn,),
        in_specs=[pl.BlockSpec((MOD_ROWS, D_MODEL), lambda j: (0, 0)),
                  pl.BlockSpec((D_MODEL, bn), lambda j: (0, j)),
                  pl.BlockSpec((1, bn), lambda j: (0, j))],
        out_specs=pl.BlockSpec((MOD_ROWS, bn), lambda j: (0, j)),
        out_shape=jax.ShapeDtypeStruct((MOD_ROWS, n_out), F32),
        compiler_params=_cparams("arbitrary"),
        name="adaln",
    )(cond, w_ada, b_ada.reshape(1, n_out))


def _swap_halves(w, half_pair):
    g = w.reshape(*w.shape[:-1], w.shape[-1] // (2 * half_pair), 2, half_pair)
    return g[..., ::-1, :].reshape(w.shape)


def _rope_tables(n_tokens):
    t = np.arange(n_tokens)
    row = (t // GRID_W).astype(np.float64)
    col = (t % GRID_W).astype(np.float64)

    def axis_tables(width, lane0, period, fill):
        p = width // 2
        inv = 1.0 / (ROPE_BASE ** (np.arange(p, dtype=np.float64) / p))
        cos = np.full((n_tokens, LANES), fill)
        sin = np.zeros((n_tokens, LANES))
        starts = range(lane0, LANES, period) if period else (lane0,)
        for s0 in starts:
            for base, pos in ((s0, row), (s0 + width, col)):
                ang = pos[:, None] * inv[None, :]
                c, sn = np.cos(ang), np.sin(ang)
                cos[:, base:base + p] = c
                cos[:, base + p:base + 2 * p] = c
                sin[:, base:base + p] = -sn
                sin[:, base + p:base + 2 * p] = sn
        return cos, sin

    qk_cos, qk_sin = axis_tables(D_ROPE // 2, D_NOPE, 0, 1.0)
    ret_cos, ret_sin = axis_tables(D_RET // 2, 0, D_RET, 1.0)
    as_f32 = lambda *a: tuple(jnp.asarray(x, F32) for x in a)
    return as_f32(qk_cos, qk_sin), as_f32(ret_cos, ret_sin)


def _head_sums(x):
    r = lax.broadcasted_iota(jnp.int32, (2 * HEAD_SLOT, 2 * HEAD_SLOT), 0) // HEAD_SLOT
    c = lax.broadcasted_iota(jnp.int32, (2 * HEAD_SLOT, 2 * HEAD_SLOT), 1) // HEAD_SLOT
    ones_bd = jnp.where(r == c, 1.0, 0.0).astype(BF16)
    xb = x.astype(BF16)
    w = 2 * HEAD_SLOT
    return jnp.concatenate([_dot(xb[:, g * w:(g + 1) * w], ones_bd) for g in range(x.shape[1] // w)], axis=1)


def _tile_heads(v):
    return jnp.concatenate([v] * H_MLA, axis=1)


def _kv_heads(ckv_n, krp, krp_sw, wkv_ref, kgain, kgain_sw, tabs, k_ref, v_ref):
    kvp = _dot(ckv_n.astype(BF16), wkv_ref[...])
    kn = kvp[:, :H_MLA * HEAD_SLOT]
    ms = _head_sums(kn * kn + _tile_heads(krp * krp)) * (1.0 / D_QK)
    r = lax.rsqrt(ms + EPS)
    k = (kn + _tile_heads(krp)) * r * _tile_heads(kgain)
    if tabs is not None:
        cos, sin = tabs
        k = k * _tile_heads(cos) + r * _tile_heads(krp_sw * kgain_sw * sin)
    k_ref[...] = k.astype(BF16)
    v_lane = lax.broadcasted_iota(jnp.int32, (1, H_MLA * HEAD_SLOT), 1) % HEAD_SLOT
    v_ref[...] = (kvp[:, H_MLA * HEAD_SLOT:] + jnp.where(v_lane == D_V, 1.0, 0.0)).astype(BF16)


def _inproj_kernel(rope, *refs):
    (x_ref, mod_ref, n1g_ref, w1_ref, qng_ref, wuq_ref, kvg_ref, wkv_ref, qgain_ref,
     kgain_ref) = refs[:10]
    refs = refs[10:]
    if rope:
        qk_tabs = tuple(r[...] for r in refs[:2])
        ret_cos, ret_sin = (r[...] for r in refs[2:4])
        refs = refs[4:]
    else:
        qk_tabs = None
    ckvn_ref, krp_ref, q_ref, k_ref, v_ref, rq_ref, rk_ref, rv_ref, rg_ref = refs

    mod = mod_ref[0]
    h = _rms(x_ref[...], n1g_ref[...]) * (1.0 + mod[1:2]) + mod[0:1]
    z = _dot(h.astype(BF16), w1_ref[...])
    o = 0
    cq = z[:, o:o + Q_RANK]; o += Q_RANK
    ckv = z[:, o:o + KV_RANK]; o += KV_RANK
    krp = z[:, o:o + LANES]; o += LANES
    rq = z[:, o:o + D_RET_ALL]; o += D_RET_ALL
    rk = z[:, o:o + D_RET_ALL]; o += D_RET_ALL
    rv = z[:, o:o + D_RET_ALL]; o += D_RET_ALL
    rg = z[:, o:o + D_RET_ALL]; o += D_RET_ALL
    krp_sw = None
    if rope:
        krp_sw = z[:, o:o + LANES]; o += LANES
        rq_sw = z[:, o:o + D_RET_ALL]; o += D_RET_ALL
        rk_sw = z[:, o:o + D_RET_ALL]

    ckv_n = _rms(ckv, kvg_ref[...])
    ckvn_ref[...] = ckv_n
    krp_ref[...] = krp

    qn = _rms(cq, qng_ref[...]).astype(BF16)
    scale = D_QK ** -0.5 * LOG2_E
    q = _dot(qn, wuq_ref[:, :H_MLA * HEAD_SLOT])
    r = lax.rsqrt(_head_sums(q * q) * (1.0 / D_QK) + EPS)
    q = q * r * _tile_heads(qgain_ref[0:1] * scale)
    if rope:
        cos, sin = qk_tabs
        q_sw = _dot(qn, wuq_ref[:, H_MLA * HEAD_SLOT:])
        q = q * _tile_heads(cos) + q_sw * r * _tile_heads(qgain_ref[1:2] * scale * sin)
    q_ref[...] = q.astype(BF16)

    _kv_heads(ckv_n, krp, krp_sw, wkv_ref, kgain_ref[0:1], kgain_ref[1:2], qk_tabs, k_ref, v_ref)

    for t in range(D_RET_ALL // LANES):
        sl = slice(t * LANES, (t + 1) * LANES)
        rq_t, rk_t = rq[:, sl], rk[:, sl]
        if rope:
            rq_t = rq_t * ret_cos + rq_sw[:, sl] * ret_sin
            rk_t = rk_t * ret_cos + rk_sw[:, sl] * ret_sin
        rq_ref[:, sl] = rq_t.astype(BF16)
        rk_ref[:, sl] = (rk_t * (D_RET ** -0.5)).astype(BF16)
    rv_ref[...] = rv.astype(BF16)
    rg_ref[...] = _silu(rg).astype(BF16)


def _const_spec(shape):
    return pl.BlockSpec(shape, lambda i: (0,) * len(shape))


def _inproj(x, mod, mod_row0, tokens_per_row, wts, rope_tabs, tm):
    T = x.shape[0]
    bpr = tokens_per_row // tm
    rope = rope_tabs is not None
    tok = lambda w: pl.BlockSpec((tm, w), lambda i: (i, 0))
    in_specs = [tok(D_MODEL),
                pl.BlockSpec((1, 6, D_MODEL), lambda i: (mod_row0 + i // bpr, 0, 0)),
                _const_spec((1, D_MODEL)),
                _const_spec((D_MODEL, D_IN_PAD + (LANES + 2 * D_RET_ALL if rope else 0))),
                _const_spec((1, Q_RANK)),
                _const_spec((Q_RANK, 2 * H_MLA * HEAD_SLOT)),
                _const_spec((1, KV_RANK)),
                _const_spec((KV_RANK, 2 * H_MLA * HEAD_SLOT)),
                _const_spec((2, LANES)),
                _const_spec((2, LANES))]
    args = [x, mod, wts["norm1_g"], wts["w1_rope" if rope else "w1"], wts["q_norm_g"], wts["w_uq"],
            wts["kv_norm_g"], wts["w_kv"], wts["q_gain"], wts["k_gain"]]
    if rope:
        qk_tabs, ret_tabs = rope_tabs
        nb = qk_tabs[0].shape[0] // tm
        in_specs += [pl.BlockSpec((tm, LANES), lambda i: (i % nb, 0))] * 4
        args += list(qk_tabs) + list(ret_tabs)
    widths = [(KV_RANK, F32), (LANES, F32), (H_MLA * HEAD_SLOT, BF16), (H_MLA * HEAD_SLOT, BF16),
              (H_MLA * HEAD_SLOT, BF16)] + [(D_RET_ALL, BF16)] * 4
    return pl.pallas_call(
        functools.partial(_inproj_kernel, rope),
        grid=(T // tm,),
        in_specs=in_specs,
        out_specs=[tok(w) for w, _ in widths],
        out_shape=[jax.ShapeDtypeStruct((T, w), dt) for w, dt in widths],
        compiler_params=_cparams("parallel"),
        name="inproj_rope" if rope else "inproj",
    )(*args)


def _ctx_kv_kernel(ckv_ref, krp_ref, wkv_ref, kgain_ref, k_ref, v_ref):
    _kv_heads(ckv_ref[...], krp_ref[...], None, wkv_ref, kgain_ref[0:1], None, None, k_ref, v_ref)


def _ctx_kv(ckv, krp, wts, tm):
    T = ckv.shape[0]
    tok = lambda w: pl.BlockSpec((tm, w), lambda i: (i, 0))
    return pl.pallas_call(
        _ctx_kv_kernel,
        grid=(T // tm,),
        in_specs=[tok(KV_RANK), tok(LANES),
                  _const_spec((KV_RANK, 2 * H_MLA * HEAD_SLOT)), _const_spec((2, LANES))],
        out_specs=[tok(H_MLA * HEAD_SLOT), tok(H_MLA * HEAD_SLOT)],
        out_shape=[jax.ShapeDtypeStruct((T, H_MLA * HEAD_SLOT), BF16),
                   jax.ShapeDtypeStruct((T, H_MLA * HEAD_SLOT), BF16)],
        compiler_params=_cparams("parallel"),
        name="ctx_kv",
    )(ckv, krp, wts["w_kv"], wts["k_gain"])


ATTN_KEY_CHUNK = 2048


def _attn_kernel(has_ctx, *refs):
    if has_ctx:
        q_ref, k_ref, v_ref, kc_ref, vc_ref, o_ref = refs
    else:
        q_ref, k_ref, v_ref, o_ref = refs
    sources = [(k_ref, v_ref)] + ([(kc_ref, vc_ref)] if has_ctx else [])
    chunks = []
    for kr, vr in sources:
        n = kr.shape[1]
        kc = _pick(n, (ATTN_KEY_CHUNK, 256, 128))
        chunks += [(kr, vr, slice(c * kc, (c + 1) * kc)) for c in range(n // kc)]
    tq = q_ref.shape[1]
    outs = []
    for j in range(2):
        hs = slice(j * HEAD_SLOT, (j + 1) * HEAD_SLOT)
        q = q_ref[0, :, hs]
        m = jnp.full((tq, 1), -jnp.inf, F32)
        acc = jnp.zeros((tq, HEAD_SLOT), F32)
        for kr, vr, rows in chunks:
            s = _dot_nt(q, kr[0, rows, hs])
            m_new = jnp.maximum(m, jnp.max(s, axis=-1, keepdims=True))
            p = jnp.exp2(s - m_new).astype(BF16)
            acc = acc * jnp.exp2(m - m_new) + _dot(p, vr[0, rows, hs])
            m = m_new
        outs.append(acc[:, :D_V] / acc[:, D_V:D_V + 1])
    o_ref[0] = jnp.concatenate(outs, axis=-1).astype(BF16)


def _attention(q, k, v, kc, vc, tq):
    B, L, _ = q.shape
    has_ctx = kc is not None
    in_specs = [pl.BlockSpec((1, tq, 2 * HEAD_SLOT), lambda b, hp, i: (b, i, hp)),
                pl.BlockSpec((1, L, 2 * HEAD_SLOT), lambda b, hp, i: (b, 0, hp)),
                pl.BlockSpec((1, L, 2 * HEAD_SLOT), lambda b, hp, i: (b, 0, hp))]
    args = [q, k, v]
    if has_ctx:
        Lc = kc.shape[1]
        in_specs += [pl.BlockSpec((1, Lc, 2 * HEAD_SLOT), lambda b, hp, i: (b, 0, hp)),
                     pl.BlockSpec((1, Lc, 2 * HEAD_SLOT), lambda b, hp, i: (b, 0, hp))]
        args += [kc, vc]
    return pl.pallas_call(
        functools.partial(_attn_kernel, has_ctx),
        grid=(B, H_MLA // 2, L // tq),
        in_specs=in_specs,
        out_specs=pl.BlockSpec((1, tq, 2 * D_V), lambda b, hp, i: (b, i, hp)),
        out_shape=jax.ShapeDtypeStruct((B, L, H_MLA * D_V), BF16),
        compiler_params=_cparams("parallel", "parallel", "arbitrary"),
        name="attn_ctx" if has_ctx else "attn",
    )(*args)


RET_UNROLL = 8
RET_SHORT_SEQ = 1024


def _ret_pair(has_s0, want_state, n_chunks, pair, hp, *refs):
    rq_ref, rk_ref, rv_ref, rg_ref, decf_ref, decb_ref, g_ref = refs[:7]
    refs = refs[7:]
    if has_s0:
        s0f_ref, s0b_ref = refs[:2]
        refs = refs[2:]
    o_ref = refs[0]
    refs = refs[1:]
    if want_state:
        sf_ref, sb_ref = refs[:2]
        refs = refs[2:]
    kv_ref, sp_ref = refs

    C = RET_CHUNK
    W = 2 * D_RET
    ls = slice(pair * W, (pair + 1) * W)
    lane = lax.broadcasted_iota(jnp.int32, (1, W), 1)
    lane_h1 = lane >= D_RET
    row_h1 = lax.broadcasted_iota(jnp.int32, (W, 1), 0) >= D_RET
    blockdiag = row_h1 == lane_h1
    pos = lax.broadcasted_iota(jnp.int32, (C, 1), 0).astype(F32)
    diff = (lax.broadcasted_iota(jnp.int32, (C, C), 0)
            - lax.broadcasted_iota(jnp.int32, (C, C), 1)).astype(F32)

    lg = {}
    for name, ref in (("f", decf_ref), ("b", decb_ref)):
        a = -jnp.exp(ref[pl.ds(2 * hp, 1), :])
        b = -jnp.exp(ref[pl.ds(2 * hp + 1, 1), :])
        lg[name] = (a[:, :1], b[:, :1], jnp.where(lane_h1, b, a))
    lgf0, lgf1, lgf = lg["f"]
    lgb0, lgb1, lgb = lg["b"]
    qdf, kdf = jnp.exp(lgf * (pos + 1.0)), jnp.exp(lgf * (C - 1.0 - pos))
    qdb, kdb = jnp.exp(lgb * (C - pos)), jnp.exp(lgb * pos)
    row_h1_full = lax.broadcasted_iota(jnp.int32, (W, W), 0) >= D_RET
    cdf = jnp.where(row_h1_full, jnp.exp(lgf1 * C), jnp.exp(lgf0 * C))
    cdb = jnp.where(row_h1_full, jnp.exp(lgb1 * C), jnp.exp(lgb0 * C))

    def intra_decay(lf, lb):
        return jnp.where(diff >= 0, jnp.exp(lf * jnp.maximum(diff, 0.0)), jnp.exp(lb * jnp.maximum(-diff, 0.0)))

    dmat0, dmat1 = intra_decay(lgf0, lgb0), intra_decay(lgf1, lgb1)
    g = g_ref[...]

    def chunk_rows(n):
        return pl.ds(pl.multiple_of(n * C, C), C)

    def phase_a(n, carry):
        rows = chunk_rows(n)
        k = rk_ref[0, rows, ls].astype(F32)
        k2 = jnp.concatenate([(k * kdf).astype(BF16), (k * kdb).astype(BF16)], axis=1)
        kv_ref[n] = _dot_tn(k2, rv_ref[0, rows, ls])
        return carry

    unroll = min(RET_UNROLL, n_chunks)
    lax.fori_loop(0, n_chunks, phase_a, 0, unroll=unroll)

    def init_state(ref):
        if not has_s0:
            return jnp.zeros((W, W), F32)
        z = jnp.zeros((D_RET, D_RET), F32)
        return jnp.concatenate([jnp.concatenate([ref[0, 2 * pair], z], axis=1),
                                jnp.concatenate([z, ref[0, 2 * pair + 1]], axis=1)], axis=0)

    def scan_f(n, S):
        sp_ref[n, 0:W, :] = S.astype(BF16)
        return cdf * S + jnp.where(blockdiag, kv_ref[n, 0:W, :], 0.0)

    def scan_b(i, S):
        n = n_chunks - 1 - i
        sp_ref[n, W:2 * W, :] = S.astype(BF16)
        return cdb * S + jnp.where(blockdiag, kv_ref[n, W:2 * W, :], 0.0)

    Sf = lax.fori_loop(0, n_chunks, scan_f, init_state(s0f_ref if has_s0 else None))
    Sb = lax.fori_loop(0, n_chunks, scan_b, init_state(s0b_ref if has_s0 else None))
    if want_state:
        for S, ref in ((Sf, sf_ref), (Sb, sb_ref)):
            ref[0, 2 * pair] = S[0:D_RET, 0:D_RET]
            ref[0, 2 * pair + 1] = S[D_RET:W, D_RET:W]

    def phase_c(n, carry):
        rows = chunk_rows(n)
        q = rq_ref[0, rows, ls]
        k = rk_ref[0, rows, ls]
        v = rv_ref[0, rows, ls]
        zero = jnp.zeros_like(q)
        sc0 = _dot_nt(jnp.where(lane_h1, zero, q), k) * dmat0
        sc1 = _dot_nt(jnp.where(lane_h1, q, zero), k) * dmat1
        p = jnp.concatenate([sc0.astype(BF16), sc1.astype(BF16)], axis=1)
        v_bd = jnp.concatenate([jnp.where(lane_h1, zero, v), jnp.where(lane_h1, v, zero)], axis=0)
        qf = q.astype(F32)
        q2 = jnp.concatenate([(qf * qdf).astype(BF16), (qf * qdb).astype(BF16)], axis=1)
        o = _dot(p, v_bd) + _dot(q2, sp_ref[n])
        o2 = o * o
        ss0 = jnp.sum(jnp.where(lane_h1, 0.0, o2), axis=-1, keepdims=True)
        ss1 = jnp.sum(jnp.where(lane_h1, o2, 0.0), axis=-1, keepdims=True)
        ms = jnp.where(lane_h1, ss1, ss0) * (1.0 / D_RET)
        o = o * lax.rsqrt(ms + EPS) * g * rg_ref[0, rows, ls].astype(F32)
        o_ref[0, rows, ls] = o.astype(BF16)
        return carry

    lax.fori_loop(0, n_chunks, phase_c, 0, unroll=unroll)


def _ret_kernel(has_s0, want_state, n_chunks, pairs, *refs):
    for pair in range(pairs):
        _ret_pair(has_s0, want_state, n_chunks, pair, pl.program_id(1) * pairs + pair, *refs)


def _retention(rq, rk, rv, rg, dec_f, dec_b, g, s0f, s0b, want_state):
    B, L, _ = rq.shape
    has_s0 = s0f is not None
    pairs = H_RET // 2 if L <= RET_SHORT_SEQ else 1
    seq = pl.BlockSpec((1, L, pairs * LANES), lambda b, hp: (b, 0, hp))
    st = pl.BlockSpec((1, 2 * pairs, D_RET, D_RET), lambda b, hp: (b, hp, 0, 0))
    cst = lambda shape: pl.BlockSpec(shape, lambda b, hp: (0,) * len(shape))
    in_specs = [seq] * 4 + [cst((H_RET, LANES)), cst((H_RET, LANES)), cst((1, LANES))]
    args = [rq, rk, rv, rg, dec_f, dec_b, g]
    if has_s0:
        in_specs += [st, st]
        args += [s0f, s0b]
    out_specs = [seq]
    out_shape = [jax.ShapeDtypeStruct((B, L, D_RET_ALL), BF16)]
    if want_state:
        out_specs += [st, st]
        out_shape += [jax.ShapeDtypeStruct((B, H_RET, D_RET, D_RET), F32)] * 2
    return pl.pallas_call(
        functools.partial(_ret_kernel, has_s0, want_state, L // RET_CHUNK, pairs),
        grid=(B, H_RET // 2 // pairs),
        in_specs=in_specs,
        out_specs=out_specs,
        out_shape=out_shape,
        scratch_shapes=[pltpu.VMEM((L // RET_CHUNK, 4 * D_RET, LANES), F32),
                        pltpu.VMEM((L // RET_CHUNK, 4 * D_RET, LANES), BF16)],
        compiler_params=_cparams("parallel", "parallel"),
        name="retention_s0" if has_s0 else "retention",
    )(*args)


def _pack_pairs(x):
    w = x.shape[1] // 2
    hi = lax.bitcast_convert_type(x[:, :w].astype(BF16).astype(F32), jnp.uint32)
    lo = lax.bitcast_convert_type(x[:, w:].astype(BF16).astype(F32), jnp.uint32)
    return hi | (lo >> 16)


def _unpack_pairs(p):
    hi = lax.bitcast_convert_type(p & jnp.uint32(0xFFFF0000), F32)
    lo = lax.bitcast_convert_type(p << 16, F32)
    return jnp.concatenate([hi, lo], axis=1)


D_PACKED = D_MODEL // 2


def _outproj_kernel(omla_ref, oret_ref, x_ref, mod_ref, wo_ref, n2g_ref, rwh_ref, rwl_ref, rb_ref, tri_ref,
                    x2_ref, h2p_ref, code_ref, rank_ref, wk_ref, cnt_ref):
    @pl.when(pl.program_id(0) == 0)
    def _():
        cnt_ref[...] = jnp.zeros_like(cnt_ref)

    mod = mod_ref[0]
    half = H_MLA * D_V
    a = _dot(omla_ref[...], wo_ref[:half, :]) + _dot(oret_ref[...], wo_ref[half:, :])
    x2 = x_ref[...] + mod[2:3] * a
    x2_ref[...] = x2
    h2 = _rms(x2, n2g_ref[...]) * (1.0 + mod[4:5]) + mod[3:4]
    h2p_ref[...] = _pack_pairs(h2)
    h2_hi = h2.astype(BF16)
    h2_lo = (h2 - h2_hi.astype(F32)).astype(BF16)
    rw_hi = rwh_ref[...]
    logits = _dot_nt(rw_hi, h2_hi) + _dot_nt(rw_hi, h2_lo) + _dot_nt(rwl_ref[...], h2_hi)
    scores = jax.nn.sigmoid(logits)
    tm = scores.shape[1]
    expert = lax.broadcasted_iota(jnp.int32, scores.shape, 0)
    cand = scores + rb_ref[...]
    code = jnp.zeros(scores.shape, jnp.int32)
    for k in range(TOP_K):
        mx = jnp.max(cand, axis=0, keepdims=True)
        first = jnp.min(jnp.where(cand == mx, expert, LANES), axis=0, keepdims=True)
        hit = expert == first
        code = jnp.where(hit, k + 1, code)
        cand = jnp.where(hit, -jnp.inf, cand)
    picked = code > 0
    sel = jnp.where(picked, scores, 0.0)
    gates = sel * (ROUTED_SCALE / jnp.sum(sel, axis=0, keepdims=True))
    code_ref[...] = code

    onehot = jnp.where(picked, 1.0, 0.0)
    carry = cnt_ref[...].astype(F32)
    rank = _dot(onehot.astype(BF16), tri_ref[...])
    rank_ref[...] = rank + jnp.concatenate([carry] * (tm // LANES), axis=1)
    cnt_ref[...] = (carry + jnp.sum(onehot, axis=1, keepdims=True)).astype(jnp.int32)

    row = lax.broadcasted_iota(jnp.int32, scores.shape, 0)
    wk_t = jnp.zeros(scores.shape, F32)
    for k in range(TOP_K):
        w = jnp.sum(jnp.where(code == k + 1, gates, 0.0), axis=0, keepdims=True)
        wk_t = jnp.where(row == k, w, wk_t)
    wk_ref[...] = wk_t.T


def _outproj(o_mla, o_ret, x, mod, mod_row0, tokens_per_row, wts, tm):
    T = x.shape[0]
    bpr = tokens_per_row // tm
    tok = lambda w: pl.BlockSpec((tm, w), lambda i: (i, 0))
    emaj = pl.BlockSpec((LANES, tm), lambda i: (0, i))
    tri = jnp.triu(jnp.ones((tm, tm), BF16), 1)
    bias = jnp.broadcast_to(wts["router_bias"].reshape(LANES, 1), (LANES, tm))
    return pl.pallas_call(
        _outproj_kernel,
        grid=(T // tm,),
        in_specs=[tok(H_MLA * D_V), tok(D_RET_ALL), tok(D_MODEL),
                  pl.BlockSpec((1, 6, D_MODEL), lambda i: (mod_row0 + i // bpr, 0, 0)),
                  _const_spec((D_MODEL, D_MODEL)), _const_spec((1, D_MODEL)),
                  _const_spec((LANES, D_MODEL)), _const_spec((LANES, D_MODEL)), _const_spec((LANES, tm)),
                  _const_spec((tm, tm))],
        out_specs=[tok(D_MODEL), tok(D_PACKED), emaj, emaj, tok(LANES), _const_spec((LANES, LANES))],
        out_shape=[jax.ShapeDtypeStruct((T, D_MODEL), F32),
                   jax.ShapeDtypeStruct((T, D_PACKED), jnp.uint32),
                   jax.ShapeDtypeStruct((LANES, T), jnp.int32),
                   jax.ShapeDtypeStruct((LANES, T), F32),
                   jax.ShapeDtypeStruct((T, LANES), F32),
                   jax.ShapeDtypeStruct((LANES, LANES), jnp.int32)],
        compiler_params=_cparams("arbitrary"),
        name="outproj_router",
    )(o_mla, o_ret, x, mod, wts["w_o"], wts["norm2_g"], wts["router_hi"], wts["router_lo"], bias, tri)


def _plan_kernel(code_ref, rank_ref, first_ref, pos_ref):
    code = code_ref[...]
    tm = code.shape[1]
    row = rank_ref[...] + jnp.concatenate([first_ref[...]] * (tm // LANES), axis=1)
    krow = lax.broadcasted_iota(jnp.int32, pos_ref.shape, 0)
    pos = jnp.zeros(pos_ref.shape, F32)
    for k in range(TOP_K):
        pos = jnp.where(krow == k, jnp.sum(jnp.where(code == k + 1, row, 0.0), axis=0, keepdims=True), pos)
    pos_ref[...] = pos.astype(jnp.int32)


def _plan(code, rank, first_rows, tm):
    T = code.shape[1]
    first = jnp.broadcast_to(jnp.pad(first_rows.astype(F32), (0, LANES - N_EXPERTS)).reshape(LANES, 1),
                             (LANES, LANES))
    emaj = pl.BlockSpec((LANES, tm), lambda i: (0, i))
    return pl.pallas_call(
        _plan_kernel,
        grid=(T // tm,),
        in_specs=[emaj, emaj, _const_spec((LANES, LANES))],
        out_specs=pl.BlockSpec((8, tm), lambda i: (0, i)),
        out_shape=jax.ShapeDtypeStruct((8, T), jnp.int32),
        compiler_params=_cparams("parallel"),
        name="moe_plan",
    )(code, rank, first)


SC_WORKERS = 32
SC_CHUNK = 64


def _sc_worker_id():
    return lax.axis_index("s") * 2 + lax.axis_index("c")


def _sc_mesh():
    return plsc.VectorSubcoreMesh(core_axis_name="c", subcore_axis_name="s")


def _sc_gather_rows(table, idx):
    N = idx.shape[0]
    W = table.shape[1]
    per_w = N // SC_WORKERS
    n_chunks = per_w // SC_CHUNK
    assert N == SC_WORKERS * n_chunks * SC_CHUNK and n_chunks % 2 == 0

    @functools.partial(
        pl.kernel, mesh=_sc_mesh(), out_type=jax.ShapeDtypeStruct((N, W), table.dtype),
        scratch_types=[pltpu.VMEM((n_chunks, SC_CHUNK), jnp.int32),
                       pltpu.VMEM((2, SC_CHUNK, W), table.dtype),
                       pltpu.SemaphoreType.DMA((2,)), pltpu.SemaphoreType.DMA((2,))],
        name="sc_gather_rows")
    def k(table_hbm, idx_hbm, out_hbm, idx_v, rows_v, gsem, wsem):
        wid = _sc_worker_id()
        base = wid * per_w
        pltpu.sync_copy(idx_hbm.at[wid], idx_v)

        def gather(j, b):
            return pltpu.make_async_copy(table_hbm.at[idx_v.at[j]], rows_v.at[b], gsem.at[b])

        def writeback(j, b):
            off = pl.multiple_of(base + j * SC_CHUNK, SC_CHUNK)
            return pltpu.make_async_copy(rows_v.at[b], out_hbm.at[pl.ds(off, SC_CHUNK)], wsem.at[b])

        gather(0, 0).start()

        @pl.loop(0, n_chunks, step=2)
        def _(j):
            for b in range(2):
                jj = j + b

                @pl.when(jj + 1 < n_chunks)
                def _():
                    @pl.when(jj >= 1)
                    def _():
                        writeback(jj - 1, 1 - b).wait()
                    gather(jj + 1, 1 - b).start()

                gather(jj, b).wait()
                writeback(jj, b).start()

        writeback(n_chunks - 2, 0).wait()
        writeback(n_chunks - 1, 1).wait()

    return k(table, idx.reshape(SC_WORKERS, n_chunks, SC_CHUNK))


def _sc_dispatch_rows(rows_list, pos_list, n_out):
    W = rows_list[0].shape[1]
    K = pos_list[0].shape[0]
    n_src = len(rows_list)
    plan = []
    idx_list = []
    for rows, pos in zip(rows_list, pos_list):
        T = rows.shape[0]
        per_w = T // SC_WORKERS
        n_chunks = per_w // SC_CHUNK
        assert T == SC_WORKERS * n_chunks * SC_CHUNK and n_chunks % 2 == 0
        plan.append((per_w, n_chunks))
        idx_list.append(pos.reshape(K, SC_WORKERS, n_chunks, SC_CHUNK).transpose(1, 2, 0, 3))
    max_chunks = max(n for _, n in plan)

    @functools.partial(
        pl.kernel, mesh=_sc_mesh(), out_type=jax.ShapeDtypeStruct((n_out, W), rows_list[0].dtype),
        scratch_types=[pltpu.VMEM((max_chunks, K, SC_CHUNK), jnp.int32),
                       pltpu.VMEM((2, SC_CHUNK, W), rows_list[0].dtype),
                       pltpu.SemaphoreType.DMA((2,)), pltpu.SemaphoreType.DMA((2,))],
        name="sc_dispatch_rows")
    def k(*refs):
        rows_refs, idx_refs = refs[:n_src], refs[n_src:2 * n_src]
        out_hbm, idx_v, rows_v, lsem, ssem = refs[2 * n_src:]
        wid = _sc_worker_id()
        for rows_hbm, idx_hbm, (per_w, n_chunks) in zip(rows_refs, idx_refs, plan):
            base = wid * per_w
            pltpu.sync_copy(idx_hbm.at[wid], idx_v.at[pl.ds(0, n_chunks)])

            def load(j, b):
                off = pl.multiple_of(base + j * SC_CHUNK, SC_CHUNK)
                return pltpu.make_async_copy(rows_hbm.at[pl.ds(off, SC_CHUNK)], rows_v.at[b], lsem.at[b])

            def scatter(j, b, kk):
                return pltpu.make_async_copy(rows_v.at[b], out_hbm.at[idx_v.at[j, kk]], ssem.at[b])

            load(0, 0).start()

            @pl.loop(0, n_chunks, step=2)
            def _(j):
                for b in range(2):
                    jj = j + b

                    @pl.when(jj + 1 < n_chunks)
                    def _():
                        @pl.when(jj >= 1)
                        def _():
                            for kk in range(K):
                                scatter(jj - 1, 1 - b, kk).wait()
                        load(jj + 1, 1 - b).start()

                    load(jj, b).wait()
                    for kk in range(K):
                        scatter(jj, b, kk).start()

            for kk in range(K):
                scatter(n_chunks - 2, 0, kk).wait()
            for kk in range(K):
                scatter(n_chunks - 1, 1, kk).wait()

    return k(*rows_list, *idx_list)


ROW_TILE = 1024
MOE_GATHER_GROUP = 8192


def _experts_kernel(te_ref, tv_ref, tb_ref, xs_ref, wg_ref, wu_ref, wd_ref, ys_ref, wg_s, wu_s, wd_s):
    del tb_ref
    i = pl.program_id(0)
    valid = tv_ref[i]

    @pl.when(jnp.logical_or(i == 0, te_ref[i] != te_ref[jnp.maximum(i - 1, 0)]))
    def _():
        wg_s[...] = wg_ref[0].astype(BF16)
        wu_s[...] = wu_ref[0].astype(BF16)
        wd_s[...] = wd_ref[0].astype(BF16)

    @pl.when(valid > 0)
    def _():
        row = lax.broadcasted_iota(jnp.int32, (ROW_TILE, 1), 0)
        p = jnp.where(row < valid, xs_ref[...], jnp.uint32(0))
        x = _unpack_pairs(p).astype(BF16)
        act = _silu(_dot(x, wg_s[...])) * _dot(x, wu_s[...])
        ys_ref[...] = _pack_pairs(_dot(act.astype(BF16), wd_s[...]))


def _experts(xs, tile_expert, tile_valid, tile_block, wts):
    n_tiles = tile_expert.shape[0]
    ex = lambda a, b: pl.BlockSpec((1, a, b), lambda i, te, tv, tb: (te[i], 0, 0))
    row = pl.BlockSpec((ROW_TILE, D_PACKED), lambda i, te, tv, tb: (tb[i], 0))
    return pl.pallas_call(
        _experts_kernel,
        grid_spec=pltpu.PrefetchScalarGridSpec(
            num_scalar_prefetch=3, grid=(n_tiles,),
            in_specs=[row, ex(D_MODEL, D_EXPERT), ex(D_MODEL, D_EXPERT), ex(D_EXPERT, D_MODEL)],
            out_specs=row,
            scratch_shapes=[pltpu.VMEM((D_MODEL, D_EXPERT), BF16), pltpu.VMEM((D_MODEL, D_EXPERT), BF16),
                            pltpu.VMEM((D_EXPERT, D_MODEL), BF16)]),
        out_shape=jax.ShapeDtypeStruct(xs.shape, jnp.uint32),
        compiler_params=_cparams("arbitrary"),
        name="experts",
    )(tile_expert, tile_valid, tile_block, xs, wts["exp_gate"], wts["exp_up"], wts["exp_down"])


def _combine_kernel(h2p_ref, ys_ref, wk_ref, x2_ref, mod_ref, shg_ref, shu_ref, shd_ref, *rest):
    y_ref = rest[-1]
    t = _unpack_pairs(h2p_ref[...]).astype(BF16)
    act = _silu(_dot(t, shg_ref[...])) * _dot(t, shu_ref[...])
    acc = _dot(act.astype(BF16), shd_ref[...])
    wk = wk_ref[...]
    for k in range(TOP_K):
        acc = acc + wk[:, k:k + 1] * _unpack_pairs(ys_ref[k])
    y_ref[...] = x2_ref[...] + mod_ref[0][5:6] * acc


def _combine(h2p, ys, row0, wk, x2, y_prev, mod, mod_row0, tokens_per_row, wts, tm):
    T = h2p.shape[0]
    G = ys.shape[1]
    bpr = tokens_per_row // tm
    blk0 = row0 // tm
    tok = lambda w: pl.BlockSpec((tm, w), lambda i: (blk0 + i, 0))
    in_specs = [tok(D_PACKED), pl.BlockSpec((TOP_K, tm, D_PACKED), lambda i: (0, i, 0)), tok(LANES),
                tok(D_MODEL), pl.BlockSpec((1, 6, D_MODEL), lambda i: (mod_row0 + (blk0 + i) // bpr, 0, 0)),
                _const_spec((D_MODEL, D_SHARED)), _const_spec((D_MODEL, D_SHARED)),
                _const_spec((D_SHARED, D_MODEL))]
    args = [h2p, ys, wk, x2, mod, wts["sh_gate"], wts["sh_up"], wts["sh_down"]]
    aliases = {}
    if y_prev is not None:
        aliases = {len(args): 0}
        in_specs.append(pl.BlockSpec(memory_space=pl.ANY))
        args.append(y_prev)
    return pl.pallas_call(
        _combine_kernel,
        grid=(G // tm,),
        in_specs=in_specs,
        out_specs=tok(D_MODEL),
        out_shape=jax.ShapeDtypeStruct((T, D_MODEL), F32),
        input_output_aliases=aliases,
        compiler_params=_cparams("parallel"),
        name="moe_combine",
    )(*args)


def _moe(halves, mod, wts):
    routed = [h["routed"] for h in halves]
    sizes = [r[1].shape[0] for r in routed]
    n_tiles = (TOP_K * sum(sizes)) // ROW_TILE + N_EXPERTS
    path_counts = [r[5][:N_EXPERTS, 0] for r in routed]
    counts = sum(path_counts)
    padded = ((counts + ROW_TILE - 1) // ROW_TILE) * ROW_TILE
    ends = jnp.cumsum(padded)
    starts = ends - padded
    tile_start = jnp.arange(n_tiles, dtype=jnp.int32) * ROW_TILE
    tile_expert = jnp.minimum(jnp.sum(tile_start[:, None] >= ends[None, :], axis=1), N_EXPERTS - 1).astype(jnp.int32)
    overlap = (jnp.minimum(tile_start[:, None] + ROW_TILE, (starts + counts)[None, :])
               - jnp.maximum(tile_start[:, None], starts[None, :]))
    tile_valid = jnp.sum(jnp.maximum(overlap, 0), axis=1).astype(jnp.int32)

    pos, first = [], starts
    for h, r, c in zip(halves, routed, path_counts):
        pos.append(_plan(r[2], r[3], first, h["tm"])[:TOP_K])
        first = first + c
    xs = _sc_dispatch_rows([r[1] for r in routed], pos, n_tiles * ROW_TILE)
    tile_block = jnp.minimum(jnp.arange(n_tiles, dtype=jnp.int32), ends[-1] // ROW_TILE - 1).astype(jnp.int32)
    ys = _experts(xs, tile_expert, tile_valid, tile_block, wts)
    outs = []
    for h, r, p in zip(halves, routed, pos):
        T = r[1].shape[0]
        G = MOE_GATHER_GROUP if T % MOE_GATHER_GROUP == 0 else T
        y = None
        for row0 in range(0, T, G):
            ysel = _sc_gather_rows(ys, p[:, row0:row0 + G].reshape(-1)).reshape(TOP_K, G, D_PACKED)
            y = _combine(r[1], ysel, row0, r[4], r[0], y, mod, h["mod_row0"], h["tokens_per_row"], wts, h["tm"])
        outs.append(y.reshape(h["shape"]))
    return outs


def _pad_heads(w, d_used):
    k = w.shape[0]
    w = w.reshape(k, H_MLA, d_used)
    return jnp.pad(w, ((0, 0), (0, 0), (0, HEAD_SLOT - d_used))).reshape(k, H_MLA * HEAD_SLOT)


def _lane_pad(v):
    return jnp.pad(v, (0, LANES - v.shape[0])).reshape(1, LANES)


def _rot_partners(w):
    rot = w[..., w.shape[-1] - D_ROPE:]
    return jnp.concatenate([jnp.zeros_like(w[..., :w.shape[-1] - D_ROPE]),
                            _swap_halves(rot, D_ROPE // 4)], axis=-1)


def _prep_weights(l, w_in, norm1_g, mla_q_norm_g, w_uq, mla_kv_norm_g, w_ukv, mla_q_gain, mla_k_gain,
                  w_o, norm2_g, router_w, router_bias, exp_w_gate, exp_w_up, exp_w_down,
                  sh_w_gate, sh_w_up, sh_w_down):
    w = w_in[l]
    a, b, c = Q_RANK, Q_RANK + KV_RANK, Q_RANK + KV_RANK + D_ROPE
    kr_slot = lambda wk: jnp.pad(wk, ((0, 0), (D_NOPE, LANES - D_QK)))
    w1 = jnp.concatenate([w[:, :b], kr_slot(w[:, b:c]), w[:, c:]], axis=1).astype(BF16)
    w_rq, w_rk = w[:, c:c + D_RET_ALL], w[:, c + D_RET_ALL:c + 2 * D_RET_ALL]
    w1_rope = jnp.concatenate([w1, kr_slot(_rot_partners(w[:, b:c])).astype(BF16),
                               _swap_halves(w_rq, D_RET // 4).astype(BF16),
                               _swap_halves(w_rk, D_RET // 4).astype(BF16)], axis=1)
    uq = w_uq[l].reshape(Q_RANK, H_MLA, D_QK)
    pad_slots = lambda u: jnp.pad(u, ((0, 0), (0, 0), (0, HEAD_SLOT - D_QK))).reshape(Q_RANK, H_MLA * HEAD_SLOT)
    gains = lambda g: jnp.concatenate([_lane_pad(g), _lane_pad(_rot_partners(g))], axis=0)
    ukv = w_ukv[l].reshape(KV_RANK, H_MLA, D_NOPE + D_V)
    w_k = _pad_heads(ukv[:, :, :D_NOPE].reshape(KV_RANK, H_MLA * D_NOPE), D_NOPE)
    w_v = _pad_heads(ukv[:, :, D_NOPE:].reshape(KV_RANK, H_MLA * D_V), D_V)
    rw = jnp.pad(router_w[l].T, ((0, LANES - N_EXPERTS), (0, 0)))
    rw_hi = rw.astype(BF16)
    rw_lo = (rw - rw_hi.astype(F32)).astype(BF16)
    rb = jnp.concatenate([router_bias[l].astype(F32), jnp.full((LANES - N_EXPERTS,), -jnp.inf, F32)])
    return dict(
        norm1_g=norm1_g[l].reshape(1, D_MODEL), w1=w1, w1_rope=w1_rope,
        q_norm_g=mla_q_norm_g[l].reshape(1, Q_RANK),
        w_uq=jnp.concatenate([pad_slots(uq), pad_slots(_rot_partners(uq))], axis=1).astype(BF16),
        kv_norm_g=mla_kv_norm_g[l].reshape(1, KV_RANK),
        w_kv=jnp.concatenate([w_k, w_v], axis=1).astype(BF16),
        q_gain=gains(mla_q_gain[l]), k_gain=gains(mla_k_gain[l]),
        w_o=w_o[l].astype(BF16), norm2_g=norm2_g[l].reshape(1, D_MODEL),
        router_hi=rw_hi, router_lo=rw_lo, router_bias=rb.reshape(1, LANES),
        exp_gate=exp_w_gate[l], exp_up=exp_w_up[l], exp_down=exp_w_down[l],
        sh_gate=sh_w_gate[l].astype(BF16), sh_up=sh_w_up[l].astype(BF16),
        sh_down=sh_w_down[l].astype(BF16))


def _pick(n, prefs):
    for p in prefs:
        if n % p == 0:
            return p
    return n


def _mixer_half(x, mod, mod_row0, wts, ret_wts, ctx):
    B, L, D = x.shape
    T = B * L
    tokens_per_row = L if ctx is not None else T
    xt = x.reshape(T, D)
    rope_tabs = None
    if ctx is not None:
        rope_tabs = _rope_tables(L)
    tm = _pick(L, (512, 256, 128))
    ckv_n, krp, q, k, v, rq, rk, rv, rg = _inproj(xt, mod, mod_row0, tokens_per_row, wts, rope_tabs, tm)

    seq = lambda a: a.reshape(B, L, a.shape[-1])
    kc = vc = s0f = s0b = None
    if ctx is not None:
        ckv_c, kr_c, s0f, s0b = ctx
        Lc = ckv_c.shape[1]
        krp_c = jnp.pad(kr_c, ((0, 0), (0, 0), (D_NOPE, LANES - D_NOPE - D_ROPE)))
        kc, vc = _ctx_kv(ckv_c.reshape(B * Lc, KV_RANK), krp_c.reshape(B * Lc, LANES), wts,
                         _pick(B * Lc, (512, 256, 128)))
        kc, vc = kc.reshape(B, Lc, -1), vc.reshape(B, Lc, -1)
    o_mla = _attention(seq(q), seq(k), seq(v), kc, vc, _pick(L, (1024, 512, 256, 128)))

    dec_f, dec_b, ret_g = ret_wts
    want_state = ctx is None
    ret = _retention(seq(rq), seq(rk), seq(rv), seq(rg), dec_f, dec_b, ret_g, s0f, s0b, want_state)
    o_ret = ret[0]

    routed = _outproj(o_mla.reshape(T, -1), o_ret.reshape(T, -1), xt, mod, mod_row0, tokens_per_row, wts, tm)
    new = None
    if want_state:
        new = (ckv_n.reshape(B, L, KV_RANK), krp[:, D_NOPE:D_NOPE + D_ROPE].reshape(B, L, D_ROPE),
               ret[1], ret[2])
    return dict(routed=routed, mod_row0=mod_row0, tokens_per_row=tokens_per_row, tm=tm, shape=(B, L, D)), new


def kernel(x_prompt, x_sample, cache_mla_ckv, cache_mla_krope, state_ret_fwd, state_ret_bwd, c, c_ctx,
           w_ada, b_ada, norm1_g, w_in, mla_q_norm_g, w_uq, mla_kv_norm_g, w_ukv, mla_q_gain, mla_k_gain,
           ret_decay_fwd, ret_decay_bwd, ret_norm_g, w_o, norm2_g, router_w, router_bias, exp_w_gate,
           exp_w_up, exp_w_down, sh_w_gate, sh_w_up, sh_w_down):
    depth = w_ada.shape[0]
    n_dec = c.shape[0]
    assert 1 + n_dec <= MOD_ROWS
    cond = jnp.concatenate([c_ctx[None], c, jnp.zeros((MOD_ROWS - 1 - n_dec, D_MODEL), F32)], axis=0)

    y_prompt, y_sample = x_prompt, x_sample
    ckv_l, kr_l, sf_l, sb_l = [], [], [], []
    for l in range(depth):
        wts = _prep_weights(l, w_in, norm1_g, mla_q_norm_g, w_uq, mla_kv_norm_g, w_ukv, mla_q_gain,
                            mla_k_gain, w_o, norm2_g, router_w, router_bias, exp_w_gate, exp_w_up,
                            exp_w_down, sh_w_gate, sh_w_up, sh_w_down)
        ret_wts = (jnp.broadcast_to(ret_decay_fwd[l].astype(F32)[:, None], (H_RET, LANES)),
                   jnp.broadcast_to(ret_decay_bwd[l].astype(F32)[:, None], (H_RET, LANES)),
                   jnp.tile(ret_norm_g[l].reshape(1, D_RET), (1, LANES // D_RET)))
        mod = _adaln(cond, w_ada[l], b_ada[l]).reshape(MOD_ROWS, 6, D_MODEL)
        half_p, new = _mixer_half(y_prompt, mod, 0, wts, ret_wts, None)
        ckv_l.append(new[0]); kr_l.append(new[1]); sf_l.append(new[2]); sb_l.append(new[3])
        ctx = (cache_mla_ckv[:, l], cache_mla_krope[:, l], state_ret_fwd[:, l], state_ret_bwd[:, l])
        half_s, _ = _mixer_half(y_sample, mod, 1, wts, ret_wts, ctx)
        y_prompt, y_sample = _moe([half_p, half_s], mod, wts)

    return (y_prompt, y_sample, jnp.stack(ckv_l, axis=1), jnp.stack(kr_l, axis=1),
            jnp.stack(sf_l, axis=1), jnp.stack(sb_l, axis=1))
```

```python
import functools
import math

import numpy as np
import jax
import jax.numpy as jnp
from jax import lax
from jax.experimental import pallas as pl
from jax.experimental.pallas import tpu as pltpu
from jax.experimental.pallas import tpu_sc as plsc

F32 = jnp.float32
BF16 = jnp.bfloat16

D_MODEL = 1024
GRID_W = 64
H_MLA = 8
D_NOPE = 64
D_ROPE = 32
D_QK = D_NOPE + D_ROPE
D_V = 64
Q_RANK = 256
KV_RANK = 128
H_RET = 8
D_RET = 64
RET_CHUNK = 128
D_RET_ALL = H_RET * D_RET
N_EXPERTS = 64
TOP_K = 6
D_EXPERT = 256
D_SHARED = 256
ROUTED_SCALE = 2.5
ROPE_BASE = 10000.0
EPS = 1e-6
LOG2_E = math.log2(math.e)

LANES = 128
HEAD_SLOT = LANES
D_IN_PAD = Q_RANK + KV_RANK + LANES + 4 * D_RET_ALL
MOD_ROWS = 16
VMEM_LIMIT = 56 * 1024 * 1024


def _cparams(*sem):
    return pltpu.CompilerParams(dimension_semantics=sem, vmem_limit_bytes=VMEM_LIMIT)


def _dot(a, b):
    return jnp.dot(a, b, preferred_element_type=F32)


def _dot_nt(a, b):
    return lax.dot_general(a, b, (((1,), (1,)), ((), ())), preferred_element_type=F32)


def _dot_tn(a, b):
    return lax.dot_general(a, b, (((0,), (0,)), ((), ())), preferred_element_type=F32)


def _rms(x, g):
    return x * lax.rsqrt(jnp.mean(x * x, axis=-1, keepdims=True) + EPS) * g


def _silu(x):
    return x * jax.nn.sigmoid(x)


def _adaln_kernel(c_ref, w_ref, b_ref, o_ref):
    s = _silu(c_ref[...])
    o_ref[...] = _dot(s.astype(BF16), w_ref[...].astype(BF16)) + b_ref[...]


def _adaln(cond, w_ada, b_ada):
    n_out = w_ada.shape[1]
    bn = 1536
    return pl.pallas_call(
        _adaln_kernel,
        grid=(n_out // bn,),
        in_specs=[pl.BlockSpec((MOD_ROWS, D_MODEL), lambda j: (0, 0)),
                  pl.BlockSpec((D_MODEL, bn), lambda j: (0, j)),
                  pl.BlockSpec((1, bn), lambda j: (0, j))],
        out_specs=pl.BlockSpec((MOD_ROWS, bn), lambda j: (0, j)),
        out_shape=jax.ShapeDtypeStruct((MOD_ROWS, n_out), F32),
        compiler_params=_cparams("arbitrary"),
        name="adaln",
    )(cond, w_ada, b_ada.reshape(1, n_out))


def _swap_halves(w, half_pair):
    g = w.reshape(*w.shape[:-1], w.shape[-1] // (2 * half_pair), 2, half_pair)
    return g[..., ::-1, :].reshape(w.shape)


def _rope_tables(n_tokens):
    t = np.arange(n_tokens)
    row = (t // GRID_W).astype(np.float64)
    col = (t % GRID_W).astype(np.float64)

    def axis_tables(width, lane0, period, fill):
        p = width // 2
        inv = 1.0 / (ROPE_BASE ** (np.arange(p, dtype=np.float64) / p))
        cos = np.full((n_tokens, LANES), fill)
        sin = np.zeros((n_tokens, LANES))
        starts = range(lane0, LANES, period) if period else (lane0,)
        for s0 in starts:
            for base, pos in ((s0, row), (s0 + width, col)):
                ang = pos[:, None] * inv[None, :]
                c, sn = np.cos(ang), np.sin(ang)
                cos[:, base:base + p] = c
                cos[:, base + p:base + 2 * p] = c
                sin[:, base:base + p] = -sn
                sin[:, base + p:base + 2 * p] = sn
        return cos, sin

    qk_cos, qk_sin = axis_tables(D_ROPE // 2, D_NOPE, 0, 1.0)
    ret_cos, ret_sin = axis_tables(D_RET // 2, 0, D_RET, 1.0)
    as_f32 = lambda *a: tuple(jnp.asarray(x, F32) for x in a)
    return as_f32(qk_cos, qk_sin), as_f32(ret_cos, ret_sin)


def _head_sums(x):
    r = lax.broadcasted_iota(jnp.int32, (2 * HEAD_SLOT, 2 * HEAD_SLOT), 0) // HEAD_SLOT
    c = lax.broadcasted_iota(jnp.int32, (2 * HEAD_SLOT, 2 * HEAD_SLOT), 1) // HEAD_SLOT
    ones_bd = jnp.where(r == c, 1.0, 0.0).astype(BF16)
    xb = x.astype(BF16)
    w = 2 * HEAD_SLOT
    return jnp.concatenate([_dot(xb[:, g * w:(g + 1) * w], ones_bd) for g in range(x.shape[1] // w)], axis=1)


def _tile_heads(v):
    return jnp.concatenate([v] * H_MLA, axis=1)


def _kv_heads(ckv_n, krp, krp_sw, wkv_ref, kgain, kgain_sw, tabs, k_ref, v_ref):
    kvp = _dot(ckv_n.astype(BF16), wkv_ref[...])
    kn = kvp[:, :H_MLA * HEAD_SLOT]
    ms = _head_sums(kn * kn + _tile_heads(krp * krp)) * (1.0 / D_QK)
    r = lax.rsqrt(ms + EPS)
    k = (kn + _tile_heads(krp)) * r * _tile_heads(kgain)
    if tabs is not None:
        cos, sin = tabs
        k = k * _tile_heads(cos) + r * _tile_heads(krp_sw * kgain_sw * sin)
    k_ref[...] = k.astype(BF16)
    v_lane = lax.broadcasted_iota(jnp.int32, (1, H_MLA * HEAD_SLOT), 1) % HEAD_SLOT
    v_ref[...] = (kvp[:, H_MLA * HEAD_SLOT:] + jnp.where(v_lane == D_V, 1.0, 0.0)).astype(BF16)


def _inproj_kernel(rope, *refs):
    (x_ref, mod_ref, n1g_ref, w1_ref, qng_ref, wuq_ref, kvg_ref, wkv_ref, qgain_ref,
     kgain_ref) = refs[:10]
    refs = refs[10:]
    if rope:
        qk_tabs = tuple(r[...] for r in refs[:2])
        ret_cos, ret_sin = (r[...] for r in refs[2:4])
        refs = refs[4:]
    else:
        qk_tabs = None
    ckvn_ref, krp_ref, q_ref, k_ref, v_ref, rq_ref, rk_ref, rv_ref, rg_ref = refs

    mod = mod_ref[0]
    h = _rms(x_ref[...], n1g_ref[...]) * (1.0 + mod[1:2]) + mod[0:1]
    z = _dot(h.astype(BF16), w1_ref[...])
    o = 0
    cq = z[:, o:o + Q_RANK]; o += Q_RANK
    ckv = z[:, o:o + KV_RANK]; o += KV_RANK
    krp = z[:, o:o + LANES]; o += LANES
    rq = z[:, o:o + D_RET_ALL]; o += D_RET_ALL
    rk = z[:, o:o + D_RET_ALL]; o += D_RET_ALL
    rv = z[:, o:o + D_RET_ALL]; o += D_RET_ALL
    rg = z[:, o:o + D_RET_ALL]; o += D_RET_ALL
    krp_sw = None
    if rope:
        krp_sw = z[:, o:o + LANES]; o += LANES
        rq_sw = z[:, o:o + D_RET_ALL]; o += D_RET_ALL
        rk_sw = z[:, o:o + D_RET_ALL]

    ckv_n = _rms(ckv, kvg_ref[...])
    ckvn_ref[...] = ckv_n
    krp_ref[...] = krp

    qn = _rms(cq, qng_ref[...]).astype(BF16)
    scale = D_QK ** -0.5 * LOG2_E
    q = _dot(qn, wuq_ref[:, :H_MLA * HEAD_SLOT])
    r = lax.rsqrt(_head_sums(q * q) * (1.0 / D_QK) + EPS)
    q = q * r * _tile_heads(qgain_ref[0:1] * scale)
    if rope:
        cos, sin = qk_tabs
        q_sw = _dot(qn, wuq_ref[:, H_MLA * HEAD_SLOT:])
        q = q * _tile_heads(cos) + q_sw * r * _tile_heads(qgain_ref[1:2] * scale * sin)
    q_ref[...] = q.astype(BF16)

    _kv_heads(ckv_n, krp, krp_sw, wkv_ref, kgain_ref[0:1], kgain_ref[1:2], qk_tabs, k_ref, v_ref)

    for t in range(D_RET_ALL // LANES):
        sl = slice(t * LANES, (t + 1) * LANES)
        rq_t, rk_t = rq[:, sl], rk[:, sl]
        if rope:
            rq_t = rq_t * ret_cos + rq_sw[:, sl] * ret_sin
            rk_t = rk_t * ret_cos + rk_sw[:, sl] * ret_sin
        rq_ref[:, sl] = rq_t.astype(BF16)
        rk_ref[:, sl] = (rk_t * (D_RET ** -0.5)).astype(BF16)
    rv_ref[...] = rv.astype(BF16)
    rg_ref[...] = _silu(rg).astype(BF16)


def _const_spec(shape):
    return pl.BlockSpec(shape, lambda i: (0,) * len(shape))


def _inproj(x, mod, mod_row0, tokens_per_row, wts, rope_tabs, tm):
    T = x.shape[0]
    bpr = tokens_per_row // tm
    rope = rope_tabs is not None
    tok = lambda w: pl.BlockSpec((tm, w), lambda i: (i, 0))
    in_specs = [tok(D_MODEL),
                pl.BlockSpec((1, 6, D_MODEL), lambda i: (mod_row0 + i // bpr, 0, 0)),
                _const_spec((1, D_MODEL)),
                _const_spec((D_MODEL, D_IN_PAD + (LANES + 2 * D_RET_ALL if rope else 0))),
                _const_spec((1, Q_RANK)),
                _const_spec((Q_RANK, 2 * H_MLA * HEAD_SLOT)),
                _const_spec((1, KV_RANK)),
                _const_spec((KV_RANK, 2 * H_MLA * HEAD_SLOT)),
                _const_spec((2, LANES)),
                _const_spec((2, LANES))]
    args = [x, mod, wts["norm1_g"], wts["w1_rope" if rope else "w1"], wts["q_norm_g"], wts["w_uq"],
            wts["kv_norm_g"], wts["w_kv"], wts["q_gain"], wts["k_gain"]]
    if rope:
        qk_tabs, ret_tabs = rope_tabs
        nb = qk_tabs[0].shape[0] // tm
        in_specs += [pl.BlockSpec((tm, LANES), lambda i: (i % nb, 0))] * 4
        args += list(qk_tabs) + list(ret_tabs)
    widths = [(KV_RANK, F32), (LANES, F32), (H_MLA * HEAD_SLOT, BF16), (H_MLA * HEAD_SLOT, BF16),
              (H_MLA * HEAD_SLOT, BF16)] + [(D_RET_ALL, BF16)] * 4
    return pl.pallas_call(
        functools.partial(_inproj_kernel, rope),
        grid=(T // tm,),
        in_specs=in_specs,
        out_specs=[tok(w) for w, _ in widths],
        out_shape=[jax.ShapeDtypeStruct((T, w), dt) for w, dt in widths],
        compiler_params=_cparams("parallel"),
        name="inproj_rope" if rope else "inproj",
    )(*args)


def _ctx_kv_kernel(ckv_ref, krp_ref, wkv_ref, kgain_ref, k_ref, v_ref):
    _kv_heads(ckv_ref[...], krp_ref[...], None, wkv_ref, kgain_ref[0:1], None, None, k_ref, v_ref)


def _ctx_kv(ckv, krp, wts, tm):
    T = ckv.shape[0]
    tok = lambda w: pl.BlockSpec((tm, w), lambda i: (i, 0))
    return pl.pallas_call(
        _ctx_kv_kernel,
        grid=(T // tm,),
        in_specs=[tok(KV_RANK), tok(LANES),
                  _const_spec((KV_RANK, 2 * H_MLA * HEAD_SLOT)), _const_spec((2, LANES))],
        out_specs=[tok(H_MLA * HEAD_SLOT), tok(H_MLA * HEAD_SLOT)],
        out_shape=[jax.ShapeDtypeStruct((T, H_MLA * HEAD_SLOT), BF16),
                   jax.ShapeDtypeStruct((T, H_MLA * HEAD_SLOT), BF16)],
        compiler_params=_cparams("parallel"),
        name="ctx_kv",
    )(ckv, krp, wts["w_kv"], wts["k_gain"])


ATTN_KEY_CHUNK = 2048


def _attn_kernel(has_ctx, *refs):
    if has_ctx:
        q_ref, k_ref, v_ref, kc_ref, vc_ref, o_ref = refs
    else:
        q_ref, k_ref, v_ref, o_ref = refs
    sources = [(k_ref, v_ref)] + ([(kc_ref, vc_ref)] if has_ctx else [])
    chunks = []
    for kr, vr in sources:
        n = kr.shape[1]
        kc = _pick(n, (ATTN_KEY_CHUNK, 256, 128))
        chunks += [(kr, vr, slice(c * kc, (c + 1) * kc)) for c in range(n // kc)]
    tq = q_ref.shape[1]
    outs = []
    for j in range(2):
        hs = slice(j * HEAD_SLOT, (j + 1) * HEAD_SLOT)
        q = q_ref[0, :, hs]
        m = jnp.full((tq, 1), -jnp.inf, F32)
        acc = jnp.zeros((tq, HEAD_SLOT), F32)
        for kr, vr, rows in chunks:
            s = _dot_nt(q, kr[0, rows, hs])
            m_new = jnp.maximum(m, jnp.max(s, axis=-1, keepdims=True))
            p = jnp.exp2(s - m_new).astype(BF16)
            acc = acc * jnp.exp2(m - m_new) + _dot(p, vr[0, rows, hs])
            m = m_new
        outs.append(acc[:, :D_V] / acc[:, D_V:D_V + 1])
    o_ref[0] = jnp.concatenate(outs, axis=-1).astype(BF16)


def _attention(q, k, v, kc, vc, tq):
    B, L, _ = q.shape
    has_ctx = kc is not None
    in_specs = [pl.BlockSpec((1, tq, 2 * HEAD_SLOT), lambda b, hp, i: (b, i, hp)),
                pl.BlockSpec((1, L, 2 * HEAD_SLOT), lambda b, hp, i: (b, 0, hp)),
                pl.BlockSpec((1, L, 2 * HEAD_SLOT), lambda b, hp, i: (b, 0, hp))]
    args = [q, k, v]
    if has_ctx:
        Lc = kc.shape[1]
        in_specs += [pl.BlockSpec((1, Lc, 2 * HEAD_SLOT), lambda b, hp, i: (b, 0, hp)),
                     pl.BlockSpec((1, Lc, 2 * HEAD_SLOT), lambda b, hp, i: (b, 0, hp))]
        args += [kc, vc]
    return pl.pallas_call(
        functools.partial(_attn_kernel, has_ctx),
        grid=(B, H_MLA // 2, L // tq),
        in_specs=in_specs,
        out_specs=pl.BlockSpec((1, tq, 2 * D_V), lambda b, hp, i: (b, i, hp)),
        out_shape=jax.ShapeDtypeStruct((B, L, H_MLA * D_V), BF16),
        compiler_params=_cparams("parallel", "parallel", "arbitrary"),
        name="attn_ctx" if has_ctx else "attn",
    )(*args)


RET_UNROLL = 8
RET_SHORT_SEQ = 1024


def _ret_pair(has_s0, want_state, n_chunks, pair, hp, *refs):
    rq_ref, rk_ref, rv_ref, rg_ref, decf_ref, decb_ref, g_ref = refs[:7]
    refs = refs[7:]
    if has_s0:
        s0f_ref, s0b_ref = refs[:2]
        refs = refs[2:]
    o_ref = refs[0]
    refs = refs[1:]
    if want_state:
        sf_ref, sb_ref = refs[:2]
        refs = refs[2:]
    kv_ref, sp_ref = refs

    C = RET_CHUNK
    W = 2 * D_RET
    ls = slice(pair * W, (pair + 1) * W)
    lane = lax.broadcasted_iota(jnp.int32, (1, W), 1)
    lane_h1 = lane >= D_RET
    row_h1 = lax.broadcasted_iota(jnp.int32, (W, 1), 0) >= D_RET
    blockdiag = row_h1 == lane_h1
    pos = lax.broadcasted_iota(jnp.int32, (C, 1), 0).astype(F32)
    diff = (lax.broadcasted_iota(jnp.int32, (C, C), 0)
            - lax.broadcasted_iota(jnp.int32, (C, C), 1)).astype(F32)

    lg = {}
    for name, ref in (("f", decf_ref), ("b", decb_ref)):
        a = -jnp.exp(ref[pl.ds(2 * hp, 1), :])
        b = -jnp.exp(ref[pl.ds(2 * hp + 1, 1), :])
        lg[name] = (a[:, :1], b[:, :1], jnp.where(lane_h1, b, a))
    lgf0, lgf1, lgf = lg["f"]
    lgb0, lgb1, lgb = lg["b"]
    qdf, kdf = jnp.exp(lgf * (pos + 1.0)), jnp.exp(lgf * (C - 1.0 - pos))
    qdb, kdb = jnp.exp(lgb * (C - pos)), jnp.exp(lgb * pos)
    row_h1_full = lax.broadcasted_iota(jnp.int32, (W, W), 0) >= D_RET
    cdf = jnp.where(row_h1_full, jnp.exp(lgf1 * C), jnp.exp(lgf0 * C))
    cdb = jnp.where(row_h1_full, jnp.exp(lgb1 * C), jnp.exp(lgb0 * C))

    def intra_decay(lf, lb):
        return jnp.where(diff >= 0, jnp.exp(lf * jnp.maximum(diff, 0.0)), jnp.exp(lb * jnp.maximum(-diff, 0.0)))

    dmat0, dmat1 = intra_decay(lgf0, lgb0), intra_decay(lgf1, lgb1)
    g = g_ref[...]

    def chunk_rows(n):
        return pl.ds(pl.multiple_of(n * C, C), C)

    def phase_a(n, carry):
        rows = chunk_rows(n)
        k = rk_ref[0, rows, ls].astype(F32)
        k2 = jnp.concatenate([(k * kdf).astype(BF16), (k * kdb).astype(BF16)], axis=1)
        kv_ref[n] = _dot_tn(k2, rv_ref[0, rows, ls])
        return carry

    unroll = min(RET_UNROLL, n_chunks)
    lax.fori_loop(0, n_chunks, phase_a, 0, unroll=unroll)

    def init_state(ref):
        if not has_s0:
            return jnp.zeros((W, W), F32)
        z = jnp.zeros((D_RET, D_RET), F32)
        return jnp.concatenate([jnp.concatenate([ref[0, 2 * pair], z], axis=1),
                                jnp.concatenate([z, ref[0, 2 * pair + 1]], axis=1)], axis=0)

    def scan_f(n, S):
        sp_ref[n, 0:W, :] = S.astype(BF16)
        return cdf * S + jnp.where(blockdiag, kv_ref[n, 0:W, :], 0.0)

    def scan_b(i, S):
        n = n_chunks - 1 - i
        sp_ref[n, W:2 * W, :] = S.astype(BF16)
        return cdb * S + jnp.where(blockdiag, kv_ref[n, W:2 * W, :], 0.0)

    Sf = lax.fori_loop(0, n_chunks, scan_f, init_state(s0f_ref if has_s0 else None))
    Sb = lax.fori_loop(0, n_chunks, scan_b, init_state(s0b_ref if has_s0 else None))
    if want_state:
        for S, ref in ((Sf, sf_ref), (Sb, sb_ref)):
            ref[0, 2 * pair] = S[0:D_RET, 0:D_RET]
            ref[0, 2 * pair + 1] = S[D_RET:W, D_RET:W]

    def phase_c(n, carry):
        rows = chunk_rows(n)
        q = rq_ref[0, rows, ls]
        k = rk_ref[0, rows, ls]
        v = rv_ref[0, rows, ls]
        zero = jnp.zeros_like(q)
        sc0 = _dot_nt(jnp.where(lane_h1, zero, q), k) * dmat0
        sc1 = _dot_nt(jnp.where(lane_h1, q, zero), k) * dmat1
        p = jnp.concatenate([sc0.astype(BF16), sc1.astype(BF16)], axis=1)
        v_bd = jnp.concatenate([jnp.where(lane_h1, zero, v), jnp.where(lane_h1, v, zero)], axis=0)
        qf = q.astype(F32)
        q2 = jnp.concatenate([(qf * qdf).astype(BF16), (qf * qdb).astype(BF16)], axis=1)
        o = _dot(p, v_bd) + _dot(q2, sp_ref[n])
        o2 = o * o
        ss0 = jnp.sum(jnp.where(lane_h1, 0.0, o2), axis=-1, keepdims=True)
        ss1 = jnp.sum(jnp.where(lane_h1, o2, 0.0), axis=-1, keepdims=True)
        ms = jnp.where(lane_h1, ss1, ss0) * (1.0 / D_RET)
        o = o * lax.rsqrt(ms + EPS) * g * rg_ref[0, rows, ls].astype(F32)
        o_ref[0, rows, ls] = o.astype(BF16)
        return carry

    lax.fori_loop(0, n_chunks, phase_c, 0, unroll=unroll)


def _ret_kernel(has_s0, want_state, n_chunks, pairs, *refs):
    for pair in range(pairs):
        _ret_pair(has_s0, want_state, n_chunks, pair, pl.program_id(1) * pairs + pair, *refs)


def _retention(rq, rk, rv, rg, dec_f, dec_b, g, s0f, s0b, want_state):
    B, L, _ = rq.shape
    has_s0 = s0f is not None
    pairs = H_RET // 2 if L <= RET_SHORT_SEQ else 1
    seq = pl.BlockSpec((1, L, pairs * LANES), lambda b, hp: (b, 0, hp))
    st = pl.BlockSpec((1, 2 * pairs, D_RET, D_RET), lambda b, hp: (b, hp, 0, 0))
    cst = lambda shape: pl.BlockSpec(shape, lambda b, hp: (0,) * len(shape))
    in_specs = [seq] * 4 + [cst((H_RET, LANES)), cst((H_RET, LANES)), cst((1, LANES))]
    args = [rq, rk, rv, rg, dec_f, dec_b, g]
    if has_s0:
        in_specs += [st, st]
        args += [s0f, s0b]
    out_specs = [seq]
    out_shape = [jax.ShapeDtypeStruct((B, L, D_RET_ALL), BF16)]
    if want_state:
        out_specs += [st, st]
        out_shape += [jax.ShapeDtypeStruct((B, H_RET, D_RET, D_RET), F32)] * 2
    return pl.pallas_call(
        functools.partial(_ret_kernel, has_s0, want_state, L // RET_CHUNK, pairs),
        grid=(B, H_RET // 2 // pairs),
        in_specs=in_specs,
        out_specs=out_specs,
        out_shape=out_shape,
        scratch_shapes=[pltpu.VMEM((L // RET_CHUNK, 4 * D_RET, LANES), F32),
                        pltpu.VMEM((L // RET_CHUNK, 4 * D_RET, LANES), BF16)],
        compiler_params=_cparams("parallel", "parallel"),
        name="retention_s0" if has_s0 else "retention",
    )(*args)


def _pack_pairs(x):
    w = x.shape[1] // 2
    hi = lax.bitcast_convert_type(x[:, :w].astype(BF16).astype(F32), jnp.uint32)
    lo = lax.bitcast_convert_type(x[:, w:].astype(BF16).astype(F32), jnp.uint32)
    return hi | (lo >> 16)


def _unpack_pairs(p):
    hi = lax.bitcast_convert_type(p & jnp.uint32(0xFFFF0000), F32)
    lo = lax.bitcast_convert_type(p << 16, F32)
    return jnp.concatenate([hi, lo], axis=1)


D_PACKED = D_MODEL // 2


def _outproj_kernel(omla_ref, oret_ref, x_ref, mod_ref, wo_ref, n2g_ref, rwh_ref, rwl_ref, rb_ref, tri_ref,
                    x2_ref, h2p_ref, code_ref, rank_ref, wk_ref, cnt_ref):
    @pl.when(pl.program_id(0) == 0)
    def _():
        cnt_ref[...] = jnp.zeros_like(cnt_ref)

    mod = mod_ref[0]
    half = H_MLA * D_V
    a = _dot(omla_ref[...], wo_ref[:half, :]) + _dot(oret_ref[...], wo_ref[half:, :])
    x2 = x_ref[...] + mod[2:3] * a
    x2_ref[...] = x2
    h2 = _rms(x2, n2g_ref[...]) * (1.0 + mod[4:5]) + mod[3:4]
    h2p_ref[...] = _pack_pairs(h2)
    h2_hi = h2.astype(BF16)
    h2_lo = (h2 - h2_hi.astype(F32)).astype(BF16)
    rw_hi = rwh_ref[...]
    logits = _dot_nt(rw_hi, h2_hi) + _dot_nt(rw_hi, h2_lo) + _dot_nt(rwl_ref[...], h2_hi)
    scores = jax.nn.sigmoid(logits)
    tm = scores.shape[1]
    expert = lax.broadcasted_iota(jnp.int32, scores.shape, 0)
    cand = scores + rb_ref[...]
    code = jnp.zeros(scores.shape, jnp.int32)
    for k in range(TOP_K):
        mx = jnp.max(cand, axis=0, keepdims=True)
        first = jnp.min(jnp.where(cand == mx, expert, LANES), axis=0, keepdims=True)
        hit = expert == first
        code = jnp.where(hit, k + 1, code)
        cand = jnp.where(hit, -jnp.inf, cand)
    picked = code > 0
    sel = jnp.where(picked, scores, 0.0)
    gates = sel * (ROUTED_SCALE / jnp.sum(sel, axis=0, keepdims=True))
    code_ref[...] = code

    onehot = jnp.where(picked, 1.0, 0.0)
    carry = cnt_ref[...].astype(F32)
    rank = _dot(onehot.astype(BF16), tri_ref[...])
    rank_ref[...] = rank + jnp.concatenate([carry] * (tm // LANES), axis=1)
    cnt_ref[...] = (carry + jnp.sum(onehot, axis=1, keepdims=True)).astype(jnp.int32)

    row = lax.broadcasted_iota(jnp.int32, scores.shape, 0)
    wk_t = jnp.zeros(scores.shape, F32)
    for k in range(TOP_K):
        w = jnp.sum(jnp.where(code == k + 1, gates, 0.0), axis=0, keepdims=True)
        wk_t = jnp.where(row == k, w, wk_t)
    wk_ref[...] = wk_t.T


def _outproj(o_mla, o_ret, x, mod, mod_row0, tokens_per_row, wts, tm):
    T = x.shape[0]
    bpr = tokens_per_row // tm
    tok = lambda w: pl.BlockSpec((tm, w), lambda i: (i, 0))
    emaj = pl.BlockSpec((LANES, tm), lambda i: (0, i))
    tri = jnp.triu(jnp.ones((tm, tm), BF16), 1)
    bias = jnp.broadcast_to(wts["router_bias"].reshape(LANES, 1), (LANES, tm))
    return pl.pallas_call(
        _outproj_kernel,
        grid=(T // tm,),
        in_specs=[tok(H_MLA * D_V), tok(D_RET_ALL), tok(D_MODEL),
                  pl.BlockSpec((1, 6, D_MODEL), lambda i: (mod_row0 + i // bpr, 0, 0)),
                  _const_spec((D_MODEL, D_MODEL)), _const_spec((1, D_MODEL)),
                  _const_spec((LANES, D_MODEL)), _const_spec((LANES, D_MODEL)), _const_spec((LANES, tm)),
                  _const_spec((tm, tm))],
        out_specs=[tok(D_MODEL), tok(D_PACKED), emaj, emaj, tok(LANES), _const_spec((LANES, LANES))],
        out_shape=[jax.ShapeDtypeStruct((T, D_MODEL), F32),
                   jax.ShapeDtypeStruct((T, D_PACKED), jnp.uint32),
                   jax.ShapeDtypeStruct((LANES, T), jnp.int32),
                   jax.ShapeDtypeStruct((LANES, T), F32),
                   jax.ShapeDtypeStruct((T, LANES), F32),
                   jax.ShapeDtypeStruct((LANES, LANES), jnp.int32)],
        compiler_params=_cparams("arbitrary"),
        name="outproj_router",
    )(o_mla, o_ret, x, mod, wts["w_o"], wts["norm2_g"], wts["router_hi"], wts["router_lo"], bias, tri)


def _plan_kernel(code_ref, rank_ref, first_ref, pos_ref):
    code = code_ref[...]
    tm = code.shape[1]
    row = rank_ref[...] + jnp.concatenate([first_ref[...]] * (tm // LANES), axis=1)
    krow = lax.broadcasted_iota(jnp.int32, pos_ref.shape, 0)
    pos = jnp.zeros(pos_ref.shape, F32)
    for k in range(TOP_K):
        pos = jnp.where(krow == k, jnp.sum(jnp.where(code == k + 1, row, 0.0), axis=0, keepdims=True), pos)
    pos_ref[...] = pos.astype(jnp.int32)


def _plan(code, rank, first_rows, tm):
    T = code.shape[1]
    first = jnp.broadcast_to(jnp.pad(first_rows.astype(F32), (0, LANES - N_EXPERTS)).reshape(LANES, 1),
                             (LANES, LANES))
    emaj = pl.BlockSpec((LANES, tm), lambda i: (0, i))
    return pl.pallas_call(
        _plan_kernel,
        grid=(T // tm,),
        in_specs=[emaj, emaj, _const_spec((LANES, LANES))],
        out_specs=pl.BlockSpec((8, tm), lambda i: (0, i)),
        out_shape=jax.ShapeDtypeStruct((8, T), jnp.int32),
        compiler_params=_cparams("parallel"),
        name="moe_plan",
    )(code, rank, first)


SC_WORKERS = 32
SC_CHUNK = 64


def _sc_worker_id():
    return lax.axis_index("s") * 2 + lax.axis_index("c")


def _sc_mesh():
    return plsc.VectorSubcoreMesh(core_axis_name="c", subcore_axis_name="s")


def _sc_gather_rows(table, idx):
    N = idx.shape[0]
    W = table.shape[1]
    per_w = N // SC_WORKERS
    n_chunks = per_w // SC_CHUNK
    assert N == SC_WORKERS * n_chunks * SC_CHUNK and n_chunks % 2 == 0

    @functools.partial(
        pl.kernel, mesh=_sc_mesh(), out_type=jax.ShapeDtypeStruct((N, W), table.dtype),
        scratch_types=[pltpu.VMEM((n_chunks, SC_CHUNK), jnp.int32),
                       pltpu.VMEM((2, SC_CHUNK, W), table.dtype),
                       pltpu.SemaphoreType.DMA((2,)), pltpu.SemaphoreType.DMA((2,))],
        name="sc_gather_rows")
    def k(table_hbm, idx_hbm, out_hbm, idx_v, rows_v, gsem, wsem):
        wid = _sc_worker_id()
        base = wid * per_w
        pltpu.sync_copy(idx_hbm.at[wid], idx_v)

        def gather(j, b):
            return pltpu.make_async_copy(table_hbm.at[idx_v.at[j]], rows_v.at[b], gsem.at[b])

        def writeback(j, b):
            off = pl.multiple_of(base + j * SC_CHUNK, SC_CHUNK)
            return pltpu.make_async_copy(rows_v.at[b], out_hbm.at[pl.ds(off, SC_CHUNK)], wsem.at[b])

        gather(0, 0).start()

        @pl.loop(0, n_chunks, step=2)
        def _(j):
            for b in range(2):
                jj = j + b

                @pl.when(jj + 1 < n_chunks)
                def _():
                    @pl.when(jj >= 1)
                    def _():
                        writeback(jj - 1, 1 - b).wait()
                    gather(jj + 1, 1 - b).start()

                gather(jj, b).wait()
                writeback(jj, b).start()

        writeback(n_chunks - 2, 0).wait()
        writeback(n_chunks - 1, 1).wait()

    return k(table, idx.reshape(SC_WORKERS, n_chunks, SC_CHUNK))


def _sc_dispatch_rows(rows_list, pos_list, n_out):
    W = rows_list[0].shape[1]
    K = pos_list[0].shape[0]
    n_src = len(rows_list)
    plan = []
    idx_list = []
    for rows, pos in zip(rows_list, pos_list):
        T = rows.shape[0]
        per_w = T // SC_WORKERS
        n_chunks = per_w // SC_CHUNK
        assert T == SC_WORKERS * n_chunks * SC_CHUNK and n_chunks % 2 == 0
        plan.append((per_w, n_chunks))
        idx_list.append(pos.reshape(K, SC_WORKERS, n_chunks, SC_CHUNK).transpose(1, 2, 0, 3))
    max_chunks = max(n for _, n in plan)

    @functools.partial(
        pl.kernel, mesh=_sc_mesh(), out_type=jax.ShapeDtypeStruct((n_out, W), rows_list[0].dtype),
        scratch_types=[pltpu.VMEM((max_chunks, K, SC_CHUNK), jnp.int32),
                       pltpu.VMEM((2, SC_CHUNK, W), rows_list[0].dtype),
                       pltpu.SemaphoreType.DMA((2,)), pltpu.SemaphoreType.DMA((2,))],
        name="sc_dispatch_rows")
    def k(*refs):
        rows_refs, idx_refs = refs[:n_src], refs[n_src:2 * n_src]
        out_hbm, idx_v, rows_v, lsem, ssem = refs[2 * n_src:]
        wid = _sc_worker_id()
        for rows_hbm, idx_hbm, (per_w, n_chunks) in zip(rows_refs, idx_refs, plan):
            base = wid * per_w
            pltpu.sync_copy(idx_hbm.at[wid], idx_v.at[pl.ds(0, n_chunks)])

            def load(j, b):
                off = pl.multiple_of(base + j * SC_CHUNK, SC_CHUNK)
                return pltpu.make_async_copy(rows_hbm.at[pl.ds(off, SC_CHUNK)], rows_v.at[b], lsem.at[b])

            def scatter(j, b, kk):
                return pltpu.make_async_copy(rows_v.at[b], out_hbm.at[idx_v.at[j, kk]], ssem.at[b])

            load(0, 0).start()

            @pl.loop(0, n_chunks, step=2)
            def _(j):
                for b in range(2):
                    jj = j + b

                    @pl.when(jj + 1 < n_chunks)
                    def _():
                        @pl.when(jj >= 1)
                        def _():
                            for kk in range(K):
                                scatter(jj - 1, 1 - b, kk).wait()
                        load(jj + 1, 1 - b).start()

                    load(jj, b).wait()
                    for kk in range(K):
                        scatter(jj, b, kk).start()

            for kk in range(K):
                scatter(n_chunks - 2, 0, kk).wait()
            for kk in range(K):
                scatter(n_chunks - 1, 1, kk).wait()

    return k(*rows_list, *idx_list)


ROW_TILE = 1024
ROW_SUB = 256
MOE_GATHER_GROUP = 8192


def _experts_kernel(te_ref, tv_ref, tb_ref, xs_ref, wg_ref, wu_ref, wd_ref, ys_ref, wg_s, wu_s, wd_s):
    del tb_ref
    i = pl.program_id(0)
    valid = tv_ref[i]

    @pl.when(jnp.logical_or(i == 0, te_ref[i] != te_ref[jnp.maximum(i - 1, 0)]))
    def _():
        wg_s[...] = wg_ref[0].astype(BF16)
        wu_s[...] = wu_ref[0].astype(BF16)
        wd_s[...] = wd_ref[0].astype(BF16)

    def ffn(r0, n):
        row = r0 + lax.broadcasted_iota(jnp.int32, (n, 1), 0)
        p = jnp.where(row < valid, xs_ref[r0:r0 + n, :], jnp.uint32(0))
        x = _unpack_pairs(p).astype(BF16)
        act = _silu(_dot(x, wg_s[...])) * _dot(x, wu_s[...])
        ys_ref[r0:r0 + n, :] = _pack_pairs(_dot(act.astype(BF16), wd_s[...]))

    @pl.when(valid == ROW_TILE)
    def _():
        ffn(0, ROW_TILE)

    for sb in range(ROW_TILE // ROW_SUB):
        @pl.when(jnp.logical_and(valid < ROW_TILE, valid > sb * ROW_SUB))
        def _():
            ffn(sb * ROW_SUB, ROW_SUB)


def _experts(xs, tile_expert, tile_valid, tile_block, wts):
    n_tiles = tile_expert.shape[0]
    ex = lambda a, b: pl.BlockSpec((1, a, b), lambda i, te, tv, tb: (te[i], 0, 0))
    row = pl.BlockSpec((ROW_TILE, D_PACKED), lambda i, te, tv, tb: (tb[i], 0))
    return pl.pallas_call(
        _experts_kernel,
        grid_spec=pltpu.PrefetchScalarGridSpec(
            num_scalar_prefetch=3, grid=(n_tiles,),
            in_specs=[row, ex(D_MODEL, D_EXPERT), ex(D_MODEL, D_EXPERT), ex(D_EXPERT, D_MODEL)],
            out_specs=row,
            scratch_shapes=[pltpu.VMEM((D_MODEL, D_EXPERT), BF16), pltpu.VMEM((D_MODEL, D_EXPERT), BF16),
                            pltpu.VMEM((D_EXPERT, D_MODEL), BF16)]),
        out_shape=jax.ShapeDtypeStruct(xs.shape, jnp.uint32),
        compiler_params=_cparams("arbitrary"),
        name="experts",
    )(tile_expert, tile_valid, tile_block, xs, wts["exp_gate"], wts["exp_up"], wts["exp_down"])


def _combine_kernel(h2p_ref, ys_ref, wk_ref, x2_ref, mod_ref, shg_ref, shu_ref, shd_ref, *rest):
    y_ref = rest[-1]
    t = _unpack_pairs(h2p_ref[...]).astype(BF16)
    act = _silu(_dot(t, shg_ref[...])) * _dot(t, shu_ref[...])
    acc = _dot(act.astype(BF16), shd_ref[...])
    wk = wk_ref[...]
    for k in range(TOP_K):
        acc = acc + wk[:, k:k + 1] * _unpack_pairs(ys_ref[k])
    y_ref[...] = x2_ref[...] + mod_ref[0][5:6] * acc


def _combine(h2p, ys, row0, wk, x2, y_prev, mod, mod_row0, tokens_per_row, wts, tm):
    T = h2p.shape[0]
    G = ys.shape[1]
    bpr = tokens_per_row // tm
    blk0 = row0 // tm
    tok = lambda w: pl.BlockSpec((tm, w), lambda i: (blk0 + i, 0))
    in_specs = [tok(D_PACKED), pl.BlockSpec((TOP_K, tm, D_PACKED), lambda i: (0, i, 0)), tok(LANES),
                tok(D_MODEL), pl.BlockSpec((1, 6, D_MODEL), lambda i: (mod_row0 + (blk0 + i) // bpr, 0, 0)),
                _const_spec((D_MODEL, D_SHARED)), _const_spec((D_MODEL, D_SHARED)),
                _const_spec((D_SHARED, D_MODEL))]
    args = [h2p, ys, wk, x2, mod, wts["sh_gate"], wts["sh_up"], wts["sh_down"]]
    aliases = {}
    if y_prev is not None:
        aliases = {len(args): 0}
        in_specs.append(pl.BlockSpec(memory_space=pl.ANY))
        args.append(y_prev)
    return pl.pallas_call(
        _combine_kernel,
        grid=(G // tm,),
        in_specs=in_specs,
        out_specs=tok(D_MODEL),
        out_shape=jax.ShapeDtypeStruct((T, D_MODEL), F32),
        input_output_aliases=aliases,
        compiler_params=_cparams("parallel"),
        name="moe_combine",
    )(*args)


def _moe(halves, mod, wts):
    routed = [h["routed"] for h in halves]
    sizes = [r[1].shape[0] for r in routed]
    n_tiles = (TOP_K * sum(sizes)) // ROW_TILE + N_EXPERTS
    path_counts = [r[5][:N_EXPERTS, 0] for r in routed]
    counts = sum(path_counts)
    padded = ((counts + ROW_TILE - 1) // ROW_TILE) * ROW_TILE
    ends = jnp.cumsum(padded)
    starts = ends - padded
    tile_start = jnp.arange(n_tiles, dtype=jnp.int32) * ROW_TILE
    tile_expert = jnp.minimum(jnp.sum(tile_start[:, None] >= ends[None, :], axis=1), N_EXPERTS - 1).astype(jnp.int32)
    overlap = (jnp.minimum(tile_start[:, None] + ROW_TILE, (starts + counts)[None, :])
               - jnp.maximum(tile_start[:, None], starts[None, :]))
    tile_valid = jnp.sum(jnp.maximum(overlap, 0), axis=1).astype(jnp.int32)

    pos, first = [], starts
    for h, r, c in zip(halves, routed, path_counts):
        pos.append(_plan(r[2], r[3], first, h["tm"])[:TOP_K])
        first = first + c
    xs = _sc_dispatch_rows([r[1] for r in routed], pos, n_tiles * ROW_TILE)
    tile_block = jnp.minimum(jnp.arange(n_tiles, dtype=jnp.int32), ends[-1] // ROW_TILE - 1).astype(jnp.int32)
    ys = _experts(xs, tile_expert, tile_valid, tile_block, wts)
    outs = []
    for h, r, p in zip(halves, routed, pos):
        T = r[1].shape[0]
        G = MOE_GATHER_GROUP if T % MOE_GATHER_GROUP == 0 else T
        y = None
        for row0 in range(0, T, G):
            ysel = _sc_gather_rows(ys, p[:, row0:row0 + G].reshape(-1)).reshape(TOP_K, G, D_PACKED)
            y = _combine(r[1], ysel, row0, r[4], r[0], y, mod, h["mod_row0"], h["tokens_per_row"], wts, h["tm"])
        outs.append(y.reshape(h["shape"]))
    return outs


def _pad_heads(w, d_used):
    k = w.shape[0]
    w = w.reshape(k, H_MLA, d_used)
    return jnp.pad(w, ((0, 0), (0, 0), (0, HEAD_SLOT - d_used))).reshape(k, H_MLA * HEAD_SLOT)


def _lane_pad(v):
    return jnp.pad(v, (0, LANES - v.shape[0])).reshape(1, LANES)


def _rot_partners(w):
    rot = w[..., w.shape[-1] - D_ROPE:]
    return jnp.concatenate([jnp.zeros_like(w[..., :w.shape[-1] - D_ROPE]),
                            _swap_halves(rot, D_ROPE // 4)], axis=-1)


def _prep_weights(l, w_in, norm1_g, mla_q_norm_g, w_uq, mla_kv_norm_g, w_ukv, mla_q_gain, mla_k_gain,
                  w_o, norm2_g, router_w, router_bias, exp_w_gate, exp_w_up, exp_w_down,
                  sh_w_gate, sh_w_up, sh_w_down):
    w = w_in[l]
    a, b, c = Q_RANK, Q_RANK + KV_RANK, Q_RANK + KV_RANK + D_ROPE
    kr_slot = lambda wk: jnp.pad(wk, ((0, 0), (D_NOPE, LANES - D_QK)))
    w1 = jnp.concatenate([w[:, :b], kr_slot(w[:, b:c]), w[:, c:]], axis=1).astype(BF16)
    w_rq, w_rk = w[:, c:c + D_RET_ALL], w[:, c + D_RET_ALL:c + 2 * D_RET_ALL]
    w1_rope = jnp.concatenate([w1, kr_slot(_rot_partners(w[:, b:c])).astype(BF16),
                               _swap_halves(w_rq, D_RET // 4).astype(BF16),
                               _swap_halves(w_rk, D_RET // 4).astype(BF16)], axis=1)
    uq = w_uq[l].reshape(Q_RANK, H_MLA, D_QK)
    pad_slots = lambda u: jnp.pad(u, ((0, 0), (0, 0), (0, HEAD_SLOT - D_QK))).reshape(Q_RANK, H_MLA * HEAD_SLOT)
    gains = lambda g: jnp.concatenate([_lane_pad(g), _lane_pad(_rot_partners(g))], axis=0)
    ukv = w_ukv[l].reshape(KV_RANK, H_MLA, D_NOPE + D_V)
    w_k = _pad_heads(ukv[:, :, :D_NOPE].reshape(KV_RANK, H_MLA * D_NOPE), D_NOPE)
    w_v = _pad_heads(ukv[:, :, D_NOPE:].reshape(KV_RANK, H_MLA * D_V), D_V)
    rw = jnp.pad(router_w[l].T, ((0, LANES - N_EXPERTS), (0, 0)))
    rw_hi = rw.astype(BF16)
    rw_lo = (rw - rw_hi.astype(F32)).astype(BF16)
    rb = jnp.concatenate([router_bias[l].astype(F32), jnp.full((LANES - N_EXPERTS,), -jnp.inf, F32)])
    return dict(
        norm1_g=norm1_g[l].reshape(1, D_MODEL), w1=w1, w1_rope=w1_rope,
        q_norm_g=mla_q_norm_g[l].reshape(1, Q_RANK),
        w_uq=jnp.concatenate([pad_slots(uq), pad_slots(_rot_partners(uq))], axis=1).astype(BF16),
        kv_norm_g=mla_kv_norm_g[l].reshape(1, KV_RANK),
        w_kv=jnp.concatenate([w_k, w_v], axis=1).astype(BF16),
        q_gain=gains(mla_q_gain[l]), k_gain=gains(mla_k_gain[l]),
        w_o=w_o[l].astype(BF16), norm2_g=norm2_g[l].reshape(1, D_MODEL),
        router_hi=rw_hi, router_lo=rw_lo, router_bias=rb.reshape(1, LANES),
        exp_gate=exp_w_gate[l], exp_up=exp_w_up[l], exp_down=exp_w_down[l],
        sh_gate=sh_w_gate[l].astype(BF16), sh_up=sh_w_up[l].astype(BF16),
        sh_down=sh_w_down[l].astype(BF16))


def _pick(n, prefs):
    for p in prefs:
        if n % p == 0:
            return p
    return n


def _mixer_half(x, mod, mod_row0, wts, ret_wts, ctx):
    B, L, D = x.shape
    T = B * L
    tokens_per_row = L if ctx is not None else T
    xt = x.reshape(T, D)
    rope_tabs = None
    if ctx is not None:
        rope_tabs = _rope_tables(L)
    tm = _pick(tokens_per_row, (512, 256, 128))
    ckv_n, krp, q, k, v, rq, rk, rv, rg = _inproj(xt, mod, mod_row0, tokens_per_row, wts, rope_tabs, tm)

    seq = lambda a: a.reshape(B, L, a.shape[-1])
    kc = vc = s0f = s0b = None
    if ctx is not None:
        ckv_c, kr_c, s0f, s0b = ctx
        Lc = ckv_c.shape[1]
        krp_c = jnp.pad(kr_c, ((0, 0), (0, 0), (D_NOPE, LANES - D_NOPE - D_ROPE)))
        kc, vc = _ctx_kv(ckv_c.reshape(B * Lc, KV_RANK), krp_c.reshape(B * Lc, LANES), wts,
                         _pick(B * Lc, (512, 256, 128)))
        kc, vc = kc.reshape(B, Lc, -1), vc.reshape(B, Lc, -1)
    o_mla = _attention(seq(q), seq(k), seq(v), kc, vc, _pick(L, (1024, 512, 256, 128)))

    dec_f, dec_b, ret_g = ret_wts
    want_state = ctx is None
    ret = _retention(seq(rq), seq(rk), seq(rv), seq(rg), dec_f, dec_b, ret_g, s0f, s0b, want_state)
    o_ret = ret[0]

    routed = _outproj(o_mla.reshape(T, -1), o_ret.reshape(T, -1), xt, mod, mod_row0, tokens_per_row, wts, tm)
    new = None
    if want_state:
        new = (ckv_n.reshape(B, L, KV_RANK), krp[:, D_NOPE:D_NOPE + D_ROPE].reshape(B, L, D_ROPE),
               ret[1], ret[2])
    return dict(routed=routed, mod_row0=mod_row0, tokens_per_row=tokens_per_row, tm=tm, shape=(B, L, D)), new


def kernel(x_prompt, x_sample, cache_mla_ckv, cache_mla_krope, state_ret_fwd, state_ret_bwd, c, c_ctx,
           w_ada, b_ada, norm1_g, w_in, mla_q_norm_g, w_uq, mla_kv_norm_g, w_ukv, mla_q_gain, mla_k_gain,
           ret_decay_fwd, ret_decay_bwd, ret_norm_g, w_o, norm2_g, router_w, router_bias, exp_w_gate,
           exp_w_up, exp_w_down, sh_w_gate, sh_w_up, sh_w_down):
    depth = w_ada.shape[0]
    n_dec = c.shape[0]
    assert 1 + n_dec <= MOD_ROWS
    cond = jnp.concatenate([c_ctx[None], c, jnp.zeros((MOD_ROWS - 1 - n_dec, D_MODEL), F32)], axis=0)

    y_prompt, y_sample = x_prompt, x_sample
    ckv_l, kr_l, sf_l, sb_l = [], [], [], []
    for l in range(depth):
        wts = _prep_weights(l, w_in, norm1_g, mla_q_norm_g, w_uq, mla_kv_norm_g, w_ukv, mla_q_gain,
                            mla_k_gain, w_o, norm2_g, router_w, router_bias, exp_w_gate, exp_w_up,
                            exp_w_down, sh_w_gate, sh_w_up, sh_w_down)
        ret_wts = (jnp.broadcast_to(ret_decay_fwd[l].astype(F32)[:, None], (H_RET, LANES)),
                   jnp.broadcast_to(ret_decay_bwd[l].astype(F32)[:, None], (H_RET, LANES)),
                   jnp.tile(ret_norm_g[l].reshape(1, D_RET), (1, LANES // D_RET)))
        mod = _adaln(cond, w_ada[l], b_ada[l]).reshape(MOD_ROWS, 6, D_MODEL)
        half_p, new = _mixer_half(y_prompt, mod, 0, wts, ret_wts, None)
        ckv_l.append(new[0]); kr_l.append(new[1]); sf_l.append(new[2]); sb_l.append(new[3])
        ctx = (cache_mla_ckv[:, l], cache_mla_krope[:, l], state_ret_fwd[:, l], state_ret_bwd[:, l])
        half_s, _ = _mixer_half(y_sample, mod, 1, wts, ret_wts, ctx)
        y_prompt, y_sample = _moe([half_p, half_s], mod, wts)

    return (y_prompt, y_sample, jnp.stack(ckv_l, axis=1), jnp.stack(kr_l, axis=1),
            jnp.stack(sf_l, axis=1), jnp.stack(sb_l, axis=1))
```

```python
import functools
import math

import numpy as np
import jax
import jax.numpy as jnp
from jax import lax
from jax.experimental import pallas as pl
from jax.experimental.pallas import tpu as pltpu
from jax.experimental.pallas import tpu_sc as plsc

F32 = jnp.float32
BF16 = jnp.bfloat16

D_MODEL = 1024
GRID_W = 64
H_MLA = 8
D_NOPE = 64
D_ROPE = 32
D_QK = D_NOPE + D_ROPE
D_V = 64
Q_RANK = 256
KV_RANK = 128
H_RET = 8
D_RET = 64
RET_CHUNK = 128
D_RET_ALL = H_RET * D_RET
N_EXPERTS = 64
TOP_K = 6
D_EXPERT = 256
D_SHARED = 256
ROUTED_SCALE = 2.5
ROPE_BASE = 10000.0
EPS = 1e-6
LOG2_E = math.log2(math.e)

LANES = 128
HEAD_SLOT = LANES
D_IN_PAD = Q_RANK + KV_RANK + LANES + 4 * D_RET_ALL
MOD_ROWS = 16
VMEM_LIMIT = 56 * 1024 * 1024


def _cparams(*sem):
    return pltpu.CompilerParams(dimension_semantics=sem, vmem_limit_bytes=VMEM_LIMIT)


def _dot(a, b):
    return jnp.dot(a, b, preferred_element_type=F32)


def _dot_nt(a, b):
    return lax.dot_general(a, b, (((1,), (1,)), ((), ())), preferred_element_type=F32)


def _dot_tn(a, b):
    return lax.dot_general(a, b, (((0,), (0,)), ((), ())), preferred_element_type=F32)


def _rms(x, g):
    return x * lax.rsqrt(jnp.mean(x * x, axis=-1, keepdims=True) + EPS) * g


def _silu(x):
    return x * jax.nn.sigmoid(x)


def _adaln_kernel(c_ref, w_ref, b_ref, o_ref):
    s = _silu(c_ref[...])
    o_ref[...] = _dot(s.astype(BF16), w_ref[...].astype(BF16)) + b_ref[...]


def _adaln(cond, w_ada, b_ada):
    n_out = w_ada.shape[1]
    bn = 1536
    return pl.pallas_call(
        _adaln_kernel,
        grid=(n_out // bn,),
        in_specs=[pl.BlockSpec((MOD_ROWS, D_MODEL), lambda j: (0, 0)),
                  pl.BlockSpec((D_MODEL, bn), lambda j: (0, j)),
                  pl.BlockSpec((1, bn), lambda j: (0, j))],
        out_specs=pl.BlockSpec((MOD_ROWS, bn), lambda j: (0, j)),
        out_shape=jax.ShapeDtypeStruct((MOD_ROWS, n_out), F32),
        compiler_params=_cparams("arbitrary"),
        name="adaln",
    )(cond, w_ada, b_ada.reshape(1, n_out))


def _swap_halves(w, half_pair):
    g = w.reshape(*w.shape[:-1], w.shape[-1] // (2 * half_pair), 2, half_pair)
    return g[..., ::-1, :].reshape(w.shape)


def _rope_tables(n_tokens):
    t = np.arange(n_tokens)
    row = (t // GRID_W).astype(np.float64)
    col = (t % GRID_W).astype(np.float64)

    def axis_tables(width, lane0, period, fill):
        p = width // 2
        inv = 1.0 / (ROPE_BASE ** (np.arange(p, dtype=np.float64) / p))
        cos = np.full((n_tokens, LANES), fill)
        sin = np.zeros((n_tokens, LANES))
        starts = range(lane0, LANES, period) if period else (lane0,)
        for s0 in starts:
            for base, pos in ((s0, row), (s0 + width, col)):
                ang = pos[:, None] * inv[None, :]
                c, sn = np.cos(ang), np.sin(ang)
                cos[:, base:base + p] = c
                cos[:, base + p:base + 2 * p] = c
                sin[:, base:base + p] = -sn
                sin[:, base + p:base + 2 * p] = sn
        return cos, sin

    qk_cos, qk_sin = axis_tables(D_ROPE // 2, D_NOPE, 0, 1.0)
    ret_cos, ret_sin = axis_tables(D_RET // 2, 0, D_RET, 1.0)
    as_f32 = lambda *a: tuple(jnp.asarray(x, F32) for x in a)
    return as_f32(qk_cos, qk_sin), as_f32(ret_cos, ret_sin)


def _head_sums(x):
    r = lax.broadcasted_iota(jnp.int32, (2 * HEAD_SLOT, 2 * HEAD_SLOT), 0) // HEAD_SLOT
    c = lax.broadcasted_iota(jnp.int32, (2 * HEAD_SLOT, 2 * HEAD_SLOT), 1) // HEAD_SLOT
    ones_bd = jnp.where(r == c, 1.0, 0.0).astype(BF16)
    xb = x.astype(BF16)
    w = 2 * HEAD_SLOT
    return jnp.concatenate([_dot(xb[:, g * w:(g + 1) * w], ones_bd) for g in range(x.shape[1] // w)], axis=1)


def _tile_heads(v):
    return jnp.concatenate([v] * H_MLA, axis=1)


def _kv_heads(ckv_n, krp, krp_sw, wkv_ref, kgain, kgain_sw, tabs, k_ref, v_ref):
    kvp = _dot(ckv_n.astype(BF16), wkv_ref[...])
    kn = kvp[:, :H_MLA * HEAD_SLOT]
    ms = _head_sums(kn * kn + _tile_heads(krp * krp)) * (1.0 / D_QK)
    r = lax.rsqrt(ms + EPS)
    k = (kn + _tile_heads(krp)) * r * _tile_heads(kgain)
    if tabs is not None:
        cos, sin = tabs
        k = k * _tile_heads(cos) + r * _tile_heads(krp_sw * kgain_sw * sin)
    k_ref[...] = k.astype(BF16)
    v_lane = lax.broadcasted_iota(jnp.int32, (1, H_MLA * HEAD_SLOT), 1) % HEAD_SLOT
    v_ref[...] = (kvp[:, H_MLA * HEAD_SLOT:] + jnp.where(v_lane == D_V, 1.0, 0.0)).astype(BF16)


def _inproj_kernel(rope, *refs):
    (x_ref, mod_ref, n1g_ref, w1_ref, qng_ref, wuq_ref, kvg_ref, wkv_ref, qgain_ref,
     kgain_ref) = refs[:10]
    refs = refs[10:]
    if rope:
        qk_tabs = tuple(r[...] for r in refs[:2])
        ret_cos, ret_sin = (r[...] for r in refs[2:4])
        refs = refs[4:]
    else:
        qk_tabs = None
    ckvn_ref, krp_ref, q_ref, k_ref, v_ref, rq_ref, rk_ref, rv_ref, rg_ref = refs

    mod = mod_ref[0]
    h = _rms(x_ref[...], n1g_ref[...]) * (1.0 + mod[1:2]) + mod[0:1]
    z = _dot(h.astype(BF16), w1_ref[...])
    o = 0
    cq = z[:, o:o + Q_RANK]; o += Q_RANK
    ckv = z[:, o:o + KV_RANK]; o += KV_RANK
    krp = z[:, o:o + LANES]; o += LANES
    rq = z[:, o:o + D_RET_ALL]; o += D_RET_ALL
    rk = z[:, o:o + D_RET_ALL]; o += D_RET_ALL
    rv = z[:, o:o + D_RET_ALL]; o += D_RET_ALL
    rg = z[:, o:o + D_RET_ALL]; o += D_RET_ALL
    krp_sw = None
    if rope:
        krp_sw = z[:, o:o + LANES]; o += LANES
        rq_sw = z[:, o:o + D_RET_ALL]; o += D_RET_ALL
        rk_sw = z[:, o:o + D_RET_ALL]

    ckv_n = _rms(ckv, kvg_ref[...])
    ckvn_ref[...] = ckv_n
    krp_ref[...] = krp

    qn = _rms(cq, qng_ref[...]).astype(BF16)
    scale = D_QK ** -0.5 * LOG2_E
    q = _dot(qn, wuq_ref[:, :H_MLA * HEAD_SLOT])
    r = lax.rsqrt(_head_sums(q * q) * (1.0 / D_QK) + EPS)
    q = q * r * _tile_heads(qgain_ref[0:1] * scale)
    if rope:
        cos, sin = qk_tabs
        q_sw = _dot(qn, wuq_ref[:, H_MLA * HEAD_SLOT:])
        q = q * _tile_heads(cos) + q_sw * r * _tile_heads(qgain_ref[1:2] * scale * sin)
    q_ref[...] = q.astype(BF16)

    _kv_heads(ckv_n, krp, krp_sw, wkv_ref, kgain_ref[0:1], kgain_ref[1:2], qk_tabs, k_ref, v_ref)

    for t in range(D_RET_ALL // LANES):
        sl = slice(t * LANES, (t + 1) * LANES)
        rq_t, rk_t = rq[:, sl], rk[:, sl]
        if rope:
            rq_t = rq_t * ret_cos + rq_sw[:, sl] * ret_sin
            rk_t = rk_t * ret_cos + rk_sw[:, sl] * ret_sin
        rq_ref[:, sl] = rq_t.astype(BF16)
        rk_ref[:, sl] = (rk_t * (D_RET ** -0.5)).astype(BF16)
    rv_ref[...] = rv.astype(BF16)
    rg_ref[...] = _silu(rg).astype(BF16)


def _const_spec(shape):
    return pl.BlockSpec(shape, lambda i: (0,) * len(shape))


def _inproj(x, mod, mod_row0, tokens_per_row, wts, rope_tabs, tm):
    T = x.shape[0]
    bpr = tokens_per_row // tm
    rope = rope_tabs is not None
    tok = lambda w: pl.BlockSpec((tm, w), lambda i: (i, 0))
    in_specs = [tok(D_MODEL),
                pl.BlockSpec((1, 6, D_MODEL), lambda i: (mod_row0 + i // bpr, 0, 0)),
                _const_spec((1, D_MODEL)),
                _const_spec((D_MODEL, D_IN_PAD + (LANES + 2 * D_RET_ALL if rope else 0))),
                _const_spec((1, Q_RANK)),
                _const_spec((Q_RANK, 2 * H_MLA * HEAD_SLOT)),
                _const_spec((1, KV_RANK)),
                _const_spec((KV_RANK, 2 * H_MLA * HEAD_SLOT)),
                _const_spec((2, LANES)),
                _const_spec((2, LANES))]
    args = [x, mod, wts["norm1_g"], wts["w1_rope" if rope else "w1"], wts["q_norm_g"], wts["w_uq"],
            wts["kv_norm_g"], wts["w_kv"], wts["q_gain"], wts["k_gain"]]
    if rope:
        qk_tabs, ret_tabs = rope_tabs
        nb = qk_tabs[0].shape[0] // tm
        in_specs += [pl.BlockSpec((tm, LANES), lambda i: (i % nb, 0))] * 4
        args += list(qk_tabs) + list(ret_tabs)
    widths = [(KV_RANK, F32), (LANES, F32), (H_MLA * HEAD_SLOT, BF16), (H_MLA * HEAD_SLOT, BF16),
              (H_MLA * HEAD_SLOT, BF16)] + [(D_RET_ALL, BF16)] * 4
    return pl.pallas_call(
        functools.partial(_inproj_kernel, rope),
        grid=(T // tm,),
        in_specs=in_specs,
        out_specs=[tok(w) for w, _ in widths],
        out_shape=[jax.ShapeDtypeStruct((T, w), dt) for w, dt in widths],
        compiler_params=_cparams("parallel"),
        name="inproj_rope" if rope else "inproj",
    )(*args)


def _ctx_kv_kernel(ckv_ref, krp_ref, wkv_ref, kgain_ref, k_ref, v_ref):
    _kv_heads(ckv_ref[...], krp_ref[...], None, wkv_ref, kgain_ref[0:1], None, None, k_ref, v_ref)


def _ctx_kv(ckv, krp, wts, tm):
    T = ckv.shape[0]
    tok = lambda w: pl.BlockSpec((tm, w), lambda i: (i, 0))
    return pl.pallas_call(
        _ctx_kv_kernel,
        grid=(T // tm,),
        in_specs=[tok(KV_RANK), tok(LANES),
                  _const_spec((KV_RANK, 2 * H_MLA * HEAD_SLOT)), _const_spec((2, LANES))],
        out_specs=[tok(H_MLA * HEAD_SLOT), tok(H_MLA * HEAD_SLOT)],
        out_shape=[jax.ShapeDtypeStruct((T, H_MLA * HEAD_SLOT), BF16),
                   jax.ShapeDtypeStruct((T, H_MLA * HEAD_SLOT), BF16)],
        compiler_params=_cparams("parallel"),
        name="ctx_kv",
    )(ckv, krp, wts["w_kv"], wts["k_gain"])


ATTN_KEY_CHUNK = 2048
ATTN_HEADS_PER_STEP = 4


def _attn_kernel(has_ctx, *refs):
    if has_ctx:
        q_ref, k_ref, v_ref, kc_ref, vc_ref, o_ref = refs
    else:
        q_ref, k_ref, v_ref, o_ref = refs
    sources = [(k_ref, v_ref)] + ([(kc_ref, vc_ref)] if has_ctx else [])
    chunks = []
    for kr, vr in sources:
        n = kr.shape[1]
        kc = _pick(n, (ATTN_KEY_CHUNK, 256, 128))
        chunks += [(kr, vr, slice(c * kc, (c + 1) * kc)) for c in range(n // kc)]
    tq = q_ref.shape[1]
    outs = []
    for j in range(ATTN_HEADS_PER_STEP):
        hs = slice(j * HEAD_SLOT, (j + 1) * HEAD_SLOT)
        q = q_ref[0, :, hs]
        m = jnp.full((tq, 1), -jnp.inf, F32)
        acc = jnp.zeros((tq, HEAD_SLOT), F32)
        for kr, vr, rows in chunks:
            s = _dot_nt(q, kr[0, rows, hs])
            m_new = jnp.maximum(m, jnp.max(s, axis=-1, keepdims=True))
            p = jnp.exp2(s - m_new).astype(BF16)
            acc = acc * jnp.exp2(m - m_new) + _dot(p, vr[0, rows, hs])
            m = m_new
        outs.append(acc[:, :D_V] / acc[:, D_V:D_V + 1])
    o_ref[0] = jnp.concatenate(outs, axis=-1).astype(BF16)


def _attention(q, k, v, kc, vc, tq):
    B, L, _ = q.shape
    has_ctx = kc is not None
    hps = ATTN_HEADS_PER_STEP
    in_specs = [pl.BlockSpec((1, tq, hps * HEAD_SLOT), lambda b, hp, i: (b, i, hp)),
                pl.BlockSpec((1, L, hps * HEAD_SLOT), lambda b, hp, i: (b, 0, hp)),
                pl.BlockSpec((1, L, hps * HEAD_SLOT), lambda b, hp, i: (b, 0, hp))]
    args = [q, k, v]
    if has_ctx:
        Lc = kc.shape[1]
        in_specs += [pl.BlockSpec((1, Lc, hps * HEAD_SLOT), lambda b, hp, i: (b, 0, hp)),
                     pl.BlockSpec((1, Lc, hps * HEAD_SLOT), lambda b, hp, i: (b, 0, hp))]
        args += [kc, vc]
    return pl.pallas_call(
        functools.partial(_attn_kernel, has_ctx),
        grid=(B, H_MLA // hps, L // tq),
        in_specs=in_specs,
        out_specs=pl.BlockSpec((1, tq, hps * D_V), lambda b, hp, i: (b, i, hp)),
        out_shape=jax.ShapeDtypeStruct((B, L, H_MLA * D_V), BF16),
        compiler_params=_cparams("parallel", "parallel", "arbitrary"),
        name="attn_ctx" if has_ctx else "attn",
    )(*args)


RET_UNROLL = 8
RET_SHORT_SEQ = 1024


def _ret_pair(has_s0, want_state, n_chunks, pair, hp, *refs):
    rq_ref, rk_ref, rv_ref, rg_ref, decf_ref, decb_ref, g_ref = refs[:7]
    refs = refs[7:]
    if has_s0:
        s0f_ref, s0b_ref = refs[:2]
        refs = refs[2:]
    o_ref = refs[0]
    refs = refs[1:]
    if want_state:
        sf_ref, sb_ref = refs[:2]
        refs = refs[2:]
    kv_ref, sp_ref = refs

    C = RET_CHUNK
    W = 2 * D_RET
    ls = slice(pair * W, (pair + 1) * W)
    lane = lax.broadcasted_iota(jnp.int32, (1, W), 1)
    lane_h1 = lane >= D_RET
    row_h1 = lax.broadcasted_iota(jnp.int32, (W, 1), 0) >= D_RET
    blockdiag = row_h1 == lane_h1
    pos = lax.broadcasted_iota(jnp.int32, (C, 1), 0).astype(F32)
    diff = (lax.broadcasted_iota(jnp.int32, (C, C), 0)
            - lax.broadcasted_iota(jnp.int32, (C, C), 1)).astype(F32)

    lg = {}
    for name, ref in (("f", decf_ref), ("b", decb_ref)):
        a = -jnp.exp(ref[pl.ds(2 * hp, 1), :])
        b = -jnp.exp(ref[pl.ds(2 * hp + 1, 1), :])
        lg[name] = (a[:, :1], b[:, :1], jnp.where(lane_h1, b, a))
    lgf0, lgf1, lgf = lg["f"]
    lgb0, lgb1, lgb = lg["b"]
    qdf, kdf = jnp.exp(lgf * (pos + 1.0)), jnp.exp(lgf * (C - 1.0 - pos))
    qdb, kdb = jnp.exp(lgb * (C - pos)), jnp.exp(lgb * pos)
    row_h1_full = lax.broadcasted_iota(jnp.int32, (W, W), 0) >= D_RET
    cdf = jnp.where(row_h1_full, jnp.exp(lgf1 * C), jnp.exp(lgf0 * C))
    cdb = jnp.where(row_h1_full, jnp.exp(lgb1 * C), jnp.exp(lgb0 * C))

    def intra_decay(lf, lb):
        return jnp.where(diff >= 0, jnp.exp(lf * jnp.maximum(diff, 0.0)), jnp.exp(lb * jnp.maximum(-diff, 0.0)))

    dmat0, dmat1 = intra_decay(lgf0, lgb0), intra_decay(lgf1, lgb1)
    g = g_ref[...]

    def chunk_rows(n):
        return pl.ds(pl.multiple_of(n * C, C), C)

    def phase_a(n, carry):
        rows = chunk_rows(n)
        k = rk_ref[0, rows, ls].astype(F32)
        k2 = jnp.concatenate([(k * kdf).astype(BF16), (k * kdb).astype(BF16)], axis=1)
        kv_ref[n] = _dot_tn(k2, rv_ref[0, rows, ls])
        return carry

    unroll = min(RET_UNROLL, n_chunks)
    lax.fori_loop(0, n_chunks, phase_a, 0, unroll=unroll)

    def init_state(ref):
        if not has_s0:
            return jnp.zeros((W, W), F32)
        z = jnp.zeros((D_RET, D_RET), F32)
        return jnp.concatenate([jnp.concatenate([ref[0, 2 * pair], z], axis=1),
                                jnp.concatenate([z, ref[0, 2 * pair + 1]], axis=1)], axis=0)

    def scan_f(n, S):
        sp_ref[n, 0:W, :] = S.astype(BF16)
        return cdf * S + jnp.where(blockdiag, kv_ref[n, 0:W, :], 0.0)

    def scan_b(i, S):
        n = n_chunks - 1 - i
        sp_ref[n, W:2 * W, :] = S.astype(BF16)
        return cdb * S + jnp.where(blockdiag, kv_ref[n, W:2 * W, :], 0.0)

    Sf = lax.fori_loop(0, n_chunks, scan_f, init_state(s0f_ref if has_s0 else None))
    Sb = lax.fori_loop(0, n_chunks, scan_b, init_state(s0b_ref if has_s0 else None))
    if want_state:
        for S, ref in ((Sf, sf_ref), (Sb, sb_ref)):
            ref[0, 2 * pair] = S[0:D_RET, 0:D_RET]
            ref[0, 2 * pair + 1] = S[D_RET:W, D_RET:W]

    def phase_c(n, carry):
        rows = chunk_rows(n)
        q = rq_ref[0, rows, ls]
        k = rk_ref[0, rows, ls]
        v = rv_ref[0, rows, ls]
        zero = jnp.zeros_like(q)
        q_heads = jnp.concatenate([jnp.where(lane_h1, zero, q), jnp.where(lane_h1, q, zero)], axis=0)
        sc = _dot_nt(q_heads, k)
        p = jnp.concatenate([(sc[:C] * dmat0).astype(BF16), (sc[C:] * dmat1).astype(BF16)], axis=1)
        v_bd = jnp.concatenate([jnp.where(lane_h1, zero, v), jnp.where(lane_h1, v, zero)], axis=0)
        qf = q.astype(F32)
        q2 = jnp.concatenate([(qf * qdf).astype(BF16), (qf * qdb).astype(BF16)], axis=1)
        o = _dot(p, v_bd) + _dot(q2, sp_ref[n])
        o2 = o * o
        ss0 = jnp.sum(jnp.where(lane_h1, 0.0, o2), axis=-1, keepdims=True)
        ss1 = jnp.sum(jnp.where(lane_h1, o2, 0.0), axis=-1, keepdims=True)
        ms = jnp.where(lane_h1, ss1, ss0) * (1.0 / D_RET)
        o = o * lax.rsqrt(ms + EPS) * g * rg_ref[0, rows, ls].astype(F32)
        o_ref[0, rows, ls] = o.astype(BF16)
        return carry

    lax.fori_loop(0, n_chunks, phase_c, 0, unroll=unroll)


def _ret_kernel(has_s0, want_state, n_chunks, pairs, *refs):
    for pair in range(pairs):
        _ret_pair(has_s0, want_state, n_chunks, pair, pl.program_id(1) * pairs + pair, *refs)


def _retention(rq, rk, rv, rg, dec_f, dec_b, g, s0f, s0b, want_state):
    B, L, _ = rq.shape
    has_s0 = s0f is not None
    pairs = H_RET // 2 if L <= RET_SHORT_SEQ else 1
    seq = pl.BlockSpec((1, L, pairs * LANES), lambda b, hp: (b, 0, hp))
    st = pl.BlockSpec((1, 2 * pairs, D_RET, D_RET), lambda b, hp: (b, hp, 0, 0))
    cst = lambda shape: pl.BlockSpec(shape, lambda b, hp: (0,) * len(shape))
    in_specs = [seq] * 4 + [cst((H_RET, LANES)), cst((H_RET, LANES)), cst((1, LANES))]
    args = [rq, rk, rv, rg, dec_f, dec_b, g]
    if has_s0:
        in_specs += [st, st]
        args += [s0f, s0b]
    out_specs = [seq]
    out_shape = [jax.ShapeDtypeStruct((B, L, D_RET_ALL), BF16)]
    if want_state:
        out_specs += [st, st]
        out_shape += [jax.ShapeDtypeStruct((B, H_RET, D_RET, D_RET), F32)] * 2
    return pl.pallas_call(
        functools.partial(_ret_kernel, has_s0, want_state, L // RET_CHUNK, pairs),
        grid=(B, H_RET // 2 // pairs),
        in_specs=in_specs,
        out_specs=out_specs,
        out_shape=out_shape,
        scratch_shapes=[pltpu.VMEM((L // RET_CHUNK, 4 * D_RET, LANES), F32),
                        pltpu.VMEM((L // RET_CHUNK, 4 * D_RET, LANES), BF16)],
        compiler_params=_cparams("parallel", "parallel"),
        name="retention_s0" if has_s0 else "retention",
    )(*args)


def _pack_pairs(x):
    w = x.shape[1] // 2
    hi = lax.bitcast_convert_type(x[:, :w].astype(BF16).astype(F32), jnp.uint32)
    lo = lax.bitcast_convert_type(x[:, w:].astype(BF16).astype(F32), jnp.uint32)
    return hi | (lo >> 16)


def _unpack_pairs(p):
    hi = lax.bitcast_convert_type(p & jnp.uint32(0xFFFF0000), F32)
    lo = lax.bitcast_convert_type(p << 16, F32)
    return jnp.concatenate([hi, lo], axis=1)


D_PACKED = D_MODEL // 2


def _outproj_kernel(omla_ref, oret_ref, x_ref, mod_ref, wo_ref, n2g_ref, rwh_ref, rwl_ref, rb_ref, tri_ref,
                    x2_ref, h2p_ref, code_ref, rank_ref, wk_ref, cnt_ref):
    @pl.when(pl.program_id(0) == 0)
    def _():
        cnt_ref[...] = jnp.zeros_like(cnt_ref)

    mod = mod_ref[0]
    half = H_MLA * D_V
    a = _dot(omla_ref[...], wo_ref[:half, :]) + _dot(oret_ref[...], wo_ref[half:, :])
    x2 = x_ref[...] + mod[2:3] * a
    x2_ref[...] = x2
    h2 = _rms(x2, n2g_ref[...]) * (1.0 + mod[4:5]) + mod[3:4]
    h2p_ref[...] = _pack_pairs(h2)
    h2_hi = h2.astype(BF16)
    h2_lo = (h2 - h2_hi.astype(F32)).astype(BF16)
    rw_hi = rwh_ref[...]
    logits = _dot_nt(rw_hi, h2_hi) + _dot_nt(rw_hi, h2_lo) + _dot_nt(rwl_ref[...], h2_hi)
    scores = jax.nn.sigmoid(logits)
    tm = scores.shape[1]
    expert = lax.broadcasted_iota(jnp.int32, scores.shape, 0)
    cand = scores + rb_ref[...]
    code = jnp.zeros(scores.shape, jnp.int32)
    for k in range(TOP_K):
        mx = jnp.max(cand, axis=0, keepdims=True)
        first = jnp.min(jnp.where(cand == mx, expert, LANES), axis=0, keepdims=True)
        hit = expert == first
        code = jnp.where(hit, k + 1, code)
        cand = jnp.where(hit, -jnp.inf, cand)
    picked = code > 0
    sel = jnp.where(picked, scores, 0.0)
    gates = sel * (ROUTED_SCALE / jnp.sum(sel, axis=0, keepdims=True))
    code_ref[...] = code

    onehot = jnp.where(picked, 1.0, 0.0)
    carry = cnt_ref[...].astype(F32)
    rank = _dot(onehot.astype(BF16), tri_ref[...])
    rank_ref[...] = rank + jnp.concatenate([carry] * (tm // LANES), axis=1)
    cnt_ref[...] = (carry + jnp.sum(onehot, axis=1, keepdims=True)).astype(jnp.int32)

    row = lax.broadcasted_iota(jnp.int32, scores.shape, 0)
    wk_t = jnp.zeros(scores.shape, F32)
    for k in range(TOP_K):
        w = jnp.sum(jnp.where(code == k + 1, gates, 0.0), axis=0, keepdims=True)
        wk_t = jnp.where(row == k, w, wk_t)
    wk_ref[...] = wk_t.T


def _outproj(o_mla, o_ret, x, mod, mod_row0, tokens_per_row, wts, tm):
    T = x.shape[0]
    bpr = tokens_per_row // tm
    tok = lambda w: pl.BlockSpec((tm, w), lambda i: (i, 0))
    emaj = pl.BlockSpec((LANES, tm), lambda i: (0, i))
    tri = jnp.triu(jnp.ones((tm, tm), BF16), 1)
    bias = jnp.broadcast_to(wts["router_bias"].reshape(LANES, 1), (LANES, tm))
    return pl.pallas_call(
        _outproj_kernel,
        grid=(T // tm,),
        in_specs=[tok(H_MLA * D_V), tok(D_RET_ALL), tok(D_MODEL),
                  pl.BlockSpec((1, 6, D_MODEL), lambda i: (mod_row0 + i // bpr, 0, 0)),
                  _const_spec((D_MODEL, D_MODEL)), _const_spec((1, D_MODEL)),
                  _const_spec((LANES, D_MODEL)), _const_spec((LANES, D_MODEL)), _const_spec((LANES, tm)),
                  _const_spec((tm, tm))],
        out_specs=[tok(D_MODEL), tok(D_PACKED), emaj, emaj, tok(LANES), _const_spec((LANES, LANES))],
        out_shape=[jax.ShapeDtypeStruct((T, D_MODEL), F32),
                   jax.ShapeDtypeStruct((T, D_PACKED), jnp.uint32),
                   jax.ShapeDtypeStruct((LANES, T), jnp.int32),
                   jax.ShapeDtypeStruct((LANES, T), F32),
                   jax.ShapeDtypeStruct((T, LANES), F32),
                   jax.ShapeDtypeStruct((LANES, LANES), jnp.int32)],
        compiler_params=_cparams("arbitrary"),
        name="outproj_router",
    )(o_mla, o_ret, x, mod, wts["w_o"], wts["norm2_g"], wts["router_hi"], wts["router_lo"], bias, tri)


def _plan_kernel(code_ref, rank_ref, first_ref, pos_ref):
    code = code_ref[...]
    tm = code.shape[1]
    row = rank_ref[...] + jnp.concatenate([first_ref[...]] * (tm // LANES), axis=1)
    krow = lax.broadcasted_iota(jnp.int32, pos_ref.shape, 0)
    pos = jnp.zeros(pos_ref.shape, F32)
    for k in range(TOP_K):
        pos = jnp.where(krow == k, jnp.sum(jnp.where(code == k + 1, row, 0.0), axis=0, keepdims=True), pos)
    pos_ref[...] = pos.astype(jnp.int32)


def _plan(code, rank, first_rows, tm):
    T = code.shape[1]
    first = jnp.broadcast_to(jnp.pad(first_rows.astype(F32), (0, LANES - N_EXPERTS)).reshape(LANES, 1),
                             (LANES, LANES))
    emaj = pl.BlockSpec((LANES, tm), lambda i: (0, i))
    return pl.pallas_call(
        _plan_kernel,
        grid=(T // tm,),
        in_specs=[emaj, emaj, _const_spec((LANES, LANES))],
        out_specs=pl.BlockSpec((8, tm), lambda i: (0, i)),
        out_shape=jax.ShapeDtypeStruct((8, T), jnp.int32),
        compiler_params=_cparams("parallel"),
        name="moe_plan",
    )(code, rank, first)


SC_WORKERS = 32
SC_CHUNK = 64


def _sc_worker_id():
    return lax.axis_index("s") * 2 + lax.axis_index("c")


def _sc_mesh():
    return plsc.VectorSubcoreMesh(core_axis_name="c", subcore_axis_name="s")


def _sc_gather_rows(table, idx):
    N = idx.shape[0]
    W = table.shape[1]
    per_w = N // SC_WORKERS
    n_chunks = per_w // SC_CHUNK
    assert N == SC_WORKERS * n_chunks * SC_CHUNK and n_chunks % 2 == 0

    @functools.partial(
        pl.kernel, mesh=_sc_mesh(), out_type=jax.ShapeDtypeStruct((N, W), table.dtype),
        scratch_types=[pltpu.VMEM((n_chunks, SC_CHUNK), jnp.int32),
                       pltpu.VMEM((2, SC_CHUNK, W), table.dtype),
                       pltpu.SemaphoreType.DMA((2,)), pltpu.SemaphoreType.DMA((2,))],
        name="sc_gather_rows")
    def k(table_hbm, idx_hbm, out_hbm, idx_v, rows_v, gsem, wsem):
        wid = _sc_worker_id()
        base = wid * per_w
        pltpu.sync_copy(idx_hbm.at[wid], idx_v)

        def gather(j, b):
            return pltpu.make_async_copy(table_hbm.at[idx_v.at[j]], rows_v.at[b], gsem.at[b])

        def writeback(j, b):
            off = pl.multiple_of(base + j * SC_CHUNK, SC_CHUNK)
            return pltpu.make_async_copy(rows_v.at[b], out_hbm.at[pl.ds(off, SC_CHUNK)], wsem.at[b])

        gather(0, 0).start()

        @pl.loop(0, n_chunks, step=2)
        def _(j):
            for b in range(2):
                jj = j + b

                @pl.when(jj + 1 < n_chunks)
                def _():
                    @pl.when(jj >= 1)
                    def _():
                        writeback(jj - 1, 1 - b).wait()
                    gather(jj + 1, 1 - b).start()

                gather(jj, b).wait()
                writeback(jj, b).start()

        writeback(n_chunks - 2, 0).wait()
        writeback(n_chunks - 1, 1).wait()

    return k(table, idx.reshape(SC_WORKERS, n_chunks, SC_CHUNK))


def _sc_dispatch_rows(rows_list, pos_list, n_out):
    W = rows_list[0].shape[1]
    K = pos_list[0].shape[0]
    n_src = len(rows_list)
    plan = []
    idx_list = []
    for rows, pos in zip(rows_list, pos_list):
        T = rows.shape[0]
        per_w = T // SC_WORKERS
        n_chunks = per_w // SC_CHUNK
        assert T == SC_WORKERS * n_chunks * SC_CHUNK and n_chunks % 2 == 0
        plan.append((per_w, n_chunks))
        idx_list.append(pos.reshape(K, SC_WORKERS, n_chunks, SC_CHUNK).transpose(1, 2, 0, 3))
    max_chunks = max(n for _, n in plan)

    @functools.partial(
        pl.kernel, mesh=_sc_mesh(), out_type=jax.ShapeDtypeStruct((n_out, W), rows_list[0].dtype),
        scratch_types=[pltpu.VMEM((max_chunks, K, SC_CHUNK), jnp.int32),
                       pltpu.VMEM((2, SC_CHUNK, W), rows_list[0].dtype),
                       pltpu.SemaphoreType.DMA((2,)), pltpu.SemaphoreType.DMA((2,))],
        name="sc_dispatch_rows")
    def k(*refs):
        rows_refs, idx_refs = refs[:n_src], refs[n_src:2 * n_src]
        out_hbm, idx_v, rows_v, lsem, ssem = refs[2 * n_src:]
        wid = _sc_worker_id()
        for rows_hbm, idx_hbm, (per_w, n_chunks) in zip(rows_refs, idx_refs, plan):
            base = wid * per_w
            pltpu.sync_copy(idx_hbm.at[wid], idx_v.at[pl.ds(0, n_chunks)])

            def load(j, b):
                off = pl.multiple_of(base + j * SC_CHUNK, SC_CHUNK)
                return pltpu.make_async_copy(rows_hbm.at[pl.ds(off, SC_CHUNK)], rows_v.at[b], lsem.at[b])

            def scatter(j, b, kk):
                return pltpu.make_async_copy(rows_v.at[b], out_hbm.at[idx_v.at[j, kk]], ssem.at[b])

            load(0, 0).start()

            @pl.loop(0, n_chunks, step=2)
            def _(j):
                for b in range(2):
                    jj = j + b

                    @pl.when(jj + 1 < n_chunks)
                    def _():
                        @pl.when(jj >= 1)
                        def _():
                            for kk in range(K):
                                scatter(jj - 1, 1 - b, kk).wait()
                        load(jj + 1, 1 - b).start()

                    load(jj, b).wait()
                    for kk in range(K):
                        scatter(jj, b, kk).start()

            for kk in range(K):
                scatter(n_chunks - 2, 0, kk).wait()
            for kk in range(K):
                scatter(n_chunks - 1, 1, kk).wait()

    return k(*rows_list, *idx_list)


ROW_TILE = 1024
MOE_GATHER_GROUP = 8192


def _experts_kernel(te_ref, tv_ref, tb_ref, xs_ref, wg_ref, wu_ref, wd_ref, ys_ref, wg_s, wu_s, wd_s):
    del tb_ref
    i = pl.program_id(0)
    valid = tv_ref[i]

    @pl.when(jnp.logical_or(i == 0, te_ref[i] != te_ref[jnp.maximum(i - 1, 0)]))
    def _():
        wg_s[...] = wg_ref[0].astype(BF16)
        wu_s[...] = wu_ref[0].astype(BF16)
        wd_s[...] = wd_ref[0].astype(BF16)

    @pl.when(valid > 0)
    def _():
        row = lax.broadcasted_iota(jnp.int32, (ROW_TILE, 1), 0)
        p = jnp.where(row < valid, xs_ref[...], jnp.uint32(0))
        x = _unpack_pairs(p).astype(BF16)
        act = _silu(_dot(x, wg_s[...])) * _dot(x, wu_s[...])
        ys_ref[...] = _pack_pairs(_dot(act.astype(BF16), wd_s[...]))


def _experts(xs, tile_expert, tile_valid, tile_block, wts):
    n_tiles = tile_expert.shape[0]
    ex = lambda a, b: pl.BlockSpec((1, a, b), lambda i, te, tv, tb: (te[i], 0, 0))
    row = pl.BlockSpec((ROW_TILE, D_PACKED), lambda i, te, tv, tb: (tb[i], 0))
    return pl.pallas_call(
        _experts_kernel,
        grid_spec=pltpu.PrefetchScalarGridSpec(
            num_scalar_prefetch=3, grid=(n_tiles,),
            in_specs=[row, ex(D_MODEL, D_EXPERT), ex(D_MODEL, D_EXPERT), ex(D_EXPERT, D_MODEL)],
            out_specs=row,
            scratch_shapes=[pltpu.VMEM((D_MODEL, D_EXPERT), BF16), pltpu.VMEM((D_MODEL, D_EXPERT), BF16),
                            pltpu.VMEM((D_EXPERT, D_MODEL), BF16)]),
        out_shape=jax.ShapeDtypeStruct(xs.shape, jnp.uint32),
        compiler_params=_cparams("arbitrary"),
        name="experts",
    )(tile_expert, tile_valid, tile_block, xs, wts["exp_gate"], wts["exp_up"], wts["exp_down"])


def _combine_kernel(h2p_ref, ys_ref, wk_ref, x2_ref, mod_ref, shg_ref, shu_ref, shd_ref, *rest):
    y_ref = rest[-1]
    t = _unpack_pairs(h2p_ref[...]).astype(BF16)
    act = _silu(_dot(t, shg_ref[...])) * _dot(t, shu_ref[...])
    acc = _dot(act.astype(BF16), shd_ref[...])
    wk = wk_ref[...]
    for k in range(TOP_K):
        acc = acc + wk[:, k:k + 1] * _unpack_pairs(ys_ref[k])
    y_ref[...] = x2_ref[...] + mod_ref[0][5:6] * acc


def _combine(h2p, ys, row0, wk, x2, y_prev, mod, mod_row0, tokens_per_row, wts, tm):
    T = h2p.shape[0]
    G = ys.shape[1]
    bpr = tokens_per_row // tm
    blk0 = row0 // tm
    tok = lambda w: pl.BlockSpec((tm, w), lambda i: (blk0 + i, 0))
    in_specs = [tok(D_PACKED), pl.BlockSpec((TOP_K, tm, D_PACKED), lambda i: (0, i, 0)), tok(LANES),
                tok(D_MODEL), pl.BlockSpec((1, 6, D_MODEL), lambda i: (mod_row0 + (blk0 + i) // bpr, 0, 0)),
                _const_spec((D_MODEL, D_SHARED)), _const_spec((D_MODEL, D_SHARED)),
                _const_spec((D_SHARED, D_MODEL))]
    args = [h2p, ys, wk, x2, mod, wts["sh_gate"], wts["sh_up"], wts["sh_down"]]
    aliases = {}
    if y_prev is not None:
        aliases = {len(args): 0}
        in_specs.append(pl.BlockSpec(memory_space=pl.ANY))
        args.append(y_prev)
    return pl.pallas_call(
        _combine_kernel,
        grid=(G // tm,),
        in_specs=in_specs,
        out_specs=tok(D_MODEL),
        out_shape=jax.ShapeDtypeStruct((T, D_MODEL), F32),
        input_output_aliases=aliases,
        compiler_params=_cparams("parallel"),
        name="moe_combine",
    )(*args)


def _moe(halves, mod, wts):
    routed = [h["routed"] for h in halves]
    sizes = [r[1].shape[0] for r in routed]
    n_tiles = (TOP_K * sum(sizes)) // ROW_TILE + N_EXPERTS
    path_counts = [r[5][:N_EXPERTS, 0] for r in routed]
    counts = sum(path_counts)
    padded = ((counts + ROW_TILE - 1) // ROW_TILE) * ROW_TILE
    ends = jnp.cumsum(padded)
    starts = ends - padded
    tile_start = jnp.arange(n_tiles, dtype=jnp.int32) * ROW_TILE
    tile_expert = jnp.minimum(jnp.sum(tile_start[:, None] >= ends[None, :], axis=1), N_EXPERTS - 1).astype(jnp.int32)
    overlap = (jnp.minimum(tile_start[:, None] + ROW_TILE, (starts + counts)[None, :])
               - jnp.maximum(tile_start[:, None], starts[None, :]))
    tile_valid = jnp.sum(jnp.maximum(overlap, 0), axis=1).astype(jnp.int32)

    pos, first = [], starts
    for h, r, c in zip(halves, routed, path_counts):
        pos.append(_plan(r[2], r[3], first, h["tm"])[:TOP_K])
        first = first + c
    xs = _sc_dispatch_rows([r[1] for r in routed], pos, n_tiles * ROW_TILE)
    tile_block = jnp.minimum(jnp.arange(n_tiles, dtype=jnp.int32), ends[-1] // ROW_TILE - 1).astype(jnp.int32)
    ys = _experts(xs, tile_expert, tile_valid, tile_block, wts)
    outs = []
    for h, r, p in zip(halves, routed, pos):
        T = r[1].shape[0]
        G = MOE_GATHER_GROUP if T % MOE_GATHER_GROUP == 0 else T
        y = None
        for row0 in range(0, T, G):
            ysel = _sc_gather_rows(ys, p[:, row0:row0 + G].reshape(-1)).reshape(TOP_K, G, D_PACKED)
            y = _combine(r[1], ysel, row0, r[4], r[0], y, mod, h["mod_row0"], h["tokens_per_row"], wts, h["tm"])
        outs.append(y.reshape(h["shape"]))
    return outs


def _pad_heads(w, d_used):
    k = w.shape[0]
    w = w.reshape(k, H_MLA, d_used)
    return jnp.pad(w, ((0, 0), (0, 0), (0, HEAD_SLOT - d_used))).reshape(k, H_MLA * HEAD_SLOT)


def _lane_pad(v):
    return jnp.pad(v, (0, LANES - v.shape[0])).reshape(1, LANES)


def _rot_partners(w):
    rot = w[..., w.shape[-1] - D_ROPE:]
    return jnp.concatenate([jnp.zeros_like(w[..., :w.shape[-1] - D_ROPE]),
                            _swap_halves(rot, D_ROPE // 4)], axis=-1)


def _prep_weights(l, w_in, norm1_g, mla_q_norm_g, w_uq, mla_kv_norm_g, w_ukv, mla_q_gain, mla_k_gain,
                  w_o, norm2_g, router_w, router_bias, exp_w_gate, exp_w_up, exp_w_down,
                  sh_w_gate, sh_w_up, sh_w_down):
    w = w_in[l]
    a, b, c = Q_RANK, Q_RANK + KV_RANK, Q_RANK + KV_RANK + D_ROPE
    kr_slot = lambda wk: jnp.pad(wk, ((0, 0), (D_NOPE, LANES - D_QK)))
    w1 = jnp.concatenate([w[:, :b], kr_slot(w[:, b:c]), w[:, c:]], axis=1).astype(BF16)
    w_rq, w_rk = w[:, c:c + D_RET_ALL], w[:, c + D_RET_ALL:c + 2 * D_RET_ALL]
    w1_rope = jnp.concatenate([w1, kr_slot(_rot_partners(w[:, b:c])).astype(BF16),
                               _swap_halves(w_rq, D_RET // 4).astype(BF16),
                               _swap_halves(w_rk, D_RET // 4).astype(BF16)], axis=1)
    uq = w_uq[l].reshape(Q_RANK, H_MLA, D_QK)
    pad_slots = lambda u: jnp.pad(u, ((0, 0), (0, 0), (0, HEAD_SLOT - D_QK))).reshape(Q_RANK, H_MLA * HEAD_SLOT)
    gains = lambda g: jnp.concatenate([_lane_pad(g), _lane_pad(_rot_partners(g))], axis=0)
    ukv = w_ukv[l].reshape(KV_RANK, H_MLA, D_NOPE + D_V)
    w_k = _pad_heads(ukv[:, :, :D_NOPE].reshape(KV_RANK, H_MLA * D_NOPE), D_NOPE)
    w_v = _pad_heads(ukv[:, :, D_NOPE:].reshape(KV_RANK, H_MLA * D_V), D_V)
    rw = jnp.pad(router_w[l].T, ((0, LANES - N_EXPERTS), (0, 0)))
    rw_hi = rw.astype(BF16)
    rw_lo = (rw - rw_hi.astype(F32)).astype(BF16)
    rb = jnp.concatenate([router_bias[l].astype(F32), jnp.full((LANES - N_EXPERTS,), -jnp.inf, F32)])
    return dict(
        norm1_g=norm1_g[l].reshape(1, D_MODEL), w1=w1, w1_rope=w1_rope,
        q_norm_g=mla_q_norm_g[l].reshape(1, Q_RANK),
        w_uq=jnp.concatenate([pad_slots(uq), pad_slots(_rot_partners(uq))], axis=1).astype(BF16),
        kv_norm_g=mla_kv_norm_g[l].reshape(1, KV_RANK),
        w_kv=jnp.concatenate([w_k, w_v], axis=1).astype(BF16),
        q_gain=gains(mla_q_gain[l]), k_gain=gains(mla_k_gain[l]),
        w_o=w_o[l].astype(BF16), norm2_g=norm2_g[l].reshape(1, D_MODEL),
        router_hi=rw_hi, router_lo=rw_lo, router_bias=rb.reshape(1, LANES),
        exp_gate=exp_w_gate[l], exp_up=exp_w_up[l], exp_down=exp_w_down[l],
        sh_gate=sh_w_gate[l].astype(BF16), sh_up=sh_w_up[l].astype(BF16),
        sh_down=sh_w_down[l].astype(BF16))


def _pick(n, prefs):
    for p in prefs:
        if n % p == 0:
            return p
    return n


def _mixer_half(x, mod, mod_row0, wts, ret_wts, ctx):
    B, L, D = x.shape
    T = B * L
    tokens_per_row = L if ctx is not None else T
    xt = x.reshape(T, D)
    rope_tabs = None
    if ctx is not None:
        rope_tabs = _rope_tables(L)
    tm = _pick(tokens_per_row, (512, 256, 128))
    ckv_n, krp, q, k, v, rq, rk, rv, rg = _inproj(xt, mod, mod_row0, tokens_per_row, wts, rope_tabs, tm)

    seq = lambda a: a.reshape(B, L, a.shape[-1])
    kc = vc = s0f = s0b = None
    if ctx is not None:
        ckv_c, kr_c, s0f, s0b = ctx
        Lc = ckv_c.shape[1]
        krp_c = jnp.pad(kr_c, ((0, 0), (0, 0), (D_NOPE, LANES - D_NOPE - D_ROPE)))
        kc, vc = _ctx_kv(ckv_c.reshape(B * Lc, KV_RANK), krp_c.reshape(B * Lc, LANES), wts,
                         _pick(B * Lc, (512, 256, 128)))
        kc, vc = kc.reshape(B, Lc, -1), vc.reshape(B, Lc, -1)
    o_mla = _attention(seq(q), seq(k), seq(v), kc, vc, _pick(L, (1024, 512, 256, 128)))

    dec_f, dec_b, ret_g = ret_wts
    want_state = ctx is None
    ret = _retention(seq(rq), seq(rk), seq(rv), seq(rg), dec_f, dec_b, ret_g, s0f, s0b, want_state)
    o_ret = ret[0]

    routed = _outproj(o_mla.reshape(T, -1), o_ret.reshape(T, -1), xt, mod, mod_row0, tokens_per_row, wts, tm)
    new = None
    if want_state:
        new = (ckv_n.reshape(B, L, KV_RANK), krp[:, D_NOPE:D_NOPE + D_ROPE].reshape(B, L, D_ROPE),
               ret[1], ret[2])
    return dict(routed=routed, mod_row0=mod_row0, tokens_per_row=tokens_per_row, tm=tm, shape=(B, L, D)), new


def kernel(x_prompt, x_sample, cache_mla_ckv, cache_mla_krope, state_ret_fwd, state_ret_bwd, c, c_ctx,
           w_ada, b_ada, norm1_g, w_in, mla_q_norm_g, w_uq, mla_kv_norm_g, w_ukv, mla_q_gain, mla_k_gain,
           ret_decay_fwd, ret_decay_bwd, ret_norm_g, w_o, norm2_g, router_w, router_bias, exp_w_gate,
           exp_w_up, exp_w_down, sh_w_gate, sh_w_up, sh_w_down):
    depth = w_ada.shape[0]
    n_dec = c.shape[0]
    assert 1 + n_dec <= MOD_ROWS
    cond = jnp.concatenate([c_ctx[None], c, jnp.zeros((MOD_ROWS - 1 - n_dec, D_MODEL), F32)], axis=0)

    y_prompt, y_sample = x_prompt, x_sample
    ckv_l, kr_l, sf_l, sb_l = [], [], [], []
    for l in range(depth):
        wts = _prep_weights(l, w_in, norm1_g, mla_q_norm_g, w_uq, mla_kv_norm_g, w_ukv, mla_q_gain,
                            mla_k_gain, w_o, norm2_g, router_w, router_bias, exp_w_gate, exp_w_up,
                            exp_w_down, sh_w_gate, sh_w_up, sh_w_down)
        ret_wts = (jnp.broadcast_to(ret_decay_fwd[l].astype(F32)[:, None], (H_RET, LANES)),
                   jnp.broadcast_to(ret_decay_bwd[l].astype(F32)[:, None], (H_RET, LANES)),
                   jnp.tile(ret_norm_g[l].reshape(1, D_RET), (1, LANES // D_RET)))
        mod = _adaln(cond, w_ada[l], b_ada[l]).reshape(MOD_ROWS, 6, D_MODEL)
        half_p, new = _mixer_half(y_prompt, mod, 0, wts, ret_wts, None)
        ckv_l.append(new[0]); kr_l.append(new[1]); sf_l.append(new[2]); sb_l.append(new[3])
        ctx = (cache_mla_ckv[:, l], cache_mla_krope[:, l], state_ret_fwd[:, l], state_ret_bwd[:, l])
        half_s, _ = _mixer_half(y_sample, mod, 1, wts, ret_wts, ctx)
        y_prompt, y_sample = _moe([half_p, half_s], mod, wts)

    return (y_prompt, y_sample, jnp.stack(ckv_l, axis=1), jnp.stack(kr_l, axis=1),
            jnp.stack(sf_l, axis=1), jnp.stack(sb_l, axis=1))
```

```python
import functools
import math

import numpy as np
import jax
import jax.numpy as jnp
from jax import lax
from jax.experimental import pallas as pl
from jax.experimental.pallas import tpu as pltpu
from jax.experimental.pallas import tpu_sc as plsc

F32 = jnp.float32
BF16 = jnp.bfloat16

D_MODEL = 1024
GRID_W = 64
H_MLA = 8
D_NOPE = 64
D_ROPE = 32
D_QK = D_NOPE + D_ROPE
D_V = 64
Q_RANK = 256
KV_RANK = 128
H_RET = 8
D_RET = 64
RET_CHUNK = 128
D_RET_ALL = H_RET * D_RET
N_EXPERTS = 64
TOP_K = 6
D_EXPERT = 256
D_SHARED = 256
ROUTED_SCALE = 2.5
ROPE_BASE = 10000.0
EPS = 1e-6
LOG2_E = math.log2(math.e)

LANES = 128
HEAD_SLOT = LANES
D_IN_PAD = Q_RANK + KV_RANK + LANES + 4 * D_RET_ALL
MOD_ROWS = 16
VMEM_LIMIT = 56 * 1024 * 1024


def _cparams(*sem):
    return pltpu.CompilerParams(dimension_semantics=sem, vmem_limit_bytes=VMEM_LIMIT)


def _dot(a, b):
    return jnp.dot(a, b, preferred_element_type=F32)


def _dot_nt(a, b):
    return lax.dot_general(a, b, (((1,), (1,)), ((), ())), preferred_element_type=F32)


def _dot_tn(a, b):
    return lax.dot_general(a, b, (((0,), (0,)), ((), ())), preferred_element_type=F32)


def _rms(x, g):
    return x * lax.rsqrt(jnp.mean(x * x, axis=-1, keepdims=True) + EPS) * g


def _silu(x):
    return x * jax.nn.sigmoid(x)


def _adaln_kernel(c_ref, w_ref, b_ref, o_ref):
    s = _silu(c_ref[...])
    o_ref[...] = _dot(s.astype(BF16), w_ref[...].astype(BF16)) + b_ref[...]


def _adaln(cond, w_ada, b_ada):
    n_out = w_ada.shape[1]
    bn = 1536
    return pl.pallas_call(
        _adaln_kernel,
        grid=(n_out // bn,),
        in_specs=[pl.BlockSpec((MOD_ROWS, D_MODEL), lambda j: (0, 0)),
                  pl.BlockSpec((D_MODEL, bn), lambda j: (0, j)),
                  pl.BlockSpec((1, bn), lambda j: (0, j))],
        out_specs=pl.BlockSpec((MOD_ROWS, bn), lambda j: (0, j)),
        out_shape=jax.ShapeDtypeStruct((MOD_ROWS, n_out), F32),
        compiler_params=_cparams("arbitrary"),
        name="adaln",
    )(cond, w_ada, b_ada.reshape(1, n_out))


def _swap_halves(w, half_pair):
    g = w.reshape(*w.shape[:-1], w.shape[-1] // (2 * half_pair), 2, half_pair)
    return g[..., ::-1, :].reshape(w.shape)


def _rope_tables(n_tokens):
    t = np.arange(n_tokens)
    row = (t // GRID_W).astype(np.float64)
    col = (t % GRID_W).astype(np.float64)

    def axis_tables(width, lane0, period, fill):
        p = width // 2
        inv = 1.0 / (ROPE_BASE ** (np.arange(p, dtype=np.float64) / p))
        cos = np.full((n_tokens, LANES), fill)
        sin = np.zeros((n_tokens, LANES))
        starts = range(lane0, LANES, period) if period else (lane0,)
        for s0 in starts:
            for base, pos in ((s0, row), (s0 + width, col)):
                ang = pos[:, None] * inv[None, :]
                c, sn = np.cos(ang), np.sin(ang)
                cos[:, base:base + p] = c
                cos[:, base + p:base + 2 * p] = c
                sin[:, base:base + p] = -sn
                sin[:, base + p:base + 2 * p] = sn
        return cos, sin

    qk_cos, qk_sin = axis_tables(D_ROPE // 2, D_NOPE, 0, 1.0)
    ret_cos, ret_sin = axis_tables(D_RET // 2, 0, D_RET, 1.0)
    as_f32 = lambda *a: tuple(jnp.asarray(x, F32) for x in a)
    return as_f32(qk_cos, qk_sin), as_f32(ret_cos, ret_sin)


def _head_sums(x):
    r = lax.broadcasted_iota(jnp.int32, (2 * HEAD_SLOT, 2 * HEAD_SLOT), 0) // HEAD_SLOT
    c = lax.broadcasted_iota(jnp.int32, (2 * HEAD_SLOT, 2 * HEAD_SLOT), 1) // HEAD_SLOT
    ones_bd = jnp.where(r == c, 1.0, 0.0).astype(BF16)
    xb = x.astype(BF16)
    w = 2 * HEAD_SLOT
    return jnp.concatenate([_dot(xb[:, g * w:(g + 1) * w], ones_bd) for g in range(x.shape[1] // w)], axis=1)


def _tile_heads(v):
    return jnp.concatenate([v] * H_MLA, axis=1)


def _kv_heads(ckv_n, krp, krp_sw, wkv_ref, kgain, kgain_sw, tabs, k_ref, v_ref):
    kvp = _dot(ckv_n.astype(BF16), wkv_ref[...])
    kn = kvp[:, :H_MLA * HEAD_SLOT]
    ms = _head_sums(kn * kn + _tile_heads(krp * krp)) * (1.0 / D_QK)
    r = lax.rsqrt(ms + EPS)
    k = (kn + _tile_heads(krp)) * r * _tile_heads(kgain)
    if tabs is not None:
        cos, sin = tabs
        k = k * _tile_heads(cos) + r * _tile_heads(krp_sw * kgain_sw * sin)
    k_ref[...] = k.astype(BF16)
    v_lane = lax.broadcasted_iota(jnp.int32, (1, H_MLA * HEAD_SLOT), 1) % HEAD_SLOT
    v_ref[...] = (kvp[:, H_MLA * HEAD_SLOT:] + jnp.where(v_lane == D_V, 1.0, 0.0)).astype(BF16)


def _inproj_kernel(rope, *refs):
    (x_ref, mod_ref, n1g_ref, w1_ref, qng_ref, wuq_ref, kvg_ref, wkv_ref, qgain_ref,
     kgain_ref) = refs[:10]
    refs = refs[10:]
    if rope:
        qk_tabs = tuple(r[...] for r in refs[:2])
        ret_cos, ret_sin = (r[...] for r in refs[2:4])
        refs = refs[4:]
    else:
        qk_tabs = None
    ckvn_ref, krp_ref, q_ref, k_ref, v_ref, rq_ref, rk_ref, rv_ref, rg_ref = refs

    mod = mod_ref[0]
    h = _rms(x_ref[...], n1g_ref[...]) * (1.0 + mod[1:2]) + mod[0:1]
    z = _dot(h.astype(BF16), w1_ref[...])
    o = 0
    cq = z[:, o:o + Q_RANK]; o += Q_RANK
    ckv = z[:, o:o + KV_RANK]; o += KV_RANK
    krp = z[:, o:o + LANES]; o += LANES
    rq = z[:, o:o + D_RET_ALL]; o += D_RET_ALL
    rk = z[:, o:o + D_RET_ALL]; o += D_RET_ALL
    rv = z[:, o:o + D_RET_ALL]; o += D_RET_ALL
    rg = z[:, o:o + D_RET_ALL]; o += D_RET_ALL
    krp_sw = None
    if rope:
        krp_sw = z[:, o:o + LANES]; o += LANES
        rq_sw = z[:, o:o + D_RET_ALL]; o += D_RET_ALL
        rk_sw = z[:, o:o + D_RET_ALL]

    ckv_n = _rms(ckv, kvg_ref[...])
    ckvn_ref[...] = ckv_n
    krp_ref[...] = krp

    qn = _rms(cq, qng_ref[...]).astype(BF16)
    scale = D_QK ** -0.5 * LOG2_E
    q = _dot(qn, wuq_ref[:, :H_MLA * HEAD_SLOT])
    r = lax.rsqrt(_head_sums(q * q) * (1.0 / D_QK) + EPS)
    q = q * r * _tile_heads(qgain_ref[0:1] * scale)
    if rope:
        cos, sin = qk_tabs
        q_sw = _dot(qn, wuq_ref[:, H_MLA * HEAD_SLOT:])
        q = q * _tile_heads(cos) + q_sw * r * _tile_heads(qgain_ref[1:2] * scale * sin)
    q_ref[...] = q.astype(BF16)

    _kv_heads(ckv_n, krp, krp_sw, wkv_ref, kgain_ref[0:1], kgain_ref[1:2], qk_tabs, k_ref, v_ref)

    for t in range(D_RET_ALL // LANES):
        sl = slice(t * LANES, (t + 1) * LANES)
        rq_t, rk_t = rq[:, sl], rk[:, sl]
        if rope:
            rq_t = rq_t * ret_cos + rq_sw[:, sl] * ret_sin
            rk_t = rk_t * ret_cos + rk_sw[:, sl] * ret_sin
        rq_ref[:, sl] = rq_t.astype(BF16)
        rk_ref[:, sl] = (rk_t * (D_RET ** -0.5)).astype(BF16)
    rv_ref[...] = rv.astype(BF16)
    rg_ref[...] = _silu(rg).astype(BF16)


def _const_spec(shape):
    return pl.BlockSpec(shape, lambda i: (0,) * len(shape))


def _inproj(x, mod, mod_row0, tokens_per_row, wts, rope_tabs, tm):
    T = x.shape[0]
    bpr = tokens_per_row // tm
    rope = rope_tabs is not None
    tok = lambda w: pl.BlockSpec((tm, w), lambda i: (i, 0))
    in_specs = [tok(D_MODEL),
                pl.BlockSpec((1, 6, D_MODEL), lambda i: (mod_row0 + i // bpr, 0, 0)),
                _const_spec((1, D_MODEL)),
                _const_spec((D_MODEL, D_IN_PAD + (LANES + 2 * D_RET_ALL if rope else 0))),
                _const_spec((1, Q_RANK)),
                _const_spec((Q_RANK, 2 * H_MLA * HEAD_SLOT)),
                _const_spec((1, KV_RANK)),
                _const_spec((KV_RANK, 2 * H_MLA * HEAD_SLOT)),
                _const_spec((2, LANES)),
                _const_spec((2, LANES))]
    args = [x, mod, wts["norm1_g"], wts["w1_rope" if rope else "w1"], wts["q_norm_g"], wts["w_uq"],
            wts["kv_norm_g"], wts["w_kv"], wts["q_gain"], wts["k_gain"]]
    if rope:
        qk_tabs, ret_tabs = rope_tabs
        nb = qk_tabs[0].shape[0] // tm
        in_specs += [pl.BlockSpec((tm, LANES), lambda i: (i % nb, 0))] * 4
        args += list(qk_tabs) + list(ret_tabs)
    widths = [(KV_RANK, F32), (LANES, F32), (H_MLA * HEAD_SLOT, BF16), (H_MLA * HEAD_SLOT, BF16),
              (H_MLA * HEAD_SLOT, BF16)] + [(D_RET_ALL, BF16)] * 4
    return pl.pallas_call(
        functools.partial(_inproj_kernel, rope),
        grid=(T // tm,),
        in_specs=in_specs,
        out_specs=[tok(w) for w, _ in widths],
        out_shape=[jax.ShapeDtypeStruct((T, w), dt) for w, dt in widths],
        compiler_params=_cparams("parallel"),
        name="inproj_rope" if rope else "inproj",
    )(*args)


def _ctx_kv_kernel(ckv_ref, krp_ref, wkv_ref, kgain_ref, k_ref, v_ref):
    _kv_heads(ckv_ref[...], krp_ref[...], None, wkv_ref, kgain_ref[0:1], None, None, k_ref, v_ref)


def _ctx_kv(ckv, krp, wts, tm):
    T = ckv.shape[0]
    tok = lambda w: pl.BlockSpec((tm, w), lambda i: (i, 0))
    return pl.pallas_call(
        _ctx_kv_kernel,
        grid=(T // tm,),
        in_specs=[tok(KV_RANK), tok(LANES),
                  _const_spec((KV_RANK, 2 * H_MLA * HEAD_SLOT)), _const_spec((2, LANES))],
        out_specs=[tok(H_MLA * HEAD_SLOT), tok(H_MLA * HEAD_SLOT)],
        out_shape=[jax.ShapeDtypeStruct((T, H_MLA * HEAD_SLOT), BF16),
                   jax.ShapeDtypeStruct((T, H_MLA * HEAD_SLOT), BF16)],
        compiler_params=_cparams("parallel"),
        name="ctx_kv",
    )(ckv, krp, wts["w_kv"], wts["k_gain"])


ATTN_KEY_CHUNK = 2048
ATTN_HEADS_PER_STEP = 4


def _attn_kernel(has_ctx, *refs):
    if has_ctx:
        q_ref, k_ref, v_ref, kc_ref, vc_ref, o_ref = refs
    else:
        q_ref, k_ref, v_ref, o_ref = refs
    sources = [(k_ref, v_ref)] + ([(kc_ref, vc_ref)] if has_ctx else [])
    chunks = []
    for kr, vr in sources:
        n = kr.shape[1]
        kc = _pick(n, (ATTN_KEY_CHUNK, 256, 128))
        chunks += [(kr, vr, slice(c * kc, (c + 1) * kc)) for c in range(n // kc)]
    tq = q_ref.shape[1]
    outs = []
    for j in range(ATTN_HEADS_PER_STEP):
        hs = slice(j * HEAD_SLOT, (j + 1) * HEAD_SLOT)
        q = q_ref[0, :, hs]
        m = jnp.full((tq, 1), -jnp.inf, F32)
        acc = jnp.zeros((tq, HEAD_SLOT), F32)
        for kr, vr, rows in chunks:
            s = _dot_nt(q, kr[0, rows, hs])
            m_new = jnp.maximum(m, jnp.max(s, axis=-1, keepdims=True))
            p = jnp.exp2(s - m_new).astype(BF16)
            acc = acc * jnp.exp2(m - m_new) + _dot(p, vr[0, rows, hs])
            m = m_new
        outs.append(acc[:, :D_V] / acc[:, D_V:D_V + 1])
    o_ref[0] = jnp.concatenate(outs, axis=-1).astype(BF16)


def _attention(q, k, v, kc, vc, tq):
    B, L, _ = q.shape
    has_ctx = kc is not None
    hps = ATTN_HEADS_PER_STEP
    in_specs = [pl.BlockSpec((1, tq, hps * HEAD_SLOT), lambda b, hp, i: (b, i, hp)),
                pl.BlockSpec((1, L, hps * HEAD_SLOT), lambda b, hp, i: (b, 0, hp)),
                pl.BlockSpec((1, L, hps * HEAD_SLOT), lambda b, hp, i: (b, 0, hp))]
    args = [q, k, v]
    if has_ctx:
        Lc = kc.shape[1]
        in_specs += [pl.BlockSpec((1, Lc, hps * HEAD_SLOT), lambda b, hp, i: (b, 0, hp)),
                     pl.BlockSpec((1, Lc, hps * HEAD_SLOT), lambda b, hp, i: (b, 0, hp))]
        args += [kc, vc]
    return pl.pallas_call(
        functools.partial(_attn_kernel, has_ctx),
        grid=(B, H_MLA // hps, L // tq),
        in_specs=in_specs,
        out_specs=pl.BlockSpec((1, tq, hps * D_V), lambda b, hp, i: (b, i, hp)),
        out_shape=jax.ShapeDtypeStruct((B, L, H_MLA * D_V), BF16),
        compiler_params=_cparams("parallel", "parallel", "arbitrary"),
        name="attn_ctx" if has_ctx else "attn",
    )(*args)


RET_UNROLL = 8
RET_SHORT_SEQ = 1024


def _ret_pair(has_s0, want_state, n_chunks, pair, hp, *refs):
    rq_ref, rk_ref, rv_ref, rg_ref, decf_ref, decb_ref, g_ref = refs[:7]
    refs = refs[7:]
    if has_s0:
        s0f_ref, s0b_ref = refs[:2]
        refs = refs[2:]
    o_ref = refs[0]
    refs = refs[1:]
    if want_state:
        sf_ref, sb_ref = refs[:2]
        refs = refs[2:]
    kv_ref, sp_ref = refs

    C = RET_CHUNK
    W = 2 * D_RET
    ls = slice(pair * W, (pair + 1) * W)
    lane = lax.broadcasted_iota(jnp.int32, (1, W), 1)
    lane_h1 = lane >= D_RET
    row_h1 = lax.broadcasted_iota(jnp.int32, (W, 1), 0) >= D_RET
    blockdiag = row_h1 == lane_h1
    pos = lax.broadcasted_iota(jnp.int32, (C, 1), 0).astype(F32)
    diff = (lax.broadcasted_iota(jnp.int32, (C, C), 0)
            - lax.broadcasted_iota(jnp.int32, (C, C), 1)).astype(F32)

    lg = {}
    for name, ref in (("f", decf_ref), ("b", decb_ref)):
        a = -jnp.exp(ref[pl.ds(2 * hp, 1), :])
        b = -jnp.exp(ref[pl.ds(2 * hp + 1, 1), :])
        lg[name] = (a[:, :1], b[:, :1], jnp.where(lane_h1, b, a))
    lgf0, lgf1, lgf = lg["f"]
    lgb0, lgb1, lgb = lg["b"]
    qdf, kdf = jnp.exp(lgf * (pos + 1.0)), jnp.exp(lgf * (C - 1.0 - pos))
    qdb, kdb = jnp.exp(lgb * (C - pos)), jnp.exp(lgb * pos)
    row_h1_full = lax.broadcasted_iota(jnp.int32, (W, W), 0) >= D_RET
    cdf = jnp.where(row_h1_full, jnp.exp(lgf1 * C), jnp.exp(lgf0 * C))
    cdb = jnp.where(row_h1_full, jnp.exp(lgb1 * C), jnp.exp(lgb0 * C))

    def intra_decay(lf, lb):
        return jnp.where(diff >= 0, jnp.exp(lf * jnp.maximum(diff, 0.0)), jnp.exp(lb * jnp.maximum(-diff, 0.0)))

    dmat0, dmat1 = intra_decay(lgf0, lgb0), intra_decay(lgf1, lgb1)
    g = g_ref[...]

    def chunk_rows(n):
        return pl.ds(pl.multiple_of(n * C, C), C)

    def phase_a(n, carry):
        rows = chunk_rows(n)
        k = rk_ref[0, rows, ls].astype(F32)
        k2 = jnp.concatenate([(k * kdf).astype(BF16), (k * kdb).astype(BF16)], axis=1)
        kv_ref[n] = _dot_tn(k2, rv_ref[0, rows, ls])
        return carry

    unroll = min(RET_UNROLL, n_chunks)
    lax.fori_loop(0, n_chunks, phase_a, 0, unroll=unroll)

    def init_state(ref):
        if not has_s0:
            return jnp.zeros((W, W), F32)
        z = jnp.zeros((D_RET, D_RET), F32)
        return jnp.concatenate([jnp.concatenate([ref[0, 2 * pair], z], axis=1),
                                jnp.concatenate([z, ref[0, 2 * pair + 1]], axis=1)], axis=0)

    def scan_f(n, S):
        sp_ref[n, 0:W, :] = S.astype(BF16)
        return cdf * S + jnp.where(blockdiag, kv_ref[n, 0:W, :], 0.0)

    def scan_b(i, S):
        n = n_chunks - 1 - i
        sp_ref[n, W:2 * W, :] = S.astype(BF16)
        return cdb * S + jnp.where(blockdiag, kv_ref[n, W:2 * W, :], 0.0)

    Sf = lax.fori_loop(0, n_chunks, scan_f, init_state(s0f_ref if has_s0 else None))
    Sb = lax.fori_loop(0, n_chunks, scan_b, init_state(s0b_ref if has_s0 else None))
    if want_state:
        for S, ref in ((Sf, sf_ref), (Sb, sb_ref)):
            ref[0, 2 * pair] = S[0:D_RET, 0:D_RET]
            ref[0, 2 * pair + 1] = S[D_RET:W, D_RET:W]

    def phase_c(n, carry):
        rows = chunk_rows(n)
        q = rq_ref[0, rows, ls]
        k = rk_ref[0, rows, ls]
        v = rv_ref[0, rows, ls]
        zero = jnp.zeros_like(q)
        q_heads = jnp.concatenate([jnp.where(lane_h1, zero, q), jnp.where(lane_h1, q, zero)], axis=0)
        sc = _dot_nt(q_heads, k)
        p = jnp.concatenate([(sc[:C] * dmat0).astype(BF16), (sc[C:] * dmat1).astype(BF16)], axis=1)
        v_bd = jnp.concatenate([jnp.where(lane_h1, zero, v), jnp.where(lane_h1, v, zero)], axis=0)
        qf = q.astype(F32)
        q2 = jnp.concatenate([(qf * qdf).astype(BF16), (qf * qdb).astype(BF16)], axis=1)
        o = _dot(p, v_bd) + _dot(q2, sp_ref[n])
        o2 = o * o
        ss0 = jnp.sum(jnp.where(lane_h1, 0.0, o2), axis=-1, keepdims=True)
        ss1 = jnp.sum(jnp.where(lane_h1, o2, 0.0), axis=-1, keepdims=True)
        ms = jnp.where(lane_h1, ss1, ss0) * (1.0 / D_RET)
        o = o * lax.rsqrt(ms + EPS) * g * rg_ref[0, rows, ls].astype(F32)
        o_ref[0, rows, ls] = o.astype(BF16)
        return carry

    lax.fori_loop(0, n_chunks, phase_c, 0, unroll=unroll)


def _ret_kernel(has_s0, want_state, n_chunks, pairs, *refs):
    for pair in range(pairs):
        _ret_pair(has_s0, want_state, n_chunks, pair, pl.program_id(1) * pairs + pair, *refs)


def _retention(rq, rk, rv, rg, dec_f, dec_b, g, s0f, s0b, want_state):
    B, L, _ = rq.shape
    has_s0 = s0f is not None
    pairs = H_RET // 2 if L <= RET_SHORT_SEQ else 1
    seq = pl.BlockSpec((1, L, pairs * LANES), lambda b, hp: (b, 0, hp))
    st = pl.BlockSpec((1, 2 * pairs, D_RET, D_RET), lambda b, hp: (b, hp, 0, 0))
    cst = lambda shape: pl.BlockSpec(shape, lambda b, hp: (0,) * len(shape))
    in_specs = [seq] * 4 + [cst((H_RET, LANES)), cst((H_RET, LANES)), cst((1, LANES))]
    args = [rq, rk, rv, rg, dec_f, dec_b, g]
    if has_s0:
        in_specs += [st, st]
        args += [s0f, s0b]
    out_specs = [seq]
    out_shape = [jax.ShapeDtypeStruct((B, L, D_RET_ALL), BF16)]
    if want_state:
        out_specs += [st, st]
        out_shape += [jax.ShapeDtypeStruct((B, H_RET, D_RET, D_RET), F32)] * 2
    return pl.pallas_call(
        functools.partial(_ret_kernel, has_s0, want_state, L // RET_CHUNK, pairs),
        grid=(B, H_RET // 2 // pairs),
        in_specs=in_specs,
        out_specs=out_specs,
        out_shape=out_shape,
        scratch_shapes=[pltpu.VMEM((L // RET_CHUNK, 4 * D_RET, LANES), F32),
                        pltpu.VMEM((L // RET_CHUNK, 4 * D_RET, LANES), BF16)],
        compiler_params=_cparams("parallel", "parallel"),
        name="retention_s0" if has_s0 else "retention",
    )(*args)


def _pack_pairs(x):
    w = x.shape[1] // 2
    hi = lax.bitcast_convert_type(x[:, :w].astype(BF16).astype(F32), jnp.uint32)
    lo = lax.bitcast_convert_type(x[:, w:].astype(BF16).astype(F32), jnp.uint32)
    return hi | (lo >> 16)


def _unpack_pairs(p):
    hi = lax.bitcast_convert_type(p & jnp.uint32(0xFFFF0000), F32)
    lo = lax.bitcast_convert_type(p << 16, F32)
    return jnp.concatenate([hi, lo], axis=1)


D_PACKED = D_MODEL // 2


def _outproj_kernel(omla_ref, oret_ref, x_ref, mod_ref, wo_ref, n2g_ref, rwh_ref, rwl_ref, rb_ref, tri_ref,
                    x2_ref, h2p_ref, code_ref, rank_ref, wk_ref, cnt_ref):
    @pl.when(pl.program_id(0) == 0)
    def _():
        cnt_ref[...] = jnp.zeros_like(cnt_ref)

    mod = mod_ref[0]
    half = H_MLA * D_V
    a = _dot(omla_ref[...], wo_ref[:half, :]) + _dot(oret_ref[...], wo_ref[half:, :])
    x2 = x_ref[...] + mod[2:3] * a
    x2_ref[...] = x2
    h2 = _rms(x2, n2g_ref[...]) * (1.0 + mod[4:5]) + mod[3:4]
    h2p_ref[...] = _pack_pairs(h2)
    h2_hi = h2.astype(BF16)
    h2_lo = (h2 - h2_hi.astype(F32)).astype(BF16)
    rw_hi = rwh_ref[...]
    logits = _dot_nt(rw_hi, h2_hi) + _dot_nt(rw_hi, h2_lo) + _dot_nt(rwl_ref[...], h2_hi)
    scores = jax.nn.sigmoid(logits)
    tm = scores.shape[1]
    expert = lax.broadcasted_iota(jnp.int32, scores.shape, 0)
    cand = scores + rb_ref[...]
    code = jnp.zeros(scores.shape, jnp.int32)
    for k in range(TOP_K):
        mx = jnp.max(cand, axis=0, keepdims=True)
        first = jnp.min(jnp.where(cand == mx, expert, LANES), axis=0, keepdims=True)
        hit = expert == first
        code = jnp.where(hit, k + 1, code)
        cand = jnp.where(hit, -jnp.inf, cand)
    picked = code > 0
    sel = jnp.where(picked, scores, 0.0)
    gates = sel * (ROUTED_SCALE / jnp.sum(sel, axis=0, keepdims=True))
    code_ref[...] = code

    onehot = jnp.where(picked, 1.0, 0.0)
    carry = cnt_ref[...].astype(F32)
    rank = _dot(onehot.astype(BF16), tri_ref[...])
    rank_ref[...] = rank + jnp.concatenate([carry] * (tm // LANES), axis=1)
    cnt_ref[...] = (carry + jnp.sum(onehot, axis=1, keepdims=True)).astype(jnp.int32)

    row = lax.broadcasted_iota(jnp.int32, scores.shape, 0)
    wk_t = jnp.zeros(scores.shape, F32)
    for k in range(TOP_K):
        w = jnp.sum(jnp.where(code == k + 1, gates, 0.0), axis=0, keepdims=True)
        wk_t = jnp.where(row == k, w, wk_t)
    wk_ref[...] = wk_t.T


def _outproj(o_mla, o_ret, x, mod, mod_row0, tokens_per_row, wts, tm):
    T = x.shape[0]
    bpr = tokens_per_row // tm
    tok = lambda w: pl.BlockSpec((tm, w), lambda i: (i, 0))
    emaj = pl.BlockSpec((LANES, tm), lambda i: (0, i))
    tri = jnp.triu(jnp.ones((tm, tm), BF16), 1)
    bias = jnp.broadcast_to(wts["router_bias"].reshape(LANES, 1), (LANES, tm))
    return pl.pallas_call(
        _outproj_kernel,
        grid=(T // tm,),
        in_specs=[tok(H_MLA * D_V), tok(D_RET_ALL), tok(D_MODEL),
                  pl.BlockSpec((1, 6, D_MODEL), lambda i: (mod_row0 + i // bpr, 0, 0)),
                  _const_spec((D_MODEL, D_MODEL)), _const_spec((1, D_MODEL)),
                  _const_spec((LANES, D_MODEL)), _const_spec((LANES, D_MODEL)), _const_spec((LANES, tm)),
                  _const_spec((tm, tm))],
        out_specs=[tok(D_MODEL), tok(D_PACKED), emaj, emaj, tok(LANES), _const_spec((LANES, LANES))],
        out_shape=[jax.ShapeDtypeStruct((T, D_MODEL), F32),
                   jax.ShapeDtypeStruct((T, D_PACKED), jnp.uint32),
                   jax.ShapeDtypeStruct((LANES, T), jnp.int32),
                   jax.ShapeDtypeStruct((LANES, T), F32),
                   jax.ShapeDtypeStruct((T, LANES), F32),
                   jax.ShapeDtypeStruct((LANES, LANES), jnp.int32)],
        compiler_params=_cparams("arbitrary"),
        name="outproj_router",
    )(o_mla, o_ret, x, mod, wts["w_o"], wts["norm2_g"], wts["router_hi"], wts["router_lo"], bias, tri)


def _plan_kernel(code_ref, rank_ref, first_ref, pos_ref):
    code = code_ref[...]
    tm = code.shape[1]
    row = rank_ref[...] + jnp.concatenate([first_ref[...]] * (tm // LANES), axis=1)
    krow = lax.broadcasted_iota(jnp.int32, pos_ref.shape, 0)
    pos = jnp.zeros(pos_ref.shape, F32)
    for k in range(TOP_K):
        pos = jnp.where(krow == k, jnp.sum(jnp.where(code == k + 1, row, 0.0), axis=0, keepdims=True), pos)
    pos_ref[...] = pos.astype(jnp.int32)


def _plan(code, rank, first_rows, tm):
    T = code.shape[1]
    first = jnp.broadcast_to(jnp.pad(first_rows.astype(F32), (0, LANES - N_EXPERTS)).reshape(LANES, 1),
                             (LANES, LANES))
    emaj = pl.BlockSpec((LANES, tm), lambda i: (0, i))
    return pl.pallas_call(
        _plan_kernel,
        grid=(T // tm,),
        in_specs=[emaj, emaj, _const_spec((LANES, LANES))],
        out_specs=pl.BlockSpec((8, tm), lambda i: (0, i)),
        out_shape=jax.ShapeDtypeStruct((8, T), jnp.int32),
        compiler_params=_cparams("parallel"),
        name="moe_plan",
    )(code, rank, first)


SC_WORKERS = 32
SC_CHUNK = 64


def _sc_worker_id():
    return lax.axis_index("s") * 2 + lax.axis_index("c")


def _sc_mesh():
    return plsc.VectorSubcoreMesh(core_axis_name="c", subcore_axis_name="s")


def _sc_gather_rows(table, idx):
    N = idx.shape[0]
    W = table.shape[1]
    per_w = N // SC_WORKERS
    n_chunks = per_w // SC_CHUNK
    assert N == SC_WORKERS * n_chunks * SC_CHUNK and n_chunks % 2 == 0

    @functools.partial(
        pl.kernel, mesh=_sc_mesh(), out_type=jax.ShapeDtypeStruct((N, W), table.dtype),
        scratch_types=[pltpu.VMEM((n_chunks, SC_CHUNK), jnp.int32),
                       pltpu.VMEM((2, SC_CHUNK, W), table.dtype),
                       pltpu.SemaphoreType.DMA((2,)), pltpu.SemaphoreType.DMA((2,))],
        name="sc_gather_rows")
    def k(table_hbm, idx_hbm, out_hbm, idx_v, rows_v, gsem, wsem):
        wid = _sc_worker_id()
        base = wid * per_w
        pltpu.sync_copy(idx_hbm.at[wid], idx_v)

        def gather(j, b):
            return pltpu.make_async_copy(table_hbm.at[idx_v.at[j]], rows_v.at[b], gsem.at[b])

        def writeback(j, b):
            off = pl.multiple_of(base + j * SC_CHUNK, SC_CHUNK)
            return pltpu.make_async_copy(rows_v.at[b], out_hbm.at[pl.ds(off, SC_CHUNK)], wsem.at[b])

        gather(0, 0).start()

        @pl.loop(0, n_chunks, step=2)
        def _(j):
            for b in range(2):
                jj = j + b

                @pl.when(jj + 1 < n_chunks)
                def _():
                    @pl.when(jj >= 1)
                    def _():
                        writeback(jj - 1, 1 - b).wait()
                    gather(jj + 1, 1 - b).start()

                gather(jj, b).wait()
                writeback(jj, b).start()

        writeback(n_chunks - 2, 0).wait()
        writeback(n_chunks - 1, 1).wait()

    return k(table, idx.reshape(SC_WORKERS, n_chunks, SC_CHUNK))


def _sc_dispatch_rows(rows_list, pos_list, n_out):
    W = rows_list[0].shape[1]
    K = pos_list[0].shape[0]
    n_src = len(rows_list)
    plan = []
    idx_list = []
    for rows, pos in zip(rows_list, pos_list):
        T = rows.shape[0]
        per_w = T // SC_WORKERS
        n_chunks = per_w // SC_CHUNK
        assert T == SC_WORKERS * n_chunks * SC_CHUNK and n_chunks % 2 == 0
        plan.append((per_w, n_chunks))
        idx_list.append(pos.reshape(K, SC_WORKERS, n_chunks, SC_CHUNK).transpose(1, 2, 0, 3))
    max_chunks = max(n for _, n in plan)

    @functools.partial(
        pl.kernel, mesh=_sc_mesh(), out_type=jax.ShapeDtypeStruct((n_out, W), rows_list[0].dtype),
        scratch_types=[pltpu.VMEM((max_chunks, K, SC_CHUNK), jnp.int32),
                       pltpu.VMEM((2, SC_CHUNK, W), rows_list[0].dtype),
                       pltpu.SemaphoreType.DMA((2,)), pltpu.SemaphoreType.DMA((2,))],
        name="sc_dispatch_rows")
    def k(*refs):
        rows_refs, idx_refs = refs[:n_src], refs[n_src:2 * n_src]
        out_hbm, idx_v, rows_v, lsem, ssem = refs[2 * n_src:]
        wid = _sc_worker_id()
        for rows_hbm, idx_hbm, (per_w, n_chunks) in zip(rows_refs, idx_refs, plan):
            base = wid * per_w
            pltpu.sync_copy(idx_hbm.at[wid], idx_v.at[pl.ds(0, n_chunks)])

            def load(j, b):
                off = pl.multiple_of(base + j * SC_CHUNK, SC_CHUNK)
                return pltpu.make_async_copy(rows_hbm.at[pl.ds(off, SC_CHUNK)], rows_v.at[b], lsem.at[b])

            def scatter(j, b, kk):
                return pltpu.make_async_copy(rows_v.at[b], out_hbm.at[idx_v.at[j, kk]], ssem.at[b])

            load(0, 0).start()

            @pl.loop(0, n_chunks, step=2)
            def _(j):
                for b in range(2):
                    jj = j + b

                    @pl.when(jj + 1 < n_chunks)
                    def _():
                        @pl.when(jj >= 1)
                        def _():
                            for kk in range(K):
                                scatter(jj - 1, 1 - b, kk).wait()
                        load(jj + 1, 1 - b).start()

                    load(jj, b).wait()
                    for kk in range(K):
                        scatter(jj, b, kk).start()

            for kk in range(K):
                scatter(n_chunks - 2, 0, kk).wait()
            for kk in range(K):
                scatter(n_chunks - 1, 1, kk).wait()

    return k(*rows_list, *idx_list)


ROW_TILE = 2048
ROW_SUB = 1024
MOE_GATHER_GROUP = 8192


def _experts_kernel(te_ref, tv_ref, tb_ref, xs_ref, wg_ref, wu_ref, wd_ref, ys_ref, wg_s, wu_s, wd_s):
    del tb_ref
    i = pl.program_id(0)
    valid = tv_ref[i]

    @pl.when(jnp.logical_or(i == 0, te_ref[i] != te_ref[jnp.maximum(i - 1, 0)]))
    def _():
        wg_s[...] = wg_ref[0].astype(BF16)
        wu_s[...] = wu_ref[0].astype(BF16)
        wd_s[...] = wd_ref[0].astype(BF16)

    def ffn(r0, n):
        row = r0 + lax.broadcasted_iota(jnp.int32, (n, 1), 0)
        p = jnp.where(row < valid, xs_ref[r0:r0 + n, :], jnp.uint32(0))
        x = _unpack_pairs(p).astype(BF16)
        act = _silu(_dot(x, wg_s[...])) * _dot(x, wu_s[...])
        ys_ref[r0:r0 + n, :] = _pack_pairs(_dot(act.astype(BF16), wd_s[...]))

    for sb in range(ROW_TILE // ROW_SUB):
        @pl.when(valid > sb * ROW_SUB)
        def _():
            ffn(sb * ROW_SUB, ROW_SUB)


def _experts(xs, tile_expert, tile_valid, tile_block, wts):
    n_tiles = tile_expert.shape[0]
    ex = lambda a, b: pl.BlockSpec((1, a, b), lambda i, te, tv, tb: (te[i], 0, 0))
    row = pl.BlockSpec((ROW_TILE, D_PACKED), lambda i, te, tv, tb: (tb[i], 0))
    return pl.pallas_call(
        _experts_kernel,
        grid_spec=pltpu.PrefetchScalarGridSpec(
            num_scalar_prefetch=3, grid=(n_tiles,),
            in_specs=[row, ex(D_MODEL, D_EXPERT), ex(D_MODEL, D_EXPERT), ex(D_EXPERT, D_MODEL)],
            out_specs=row,
            scratch_shapes=[pltpu.VMEM((D_MODEL, D_EXPERT), BF16), pltpu.VMEM((D_MODEL, D_EXPERT), BF16),
                            pltpu.VMEM((D_EXPERT, D_MODEL), BF16)]),
        out_shape=jax.ShapeDtypeStruct(xs.shape, jnp.uint32),
        compiler_params=_cparams("arbitrary"),
        name="experts",
    )(tile_expert, tile_valid, tile_block, xs, wts["exp_gate"], wts["exp_up"], wts["exp_down"])


def _combine_kernel(h2p_ref, ys_ref, wk_ref, x2_ref, mod_ref, shg_ref, shu_ref, shd_ref, *rest):
    y_ref = rest[-1]
    t = _unpack_pairs(h2p_ref[...]).astype(BF16)
    act = _silu(_dot(t, shg_ref[...])) * _dot(t, shu_ref[...])
    acc = _dot(act.astype(BF16), shd_ref[...])
    wk = wk_ref[...]
    for k in range(TOP_K):
        acc = acc + wk[:, k:k + 1] * _unpack_pairs(ys_ref[k])
    y_ref[...] = x2_ref[...] + mod_ref[0][5:6] * acc


def _combine(h2p, ys, row0, wk, x2, y_prev, mod, mod_row0, tokens_per_row, wts, tm):
    T = h2p.shape[0]
    G = ys.shape[1]
    bpr = tokens_per_row // tm
    blk0 = row0 // tm
    tok = lambda w: pl.BlockSpec((tm, w), lambda i: (blk0 + i, 0))
    in_specs = [tok(D_PACKED), pl.BlockSpec((TOP_K, tm, D_PACKED), lambda i: (0, i, 0)), tok(LANES),
                tok(D_MODEL), pl.BlockSpec((1, 6, D_MODEL), lambda i: (mod_row0 + (blk0 + i) // bpr, 0, 0)),
                _const_spec((D_MODEL, D_SHARED)), _const_spec((D_MODEL, D_SHARED)),
                _const_spec((D_SHARED, D_MODEL))]
    args = [h2p, ys, wk, x2, mod, wts["sh_gate"], wts["sh_up"], wts["sh_down"]]
    aliases = {}
    if y_prev is not None:
        aliases = {len(args): 0}
        in_specs.append(pl.BlockSpec(memory_space=pl.ANY))
        args.append(y_prev)
    return pl.pallas_call(
        _combine_kernel,
        grid=(G // tm,),
        in_specs=in_specs,
        out_specs=tok(D_MODEL),
        out_shape=jax.ShapeDtypeStruct((T, D_MODEL), F32),
        input_output_aliases=aliases,
        compiler_params=_cparams("parallel"),
        name="moe_combine",
    )(*args)


def _moe(halves, mod, wts):
    routed = [h["routed"] for h in halves]
    sizes = [r[1].shape[0] for r in routed]
    n_tiles = (TOP_K * sum(sizes)) // ROW_TILE + N_EXPERTS
    path_counts = [r[5][:N_EXPERTS, 0] for r in routed]
    counts = sum(path_counts)
    padded = ((counts + ROW_TILE - 1) // ROW_TILE) * ROW_TILE
    ends = jnp.cumsum(padded)
    starts = ends - padded
    tile_start = jnp.arange(n_tiles, dtype=jnp.int32) * ROW_TILE
    tile_expert = jnp.minimum(jnp.sum(tile_start[:, None] >= ends[None, :], axis=1), N_EXPERTS - 1).astype(jnp.int32)
    overlap = (jnp.minimum(tile_start[:, None] + ROW_TILE, (starts + counts)[None, :])
               - jnp.maximum(tile_start[:, None], starts[None, :]))
    tile_valid = jnp.sum(jnp.maximum(overlap, 0), axis=1).astype(jnp.int32)

    pos, first = [], starts
    for h, r, c in zip(halves, routed, path_counts):
        pos.append(_plan(r[2], r[3], first, h["tm"])[:TOP_K])
        first = first + c
    xs = _sc_dispatch_rows([r[1] for r in routed], pos, n_tiles * ROW_TILE)
    tile_block = jnp.minimum(jnp.arange(n_tiles, dtype=jnp.int32), ends[-1] // ROW_TILE - 1).astype(jnp.int32)
    ys = _experts(xs, tile_expert, tile_valid, tile_block, wts)
    outs = []
    for h, r, p in zip(halves, routed, pos):
        T = r[1].shape[0]
        G = MOE_GATHER_GROUP if T % MOE_GATHER_GROUP == 0 else T
        y = None
        for row0 in range(0, T, G):
            ysel = _sc_gather_rows(ys, p[:, row0:row0 + G].reshape(-1)).reshape(TOP_K, G, D_PACKED)
            y = _combine(r[1], ysel, row0, r[4], r[0], y, mod, h["mod_row0"], h["tokens_per_row"], wts, h["tm"])
        outs.append(y.reshape(h["shape"]))
    return outs


def _pad_heads(w, d_used):
    k = w.shape[0]
    w = w.reshape(k, H_MLA, d_used)
    return jnp.pad(w, ((0, 0), (0, 0), (0, HEAD_SLOT - d_used))).reshape(k, H_MLA * HEAD_SLOT)


def _lane_pad(v):
    return jnp.pad(v, (0, LANES - v.shape[0])).reshape(1, LANES)


def _rot_partners(w):
    rot = w[..., w.shape[-1] - D_ROPE:]
    return jnp.concatenate([jnp.zeros_like(w[..., :w.shape[-1] - D_ROPE]),
                            _swap_halves(rot, D_ROPE // 4)], axis=-1)


def _prep_weights(l, w_in, norm1_g, mla_q_norm_g, w_uq, mla_kv_norm_g, w_ukv, mla_q_gain, mla_k_gain,
                  w_o, norm2_g, router_w, router_bias, exp_w_gate, exp_w_up, exp_w_down,
                  sh_w_gate, sh_w_up, sh_w_down):
    w = w_in[l]
    a, b, c = Q_RANK, Q_RANK + KV_RANK, Q_RANK + KV_RANK + D_ROPE
    kr_slot = lambda wk: jnp.pad(wk, ((0, 0), (D_NOPE, LANES - D_QK)))
    w1 = jnp.concatenate([w[:, :b], kr_slot(w[:, b:c]), w[:, c:]], axis=1).astype(BF16)
    w_rq, w_rk = w[:, c:c + D_RET_ALL], w[:, c + D_RET_ALL:c + 2 * D_RET_ALL]
    w1_rope = jnp.concatenate([w1, kr_slot(_rot_partners(w[:, b:c])).astype(BF16),
                               _swap_halves(w_rq, D_RET // 4).astype(BF16),
                               _swap_halves(w_rk, D_RET // 4).astype(BF16)], axis=1)
    uq = w_uq[l].reshape(Q_RANK, H_MLA, D_QK)
    pad_slots = lambda u: jnp.pad(u, ((0, 0), (0, 0), (0, HEAD_SLOT - D_QK))).reshape(Q_RANK, H_MLA * HEAD_SLOT)
    gains = lambda g: jnp.concatenate([_lane_pad(g), _lane_pad(_rot_partners(g))], axis=0)
    ukv = w_ukv[l].reshape(KV_RANK, H_MLA, D_NOPE + D_V)
    w_k = _pad_heads(ukv[:, :, :D_NOPE].reshape(KV_RANK, H_MLA * D_NOPE), D_NOPE)
    w_v = _pad_heads(ukv[:, :, D_NOPE:].reshape(KV_RANK, H_MLA * D_V), D_V)
    rw = jnp.pad(router_w[l].T, ((0, LANES - N_EXPERTS), (0, 0)))
    rw_hi = rw.astype(BF16)
    rw_lo = (rw - rw_hi.astype(F32)).astype(BF16)
    rb = jnp.concatenate([router_bias[l].astype(F32), jnp.full((LANES - N_EXPERTS,), -jnp.inf, F32)])
    return dict(
        norm1_g=norm1_g[l].reshape(1, D_MODEL), w1=w1, w1_rope=w1_rope,
        q_norm_g=mla_q_norm_g[l].reshape(1, Q_RANK),
        w_uq=jnp.concatenate([pad_slots(uq), pad_slots(_rot_partners(uq))], axis=1).astype(BF16),
        kv_norm_g=mla_kv_norm_g[l].reshape(1, KV_RANK),
        w_kv=jnp.concatenate([w_k, w_v], axis=1).astype(BF16),
        q_gain=gains(mla_q_gain[l]), k_gain=gains(mla_k_gain[l]),
        w_o=w_o[l].astype(BF16), norm2_g=norm2_g[l].reshape(1, D_MODEL),
        router_hi=rw_hi, router_lo=rw_lo, router_bias=rb.reshape(1, LANES),
        exp_gate=exp_w_gate[l], exp_up=exp_w_up[l], exp_down=exp_w_down[l],
        sh_gate=sh_w_gate[l].astype(BF16), sh_up=sh_w_up[l].astype(BF16),
        sh_down=sh_w_down[l].astype(BF16))


def _pick(n, prefs):
    for p in prefs:
        if n % p == 0:
            return p
    return n


def _mixer_half(x, mod, mod_row0, wts, ret_wts, ctx):
    B, L, D = x.shape
    T = B * L
    tokens_per_row = L if ctx is not None else T
    xt = x.reshape(T, D)
    rope_tabs = None
    if ctx is not None:
        rope_tabs = _rope_tables(L)
    tm = _pick(tokens_per_row, (512, 256, 128))
    ckv_n, krp, q, k, v, rq, rk, rv, rg = _inproj(xt, mod, mod_row0, tokens_per_row, wts, rope_tabs, tm)

    seq = lambda a: a.reshape(B, L, a.shape[-1])
    kc = vc = s0f = s0b = None
    if ctx is not None:
        ckv_c, kr_c, s0f, s0b = ctx
        Lc = ckv_c.shape[1]
        krp_c = jnp.pad(kr_c, ((0, 0), (0, 0), (D_NOPE, LANES - D_NOPE - D_ROPE)))
        kc, vc = _ctx_kv(ckv_c.reshape(B * Lc, KV_RANK), krp_c.reshape(B * Lc, LANES), wts,
                         _pick(B * Lc, (512, 256, 128)))
        kc, vc = kc.reshape(B, Lc, -1), vc.reshape(B, Lc, -1)
    o_mla = _attention(seq(q), seq(k), seq(v), kc, vc, _pick(L, (1024, 512, 256, 128)))

    dec_f, dec_b, ret_g = ret_wts
    want_state = ctx is None
    ret = _retention(seq(rq), seq(rk), seq(rv), seq(rg), dec_f, dec_b, ret_g, s0f, s0b, want_state)
    o_ret = ret[0]

    routed = _outproj(o_mla.reshape(T, -1), o_ret.reshape(T, -1), xt, mod, mod_row0, tokens_per_row, wts, tm)
    new = None
    if want_state:
        new = (ckv_n.reshape(B, L, KV_RANK), krp[:, D_NOPE:D_NOPE + D_ROPE].reshape(B, L, D_ROPE),
               ret[1], ret[2])
    return dict(routed=routed, mod_row0=mod_row0, tokens_per_row=tokens_per_row, tm=tm, shape=(B, L, D)), new


def kernel(x_prompt, x_sample, cache_mla_ckv, cache_mla_krope, state_ret_fwd, state_ret_bwd, c, c_ctx,
           w_ada, b_ada, norm1_g, w_in, mla_q_norm_g, w_uq, mla_kv_norm_g, w_ukv, mla_q_gain, mla_k_gain,
           ret_decay_fwd, ret_decay_bwd, ret_norm_g, w_o, norm2_g, router_w, router_bias, exp_w_gate,
           exp_w_up, exp_w_down, sh_w_gate, sh_w_up, sh_w_down):
    depth = w_ada.shape[0]
    n_dec = c.shape[0]
    assert 1 + n_dec <= MOD_ROWS
    cond = jnp.concatenate([c_ctx[None], c, jnp.zeros((MOD_ROWS - 1 - n_dec, D_MODEL), F32)], axis=0)

    y_prompt, y_sample = x_prompt, x_sample
    ckv_l, kr_l, sf_l, sb_l = [], [], [], []
    for l in range(depth):
        wts = _prep_weights(l, w_in, norm1_g, mla_q_norm_g, w_uq, mla_kv_norm_g, w_ukv, mla_q_gain,
                            mla_k_gain, w_o, norm2_g, router_w, router_bias, exp_w_gate, exp_w_up,
                            exp_w_down, sh_w_gate, sh_w_up, sh_w_down)
        ret_wts = (jnp.broadcast_to(ret_decay_fwd[l].astype(F32)[:, None], (H_RET, LANES)),
                   jnp.broadcast_to(ret_decay_bwd[l].astype(F32)[:, None], (H_RET, LANES)),
                   jnp.tile(ret_norm_g[l].reshape(1, D_RET), (1, LANES // D_RET)))
        mod = _adaln(cond, w_ada[l], b_ada[l]).reshape(MOD_ROWS, 6, D_MODEL)
        half_p, new = _mixer_half(y_prompt, mod, 0, wts, ret_wts, None)
        ckv_l.append(new[0]); kr_l.append(new[1]); sf_l.append(new[2]); sb_l.append(new[3])
        ctx = (cache_mla_ckv[:, l], cache_mla_krope[:, l], state_ret_fwd[:, l], state_ret_bwd[:, l])
        half_s, _ = _mixer_half(y_sample, mod, 1, wts, ret_wts, ctx)
        y_prompt, y_sample = _moe([half_p, half_s], mod, wts)

    return (y_prompt, y_sample, jnp.stack(ckv_l, axis=1), jnp.stack(kr_l, axis=1),
            jnp.stack(sf_l, axis=1), jnp.stack(sb_l, axis=1))
```

```python
import functools
import math

import numpy as np
import jax
import jax.numpy as jnp
from jax import lax
from jax.experimental import pallas as pl
from jax.experimental.pallas import tpu as pltpu
from jax.experimental.pallas import tpu_sc as plsc

F32 = jnp.float32
BF16 = jnp.bfloat16

D_MODEL = 1024
GRID_W = 64
H_MLA = 8
D_NOPE = 64
D_ROPE = 32
D_QK = D_NOPE + D_ROPE
D_V = 64
Q_RANK = 256
KV_RANK = 128
H_RET = 8
D_RET = 64
RET_CHUNK = 256
D_RET_ALL = H_RET * D_RET
N_EXPERTS = 64
TOP_K = 6
D_EXPERT = 256
D_SHARED = 256
ROUTED_SCALE = 2.5
ROPE_BASE = 10000.0
EPS = 1e-6
LOG2_E = math.log2(math.e)

LANES = 128
HEAD_SLOT = LANES
D_IN_PAD = Q_RANK + KV_RANK + LANES + 4 * D_RET_ALL
MOD_ROWS = 16
VMEM_LIMIT = 56 * 1024 * 1024


def _cparams(*sem):
    return pltpu.CompilerParams(dimension_semantics=sem, vmem_limit_bytes=VMEM_LIMIT)


def _dot(a, b):
    return jnp.dot(a, b, preferred_element_type=F32)


def _dot_nt(a, b):
    return lax.dot_general(a, b, (((1,), (1,)), ((), ())), preferred_element_type=F32)


def _dot_tn(a, b):
    return lax.dot_general(a, b, (((0,), (0,)), ((), ())), preferred_element_type=F32)


def _rms(x, g):
    return x * lax.rsqrt(jnp.mean(x * x, axis=-1, keepdims=True) + EPS) * g


def _silu(x):
    return x * jax.nn.sigmoid(x)


def _adaln_kernel(c_ref, w_ref, b_ref, o_ref):
    s = _silu(c_ref[...])
    o_ref[...] = _dot(s.astype(BF16), w_ref[...].astype(BF16)) + b_ref[...]


def _adaln(cond, w_ada, b_ada):
    n_out = w_ada.shape[1]
    bn = 1536
    return pl.pallas_call(
        _adaln_kernel,
        grid=(n_out // bn,),
        in_specs=[pl.BlockSpec((MOD_ROWS, D_MODEL), lambda j: (0, 0)),
                  pl.BlockSpec((D_MODEL, bn), lambda j: (0, j)),
                  pl.BlockSpec((1, bn), lambda j: (0, j))],
        out_specs=pl.BlockSpec((MOD_ROWS, bn), lambda j: (0, j)),
        out_shape=jax.ShapeDtypeStruct((MOD_ROWS, n_out), F32),
        compiler_params=_cparams("arbitrary"),
        name="adaln",
    )(cond, w_ada, b_ada.reshape(1, n_out))


def _swap_halves(w, half_pair):
    g = w.reshape(*w.shape[:-1], w.shape[-1] // (2 * half_pair), 2, half_pair)
    return g[..., ::-1, :].reshape(w.shape)


def _rope_tables(n_tokens):
    t = np.arange(n_tokens)
    row = (t // GRID_W).astype(np.float64)
    col = (t % GRID_W).astype(np.float64)

    def axis_tables(width, lane0, period, fill):
        p = width // 2
        inv = 1.0 / (ROPE_BASE ** (np.arange(p, dtype=np.float64) / p))
        cos = np.full((n_tokens, LANES), fill)
        sin = np.zeros((n_tokens, LANES))
        starts = range(lane0, LANES, period) if period else (lane0,)
        for s0 in starts:
            for base, pos in ((s0, row), (s0 + width, col)):
                ang = pos[:, None] * inv[None, :]
                c, sn = np.cos(ang), np.sin(ang)
                cos[:, base:base + p] = c
                cos[:, base + p:base + 2 * p] = c
                sin[:, base:base + p] = -sn
                sin[:, base + p:base + 2 * p] = sn
        return cos, sin

    qk_cos, qk_sin = axis_tables(D_ROPE // 2, D_NOPE, 0, 1.0)
    ret_cos, ret_sin = axis_tables(D_RET // 2, 0, D_RET, 1.0)
    as_f32 = lambda *a: tuple(jnp.asarray(x, F32) for x in a)
    return as_f32(qk_cos, qk_sin), as_f32(ret_cos, ret_sin)


def _head_sums(x):
    r = lax.broadcasted_iota(jnp.int32, (2 * HEAD_SLOT, 2 * HEAD_SLOT), 0) // HEAD_SLOT
    c = lax.broadcasted_iota(jnp.int32, (2 * HEAD_SLOT, 2 * HEAD_SLOT), 1) // HEAD_SLOT
    ones_bd = jnp.where(r == c, 1.0, 0.0).astype(BF16)
    xb = x.astype(BF16)
    w = 2 * HEAD_SLOT
    return jnp.concatenate([_dot(xb[:, g * w:(g + 1) * w], ones_bd) for g in range(x.shape[1] // w)], axis=1)


def _tile_heads(v):
    return jnp.concatenate([v] * H_MLA, axis=1)


def _kv_heads(ckv_n, krp, krp_sw, wkv_ref, kgain, kgain_sw, tabs, k_ref, v_ref):
    kvp = _dot(ckv_n.astype(BF16), wkv_ref[...])
    kn = kvp[:, :H_MLA * HEAD_SLOT]
    ms = _head_sums(kn * kn + _tile_heads(krp * krp)) * (1.0 / D_QK)
    r = lax.rsqrt(ms + EPS)
    k = (kn + _tile_heads(krp)) * r * _tile_heads(kgain)
    if tabs is not None:
        cos, sin = tabs
        k = k * _tile_heads(cos) + r * _tile_heads(krp_sw * kgain_sw * sin)
    k_ref[...] = k.astype(BF16)
    v_lane = lax.broadcasted_iota(jnp.int32, (1, H_MLA * HEAD_SLOT), 1) % HEAD_SLOT
    v_ref[...] = (kvp[:, H_MLA * HEAD_SLOT:] + jnp.where(v_lane == D_V, 1.0, 0.0)).astype(BF16)


def _inproj_kernel(rope, *refs):
    (x_ref, mod_ref, n1g_ref, w1_ref, qng_ref, wuq_ref, kvg_ref, wkv_ref, qgain_ref,
     kgain_ref) = refs[:10]
    refs = refs[10:]
    if rope:
        qk_tabs = tuple(r[...] for r in refs[:2])
        ret_cos, ret_sin = (r[...] for r in refs[2:4])
        refs = refs[4:]
    else:
        qk_tabs = None
    ckvn_ref, krp_ref, q_ref, k_ref, v_ref, rq_ref, rk_ref, rv_ref, rg_ref = refs

    mod = mod_ref[0]
    h = _rms(x_ref[...], n1g_ref[...]) * (1.0 + mod[1:2]) + mod[0:1]
    z = _dot(h.astype(BF16), w1_ref[...])
    o = 0
    cq = z[:, o:o + Q_RANK]; o += Q_RANK
    ckv = z[:, o:o + KV_RANK]; o += KV_RANK
    krp = z[:, o:o + LANES]; o += LANES
    rq = z[:, o:o + D_RET_ALL]; o += D_RET_ALL
    rk = z[:, o:o + D_RET_ALL]; o += D_RET_ALL
    rv = z[:, o:o + D_RET_ALL]; o += D_RET_ALL
    rg = z[:, o:o + D_RET_ALL]; o += D_RET_ALL
    krp_sw = None
    if rope:
        krp_sw = z[:, o:o + LANES]; o += LANES
        rq_sw = z[:, o:o + D_RET_ALL]; o += D_RET_ALL
        rk_sw = z[:, o:o + D_RET_ALL]

    ckv_n = _rms(ckv, kvg_ref[...])
    ckvn_ref[...] = ckv_n
    krp_ref[...] = krp

    qn = _rms(cq, qng_ref[...]).astype(BF16)
    scale = D_QK ** -0.5 * LOG2_E
    q = _dot(qn, wuq_ref[:, :H_MLA * HEAD_SLOT])
    r = lax.rsqrt(_head_sums(q * q) * (1.0 / D_QK) + EPS)
    q = q * r * _tile_heads(qgain_ref[0:1] * scale)
    if rope:
        cos, sin = qk_tabs
        q_sw = _dot(qn, wuq_ref[:, H_MLA * HEAD_SLOT:])
        q = q * _tile_heads(cos) + q_sw * r * _tile_heads(qgain_ref[1:2] * scale * sin)
    q_ref[...] = q.astype(BF16)

    _kv_heads(ckv_n, krp, krp_sw, wkv_ref, kgain_ref[0:1], kgain_ref[1:2], qk_tabs, k_ref, v_ref)

    for t in range(D_RET_ALL // LANES):
        sl = slice(t * LANES, (t + 1) * LANES)
        rq_t, rk_t = rq[:, sl], rk[:, sl]
        if rope:
            rq_t = rq_t * ret_cos + rq_sw[:, sl] * ret_sin
            rk_t = rk_t * ret_cos + rk_sw[:, sl] * ret_sin
        rq_ref[:, sl] = rq_t.astype(BF16)
        rk_ref[:, sl] = (rk_t * (D_RET ** -0.5)).astype(BF16)
    rv_ref[...] = rv.astype(BF16)
    rg_ref[...] = _silu(rg).astype(BF16)


def _const_spec(shape):
    return pl.BlockSpec(shape, lambda i: (0,) * len(shape))


def _inproj(x, mod, mod_row0, tokens_per_row, wts, rope_tabs, tm):
    T = x.shape[0]
    bpr = tokens_per_row // tm
    rope = rope_tabs is not None
    tok = lambda w: pl.BlockSpec((tm, w), lambda i: (i, 0))
    in_specs = [tok(D_MODEL),
                pl.BlockSpec((1, 6, D_MODEL), lambda i: (mod_row0 + i // bpr, 0, 0)),
                _const_spec((1, D_MODEL)),
                _const_spec((D_MODEL, D_IN_PAD + (LANES + 2 * D_RET_ALL if rope else 0))),
                _const_spec((1, Q_RANK)),
                _const_spec((Q_RANK, 2 * H_MLA * HEAD_SLOT)),
                _const_spec((1, KV_RANK)),
                _const_spec((KV_RANK, 2 * H_MLA * HEAD_SLOT)),
                _const_spec((2, LANES)),
                _const_spec((2, LANES))]
    args = [x, mod, wts["norm1_g"], wts["w1_rope" if rope else "w1"], wts["q_norm_g"], wts["w_uq"],
            wts["kv_norm_g"], wts["w_kv"], wts["q_gain"], wts["k_gain"]]
    if rope:
        qk_tabs, ret_tabs = rope_tabs
        nb = qk_tabs[0].shape[0] // tm
        in_specs += [pl.BlockSpec((tm, LANES), lambda i: (i % nb, 0))] * 4
        args += list(qk_tabs) + list(ret_tabs)
    widths = [(KV_RANK, F32), (LANES, F32), (H_MLA * HEAD_SLOT, BF16), (H_MLA * HEAD_SLOT, BF16),
              (H_MLA * HEAD_SLOT, BF16)] + [(D_RET_ALL, BF16)] * 4
    return pl.pallas_call(
        functools.partial(_inproj_kernel, rope),
        grid=(T // tm,),
        in_specs=in_specs,
        out_specs=[tok(w) for w, _ in widths],
        out_shape=[jax.ShapeDtypeStruct((T, w), dt) for w, dt in widths],
        compiler_params=_cparams("parallel"),
        name="inproj_rope" if rope else "inproj",
    )(*args)


def _ctx_kv_kernel(ckv_ref, krp_ref, wkv_ref, kgain_ref, k_ref, v_ref):
    _kv_heads(ckv_ref[...], krp_ref[...], None, wkv_ref, kgain_ref[0:1], None, None, k_ref, v_ref)


def _ctx_kv(ckv, krp, wts, tm):
    T = ckv.shape[0]
    tok = lambda w: pl.BlockSpec((tm, w), lambda i: (i, 0))
    return pl.pallas_call(
        _ctx_kv_kernel,
        grid=(T // tm,),
        in_specs=[tok(KV_RANK), tok(LANES),
                  _const_spec((KV_RANK, 2 * H_MLA * HEAD_SLOT)), _const_spec((2, LANES))],
        out_specs=[tok(H_MLA * HEAD_SLOT), tok(H_MLA * HEAD_SLOT)],
        out_shape=[jax.ShapeDtypeStruct((T, H_MLA * HEAD_SLOT), BF16),
                   jax.ShapeDtypeStruct((T, H_MLA * HEAD_SLOT), BF16)],
        compiler_params=_cparams("parallel"),
        name="ctx_kv",
    )(ckv, krp, wts["w_kv"], wts["k_gain"])


ATTN_KEY_CHUNK = 2048
ATTN_HEADS_PER_STEP = 4


def _attn_kernel(has_ctx, *refs):
    if has_ctx:
        q_ref, k_ref, v_ref, kc_ref, vc_ref, o_ref = refs
    else:
        q_ref, k_ref, v_ref, o_ref = refs
    sources = [(k_ref, v_ref)] + ([(kc_ref, vc_ref)] if has_ctx else [])
    chunks = []
    for kr, vr in sources:
        n = kr.shape[1]
        kc = _pick(n, (ATTN_KEY_CHUNK, 256, 128))
        chunks += [(kr, vr, slice(c * kc, (c + 1) * kc)) for c in range(n // kc)]
    tq = q_ref.shape[1]
    outs = []
    for j in range(ATTN_HEADS_PER_STEP):
        hs = slice(j * HEAD_SLOT, (j + 1) * HEAD_SLOT)
        q = q_ref[0, :, hs]
        m = jnp.full((tq, 1), -jnp.inf, F32)
        acc = jnp.zeros((tq, HEAD_SLOT), F32)
        for kr, vr, rows in chunks:
            s = _dot_nt(q, kr[0, rows, hs])
            m_new = jnp.maximum(m, jnp.max(s, axis=-1, keepdims=True))
            p = jnp.exp2(s - m_new).astype(BF16)
            acc = acc * jnp.exp2(m - m_new) + _dot(p, vr[0, rows, hs])
            m = m_new
        outs.append(acc[:, :D_V] / acc[:, D_V:D_V + 1])
    o_ref[0] = jnp.concatenate(outs, axis=-1).astype(BF16)


def _attention(q, k, v, kc, vc, tq):
    B, L, _ = q.shape
    has_ctx = kc is not None
    hps = ATTN_HEADS_PER_STEP
    in_specs = [pl.BlockSpec((1, tq, hps * HEAD_SLOT), lambda b, hp, i: (b, i, hp)),
                pl.BlockSpec((1, L, hps * HEAD_SLOT), lambda b, hp, i: (b, 0, hp)),
                pl.BlockSpec((1, L, hps * HEAD_SLOT), lambda b, hp, i: (b, 0, hp))]
    args = [q, k, v]
    if has_ctx:
        Lc = kc.shape[1]
        in_specs += [pl.BlockSpec((1, Lc, hps * HEAD_SLOT), lambda b, hp, i: (b, 0, hp)),
                     pl.BlockSpec((1, Lc, hps * HEAD_SLOT), lambda b, hp, i: (b, 0, hp))]
        args += [kc, vc]
    return pl.pallas_call(
        functools.partial(_attn_kernel, has_ctx),
        grid=(B, H_MLA // hps, L // tq),
        in_specs=in_specs,
        out_specs=pl.BlockSpec((1, tq, hps * D_V), lambda b, hp, i: (b, i, hp)),
        out_shape=jax.ShapeDtypeStruct((B, L, H_MLA * D_V), BF16),
        compiler_params=_cparams("parallel", "parallel", "arbitrary"),
        name="attn_ctx" if has_ctx else "attn",
    )(*args)


RET_UNROLL = 8
RET_SHORT_SEQ = 1024


def _ret_pair(has_s0, want_state, n_chunks, pair, hp, *refs):
    rq_ref, rk_ref, rv_ref, rg_ref, decf_ref, decb_ref, g_ref = refs[:7]
    refs = refs[7:]
    if has_s0:
        s0f_ref, s0b_ref = refs[:2]
        refs = refs[2:]
    o_ref = refs[0]
    refs = refs[1:]
    if want_state:
        sf_ref, sb_ref = refs[:2]
        refs = refs[2:]
    kv_ref, sp_ref = refs

    C = RET_CHUNK
    W = 2 * D_RET
    ls = slice(pair * W, (pair + 1) * W)
    lane = lax.broadcasted_iota(jnp.int32, (1, W), 1)
    lane_h1 = lane >= D_RET
    row_h1 = lax.broadcasted_iota(jnp.int32, (W, 1), 0) >= D_RET
    blockdiag = row_h1 == lane_h1
    pos = lax.broadcasted_iota(jnp.int32, (C, 1), 0).astype(F32)
    diff = (lax.broadcasted_iota(jnp.int32, (C, C), 0)
            - lax.broadcasted_iota(jnp.int32, (C, C), 1)).astype(F32)

    lg = {}
    for name, ref in (("f", decf_ref), ("b", decb_ref)):
        a = -jnp.exp(ref[pl.ds(2 * hp, 1), :])
        b = -jnp.exp(ref[pl.ds(2 * hp + 1, 1), :])
        lg[name] = (a[:, :1], b[:, :1], jnp.where(lane_h1, b, a))
    lgf0, lgf1, lgf = lg["f"]
    lgb0, lgb1, lgb = lg["b"]
    qdf, kdf = jnp.exp(lgf * (pos + 1.0)), jnp.exp(lgf * (C - 1.0 - pos))
    qdb, kdb = jnp.exp(lgb * (C - pos)), jnp.exp(lgb * pos)
    row_h1_full = lax.broadcasted_iota(jnp.int32, (W, W), 0) >= D_RET
    cdf = jnp.where(row_h1_full, jnp.exp(lgf1 * C), jnp.exp(lgf0 * C))
    cdb = jnp.where(row_h1_full, jnp.exp(lgb1 * C), jnp.exp(lgb0 * C))

    def intra_decay(lf, lb):
        return jnp.where(diff >= 0, jnp.exp(lf * jnp.maximum(diff, 0.0)), jnp.exp(lb * jnp.maximum(-diff, 0.0)))

    dmat0, dmat1 = intra_decay(lgf0, lgb0), intra_decay(lgf1, lgb1)
    g = g_ref[...]

    def chunk_rows(n):
        return pl.ds(pl.multiple_of(n * C, C), C)

    def phase_a(n, carry):
        rows = chunk_rows(n)
        k = rk_ref[0, rows, ls].astype(F32)
        k2 = jnp.concatenate([(k * kdf).astype(BF16), (k * kdb).astype(BF16)], axis=1)
        kv_ref[n] = _dot_tn(k2, rv_ref[0, rows, ls])
        return carry

    unroll = min(RET_UNROLL, n_chunks)
    lax.fori_loop(0, n_chunks, phase_a, 0, unroll=unroll)

    def init_state(ref):
        if not has_s0:
            return jnp.zeros((W, W), F32)
        z = jnp.zeros((D_RET, D_RET), F32)
        return jnp.concatenate([jnp.concatenate([ref[0, 2 * pair], z], axis=1),
                                jnp.concatenate([z, ref[0, 2 * pair + 1]], axis=1)], axis=0)

    def scan_f(n, S):
        sp_ref[n, 0:W, :] = S.astype(BF16)
        return cdf * S + jnp.where(blockdiag, kv_ref[n, 0:W, :], 0.0)

    def scan_b(i, S):
        n = n_chunks - 1 - i
        sp_ref[n, W:2 * W, :] = S.astype(BF16)
        return cdb * S + jnp.where(blockdiag, kv_ref[n, W:2 * W, :], 0.0)

    Sf = lax.fori_loop(0, n_chunks, scan_f, init_state(s0f_ref if has_s0 else None))
    Sb = lax.fori_loop(0, n_chunks, scan_b, init_state(s0b_ref if has_s0 else None))
    if want_state:
        for S, ref in ((Sf, sf_ref), (Sb, sb_ref)):
            ref[0, 2 * pair] = S[0:D_RET, 0:D_RET]
            ref[0, 2 * pair + 1] = S[D_RET:W, D_RET:W]

    def phase_c(n, carry):
        rows = chunk_rows(n)
        q = rq_ref[0, rows, ls]
        k = rk_ref[0, rows, ls]
        v = rv_ref[0, rows, ls]
        zero = jnp.zeros_like(q)
        q_heads = jnp.concatenate([jnp.where(lane_h1, zero, q), jnp.where(lane_h1, q, zero)], axis=0)
        sc = _dot_nt(q_heads, k)
        p = jnp.concatenate([(sc[:C] * dmat0).astype(BF16), (sc[C:] * dmat1).astype(BF16)], axis=1)
        v_bd = jnp.concatenate([jnp.where(lane_h1, zero, v), jnp.where(lane_h1, v, zero)], axis=0)
        qf = q.astype(F32)
        q2 = jnp.concatenate([(qf * qdf).astype(BF16), (qf * qdb).astype(BF16)], axis=1)
        o = _dot(p, v_bd) + _dot(q2, sp_ref[n])
        o2 = o * o
        ss0 = jnp.sum(jnp.where(lane_h1, 0.0, o2), axis=-1, keepdims=True)
        ss1 = jnp.sum(jnp.where(lane_h1, o2, 0.0), axis=-1, keepdims=True)
        ms = jnp.where(lane_h1, ss1, ss0) * (1.0 / D_RET)
        o = o * lax.rsqrt(ms + EPS) * g * rg_ref[0, rows, ls].astype(F32)
        o_ref[0, rows, ls] = o.astype(BF16)
        return carry

    lax.fori_loop(0, n_chunks, phase_c, 0, unroll=unroll)


def _ret_kernel(has_s0, want_state, n_chunks, pairs, *refs):
    for pair in range(pairs):
        _ret_pair(has_s0, want_state, n_chunks, pair, pl.program_id(1) * pairs + pair, *refs)


def _retention(rq, rk, rv, rg, dec_f, dec_b, g, s0f, s0b, want_state):
    B, L, _ = rq.shape
    has_s0 = s0f is not None
    pairs = H_RET // 2 if L <= RET_SHORT_SEQ else 1
    seq = pl.BlockSpec((1, L, pairs * LANES), lambda b, hp: (b, 0, hp))
    st = pl.BlockSpec((1, 2 * pairs, D_RET, D_RET), lambda b, hp: (b, hp, 0, 0))
    cst = lambda shape: pl.BlockSpec(shape, lambda b, hp: (0,) * len(shape))
    in_specs = [seq] * 4 + [cst((H_RET, LANES)), cst((H_RET, LANES)), cst((1, LANES))]
    args = [rq, rk, rv, rg, dec_f, dec_b, g]
    if has_s0:
        in_specs += [st, st]
        args += [s0f, s0b]
    out_specs = [seq]
    out_shape = [jax.ShapeDtypeStruct((B, L, D_RET_ALL), BF16)]
    if want_state:
        out_specs += [st, st]
        out_shape += [jax.ShapeDtypeStruct((B, H_RET, D_RET, D_RET), F32)] * 2
    return pl.pallas_call(
        functools.partial(_ret_kernel, has_s0, want_state, L // RET_CHUNK, pairs),
        grid=(B, H_RET // 2 // pairs),
        in_specs=in_specs,
        out_specs=out_specs,
        out_shape=out_shape,
        scratch_shapes=[pltpu.VMEM((L // RET_CHUNK, 4 * D_RET, LANES), F32),
                        pltpu.VMEM((L // RET_CHUNK, 4 * D_RET, LANES), BF16)],
        compiler_params=_cparams("parallel", "parallel"),
        name="retention_s0" if has_s0 else "retention",
    )(*args)


def _pack_pairs(x):
    w = x.shape[1] // 2
    hi = lax.bitcast_convert_type(x[:, :w].astype(BF16).astype(F32), jnp.uint32)
    lo = lax.bitcast_convert_type(x[:, w:].astype(BF16).astype(F32), jnp.uint32)
    return hi | (lo >> 16)


def _unpack_pairs(p):
    hi = lax.bitcast_convert_type(p & jnp.uint32(0xFFFF0000), F32)
    lo = lax.bitcast_convert_type(p << 16, F32)
    return jnp.concatenate([hi, lo], axis=1)


D_PACKED = D_MODEL // 2


def _outproj_kernel(omla_ref, oret_ref, x_ref, mod_ref, wo_ref, n2g_ref, rwh_ref, rwl_ref, rb_ref, tri_ref,
                    x2_ref, h2p_ref, code_ref, rank_ref, wk_ref, cnt_ref):
    @pl.when(pl.program_id(0) == 0)
    def _():
        cnt_ref[...] = jnp.zeros_like(cnt_ref)

    mod = mod_ref[0]
    half = H_MLA * D_V
    a = _dot(omla_ref[...], wo_ref[:half, :]) + _dot(oret_ref[...], wo_ref[half:, :])
    x2 = x_ref[...] + mod[2:3] * a
    x2_ref[...] = x2
    h2 = _rms(x2, n2g_ref[...]) * (1.0 + mod[4:5]) + mod[3:4]
    h2p_ref[...] = _pack_pairs(h2)
    h2_hi = h2.astype(BF16)
    h2_lo = (h2 - h2_hi.astype(F32)).astype(BF16)
    rw_hi = rwh_ref[...]
    logits = _dot_nt(rw_hi, h2_hi) + _dot_nt(rw_hi, h2_lo) + _dot_nt(rwl_ref[...], h2_hi)
    scores = jax.nn.sigmoid(logits)
    tm = scores.shape[1]
    expert = lax.broadcasted_iota(jnp.int32, scores.shape, 0)
    cand = scores + rb_ref[...]
    code = jnp.zeros(scores.shape, jnp.int32)
    for k in range(TOP_K):
        mx = jnp.max(cand, axis=0, keepdims=True)
        first = jnp.min(jnp.where(cand == mx, expert, LANES), axis=0, keepdims=True)
        hit = expert == first
        code = jnp.where(hit, k + 1, code)
        cand = jnp.where(hit, -jnp.inf, cand)
    picked = code > 0
    sel = jnp.where(picked, scores, 0.0)
    gates = sel * (ROUTED_SCALE / jnp.sum(sel, axis=0, keepdims=True))
    code_ref[...] = code

    onehot = jnp.where(picked, 1.0, 0.0)
    carry = cnt_ref[...].astype(F32)
    rank = _dot(onehot.astype(BF16), tri_ref[...])
    rank_ref[...] = rank + jnp.concatenate([carry] * (tm // LANES), axis=1)
    cnt_ref[...] = (carry + jnp.sum(onehot, axis=1, keepdims=True)).astype(jnp.int32)

    row = lax.broadcasted_iota(jnp.int32, scores.shape, 0)
    wk_t = jnp.zeros(scores.shape, F32)
    for k in range(TOP_K):
        w = jnp.sum(jnp.where(code == k + 1, gates, 0.0), axis=0, keepdims=True)
        wk_t = jnp.where(row == k, w, wk_t)
    wk_ref[...] = wk_t.T


def _outproj(o_mla, o_ret, x, mod, mod_row0, tokens_per_row, wts, tm):
    T = x.shape[0]
    bpr = tokens_per_row // tm
    tok = lambda w: pl.BlockSpec((tm, w), lambda i: (i, 0))
    emaj = pl.BlockSpec((LANES, tm), lambda i: (0, i))
    tri = jnp.triu(jnp.ones((tm, tm), BF16), 1)
    bias = jnp.broadcast_to(wts["router_bias"].reshape(LANES, 1), (LANES, tm))
    return pl.pallas_call(
        _outproj_kernel,
        grid=(T // tm,),
        in_specs=[tok(H_MLA * D_V), tok(D_RET_ALL), tok(D_MODEL),
                  pl.BlockSpec((1, 6, D_MODEL), lambda i: (mod_row0 + i // bpr, 0, 0)),
                  _const_spec((D_MODEL, D_MODEL)), _const_spec((1, D_MODEL)),
                  _const_spec((LANES, D_MODEL)), _const_spec((LANES, D_MODEL)), _const_spec((LANES, tm)),
                  _const_spec((tm, tm))],
        out_specs=[tok(D_MODEL), tok(D_PACKED), emaj, emaj, tok(LANES), _const_spec((LANES, LANES))],
        out_shape=[jax.ShapeDtypeStruct((T, D_MODEL), F32),
                   jax.ShapeDtypeStruct((T, D_PACKED), jnp.uint32),
                   jax.ShapeDtypeStruct((LANES, T), jnp.int32),
                   jax.ShapeDtypeStruct((LANES, T), F32),
                   jax.ShapeDtypeStruct((T, LANES), F32),
                   jax.ShapeDtypeStruct((LANES, LANES), jnp.int32)],
        compiler_params=_cparams("arbitrary"),
        name="outproj_router",
    )(o_mla, o_ret, x, mod, wts["w_o"], wts["norm2_g"], wts["router_hi"], wts["router_lo"], bias, tri)


def _plan_kernel(code_ref, rank_ref, first_ref, pos_ref):
    code = code_ref[...]
    tm = code.shape[1]
    row = rank_ref[...] + jnp.concatenate([first_ref[...]] * (tm // LANES), axis=1)
    krow = lax.broadcasted_iota(jnp.int32, pos_ref.shape, 0)
    pos = jnp.zeros(pos_ref.shape, F32)
    for k in range(TOP_K):
        pos = jnp.where(krow == k, jnp.sum(jnp.where(code == k + 1, row, 0.0), axis=0, keepdims=True), pos)
    pos_ref[...] = pos.astype(jnp.int32)


def _plan(code, rank, first_rows, tm):
    T = code.shape[1]
    first = jnp.broadcast_to(jnp.pad(first_rows.astype(F32), (0, LANES - N_EXPERTS)).reshape(LANES, 1),
                             (LANES, LANES))
    emaj = pl.BlockSpec((LANES, tm), lambda i: (0, i))
    return pl.pallas_call(
        _plan_kernel,
        grid=(T // tm,),
        in_specs=[emaj, emaj, _const_spec((LANES, LANES))],
        out_specs=pl.BlockSpec((8, tm), lambda i: (0, i)),
        out_shape=jax.ShapeDtypeStruct((8, T), jnp.int32),
        compiler_params=_cparams("parallel"),
        name="moe_plan",
    )(code, rank, first)


SC_WORKERS = 32
SC_CHUNK = 64


def _sc_worker_id():
    return lax.axis_index("s") * 2 + lax.axis_index("c")


def _sc_mesh():
    return plsc.VectorSubcoreMesh(core_axis_name="c", subcore_axis_name="s")


def _sc_gather_rows(table, idx):
    N = idx.shape[0]
    W = table.shape[1]
    per_w = N // SC_WORKERS
    n_chunks = per_w // SC_CHUNK
    assert N == SC_WORKERS * n_chunks * SC_CHUNK and n_chunks % 2 == 0

    @functools.partial(
        pl.kernel, mesh=_sc_mesh(), out_type=jax.ShapeDtypeStruct((N, W), table.dtype),
        scratch_types=[pltpu.VMEM((n_chunks, SC_CHUNK), jnp.int32),
                       pltpu.VMEM((2, SC_CHUNK, W), table.dtype),
                       pltpu.SemaphoreType.DMA((2,)), pltpu.SemaphoreType.DMA((2,))],
        name="sc_gather_rows")
    def k(table_hbm, idx_hbm, out_hbm, idx_v, rows_v, gsem, wsem):
        wid = _sc_worker_id()
        base = wid * per_w
        pltpu.sync_copy(idx_hbm.at[wid], idx_v)

        def gather(j, b):
            return pltpu.make_async_copy(table_hbm.at[idx_v.at[j]], rows_v.at[b], gsem.at[b])

        def writeback(j, b):
            off = pl.multiple_of(base + j * SC_CHUNK, SC_CHUNK)
            return pltpu.make_async_copy(rows_v.at[b], out_hbm.at[pl.ds(off, SC_CHUNK)], wsem.at[b])

        gather(0, 0).start()

        @pl.loop(0, n_chunks, step=2)
        def _(j):
            for b in range(2):
                jj = j + b

                @pl.when(jj + 1 < n_chunks)
                def _():
                    @pl.when(jj >= 1)
                    def _():
                        writeback(jj - 1, 1 - b).wait()
                    gather(jj + 1, 1 - b).start()

                gather(jj, b).wait()
                writeback(jj, b).start()

        writeback(n_chunks - 2, 0).wait()
        writeback(n_chunks - 1, 1).wait()

    return k(table, idx.reshape(SC_WORKERS, n_chunks, SC_CHUNK))


def _sc_dispatch_rows(rows_list, pos_list, n_out):
    W = rows_list[0].shape[1]
    K = pos_list[0].shape[0]
    n_src = len(rows_list)
    plan = []
    idx_list = []
    for rows, pos in zip(rows_list, pos_list):
        T = rows.shape[0]
        per_w = T // SC_WORKERS
        n_chunks = per_w // SC_CHUNK
        assert T == SC_WORKERS * n_chunks * SC_CHUNK and n_chunks % 2 == 0
        plan.append((per_w, n_chunks))
        idx_list.append(pos.reshape(K, SC_WORKERS, n_chunks, SC_CHUNK).transpose(1, 2, 0, 3))
    max_chunks = max(n for _, n in plan)

    @functools.partial(
        pl.kernel, mesh=_sc_mesh(), out_type=jax.ShapeDtypeStruct((n_out, W), rows_list[0].dtype),
        scratch_types=[pltpu.VMEM((max_chunks, K, SC_CHUNK), jnp.int32),
                       pltpu.VMEM((2, SC_CHUNK, W), rows_list[0].dtype),
                       pltpu.SemaphoreType.DMA((2,)), pltpu.SemaphoreType.DMA((2,))],
        name="sc_dispatch_rows")
    def k(*refs):
        rows_refs, idx_refs = refs[:n_src], refs[n_src:2 * n_src]
        out_hbm, idx_v, rows_v, lsem, ssem = refs[2 * n_src:]
        wid = _sc_worker_id()
        for rows_hbm, idx_hbm, (per_w, n_chunks) in zip(rows_refs, idx_refs, plan):
            base = wid * per_w
            pltpu.sync_copy(idx_hbm.at[wid], idx_v.at[pl.ds(0, n_chunks)])

            def load(j, b):
                off = pl.multiple_of(base + j * SC_CHUNK, SC_CHUNK)
                return pltpu.make_async_copy(rows_hbm.at[pl.ds(off, SC_CHUNK)], rows_v.at[b], lsem.at[b])

            def scatter(j, b, kk):
                return pltpu.make_async_copy(rows_v.at[b], out_hbm.at[idx_v.at[j, kk]], ssem.at[b])

            load(0, 0).start()

            @pl.loop(0, n_chunks, step=2)
            def _(j):
                for b in range(2):
                    jj = j + b

                    @pl.when(jj + 1 < n_chunks)
                    def _():
                        @pl.when(jj >= 1)
                        def _():
                            for kk in range(K):
                                scatter(jj - 1, 1 - b, kk).wait()
                        load(jj + 1, 1 - b).start()

                    load(jj, b).wait()
                    for kk in range(K):
                        scatter(jj, b, kk).start()

            for kk in range(K):
                scatter(n_chunks - 2, 0, kk).wait()
            for kk in range(K):
                scatter(n_chunks - 1, 1, kk).wait()

    return k(*rows_list, *idx_list)


ROW_TILE = 1024
MOE_GATHER_GROUP = 8192


def _experts_kernel(te_ref, tv_ref, tb_ref, xs_ref, wg_ref, wu_ref, wd_ref, ys_ref, wg_s, wu_s, wd_s):
    del tb_ref
    i = pl.program_id(0)
    valid = tv_ref[i]

    @pl.when(jnp.logical_or(i == 0, te_ref[i] != te_ref[jnp.maximum(i - 1, 0)]))
    def _():
        wg_s[...] = wg_ref[0].astype(BF16)
        wu_s[...] = wu_ref[0].astype(BF16)
        wd_s[...] = wd_ref[0].astype(BF16)

    @pl.when(valid > 0)
    def _():
        row = lax.broadcasted_iota(jnp.int32, (ROW_TILE, 1), 0)
        p = jnp.where(row < valid, xs_ref[...], jnp.uint32(0))
        x = _unpack_pairs(p).astype(BF16)
        act = _silu(_dot(x, wg_s[...])) * _dot(x, wu_s[...])
        ys_ref[...] = _pack_pairs(_dot(act.astype(BF16), wd_s[...]))


def _experts(xs, tile_expert, tile_valid, tile_block, wts):
    n_tiles = tile_expert.shape[0]
    ex = lambda a, b: pl.BlockSpec((1, a, b), lambda i, te, tv, tb: (te[i], 0, 0))
    row = pl.BlockSpec((ROW_TILE, D_PACKED), lambda i, te, tv, tb: (tb[i], 0))
    return pl.pallas_call(
        _experts_kernel,
        grid_spec=pltpu.PrefetchScalarGridSpec(
            num_scalar_prefetch=3, grid=(n_tiles,),
            in_specs=[row, ex(D_MODEL, D_EXPERT), ex(D_MODEL, D_EXPERT), ex(D_EXPERT, D_MODEL)],
            out_specs=row,
            scratch_shapes=[pltpu.VMEM((D_MODEL, D_EXPERT), BF16), pltpu.VMEM((D_MODEL, D_EXPERT), BF16),
                            pltpu.VMEM((D_EXPERT, D_MODEL), BF16)]),
        out_shape=jax.ShapeDtypeStruct(xs.shape, jnp.uint32),
        compiler_params=_cparams("arbitrary"),
        name="experts",
    )(tile_expert, tile_valid, tile_block, xs, wts["exp_gate"], wts["exp_up"], wts["exp_down"])


def _combine_kernel(h2p_ref, ys_ref, wk_ref, x2_ref, mod_ref, shg_ref, shu_ref, shd_ref, *rest):
    y_ref = rest[-1]
    t = _unpack_pairs(h2p_ref[...]).astype(BF16)
    act = _silu(_dot(t, shg_ref[...])) * _dot(t, shu_ref[...])
    acc = _dot(act.astype(BF16), shd_ref[...])
    wk = wk_ref[...]
    for k in range(TOP_K):
        acc = acc + wk[:, k:k + 1] * _unpack_pairs(ys_ref[k])
    y_ref[...] = x2_ref[...] + mod_ref[0][5:6] * acc


def _combine(h2p, ys, row0, wk, x2, y_prev, mod, mod_row0, tokens_per_row, wts, tm):
    T = h2p.shape[0]
    G = ys.shape[1]
    bpr = tokens_per_row // tm
    blk0 = row0 // tm
    tok = lambda w: pl.BlockSpec((tm, w), lambda i: (blk0 + i, 0))
    in_specs = [tok(D_PACKED), pl.BlockSpec((TOP_K, tm, D_PACKED), lambda i: (0, i, 0)), tok(LANES),
                tok(D_MODEL), pl.BlockSpec((1, 6, D_MODEL), lambda i: (mod_row0 + (blk0 + i) // bpr, 0, 0)),
                _const_spec((D_MODEL, D_SHARED)), _const_spec((D_MODEL, D_SHARED)),
                _const_spec((D_SHARED, D_MODEL))]
    args = [h2p, ys, wk, x2, mod, wts["sh_gate"], wts["sh_up"], wts["sh_down"]]
    aliases = {}
    if y_prev is not None:
        aliases = {len(args): 0}
        in_specs.append(pl.BlockSpec(memory_space=pl.ANY))
        args.append(y_prev)
    return pl.pallas_call(
        _combine_kernel,
        grid=(G // tm,),
        in_specs=in_specs,
        out_specs=tok(D_MODEL),
        out_shape=jax.ShapeDtypeStruct((T, D_MODEL), F32),
        input_output_aliases=aliases,
        compiler_params=_cparams("parallel"),
        name="moe_combine",
    )(*args)


def _moe(halves, mod, wts):
    routed = [h["routed"] for h in halves]
    sizes = [r[1].shape[0] for r in routed]
    n_tiles = (TOP_K * sum(sizes)) // ROW_TILE + N_EXPERTS
    path_counts = [r[5][:N_EXPERTS, 0] for r in routed]
    counts = sum(path_counts)
    padded = ((counts + ROW_TILE - 1) // ROW_TILE) * ROW_TILE
    ends = jnp.cumsum(padded)
    starts = ends - padded
    tile_start = jnp.arange(n_tiles, dtype=jnp.int32) * ROW_TILE
    tile_expert = jnp.minimum(jnp.sum(tile_start[:, None] >= ends[None, :], axis=1), N_EXPERTS - 1).astype(jnp.int32)
    overlap = (jnp.minimum(tile_start[:, None] + ROW_TILE, (starts + counts)[None, :])
               - jnp.maximum(tile_start[:, None], starts[None, :]))
    tile_valid = jnp.sum(jnp.maximum(overlap, 0), axis=1).astype(jnp.int32)

    pos, first = [], starts
    for h, r, c in zip(halves, routed, path_counts):
        pos.append(_plan(r[2], r[3], first, h["tm"])[:TOP_K])
        first = first + c
    xs = _sc_dispatch_rows([r[1] for r in routed], pos, n_tiles * ROW_TILE)
    tile_block = jnp.minimum(jnp.arange(n_tiles, dtype=jnp.int32), ends[-1] // ROW_TILE - 1).astype(jnp.int32)
    ys = _experts(xs, tile_expert, tile_valid, tile_block, wts)
    outs = []
    for h, r, p in zip(halves, routed, pos):
        T = r[1].shape[0]
        G = MOE_GATHER_GROUP if T % MOE_GATHER_GROUP == 0 else T
        y = None
        for row0 in range(0, T, G):
            ysel = _sc_gather_rows(ys, p[:, row0:row0 + G].reshape(-1)).reshape(TOP_K, G, D_PACKED)
            y = _combine(r[1], ysel, row0, r[4], r[0], y, mod, h["mod_row0"], h["tokens_per_row"], wts, h["tm"])
        outs.append(y.reshape(h["shape"]))
    return outs


def _pad_heads(w, d_used):
    k = w.shape[0]
    w = w.reshape(k, H_MLA, d_used)
    return jnp.pad(w, ((0, 0), (0, 0), (0, HEAD_SLOT - d_used))).reshape(k, H_MLA * HEAD_SLOT)


def _lane_pad(v):
    return jnp.pad(v, (0, LANES - v.shape[0])).reshape(1, LANES)


def _rot_partners(w):
    rot = w[..., w.shape[-1] - D_ROPE:]
    return jnp.concatenate([jnp.zeros_like(w[..., :w.shape[-1] - D_ROPE]),
                            _swap_halves(rot, D_ROPE // 4)], axis=-1)


def _prep_weights(l, w_in, norm1_g, mla_q_norm_g, w_uq, mla_kv_norm_g, w_ukv, mla_q_gain, mla_k_gain,
                  w_o, norm2_g, router_w, router_bias, exp_w_gate, exp_w_up, exp_w_down,
                  sh_w_gate, sh_w_up, sh_w_down):
    w = w_in[l]
    a, b, c = Q_RANK, Q_RANK + KV_RANK, Q_RANK + KV_RANK + D_ROPE
    kr_slot = lambda wk: jnp.pad(wk, ((0, 0), (D_NOPE, LANES - D_QK)))
    w1 = jnp.concatenate([w[:, :b], kr_slot(w[:, b:c]), w[:, c:]], axis=1).astype(BF16)
    w_rq, w_rk = w[:, c:c + D_RET_ALL], w[:, c + D_RET_ALL:c + 2 * D_RET_ALL]
    w1_rope = jnp.concatenate([w1, kr_slot(_rot_partners(w[:, b:c])).astype(BF16),
                               _swap_halves(w_rq, D_RET // 4).astype(BF16),
                               _swap_halves(w_rk, D_RET // 4).astype(BF16)], axis=1)
    uq = w_uq[l].reshape(Q_RANK, H_MLA, D_QK)
    pad_slots = lambda u: jnp.pad(u, ((0, 0), (0, 0), (0, HEAD_SLOT - D_QK))).reshape(Q_RANK, H_MLA * HEAD_SLOT)
    gains = lambda g: jnp.concatenate([_lane_pad(g), _lane_pad(_rot_partners(g))], axis=0)
    ukv = w_ukv[l].reshape(KV_RANK, H_MLA, D_NOPE + D_V)
    w_k = _pad_heads(ukv[:, :, :D_NOPE].reshape(KV_RANK, H_MLA * D_NOPE), D_NOPE)
    w_v = _pad_heads(ukv[:, :, D_NOPE:].reshape(KV_RANK, H_MLA * D_V), D_V)
    rw = jnp.pad(router_w[l].T, ((0, LANES - N_EXPERTS), (0, 0)))
    rw_hi = rw.astype(BF16)
    rw_lo = (rw - rw_hi.astype(F32)).astype(BF16)
    rb = jnp.concatenate([router_bias[l].astype(F32), jnp.full((LANES - N_EXPERTS,), -jnp.inf, F32)])
    return dict(
        norm1_g=norm1_g[l].reshape(1, D_MODEL), w1=w1, w1_rope=w1_rope,
        q_norm_g=mla_q_norm_g[l].reshape(1, Q_RANK),
        w_uq=jnp.concatenate([pad_slots(uq), pad_slots(_rot_partners(uq))], axis=1).astype(BF16),
        kv_norm_g=mla_kv_norm_g[l].reshape(1, KV_RANK),
        w_kv=jnp.concatenate([w_k, w_v], axis=1).astype(BF16),
        q_gain=gains(mla_q_gain[l]), k_gain=gains(mla_k_gain[l]),
        w_o=w_o[l].astype(BF16), norm2_g=norm2_g[l].reshape(1, D_MODEL),
        router_hi=rw_hi, router_lo=rw_lo, router_bias=rb.reshape(1, LANES),
        exp_gate=exp_w_gate[l], exp_up=exp_w_up[l], exp_down=exp_w_down[l],
        sh_gate=sh_w_gate[l].astype(BF16), sh_up=sh_w_up[l].astype(BF16),
        sh_down=sh_w_down[l].astype(BF16))


def _pick(n, prefs):
    for p in prefs:
        if n % p == 0:
            return p
    return n


def _mixer_half(x, mod, mod_row0, wts, ret_wts, ctx):
    B, L, D = x.shape
    T = B * L
    tokens_per_row = L if ctx is not None else T
    xt = x.reshape(T, D)
    rope_tabs = None
    if ctx is not None:
        rope_tabs = _rope_tables(L)
    tm = _pick(tokens_per_row, (512, 256, 128))
    ckv_n, krp, q, k, v, rq, rk, rv, rg = _inproj(xt, mod, mod_row0, tokens_per_row, wts, rope_tabs, tm)

    seq = lambda a: a.reshape(B, L, a.shape[-1])
    kc = vc = s0f = s0b = None
    if ctx is not None:
        ckv_c, kr_c, s0f, s0b = ctx
        Lc = ckv_c.shape[1]
        krp_c = jnp.pad(kr_c, ((0, 0), (0, 0), (D_NOPE, LANES - D_NOPE - D_ROPE)))
        kc, vc = _ctx_kv(ckv_c.reshape(B * Lc, KV_RANK), krp_c.reshape(B * Lc, LANES), wts,
                         _pick(B * Lc, (512, 256, 128)))
        kc, vc = kc.reshape(B, Lc, -1), vc.reshape(B, Lc, -1)
    o_mla = _attention(seq(q), seq(k), seq(v), kc, vc, _pick(L, (1024, 512, 256, 128)))

    dec_f, dec_b, ret_g = ret_wts
    want_state = ctx is None
    ret = _retention(seq(rq), seq(rk), seq(rv), seq(rg), dec_f, dec_b, ret_g, s0f, s0b, want_state)
    o_ret = ret[0]

    routed = _outproj(o_mla.reshape(T, -1), o_ret.reshape(T, -1), xt, mod, mod_row0, tokens_per_row, wts, tm)
    new = None
    if want_state:
        new = (ckv_n.reshape(B, L, KV_RANK), krp[:, D_NOPE:D_NOPE + D_ROPE].reshape(B, L, D_ROPE),
               ret[1], ret[2])
    return dict(routed=routed, mod_row0=mod_row0, tokens_per_row=tokens_per_row, tm=tm, shape=(B, L, D)), new


def kernel(x_prompt, x_sample, cache_mla_ckv, cache_mla_krope, state_ret_fwd, state_ret_bwd, c, c_ctx,
           w_ada, b_ada, norm1_g, w_in, mla_q_norm_g, w_uq, mla_kv_norm_g, w_ukv, mla_q_gain, mla_k_gain,
           ret_decay_fwd, ret_decay_bwd, ret_norm_g, w_o, norm2_g, router_w, router_bias, exp_w_gate,
           exp_w_up, exp_w_down, sh_w_gate, sh_w_up, sh_w_down):
    depth = w_ada.shape[0]
    n_dec = c.shape[0]
    assert 1 + n_dec <= MOD_ROWS
    cond = jnp.concatenate([c_ctx[None], c, jnp.zeros((MOD_ROWS - 1 - n_dec, D_MODEL), F32)], axis=0)

    y_prompt, y_sample = x_prompt, x_sample
    ckv_l, kr_l, sf_l, sb_l = [], [], [], []
    for l in range(depth):
        wts = _prep_weights(l, w_in, norm1_g, mla_q_norm_g, w_uq, mla_kv_norm_g, w_ukv, mla_q_gain,
                            mla_k_gain, w_o, norm2_g, router_w, router_bias, exp_w_gate, exp_w_up,
                            exp_w_down, sh_w_gate, sh_w_up, sh_w_down)
        ret_wts = (jnp.broadcast_to(ret_decay_fwd[l].astype(F32)[:, None], (H_RET, LANES)),
                   jnp.broadcast_to(ret_decay_bwd[l].astype(F32)[:, None], (H_RET, LANES)),
                   jnp.tile(ret_norm_g[l].reshape(1, D_RET), (1, LANES // D_RET)))
        mod = _adaln(cond, w_ada[l], b_ada[l]).reshape(MOD_ROWS, 6, D_MODEL)
        half_p, new = _mixer_half(y_prompt, mod, 0, wts, ret_wts, None)
        ckv_l.append(new[0]); kr_l.append(new[1]); sf_l.append(new[2]); sb_l.append(new[3])
        ctx = (cache_mla_ckv[:, l], cache_mla_krope[:, l], state_ret_fwd[:, l], state_ret_bwd[:, l])
        half_s, _ = _mixer_half(y_sample, mod, 1, wts, ret_wts, ctx)
        y_prompt, y_sample = _moe([half_p, half_s], mod, wts)

    return (y_prompt, y_sample, jnp.stack(ckv_l, axis=1), jnp.stack(kr_l, axis=1),
            jnp.stack(sf_l, axis=1), jnp.stack(sb_l, axis=1))
```

```python
import functools
import math

import numpy as np
import jax
import jax.numpy as jnp
from jax import lax
from jax.experimental import pallas as pl
from jax.experimental.pallas import tpu as pltpu
from jax.experimental.pallas import tpu_sc as plsc

F32 = jnp.float32
BF16 = jnp.bfloat16

D_MODEL = 1024
GRID_W = 64
H_MLA = 8
D_NOPE = 64
D_ROPE = 32
D_QK = D_NOPE + D_ROPE
D_V = 64
Q_RANK = 256
KV_RANK = 128
H_RET = 8
D_RET = 64
RET_CHUNK = 256
D_RET_ALL = H_RET * D_RET
N_EXPERTS = 64
TOP_K = 6
D_EXPERT = 256
D_SHARED = 256
ROUTED_SCALE = 2.5
ROPE_BASE = 10000.0
EPS = 1e-6
LOG2_E = math.log2(math.e)

LANES = 128
HEAD_SLOT = LANES
D_IN_PAD = Q_RANK + KV_RANK + LANES + 4 * D_RET_ALL
MOD_ROWS = 16
VMEM_LIMIT = 56 * 1024 * 1024


def _cparams(*sem):
    return pltpu.CompilerParams(dimension_semantics=sem, vmem_limit_bytes=VMEM_LIMIT)


def _dot(a, b):
    return jnp.dot(a, b, preferred_element_type=F32)


def _dot_nt(a, b):
    return lax.dot_general(a, b, (((1,), (1,)), ((), ())), preferred_element_type=F32)


def _dot_tn(a, b):
    return lax.dot_general(a, b, (((0,), (0,)), ((), ())), preferred_element_type=F32)


def _rms(x, g):
    return x * lax.rsqrt(jnp.mean(x * x, axis=-1, keepdims=True) + EPS) * g


def _silu(x):
    return x * jax.nn.sigmoid(x)


def _adaln_kernel(c_ref, w_ref, b_ref, o_ref):
    s = _silu(c_ref[...])
    o_ref[...] = _dot(s.astype(BF16), w_ref[...].astype(BF16)) + b_ref[...]


def _adaln(cond, w_ada, b_ada):
    n_out = w_ada.shape[1]
    bn = 1536
    return pl.pallas_call(
        _adaln_kernel,
        grid=(n_out // bn,),
        in_specs=[pl.BlockSpec((MOD_ROWS, D_MODEL), lambda j: (0, 0)),
                  pl.BlockSpec((D_MODEL, bn), lambda j: (0, j)),
                  pl.BlockSpec((1, bn), lambda j: (0, j))],
        out_specs=pl.BlockSpec((MOD_ROWS, bn), lambda j: (0, j)),
        out_shape=jax.ShapeDtypeStruct((MOD_ROWS, n_out), F32),
        compiler_params=_cparams("arbitrary"),
        name="adaln",
    )(cond, w_ada, b_ada.reshape(1, n_out))


def _swap_halves(w, half_pair):
    g = w.reshape(*w.shape[:-1], w.shape[-1] // (2 * half_pair), 2, half_pair)
    return g[..., ::-1, :].reshape(w.shape)


def _rope_tables(n_tokens):
    t = np.arange(n_tokens)
    row = (t // GRID_W).astype(np.float64)
    col = (t % GRID_W).astype(np.float64)

    def axis_tables(width, lane0, period, fill):
        p = width // 2
        inv = 1.0 / (ROPE_BASE ** (np.arange(p, dtype=np.float64) / p))
        cos = np.full((n_tokens, LANES), fill)
        sin = np.zeros((n_tokens, LANES))
        starts = range(lane0, LANES, period) if period else (lane0,)
        for s0 in starts:
            for base, pos in ((s0, row), (s0 + width, col)):
                ang = pos[:, None] * inv[None, :]
                c, sn = np.cos(ang), np.sin(ang)
                cos[:, base:base + p] = c
                cos[:, base + p:base + 2 * p] = c
                sin[:, base:base + p] = -sn
                sin[:, base + p:base + 2 * p] = sn
        return cos, sin

    qk_cos, qk_sin = axis_tables(D_ROPE // 2, D_NOPE, 0, 1.0)
    ret_cos, ret_sin = axis_tables(D_RET // 2, 0, D_RET, 1.0)
    as_f32 = lambda *a: tuple(jnp.asarray(x, F32) for x in a)
    return as_f32(qk_cos, qk_sin), as_f32(ret_cos, ret_sin)


def _head_sums(x):
    r = lax.broadcasted_iota(jnp.int32, (2 * HEAD_SLOT, 2 * HEAD_SLOT), 0) // HEAD_SLOT
    c = lax.broadcasted_iota(jnp.int32, (2 * HEAD_SLOT, 2 * HEAD_SLOT), 1) // HEAD_SLOT
    ones_bd = jnp.where(r == c, 1.0, 0.0).astype(BF16)
    xb = x.astype(BF16)
    w = 2 * HEAD_SLOT
    return jnp.concatenate([_dot(xb[:, g * w:(g + 1) * w], ones_bd) for g in range(x.shape[1] // w)], axis=1)


def _tile_heads(v):
    return jnp.concatenate([v] * H_MLA, axis=1)


def _kv_heads(ckv_n, krp, krp_sw, wkv_ref, kgain, kgain_sw, tabs, k_ref, v_ref):
    kvp = _dot(ckv_n.astype(BF16), wkv_ref[...])
    kn = kvp[:, :H_MLA * HEAD_SLOT]
    ms = _head_sums(kn * kn + _tile_heads(krp * krp)) * (1.0 / D_QK)
    r = lax.rsqrt(ms + EPS)
    k = (kn + _tile_heads(krp)) * r * _tile_heads(kgain)
    if tabs is not None:
        cos, sin = tabs
        k = k * _tile_heads(cos) + r * _tile_heads(krp_sw * kgain_sw * sin)
    k_ref[...] = k.astype(BF16)
    v_lane = lax.broadcasted_iota(jnp.int32, (1, H_MLA * HEAD_SLOT), 1) % HEAD_SLOT
    v_ref[...] = (kvp[:, H_MLA * HEAD_SLOT:] + jnp.where(v_lane == D_V, 1.0, 0.0)).astype(BF16)


def _inproj_kernel(rope, *refs):
    (x_ref, mod_ref, n1g_ref, w1_ref, qng_ref, wuq_ref, kvg_ref, wkv_ref, qgain_ref,
     kgain_ref) = refs[:10]
    refs = refs[10:]
    if rope:
        qk_tabs = tuple(r[...] for r in refs[:2])
        ret_cos, ret_sin = (r[...] for r in refs[2:4])
        refs = refs[4:]
    else:
        qk_tabs = None
    ckvn_ref, krp_ref, q_ref, k_ref, v_ref, rq_ref, rk_ref, rv_ref, rg_ref = refs

    mod = mod_ref[0]
    h = _rms(x_ref[...], n1g_ref[...]) * (1.0 + mod[1:2]) + mod[0:1]
    z = _dot(h.astype(BF16), w1_ref[...])
    o = 0
    cq = z[:, o:o + Q_RANK]; o += Q_RANK
    ckv = z[:, o:o + KV_RANK]; o += KV_RANK
    krp = z[:, o:o + LANES]; o += LANES
    rq = z[:, o:o + D_RET_ALL]; o += D_RET_ALL
    rk = z[:, o:o + D_RET_ALL]; o += D_RET_ALL
    rv = z[:, o:o + D_RET_ALL]; o += D_RET_ALL
    rg = z[:, o:o + D_RET_ALL]; o += D_RET_ALL
    krp_sw = None
    if rope:
        krp_sw = z[:, o:o + LANES]; o += LANES
        rq_sw = z[:, o:o + D_RET_ALL]; o += D_RET_ALL
        rk_sw = z[:, o:o + D_RET_ALL]

    ckv_n = _rms(ckv, kvg_ref[...])
    ckvn_ref[...] = ckv_n
    krp_ref[...] = krp

    qn = _rms(cq, qng_ref[...]).astype(BF16)
    scale = D_QK ** -0.5 * LOG2_E
    q = _dot(qn, wuq_ref[:, :H_MLA * HEAD_SLOT])
    r = lax.rsqrt(_head_sums(q * q) * (1.0 / D_QK) + EPS)
    q = q * r * _tile_heads(qgain_ref[0:1] * scale)
    if rope:
        cos, sin = qk_tabs
        q_sw = _dot(qn, wuq_ref[:, H_MLA * HEAD_SLOT:])
        q = q * _tile_heads(cos) + q_sw * r * _tile_heads(qgain_ref[1:2] * scale * sin)
    q_ref[...] = q.astype(BF16)

    _kv_heads(ckv_n, krp, krp_sw, wkv_ref, kgain_ref[0:1], kgain_ref[1:2], qk_tabs, k_ref, v_ref)

    for t in range(D_RET_ALL // LANES):
        sl = slice(t * LANES, (t + 1) * LANES)
        rq_t, rk_t = rq[:, sl], rk[:, sl]
        if rope:
            rq_t = rq_t * ret_cos + rq_sw[:, sl] * ret_sin
            rk_t = rk_t * ret_cos + rk_sw[:, sl] * ret_sin
        rq_ref[:, sl] = rq_t.astype(BF16)
        rk_ref[:, sl] = (rk_t * (D_RET ** -0.5)).astype(BF16)
    rv_ref[...] = rv.astype(BF16)
    rg_ref[...] = _silu(rg).astype(BF16)


def _const_spec(shape):
    return pl.BlockSpec(shape, lambda i: (0,) * len(shape))


def _inproj(x, mod, mod_row0, tokens_per_row, wts, rope_tabs, tm):
    T = x.shape[0]
    bpr = tokens_per_row // tm
    rope = rope_tabs is not None
    tok = lambda w: pl.BlockSpec((tm, w), lambda i: (i, 0))
    in_specs = [tok(D_MODEL),
                pl.BlockSpec((1, 6, D_MODEL), lambda i: (mod_row0 + i // bpr, 0, 0)),
                _const_spec((1, D_MODEL)),
                _const_spec((D_MODEL, D_IN_PAD + (LANES + 2 * D_RET_ALL if rope else 0))),
                _const_spec((1, Q_RANK)),
                _const_spec((Q_RANK, 2 * H_MLA * HEAD_SLOT)),
                _const_spec((1, KV_RANK)),
                _const_spec((KV_RANK, 2 * H_MLA * HEAD_SLOT)),
                _const_spec((2, LANES)),
                _const_spec((2, LANES))]
    args = [x, mod, wts["norm1_g"], wts["w1_rope" if rope else "w1"], wts["q_norm_g"], wts["w_uq"],
            wts["kv_norm_g"], wts["w_kv"], wts["q_gain"], wts["k_gain"]]
    if rope:
        qk_tabs, ret_tabs = rope_tabs
        nb = qk_tabs[0].shape[0] // tm
        in_specs += [pl.BlockSpec((tm, LANES), lambda i: (i % nb, 0))] * 4
        args += list(qk_tabs) + list(ret_tabs)
    widths = [(KV_RANK, F32), (LANES, F32), (H_MLA * HEAD_SLOT, BF16), (H_MLA * HEAD_SLOT, BF16),
              (H_MLA * HEAD_SLOT, BF16)] + [(D_RET_ALL, BF16)] * 4
    return pl.pallas_call(
        functools.partial(_inproj_kernel, rope),
        grid=(T // tm,),
        in_specs=in_specs,
        out_specs=[tok(w) for w, _ in widths],
        out_shape=[jax.ShapeDtypeStruct((T, w), dt) for w, dt in widths],
        compiler_params=_cparams("parallel"),
        name="inproj_rope" if rope else "inproj",
    )(*args)


def _ctx_kv_kernel(ckv_ref, krp_ref, wkv_ref, kgain_ref, k_ref, v_ref):
    _kv_heads(ckv_ref[...], krp_ref[...], None, wkv_ref, kgain_ref[0:1], None, None, k_ref, v_ref)


def _ctx_kv(ckv, krp, wts, tm):
    T = ckv.shape[0]
    tok = lambda w: pl.BlockSpec((tm, w), lambda i: (i, 0))
    return pl.pallas_call(
        _ctx_kv_kernel,
        grid=(T // tm,),
        in_specs=[tok(KV_RANK), tok(LANES),
                  _const_spec((KV_RANK, 2 * H_MLA * HEAD_SLOT)), _const_spec((2, LANES))],
        out_specs=[tok(H_MLA * HEAD_SLOT), tok(H_MLA * HEAD_SLOT)],
        out_shape=[jax.ShapeDtypeStruct((T, H_MLA * HEAD_SLOT), BF16),
                   jax.ShapeDtypeStruct((T, H_MLA * HEAD_SLOT), BF16)],
        compiler_params=_cparams("parallel"),
        name="ctx_kv",
    )(ckv, krp, wts["w_kv"], wts["k_gain"])


ATTN_KEY_CHUNK = 2048
ATTN_HEADS_PER_STEP = 4


def _attn_kernel(has_ctx, *refs):
    if has_ctx:
        q_ref, k_ref, v_ref, kc_ref, vc_ref, o_ref = refs
    else:
        q_ref, k_ref, v_ref, o_ref = refs
    sources = [(k_ref, v_ref)] + ([(kc_ref, vc_ref)] if has_ctx else [])
    chunks = []
    for kr, vr in sources:
        n = kr.shape[1]
        kc = _pick(n, (ATTN_KEY_CHUNK, 256, 128))
        chunks += [(kr, vr, slice(c * kc, (c + 1) * kc)) for c in range(n // kc)]
    tq = q_ref.shape[1]
    outs = []
    for j in range(ATTN_HEADS_PER_STEP):
        hs = slice(j * HEAD_SLOT, (j + 1) * HEAD_SLOT)
        q = q_ref[0, :, hs]
        m = jnp.full((tq, 1), -jnp.inf, F32)
        acc = jnp.zeros((tq, HEAD_SLOT), F32)
        for kr, vr, rows in chunks:
            s = _dot_nt(q, kr[0, rows, hs])
            m_new = jnp.maximum(m, jnp.max(s, axis=-1, keepdims=True))
            p = jnp.exp2(s - m_new).astype(BF16)
            acc = acc * jnp.exp2(m - m_new) + _dot(p, vr[0, rows, hs])
            m = m_new
        outs.append(acc[:, :D_V] / acc[:, D_V:D_V + 1])
    o_ref[0] = jnp.concatenate(outs, axis=-1).astype(BF16)


def _attention(q, k, v, kc, vc, tq):
    B, L, _ = q.shape
    has_ctx = kc is not None
    hps = ATTN_HEADS_PER_STEP
    in_specs = [pl.BlockSpec((1, tq, hps * HEAD_SLOT), lambda b, hp, i: (b, i, hp)),
                pl.BlockSpec((1, L, hps * HEAD_SLOT), lambda b, hp, i: (b, 0, hp)),
                pl.BlockSpec((1, L, hps * HEAD_SLOT), lambda b, hp, i: (b, 0, hp))]
    args = [q, k, v]
    if has_ctx:
        Lc = kc.shape[1]
        in_specs += [pl.BlockSpec((1, Lc, hps * HEAD_SLOT), lambda b, hp, i: (b, 0, hp)),
                     pl.BlockSpec((1, Lc, hps * HEAD_SLOT), lambda b, hp, i: (b, 0, hp))]
        args += [kc, vc]
    return pl.pallas_call(
        functools.partial(_attn_kernel, has_ctx),
        grid=(B, H_MLA // hps, L // tq),
        in_specs=in_specs,
        out_specs=pl.BlockSpec((1, tq, hps * D_V), lambda b, hp, i: (b, i, hp)),
        out_shape=jax.ShapeDtypeStruct((B, L, H_MLA * D_V), BF16),
        compiler_params=_cparams("parallel", "parallel", "arbitrary"),
        name="attn_ctx" if has_ctx else "attn",
    )(*args)


RET_UNROLL = 8
RET_SHORT_SEQ = 1024


def _ret_pair(has_s0, want_state, n_chunks, pair, hp, *refs):
    rq_ref, rk_ref, rv_ref, rg_ref, decf_ref, decb_ref, g_ref = refs[:7]
    refs = refs[7:]
    if has_s0:
        s0f_ref, s0b_ref = refs[:2]
        refs = refs[2:]
    o_ref = refs[0]
    refs = refs[1:]
    if want_state:
        sf_ref, sb_ref = refs[:2]
        refs = refs[2:]
    kv_ref, sp_ref = refs

    C = RET_CHUNK
    W = 2 * D_RET
    ls = slice(pair * W, (pair + 1) * W)
    lane = lax.broadcasted_iota(jnp.int32, (1, W), 1)
    lane_h1 = lane >= D_RET
    row_h1 = lax.broadcasted_iota(jnp.int32, (W, 1), 0) >= D_RET
    blockdiag = row_h1 == lane_h1
    pos = lax.broadcasted_iota(jnp.int32, (C, 1), 0).astype(F32)
    diff = (lax.broadcasted_iota(jnp.int32, (C, C), 0)
            - lax.broadcasted_iota(jnp.int32, (C, C), 1)).astype(F32)

    lg = {}
    for name, ref in (("f", decf_ref), ("b", decb_ref)):
        a = -jnp.exp(ref[pl.ds(2 * hp, 1), :])
        b = -jnp.exp(ref[pl.ds(2 * hp + 1, 1), :])
        lg[name] = (a[:, :1], b[:, :1], jnp.where(lane_h1, b, a))
    lgf0, lgf1, lgf = lg["f"]
    lgb0, lgb1, lgb = lg["b"]
    qdf, kdf = jnp.exp(lgf * (pos + 1.0)), jnp.exp(lgf * (C - 1.0 - pos))
    qdb, kdb = jnp.exp(lgb * (C - pos)), jnp.exp(lgb * pos)
    row_h1_full = lax.broadcasted_iota(jnp.int32, (W, W), 0) >= D_RET
    cdf = jnp.where(row_h1_full, jnp.exp(lgf1 * C), jnp.exp(lgf0 * C))
    cdb = jnp.where(row_h1_full, jnp.exp(lgb1 * C), jnp.exp(lgb0 * C))

    def intra_decay(lf, lb):
        return jnp.where(diff >= 0, jnp.exp(lf * jnp.maximum(diff, 0.0)), jnp.exp(lb * jnp.maximum(-diff, 0.0)))

    dmat0, dmat1 = intra_decay(lgf0, lgb0), intra_decay(lgf1, lgb1)
    g = g_ref[...]

    def chunk_rows(n):
        return pl.ds(pl.multiple_of(n * C, C), C)

    def phase_a(n, carry):
        rows = chunk_rows(n)
        k = rk_ref[0, rows, ls].astype(F32)
        k2 = jnp.concatenate([(k * kdf).astype(BF16), (k * kdb).astype(BF16)], axis=1)
        kv_ref[n] = _dot_tn(k2, rv_ref[0, rows, ls])
        return carry

    unroll = min(RET_UNROLL, n_chunks)
    lax.fori_loop(0, n_chunks, phase_a, 0, unroll=unroll)

    def init_state(ref):
        if not has_s0:
            return jnp.zeros((W, W), F32)
        z = jnp.zeros((D_RET, D_RET), F32)
        return jnp.concatenate([jnp.concatenate([ref[0, 2 * pair], z], axis=1),
                                jnp.concatenate([z, ref[0, 2 * pair + 1]], axis=1)], axis=0)

    def scan_f(n, S):
        sp_ref[n, 0:W, :] = S.astype(BF16)
        return cdf * S + jnp.where(blockdiag, kv_ref[n, 0:W, :], 0.0)

    def scan_b(i, S):
        n = n_chunks - 1 - i
        sp_ref[n, W:2 * W, :] = S.astype(BF16)
        return cdb * S + jnp.where(blockdiag, kv_ref[n, W:2 * W, :], 0.0)

    Sf = lax.fori_loop(0, n_chunks, scan_f, init_state(s0f_ref if has_s0 else None))
    Sb = lax.fori_loop(0, n_chunks, scan_b, init_state(s0b_ref if has_s0 else None))
    if want_state:
        for S, ref in ((Sf, sf_ref), (Sb, sb_ref)):
            ref[0, 2 * pair] = S[0:D_RET, 0:D_RET]
            ref[0, 2 * pair + 1] = S[D_RET:W, D_RET:W]

    def phase_c(n, carry):
        rows = chunk_rows(n)
        q = rq_ref[0, rows, ls]
        k = rk_ref[0, rows, ls]
        v = rv_ref[0, rows, ls]
        zero = jnp.zeros_like(q)
        q_heads = jnp.concatenate([jnp.where(lane_h1, zero, q), jnp.where(lane_h1, q, zero)], axis=0)
        sc = _dot_nt(q_heads, k)
        p = jnp.concatenate([(sc[:C] * dmat0).astype(BF16), (sc[C:] * dmat1).astype(BF16)], axis=1)
        v_bd = jnp.concatenate([jnp.where(lane_h1, zero, v), jnp.where(lane_h1, v, zero)], axis=0)
        qf = q.astype(F32)
        q2 = jnp.concatenate([(qf * qdf).astype(BF16), (qf * qdb).astype(BF16)], axis=1)
        o = _dot(p, v_bd) + _dot(q2, sp_ref[n])
        o2 = o * o
        ss0 = jnp.sum(jnp.where(lane_h1, 0.0, o2), axis=-1, keepdims=True)
        ss1 = jnp.sum(jnp.where(lane_h1, o2, 0.0), axis=-1, keepdims=True)
        ms = jnp.where(lane_h1, ss1, ss0) * (1.0 / D_RET)
        o = o * lax.rsqrt(ms + EPS) * g * rg_ref[0, rows, ls].astype(F32)
        o_ref[0, rows, ls] = o.astype(BF16)
        return carry

    lax.fori_loop(0, n_chunks, phase_c, 0, unroll=unroll)


def _ret_kernel(has_s0, want_state, n_chunks, pairs, *refs):
    for pair in range(pairs):
        _ret_pair(has_s0, want_state, n_chunks, pair, pl.program_id(1) * pairs + pair, *refs)


def _retention(rq, rk, rv, rg, dec_f, dec_b, g, s0f, s0b, want_state):
    B, L, _ = rq.shape
    has_s0 = s0f is not None
    pairs = H_RET // 2 if L <= RET_SHORT_SEQ else 1
    seq = pl.BlockSpec((1, L, pairs * LANES), lambda b, hp: (b, 0, hp))
    st = pl.BlockSpec((1, 2 * pairs, D_RET, D_RET), lambda b, hp: (b, hp, 0, 0))
    cst = lambda shape: pl.BlockSpec(shape, lambda b, hp: (0,) * len(shape))
    in_specs = [seq] * 4 + [cst((H_RET, LANES)), cst((H_RET, LANES)), cst((1, LANES))]
    args = [rq, rk, rv, rg, dec_f, dec_b, g]
    if has_s0:
        in_specs += [st, st]
        args += [s0f, s0b]
    out_specs = [seq]
    out_shape = [jax.ShapeDtypeStruct((B, L, D_RET_ALL), BF16)]
    if want_state:
        out_specs += [st, st]
        out_shape += [jax.ShapeDtypeStruct((B, H_RET, D_RET, D_RET), F32)] * 2
    return pl.pallas_call(
        functools.partial(_ret_kernel, has_s0, want_state, L // RET_CHUNK, pairs),
        grid=(B, H_RET // 2 // pairs),
        in_specs=in_specs,
        out_specs=out_specs,
        out_shape=out_shape,
        scratch_shapes=[pltpu.VMEM((L // RET_CHUNK, 4 * D_RET, LANES), F32),
                        pltpu.VMEM((L // RET_CHUNK, 4 * D_RET, LANES), BF16)],
        compiler_params=_cparams("parallel", "parallel"),
        name="retention_s0" if has_s0 else "retention",
    )(*args)


def _pack_pairs(x):
    w = x.shape[1] // 2
    hi = lax.bitcast_convert_type(x[:, :w].astype(BF16).astype(F32), jnp.uint32)
    lo = lax.bitcast_convert_type(x[:, w:].astype(BF16).astype(F32), jnp.uint32)
    return hi | (lo >> 16)


def _unpack_pairs(p):
    hi = lax.bitcast_convert_type(p & jnp.uint32(0xFFFF0000), F32)
    lo = lax.bitcast_convert_type(p << 16, F32)
    return jnp.concatenate([hi, lo], axis=1)


D_PACKED = D_MODEL // 2


def _outproj_kernel(omla_ref, oret_ref, x_ref, mod_ref, wo_ref, n2g_ref, rwh_ref, rwl_ref, rb_ref, tri_ref,
                    x2_ref, h2p_ref, code_ref, rank_ref, wk_ref, cnt_ref):
    @pl.when(pl.program_id(0) == 0)
    def _():
        cnt_ref[...] = jnp.zeros_like(cnt_ref)

    mod = mod_ref[0]
    half = H_MLA * D_V
    a = _dot(omla_ref[...], wo_ref[:half, :]) + _dot(oret_ref[...], wo_ref[half:, :])
    x2 = x_ref[...] + mod[2:3] * a
    x2_ref[...] = x2
    h2 = _rms(x2, n2g_ref[...]) * (1.0 + mod[4:5]) + mod[3:4]
    h2p_ref[...] = _pack_pairs(h2)
    h2_hi = h2.astype(BF16)
    h2_lo = (h2 - h2_hi.astype(F32)).astype(BF16)
    rw_hi = rwh_ref[...]
    logits = _dot_nt(rw_hi, h2_hi) + _dot_nt(rw_hi, h2_lo) + _dot_nt(rwl_ref[...], h2_hi)
    scores = jax.nn.sigmoid(logits)
    tm = scores.shape[1]
    expert = lax.broadcasted_iota(jnp.int32, scores.shape, 0)
    cand = scores + rb_ref[...]
    code = jnp.zeros(scores.shape, jnp.int32)
    for k in range(TOP_K):
        mx = jnp.max(cand, axis=0, keepdims=True)
        first = jnp.min(jnp.where(cand == mx, expert, LANES), axis=0, keepdims=True)
        hit = expert == first
        code = jnp.where(hit, k + 1, code)
        cand = jnp.where(hit, -jnp.inf, cand)
    picked = code > 0
    sel = jnp.where(picked, scores, 0.0)
    gates = sel * (ROUTED_SCALE / jnp.sum(sel, axis=0, keepdims=True))
    code_ref[...] = code

    onehot = jnp.where(picked, 1.0, 0.0)
    carry = cnt_ref[...].astype(F32)
    rank = _dot(onehot.astype(BF16), tri_ref[...])
    rank_ref[...] = rank + jnp.concatenate([carry] * (tm // LANES), axis=1)
    cnt_ref[...] = (carry + jnp.sum(onehot, axis=1, keepdims=True)).astype(jnp.int32)

    row = lax.broadcasted_iota(jnp.int32, scores.shape, 0)
    wk_t = jnp.zeros(scores.shape, F32)
    for k in range(TOP_K):
        w = jnp.sum(jnp.where(code == k + 1, gates, 0.0), axis=0, keepdims=True)
        wk_t = jnp.where(row == k, w, wk_t)
    wk_ref[...] = wk_t.T


def _outproj(o_mla, o_ret, x, mod, mod_row0, tokens_per_row, wts, tm):
    T = x.shape[0]
    bpr = tokens_per_row // tm
    tok = lambda w: pl.BlockSpec((tm, w), lambda i: (i, 0))
    emaj = pl.BlockSpec((LANES, tm), lambda i: (0, i))
    tri = jnp.triu(jnp.ones((tm, tm), BF16), 1)
    bias = jnp.broadcast_to(wts["router_bias"].reshape(LANES, 1), (LANES, tm))
    return pl.pallas_call(
        _outproj_kernel,
        grid=(T // tm,),
        in_specs=[tok(H_MLA * D_V), tok(D_RET_ALL), tok(D_MODEL),
                  pl.BlockSpec((1, 6, D_MODEL), lambda i: (mod_row0 + i // bpr, 0, 0)),
                  _const_spec((D_MODEL, D_MODEL)), _const_spec((1, D_MODEL)),
                  _const_spec((LANES, D_MODEL)), _const_spec((LANES, D_MODEL)), _const_spec((LANES, tm)),
                  _const_spec((tm, tm))],
        out_specs=[tok(D_MODEL), tok(D_PACKED), emaj, emaj, tok(LANES), _const_spec((LANES, LANES))],
        out_shape=[jax.ShapeDtypeStruct((T, D_MODEL), F32),
                   jax.ShapeDtypeStruct((T, D_PACKED), jnp.uint32),
                   jax.ShapeDtypeStruct((LANES, T), jnp.int32),
                   jax.ShapeDtypeStruct((LANES, T), F32),
                   jax.ShapeDtypeStruct((T, LANES), F32),
                   jax.ShapeDtypeStruct((LANES, LANES), jnp.int32)],
        compiler_params=_cparams("arbitrary"),
        name="outproj_router",
    )(o_mla, o_ret, x, mod, wts["w_o"], wts["norm2_g"], wts["router_hi"], wts["router_lo"], bias, tri)


def _plan_kernel(code_ref, rank_ref, first_ref, pos_ref):
    code = code_ref[...]
    tm = code.shape[1]
    row = rank_ref[...] + jnp.concatenate([first_ref[...]] * (tm // LANES), axis=1)
    krow = lax.broadcasted_iota(jnp.int32, pos_ref.shape, 0)
    pos = jnp.zeros(pos_ref.shape, F32)
    for k in range(TOP_K):
        pos = jnp.where(krow == k, jnp.sum(jnp.where(code == k + 1, row, 0.0), axis=0, keepdims=True), pos)
    pos_ref[...] = pos.astype(jnp.int32)


def _plan(code, rank, first_rows, tm):
    T = code.shape[1]
    first = jnp.broadcast_to(jnp.pad(first_rows.astype(F32), (0, LANES - N_EXPERTS)).reshape(LANES, 1),
                             (LANES, LANES))
    emaj = pl.BlockSpec((LANES, tm), lambda i: (0, i))
    return pl.pallas_call(
        _plan_kernel,
        grid=(T // tm,),
        in_specs=[emaj, emaj, _const_spec((LANES, LANES))],
        out_specs=pl.BlockSpec((8, tm), lambda i: (0, i)),
        out_shape=jax.ShapeDtypeStruct((8, T), jnp.int32),
        compiler_params=_cparams("parallel"),
        name="moe_plan",
    )(code, rank, first)


SC_WORKERS = 32
SC_CHUNK = 64


def _sc_worker_id():
    return lax.axis_index("s") * 2 + lax.axis_index("c")


def _sc_mesh():
    return plsc.VectorSubcoreMesh(core_axis_name="c", subcore_axis_name="s")


def _sc_gather_rows(table, idx):
    N = idx.shape[0]
    W = table.shape[1]
    per_w = N // SC_WORKERS
    n_chunks = per_w // SC_CHUNK
    assert N == SC_WORKERS * n_chunks * SC_CHUNK and n_chunks % 2 == 0

    @functools.partial(
        pl.kernel, mesh=_sc_mesh(), out_type=jax.ShapeDtypeStruct((N, W), table.dtype),
        scratch_types=[pltpu.VMEM((n_chunks, SC_CHUNK), jnp.int32),
                       pltpu.VMEM((2, SC_CHUNK, W), table.dtype),
                       pltpu.SemaphoreType.DMA((2,)), pltpu.SemaphoreType.DMA((2,))],
        name="sc_gather_rows")
    def k(table_hbm, idx_hbm, out_hbm, idx_v, rows_v, gsem, wsem):
        wid = _sc_worker_id()
        base = wid * per_w
        pltpu.sync_copy(idx_hbm.at[wid], idx_v)

        def gather(j, b):
            return pltpu.make_async_copy(table_hbm.at[idx_v.at[j]], rows_v.at[b], gsem.at[b])

        def writeback(j, b):
            off = pl.multiple_of(base + j * SC_CHUNK, SC_CHUNK)
            return pltpu.make_async_copy(rows_v.at[b], out_hbm.at[pl.ds(off, SC_CHUNK)], wsem.at[b])

        gather(0, 0).start()

        @pl.loop(0, n_chunks, step=2)
        def _(j):
            for b in range(2):
                jj = j + b

                @pl.when(jj + 1 < n_chunks)
                def _():
                    @pl.when(jj >= 1)
                    def _():
                        writeback(jj - 1, 1 - b).wait()
                    gather(jj + 1, 1 - b).start()

                gather(jj, b).wait()
                writeback(jj, b).start()

        writeback(n_chunks - 2, 0).wait()
        writeback(n_chunks - 1, 1).wait()

    return k(table, idx.reshape(SC_WORKERS, n_chunks, SC_CHUNK))


def _sc_dispatch_rows(rows_list, pos_list, n_out):
    W = rows_list[0].shape[1]
    K = pos_list[0].shape[0]
    n_src = len(rows_list)
    plan = []
    idx_list = []
    for rows, pos in zip(rows_list, pos_list):
        T = rows.shape[0]
        per_w = T // SC_WORKERS
        n_chunks = per_w // SC_CHUNK
        assert T == SC_WORKERS * n_chunks * SC_CHUNK and n_chunks % 2 == 0
        plan.append((per_w, n_chunks))
        idx_list.append(pos.reshape(K, SC_WORKERS, n_chunks, SC_CHUNK).transpose(1, 2, 0, 3))
    max_chunks = max(n for _, n in plan)

    @functools.partial(
        pl.kernel, mesh=_sc_mesh(), out_type=jax.ShapeDtypeStruct((n_out, W), rows_list[0].dtype),
        scratch_types=[pltpu.VMEM((max_chunks, K, SC_CHUNK), jnp.int32),
                       pltpu.VMEM((2, SC_CHUNK, W), rows_list[0].dtype),
                       pltpu.SemaphoreType.DMA((2,)), pltpu.SemaphoreType.DMA((2,))],
        name="sc_dispatch_rows")
    def k(*refs):
        rows_refs, idx_refs = refs[:n_src], refs[n_src:2 * n_src]
        out_hbm, idx_v, rows_v, lsem, ssem = refs[2 * n_src:]
        wid = _sc_worker_id()
        for rows_hbm, idx_hbm, (per_w, n_chunks) in zip(rows_refs, idx_refs, plan):
            base = wid * per_w
            pltpu.sync_copy(idx_hbm.at[wid], idx_v.at[pl.ds(0, n_chunks)])

            def load(j, b):
                off = pl.multiple_of(base + j * SC_CHUNK, SC_CHUNK)
                return pltpu.make_async_copy(rows_hbm.at[pl.ds(off, SC_CHUNK)], rows_v.at[b], lsem.at[b])

            def scatter(j, b, kk):
                return pltpu.make_async_copy(rows_v.at[b], out_hbm.at[idx_v.at[j, kk]], ssem.at[b])

            load(0, 0).start()

            @pl.loop(0, n_chunks, step=2)
            def _(j):
                for b in range(2):
                    jj = j + b

                    @pl.when(jj + 1 < n_chunks)
                    def _():
                        @pl.when(jj >= 1)
                        def _():
                            for kk in range(K):
                                scatter(jj - 1, 1 - b, kk).wait()
                        load(jj + 1, 1 - b).start()

                    load(jj, b).wait()
                    for kk in range(K):
                        scatter(jj, b, kk).start()

            for kk in range(K):
                scatter(n_chunks - 2, 0, kk).wait()
            for kk in range(K):
                scatter(n_chunks - 1, 1, kk).wait()

    return k(*rows_list, *idx_list)


ROW_TILE = 1024
MOE_GATHER_GROUP = 8192


def _experts_kernel(te_ref, tv_ref, tb_ref, xs_ref, wg_ref, wu_ref, wd_ref, ys_ref):
    del te_ref, tb_ref
    valid = tv_ref[pl.program_id(0)]

    @pl.when(valid > 0)
    def _():
        row = lax.broadcasted_iota(jnp.int32, (ROW_TILE, 1), 0)
        p = jnp.where(row < valid, xs_ref[...], jnp.uint32(0))
        x = _unpack_pairs(p).astype(BF16)
        act = _silu(_dot(x, wg_ref[0])) * _dot(x, wu_ref[0])
        ys_ref[...] = _pack_pairs(_dot(act.astype(BF16), wd_ref[0]))


def _experts(xs, tile_expert, tile_valid, tile_block, wts):
    n_tiles = tile_expert.shape[0]
    ex = lambda a, b: pl.BlockSpec((1, a, b), lambda i, te, tv, tb: (te[i], 0, 0))
    row = pl.BlockSpec((ROW_TILE, D_PACKED), lambda i, te, tv, tb: (tb[i], 0))
    return pl.pallas_call(
        _experts_kernel,
        grid_spec=pltpu.PrefetchScalarGridSpec(
            num_scalar_prefetch=3, grid=(n_tiles,),
            in_specs=[row, ex(D_MODEL, D_EXPERT), ex(D_MODEL, D_EXPERT), ex(D_EXPERT, D_MODEL)],
            out_specs=row),
        out_shape=jax.ShapeDtypeStruct(xs.shape, jnp.uint32),
        compiler_params=_cparams("arbitrary"),
        name="experts",
    )(tile_expert, tile_valid, tile_block, xs, wts["exp_gate"], wts["exp_up"], wts["exp_down"])


def _combine_kernel(h2p_ref, ys_ref, wk_ref, x2_ref, mod_ref, shg_ref, shu_ref, shd_ref, *rest):
    y_ref = rest[-1]
    t = _unpack_pairs(h2p_ref[...]).astype(BF16)
    act = _silu(_dot(t, shg_ref[...])) * _dot(t, shu_ref[...])
    acc = _dot(act.astype(BF16), shd_ref[...])
    wk = wk_ref[...]
    for k in range(TOP_K):
        acc = acc + wk[:, k:k + 1] * _unpack_pairs(ys_ref[k])
    y_ref[...] = x2_ref[...] + mod_ref[0][5:6] * acc


def _combine(h2p, ys, row0, wk, x2, y_prev, mod, mod_row0, tokens_per_row, wts, tm):
    T = h2p.shape[0]
    G = ys.shape[1]
    bpr = tokens_per_row // tm
    blk0 = row0 // tm
    tok = lambda w: pl.BlockSpec((tm, w), lambda i: (blk0 + i, 0))
    in_specs = [tok(D_PACKED), pl.BlockSpec((TOP_K, tm, D_PACKED), lambda i: (0, i, 0)), tok(LANES),
                tok(D_MODEL), pl.BlockSpec((1, 6, D_MODEL), lambda i: (mod_row0 + (blk0 + i) // bpr, 0, 0)),
                _const_spec((D_MODEL, D_SHARED)), _const_spec((D_MODEL, D_SHARED)),
                _const_spec((D_SHARED, D_MODEL))]
    args = [h2p, ys, wk, x2, mod, wts["sh_gate"], wts["sh_up"], wts["sh_down"]]
    aliases = {}
    if y_prev is not None:
        aliases = {len(args): 0}
        in_specs.append(pl.BlockSpec(memory_space=pl.ANY))
        args.append(y_prev)
    return pl.pallas_call(
        _combine_kernel,
        grid=(G // tm,),
        in_specs=in_specs,
        out_specs=tok(D_MODEL),
        out_shape=jax.ShapeDtypeStruct((T, D_MODEL), F32),
        input_output_aliases=aliases,
        compiler_params=_cparams("parallel"),
        name="moe_combine",
    )(*args)


def _moe(halves, mod, wts):
    routed = [h["routed"] for h in halves]
    sizes = [r[1].shape[0] for r in routed]
    n_tiles = (TOP_K * sum(sizes)) // ROW_TILE + N_EXPERTS
    path_counts = [r[5][:N_EXPERTS, 0] for r in routed]
    counts = sum(path_counts)
    padded = ((counts + ROW_TILE - 1) // ROW_TILE) * ROW_TILE
    ends = jnp.cumsum(padded)
    starts = ends - padded
    tile_start = jnp.arange(n_tiles, dtype=jnp.int32) * ROW_TILE
    tile_expert = jnp.minimum(jnp.sum(tile_start[:, None] >= ends[None, :], axis=1), N_EXPERTS - 1).astype(jnp.int32)
    overlap = (jnp.minimum(tile_start[:, None] + ROW_TILE, (starts + counts)[None, :])
               - jnp.maximum(tile_start[:, None], starts[None, :]))
    tile_valid = jnp.sum(jnp.maximum(overlap, 0), axis=1).astype(jnp.int32)

    pos, first = [], starts
    for h, r, c in zip(halves, routed, path_counts):
        pos.append(_plan(r[2], r[3], first, h["tm"])[:TOP_K])
        first = first + c
    xs = _sc_dispatch_rows([r[1] for r in routed], pos, n_tiles * ROW_TILE)
    tile_block = jnp.minimum(jnp.arange(n_tiles, dtype=jnp.int32), ends[-1] // ROW_TILE - 1).astype(jnp.int32)
    ys = _experts(xs, tile_expert, tile_valid, tile_block, wts)
    outs = []
    for h, r, p in zip(halves, routed, pos):
        T = r[1].shape[0]
        G = MOE_GATHER_GROUP if T % MOE_GATHER_GROUP == 0 else T
        y = None
        for row0 in range(0, T, G):
            ysel = _sc_gather_rows(ys, p[:, row0:row0 + G].reshape(-1)).reshape(TOP_K, G, D_PACKED)
            y = _combine(r[1], ysel, row0, r[4], r[0], y, mod, h["mod_row0"], h["tokens_per_row"], wts, h["tm"])
        outs.append(y.reshape(h["shape"]))
    return outs


def _pad_heads(w, d_used):
    k = w.shape[0]
    w = w.reshape(k, H_MLA, d_used)
    return jnp.pad(w, ((0, 0), (0, 0), (0, HEAD_SLOT - d_used))).reshape(k, H_MLA * HEAD_SLOT)


def _lane_pad(v):
    return jnp.pad(v, (0, LANES - v.shape[0])).reshape(1, LANES)


def _rot_partners(w):
    rot = w[..., w.shape[-1] - D_ROPE:]
    return jnp.concatenate([jnp.zeros_like(w[..., :w.shape[-1] - D_ROPE]),
                            _swap_halves(rot, D_ROPE // 4)], axis=-1)


def _prep_weights(l, w_in, norm1_g, mla_q_norm_g, w_uq, mla_kv_norm_g, w_ukv, mla_q_gain, mla_k_gain,
                  w_o, norm2_g, router_w, router_bias, exp_w_gate, exp_w_up, exp_w_down,
                  sh_w_gate, sh_w_up, sh_w_down):
    w = w_in[l]
    a, b, c = Q_RANK, Q_RANK + KV_RANK, Q_RANK + KV_RANK + D_ROPE
    kr_slot = lambda wk: jnp.pad(wk, ((0, 0), (D_NOPE, LANES - D_QK)))
    w1 = jnp.concatenate([w[:, :b], kr_slot(w[:, b:c]), w[:, c:]], axis=1).astype(BF16)
    w_rq, w_rk = w[:, c:c + D_RET_ALL], w[:, c + D_RET_ALL:c + 2 * D_RET_ALL]
    w1_rope = jnp.concatenate([w1, kr_slot(_rot_partners(w[:, b:c])).astype(BF16),
                               _swap_halves(w_rq, D_RET // 4).astype(BF16),
                               _swap_halves(w_rk, D_RET // 4).astype(BF16)], axis=1)
    uq = w_uq[l].reshape(Q_RANK, H_MLA, D_QK)
    pad_slots = lambda u: jnp.pad(u, ((0, 0), (0, 0), (0, HEAD_SLOT - D_QK))).reshape(Q_RANK, H_MLA * HEAD_SLOT)
    gains = lambda g: jnp.concatenate([_lane_pad(g), _lane_pad(_rot_partners(g))], axis=0)
    ukv = w_ukv[l].reshape(KV_RANK, H_MLA, D_NOPE + D_V)
    w_k = _pad_heads(ukv[:, :, :D_NOPE].reshape(KV_RANK, H_MLA * D_NOPE), D_NOPE)
    w_v = _pad_heads(ukv[:, :, D_NOPE:].reshape(KV_RANK, H_MLA * D_V), D_V)
    rw = jnp.pad(router_w[l].T, ((0, LANES - N_EXPERTS), (0, 0)))
    rw_hi = rw.astype(BF16)
    rw_lo = (rw - rw_hi.astype(F32)).astype(BF16)
    rb = jnp.concatenate([router_bias[l].astype(F32), jnp.full((LANES - N_EXPERTS,), -jnp.inf, F32)])
    return dict(
        norm1_g=norm1_g[l].reshape(1, D_MODEL), w1=w1, w1_rope=w1_rope,
        q_norm_g=mla_q_norm_g[l].reshape(1, Q_RANK),
        w_uq=jnp.concatenate([pad_slots(uq), pad_slots(_rot_partners(uq))], axis=1).astype(BF16),
        kv_norm_g=mla_kv_norm_g[l].reshape(1, KV_RANK),
        w_kv=jnp.concatenate([w_k, w_v], axis=1).astype(BF16),
        q_gain=gains(mla_q_gain[l]), k_gain=gains(mla_k_gain[l]),
        w_o=w_o[l].astype(BF16), norm2_g=norm2_g[l].reshape(1, D_MODEL),
        router_hi=rw_hi, router_lo=rw_lo, router_bias=rb.reshape(1, LANES),
        exp_gate=exp_w_gate[l].astype(BF16), exp_up=exp_w_up[l].astype(BF16),
        exp_down=exp_w_down[l].astype(BF16),
        sh_gate=sh_w_gate[l].astype(BF16), sh_up=sh_w_up[l].astype(BF16),
        sh_down=sh_w_down[l].astype(BF16))


def _pick(n, prefs):
    for p in prefs:
        if n % p == 0:
            return p
    return n


def _mixer_half(x, mod, mod_row0, wts, ret_wts, ctx):
    B, L, D = x.shape
    T = B * L
    tokens_per_row = L if ctx is not None else T
    xt = x.reshape(T, D)
    rope_tabs = None
    if ctx is not None:
        rope_tabs = _rope_tables(L)
    tm = _pick(tokens_per_row, (512, 256, 128))
    ckv_n, krp, q, k, v, rq, rk, rv, rg = _inproj(xt, mod, mod_row0, tokens_per_row, wts, rope_tabs, tm)

    seq = lambda a: a.reshape(B, L, a.shape[-1])
    kc = vc = s0f = s0b = None
    if ctx is not None:
        ckv_c, kr_c, s0f, s0b = ctx
        Lc = ckv_c.shape[1]
        krp_c = jnp.pad(kr_c, ((0, 0), (0, 0), (D_NOPE, LANES - D_NOPE - D_ROPE)))
        kc, vc = _ctx_kv(ckv_c.reshape(B * Lc, KV_RANK), krp_c.reshape(B * Lc, LANES), wts,
                         _pick(B * Lc, (512, 256, 128)))
        kc, vc = kc.reshape(B, Lc, -1), vc.reshape(B, Lc, -1)
    o_mla = _attention(seq(q), seq(k), seq(v), kc, vc, _pick(L, (1024, 512, 256, 128)))

    dec_f, dec_b, ret_g = ret_wts
    want_state = ctx is None
    ret = _retention(seq(rq), seq(rk), seq(rv), seq(rg), dec_f, dec_b, ret_g, s0f, s0b, want_state)
    o_ret = ret[0]

    routed = _outproj(o_mla.reshape(T, -1), o_ret.reshape(T, -1), xt, mod, mod_row0, tokens_per_row, wts,
                      _pick(tokens_per_row, (1024, 512, 256, 128)))
    new = None
    if want_state:
        new = (ckv_n.reshape(B, L, KV_RANK), krp[:, D_NOPE:D_NOPE + D_ROPE].reshape(B, L, D_ROPE),
               ret[1], ret[2])
    return dict(routed=routed, mod_row0=mod_row0, tokens_per_row=tokens_per_row, tm=tm, shape=(B, L, D)), new


def kernel(x_prompt, x_sample, cache_mla_ckv, cache_mla_krope, state_ret_fwd, state_ret_bwd, c, c_ctx,
           w_ada, b_ada, norm1_g, w_in, mla_q_norm_g, w_uq, mla_kv_norm_g, w_ukv, mla_q_gain, mla_k_gain,
           ret_decay_fwd, ret_decay_bwd, ret_norm_g, w_o, norm2_g, router_w, router_bias, exp_w_gate,
           exp_w_up, exp_w_down, sh_w_gate, sh_w_up, sh_w_down):
    depth = w_ada.shape[0]
    n_dec = c.shape[0]
    assert 1 + n_dec <= MOD_ROWS
    cond = jnp.concatenate([c_ctx[None], c, jnp.zeros((MOD_ROWS - 1 - n_dec, D_MODEL), F32)], axis=0)

    y_prompt, y_sample = x_prompt, x_sample
    ckv_l, kr_l, sf_l, sb_l = [], [], [], []
    for l in range(depth):
        wts = _prep_weights(l, w_in, norm1_g, mla_q_norm_g, w_uq, mla_kv_norm_g, w_ukv, mla_q_gain,
                            mla_k_gain, w_o, norm2_g, router_w, router_bias, exp_w_gate, exp_w_up,
                            exp_w_down, sh_w_gate, sh_w_up, sh_w_down)
        ret_wts = (jnp.broadcast_to(ret_decay_fwd[l].astype(F32)[:, None], (H_RET, LANES)),
                   jnp.broadcast_to(ret_decay_bwd[l].astype(F32)[:, None], (H_RET, LANES)),
                   jnp.tile(ret_norm_g[l].reshape(1, D_RET), (1, LANES // D_RET)))
        mod = _adaln(cond, w_ada[l], b_ada[l]).reshape(MOD_ROWS, 6, D_MODEL)
        half_p, new = _mixer_half(y_prompt, mod, 0, wts, ret_wts, None)
        ckv_l.append(new[0]); kr_l.append(new[1]); sf_l.append(new[2]); sb_l.append(new[3])
        ctx = (cache_mla_ckv[:, l], cache_mla_krope[:, l], state_ret_fwd[:, l], state_ret_bwd[:, l])
        half_s, _ = _mixer_half(y_sample, mod, 1, wts, ret_wts, ctx)
        y_prompt, y_sample = _moe([half_p, half_s], mod, wts)

    return (y_prompt, y_sample, jnp.stack(ckv_l, axis=1), jnp.stack(kr_l, axis=1),
            jnp.stack(sf_l, axis=1), jnp.stack(sb_l, axis=1))
```

```python
import functools
import math

import numpy as np
import jax
import jax.numpy as jnp
from jax import lax
from jax.experimental import pallas as pl
from jax.experimental.pallas import tpu as pltpu
from jax.experimental.pallas import tpu_sc as plsc

F32 = jnp.float32
BF16 = jnp.bfloat16

D_MODEL = 1024
GRID_W = 64
H_MLA = 8
D_NOPE = 64
D_ROPE = 32
D_QK = D_NOPE + D_ROPE
D_V = 64
Q_RANK = 256
KV_RANK = 128
H_RET = 8
D_RET = 64
RET_CHUNK = 256
D_RET_ALL = H_RET * D_RET
N_EXPERTS = 64
TOP_K = 6
D_EXPERT = 256
D_SHARED = 256
ROUTED_SCALE = 2.5
ROPE_BASE = 10000.0
EPS = 1e-6
LOG2_E = math.log2(math.e)

LANES = 128
HEAD_SLOT = LANES
D_IN_PAD = Q_RANK + KV_RANK + LANES + 4 * D_RET_ALL
MOD_ROWS = 16
VMEM_LIMIT = 56 * 1024 * 1024


def _cparams(*sem):
    return pltpu.CompilerParams(dimension_semantics=sem, vmem_limit_bytes=VMEM_LIMIT)


def _dot(a, b):
    return jnp.dot(a, b, preferred_element_type=F32)


def _dot_nt(a, b):
    return lax.dot_general(a, b, (((1,), (1,)), ((), ())), preferred_element_type=F32)


def _dot_tn(a, b):
    return lax.dot_general(a, b, (((0,), (0,)), ((), ())), preferred_element_type=F32)


def _rms(x, g):
    return x * lax.rsqrt(jnp.mean(x * x, axis=-1, keepdims=True) + EPS) * g


def _silu(x):
    return x * jax.nn.sigmoid(x)


def _adaln_kernel(c_ref, w_ref, b_ref, o_ref):
    s = _silu(c_ref[...])
    o_ref[...] = _dot(s.astype(BF16), w_ref[...].astype(BF16)) + b_ref[...]


def _adaln(cond, w_ada, b_ada):
    n_out = w_ada.shape[1]
    bn = 1536
    return pl.pallas_call(
        _adaln_kernel,
        grid=(n_out // bn,),
        in_specs=[pl.BlockSpec((MOD_ROWS, D_MODEL), lambda j: (0, 0)),
                  pl.BlockSpec((D_MODEL, bn), lambda j: (0, j)),
                  pl.BlockSpec((1, bn), lambda j: (0, j))],
        out_specs=pl.BlockSpec((MOD_ROWS, bn), lambda j: (0, j)),
        out_shape=jax.ShapeDtypeStruct((MOD_ROWS, n_out), F32),
        compiler_params=_cparams("arbitrary"),
        name="adaln",
    )(cond, w_ada, b_ada.reshape(1, n_out))


def _swap_halves(w, half_pair):
    g = w.reshape(*w.shape[:-1], w.shape[-1] // (2 * half_pair), 2, half_pair)
    return g[..., ::-1, :].reshape(w.shape)


def _rope_tables(n_tokens):
    t = np.arange(n_tokens)
    row = (t // GRID_W).astype(np.float64)
    col = (t % GRID_W).astype(np.float64)

    def axis_tables(width, lane0, period, fill):
        p = width // 2
        inv = 1.0 / (ROPE_BASE ** (np.arange(p, dtype=np.float64) / p))
        cos = np.full((n_tokens, LANES), fill)
        sin = np.zeros((n_tokens, LANES))
        starts = range(lane0, LANES, period) if period else (lane0,)
        for s0 in starts:
            for base, pos in ((s0, row), (s0 + width, col)):
                ang = pos[:, None] * inv[None, :]
                c, sn = np.cos(ang), np.sin(ang)
                cos[:, base:base + p] = c
                cos[:, base + p:base + 2 * p] = c
                sin[:, base:base + p] = -sn
                sin[:, base + p:base + 2 * p] = sn
        return cos, sin

    qk_cos, qk_sin = axis_tables(D_ROPE // 2, D_NOPE, 0, 1.0)
    ret_cos, ret_sin = axis_tables(D_RET // 2, 0, D_RET, 1.0)
    as_f32 = lambda *a: tuple(jnp.asarray(x, F32) for x in a)
    return as_f32(qk_cos, qk_sin), as_f32(ret_cos, ret_sin)


def _head_sums(x):
    r = lax.broadcasted_iota(jnp.int32, (2 * HEAD_SLOT, 2 * HEAD_SLOT), 0) // HEAD_SLOT
    c = lax.broadcasted_iota(jnp.int32, (2 * HEAD_SLOT, 2 * HEAD_SLOT), 1) // HEAD_SLOT
    ones_bd = jnp.where(r == c, 1.0, 0.0).astype(BF16)
    xb = x.astype(BF16)
    w = 2 * HEAD_SLOT
    return jnp.concatenate([_dot(xb[:, g * w:(g + 1) * w], ones_bd) for g in range(x.shape[1] // w)], axis=1)


def _tile_heads(v):
    return jnp.concatenate([v] * H_MLA, axis=1)


def _kv_heads(ckv_n, krp, krp_sw, wkv_ref, kgain, kgain_sw, tabs, k_ref, v_ref):
    kvp = _dot(ckv_n.astype(BF16), wkv_ref[...])
    kn = kvp[:, :H_MLA * HEAD_SLOT]
    ms = _head_sums(kn * kn + _tile_heads(krp * krp)) * (1.0 / D_QK)
    r = lax.rsqrt(ms + EPS)
    k = (kn + _tile_heads(krp)) * r * _tile_heads(kgain)
    if tabs is not None:
        cos, sin = tabs
        k = k * _tile_heads(cos) + r * _tile_heads(krp_sw * kgain_sw * sin)
    k_ref[...] = k.astype(BF16)
    v_lane = lax.broadcasted_iota(jnp.int32, (1, H_MLA * HEAD_SLOT), 1) % HEAD_SLOT
    v_ref[...] = (kvp[:, H_MLA * HEAD_SLOT:] + jnp.where(v_lane == D_V, 1.0, 0.0)).astype(BF16)


def _inproj_kernel(rope, *refs):
    (x_ref, mod_ref, n1g_ref, w1_ref, qng_ref, wuq_ref, kvg_ref, wkv_ref, qgain_ref,
     kgain_ref) = refs[:10]
    refs = refs[10:]
    if rope:
        qk_tabs = tuple(r[...] for r in refs[:2])
        ret_cos, ret_sin = (r[...] for r in refs[2:4])
        refs = refs[4:]
    else:
        qk_tabs = None
    ckvn_ref, krp_ref, q_ref, k_ref, v_ref, rq_ref, rk_ref, rv_ref, rg_ref = refs

    mod = mod_ref[0]
    h = _rms(x_ref[...], n1g_ref[...]) * (1.0 + mod[1:2]) + mod[0:1]
    z = _dot(h.astype(BF16), w1_ref[...])
    o = 0
    cq = z[:, o:o + Q_RANK]; o += Q_RANK
    ckv = z[:, o:o + KV_RANK]; o += KV_RANK
    krp = z[:, o:o + LANES]; o += LANES
    rq = z[:, o:o + D_RET_ALL]; o += D_RET_ALL
    rk = z[:, o:o + D_RET_ALL]; o += D_RET_ALL
    rv = z[:, o:o + D_RET_ALL]; o += D_RET_ALL
    rg = z[:, o:o + D_RET_ALL]; o += D_RET_ALL
    krp_sw = None
    if rope:
        krp_sw = z[:, o:o + LANES]; o += LANES
        rq_sw = z[:, o:o + D_RET_ALL]; o += D_RET_ALL
        rk_sw = z[:, o:o + D_RET_ALL]

    ckv_n = _rms(ckv, kvg_ref[...])
    ckvn_ref[...] = ckv_n
    krp_ref[...] = krp

    qn = _rms(cq, qng_ref[...]).astype(BF16)
    scale = D_QK ** -0.5 * LOG2_E
    q = _dot(qn, wuq_ref[:, :H_MLA * HEAD_SLOT])
    r = lax.rsqrt(_head_sums(q * q) * (1.0 / D_QK) + EPS)
    q = q * r * _tile_heads(qgain_ref[0:1] * scale)
    if rope:
        cos, sin = qk_tabs
        q_sw = _dot(qn, wuq_ref[:, H_MLA * HEAD_SLOT:])
        q = q * _tile_heads(cos) + q_sw * r * _tile_heads(qgain_ref[1:2] * scale * sin)
    q_ref[...] = q.astype(BF16)

    _kv_heads(ckv_n, krp, krp_sw, wkv_ref, kgain_ref[0:1], kgain_ref[1:2], qk_tabs, k_ref, v_ref)

    for t in range(D_RET_ALL // LANES):
        sl = slice(t * LANES, (t + 1) * LANES)
        rq_t, rk_t = rq[:, sl], rk[:, sl]
        if rope:
            rq_t = rq_t * ret_cos + rq_sw[:, sl] * ret_sin
            rk_t = rk_t * ret_cos + rk_sw[:, sl] * ret_sin
        rq_ref[:, sl] = rq_t.astype(BF16)
        rk_ref[:, sl] = (rk_t * (D_RET ** -0.5)).astype(BF16)
    rv_ref[...] = rv.astype(BF16)
    rg_ref[...] = _silu(rg).astype(BF16)


def _const_spec(shape):
    return pl.BlockSpec(shape, lambda i: (0,) * len(shape))


def _inproj(x, mod, mod_row0, tokens_per_row, wts, rope_tabs, tm):
    T = x.shape[0]
    bpr = tokens_per_row // tm
    rope = rope_tabs is not None
    tok = lambda w: pl.BlockSpec((tm, w), lambda i: (i, 0))
    in_specs = [tok(D_MODEL),
                pl.BlockSpec((1, 6, D_MODEL), lambda i: (mod_row0 + i // bpr, 0, 0)),
                _const_spec((1, D_MODEL)),
                _const_spec((D_MODEL, D_IN_PAD + (LANES + 2 * D_RET_ALL if rope else 0))),
                _const_spec((1, Q_RANK)),
                _const_spec((Q_RANK, 2 * H_MLA * HEAD_SLOT)),
                _const_spec((1, KV_RANK)),
                _const_spec((KV_RANK, 2 * H_MLA * HEAD_SLOT)),
                _const_spec((2, LANES)),
                _const_spec((2, LANES))]
    args = [x, mod, wts["norm1_g"], wts["w1_rope" if rope else "w1"], wts["q_norm_g"], wts["w_uq"],
            wts["kv_norm_g"], wts["w_kv"], wts["q_gain"], wts["k_gain"]]
    if rope:
        qk_tabs, ret_tabs = rope_tabs
        nb = qk_tabs[0].shape[0] // tm
        in_specs += [pl.BlockSpec((tm, LANES), lambda i: (i % nb, 0))] * 4
        args += list(qk_tabs) + list(ret_tabs)
    widths = [(KV_RANK, F32), (LANES, F32), (H_MLA * HEAD_SLOT, BF16), (H_MLA * HEAD_SLOT, BF16),
              (H_MLA * HEAD_SLOT, BF16)] + [(D_RET_ALL, BF16)] * 4
    return pl.pallas_call(
        functools.partial(_inproj_kernel, rope),
        grid=(T // tm,),
        in_specs=in_specs,
        out_specs=[tok(w) for w, _ in widths],
        out_shape=[jax.ShapeDtypeStruct((T, w), dt) for w, dt in widths],
        compiler_params=_cparams("parallel"),
        name="inproj_rope" if rope else "inproj",
    )(*args)


def _ctx_kv_kernel(ckv_ref, krp_ref, wkv_ref, kgain_ref, k_ref, v_ref):
    _kv_heads(ckv_ref[...], krp_ref[...], None, wkv_ref, kgain_ref[0:1], None, None, k_ref, v_ref)


def _ctx_kv(ckv, krp, wts, tm):
    T = ckv.shape[0]
    tok = lambda w: pl.BlockSpec((tm, w), lambda i: (i, 0))
    return pl.pallas_call(
        _ctx_kv_kernel,
        grid=(T // tm,),
        in_specs=[tok(KV_RANK), tok(LANES),
                  _const_spec((KV_RANK, 2 * H_MLA * HEAD_SLOT)), _const_spec((2, LANES))],
        out_specs=[tok(H_MLA * HEAD_SLOT), tok(H_MLA * HEAD_SLOT)],
        out_shape=[jax.ShapeDtypeStruct((T, H_MLA * HEAD_SLOT), BF16),
                   jax.ShapeDtypeStruct((T, H_MLA * HEAD_SLOT), BF16)],
        compiler_params=_cparams("parallel"),
        name="ctx_kv",
    )(ckv, krp, wts["w_kv"], wts["k_gain"])


ATTN_KEY_CHUNK = 2048
ATTN_HEADS_PER_STEP = 4


def _attn_kernel(has_ctx, *refs):
    if has_ctx:
        q_ref, k_ref, v_ref, kc_ref, vc_ref, o_ref = refs
    else:
        q_ref, k_ref, v_ref, o_ref = refs
    sources = [(k_ref, v_ref)] + ([(kc_ref, vc_ref)] if has_ctx else [])
    chunks = []
    for kr, vr in sources:
        n = kr.shape[1]
        kc = _pick(n, (ATTN_KEY_CHUNK, 256, 128))
        chunks += [(kr, vr, slice(c * kc, (c + 1) * kc)) for c in range(n // kc)]
    tq = q_ref.shape[1]
    outs = []
    for j in range(ATTN_HEADS_PER_STEP):
        hs = slice(j * HEAD_SLOT, (j + 1) * HEAD_SLOT)
        q = q_ref[0, :, hs]
        m = jnp.full((tq, 1), -jnp.inf, F32)
        acc = jnp.zeros((tq, HEAD_SLOT), F32)
        for kr, vr, rows in chunks:
            s = _dot_nt(q, kr[0, rows, hs])
            m_new = jnp.maximum(m, jnp.max(s, axis=-1, keepdims=True))
            p = jnp.exp2(s - m_new).astype(BF16)
            acc = acc * jnp.exp2(m - m_new) + _dot(p, vr[0, rows, hs])
            m = m_new
        outs.append(acc[:, :D_V] / acc[:, D_V:D_V + 1])
    o_ref[0] = jnp.concatenate(outs, axis=-1).astype(BF16)


def _attention(q, k, v, kc, vc, tq):
    B, L, _ = q.shape
    has_ctx = kc is not None
    hps = ATTN_HEADS_PER_STEP
    in_specs = [pl.BlockSpec((1, tq, hps * HEAD_SLOT), lambda b, hp, i: (b, i, hp)),
                pl.BlockSpec((1, L, hps * HEAD_SLOT), lambda b, hp, i: (b, 0, hp)),
                pl.BlockSpec((1, L, hps * HEAD_SLOT), lambda b, hp, i: (b, 0, hp))]
    args = [q, k, v]
    if has_ctx:
        Lc = kc.shape[1]
        in_specs += [pl.BlockSpec((1, Lc, hps * HEAD_SLOT), lambda b, hp, i: (b, 0, hp)),
                     pl.BlockSpec((1, Lc, hps * HEAD_SLOT), lambda b, hp, i: (b, 0, hp))]
        args += [kc, vc]
    return pl.pallas_call(
        functools.partial(_attn_kernel, has_ctx),
        grid=(B, H_MLA // hps, L // tq),
        in_specs=in_specs,
        out_specs=pl.BlockSpec((1, tq, hps * D_V), lambda b, hp, i: (b, i, hp)),
        out_shape=jax.ShapeDtypeStruct((B, L, H_MLA * D_V), BF16),
        compiler_params=_cparams("parallel", "parallel", "arbitrary"),
        name="attn_ctx" if has_ctx else "attn",
    )(*args)


RET_UNROLL = 8
RET_SHORT_SEQ = 1024


def _ret_pair(has_s0, want_state, n_chunks, pair, hp, *refs):
    rq_ref, rk_ref, rv_ref, rg_ref, decf_ref, decb_ref, g_ref = refs[:7]
    refs = refs[7:]
    if has_s0:
        s0f_ref, s0b_ref = refs[:2]
        refs = refs[2:]
    o_ref = refs[0]
    refs = refs[1:]
    if want_state:
        sf_ref, sb_ref = refs[:2]
        refs = refs[2:]
    kv_ref, sp_ref = refs

    C = RET_CHUNK
    W = 2 * D_RET
    ls = slice(pair * W, (pair + 1) * W)
    lane = lax.broadcasted_iota(jnp.int32, (1, W), 1)
    lane_h1 = lane >= D_RET
    row_h1 = lax.broadcasted_iota(jnp.int32, (W, 1), 0) >= D_RET
    blockdiag = row_h1 == lane_h1
    pos = lax.broadcasted_iota(jnp.int32, (C, 1), 0).astype(F32)
    diff = (lax.broadcasted_iota(jnp.int32, (C, C), 0)
            - lax.broadcasted_iota(jnp.int32, (C, C), 1)).astype(F32)

    lg = {}
    for name, ref in (("f", decf_ref), ("b", decb_ref)):
        a = -jnp.exp(ref[pl.ds(2 * hp, 1), :])
        b = -jnp.exp(ref[pl.ds(2 * hp + 1, 1), :])
        lg[name] = (a[:, :1], b[:, :1], jnp.where(lane_h1, b, a))
    lgf0, lgf1, lgf = lg["f"]
    lgb0, lgb1, lgb = lg["b"]
    qdf, kdf = jnp.exp(lgf * (pos + 1.0)), jnp.exp(lgf * (C - 1.0 - pos))
    qdb, kdb = jnp.exp(lgb * (C - pos)), jnp.exp(lgb * pos)
    row_h1_full = lax.broadcasted_iota(jnp.int32, (W, W), 0) >= D_RET
    cdf = jnp.where(row_h1_full, jnp.exp(lgf1 * C), jnp.exp(lgf0 * C))
    cdb = jnp.where(row_h1_full, jnp.exp(lgb1 * C), jnp.exp(lgb0 * C))

    def intra_decay(lf, lb):
        return jnp.where(diff >= 0, jnp.exp(lf * jnp.maximum(diff, 0.0)), jnp.exp(lb * jnp.maximum(-diff, 0.0)))

    dmat0, dmat1 = intra_decay(lgf0, lgb0), intra_decay(lgf1, lgb1)
    g = g_ref[...]

    def chunk_rows(n):
        return pl.ds(pl.multiple_of(n * C, C), C)

    def phase_a(n, carry):
        rows = chunk_rows(n)
        k = rk_ref[0, rows, ls].astype(F32)
        k2 = jnp.concatenate([(k * kdf).astype(BF16), (k * kdb).astype(BF16)], axis=1)
        kv_ref[n] = _dot_tn(k2, rv_ref[0, rows, ls])
        return carry

    unroll = min(RET_UNROLL, n_chunks)
    lax.fori_loop(0, n_chunks, phase_a, 0, unroll=unroll)

    def init_state(ref):
        if not has_s0:
            return jnp.zeros((W, W), F32)
        z = jnp.zeros((D_RET, D_RET), F32)
        return jnp.concatenate([jnp.concatenate([ref[0, 2 * pair], z], axis=1),
                                jnp.concatenate([z, ref[0, 2 * pair + 1]], axis=1)], axis=0)

    def scan_f(n, S):
        sp_ref[n, 0:W, :] = S.astype(BF16)
        return cdf * S + jnp.where(blockdiag, kv_ref[n, 0:W, :], 0.0)

    def scan_b(i, S):
        n = n_chunks - 1 - i
        sp_ref[n, W:2 * W, :] = S.astype(BF16)
        return cdb * S + jnp.where(blockdiag, kv_ref[n, W:2 * W, :], 0.0)

    Sf = lax.fori_loop(0, n_chunks, scan_f, init_state(s0f_ref if has_s0 else None))
    Sb = lax.fori_loop(0, n_chunks, scan_b, init_state(s0b_ref if has_s0 else None))
    if want_state:
        for S, ref in ((Sf, sf_ref), (Sb, sb_ref)):
            ref[0, 2 * pair] = S[0:D_RET, 0:D_RET]
            ref[0, 2 * pair + 1] = S[D_RET:W, D_RET:W]

    def phase_c(n, carry):
        rows = chunk_rows(n)
        q = rq_ref[0, rows, ls]
        k = rk_ref[0, rows, ls]
        v = rv_ref[0, rows, ls]
        zero = jnp.zeros_like(q)
        q_heads = jnp.concatenate([jnp.where(lane_h1, zero, q), jnp.where(lane_h1, q, zero)], axis=0)
        sc = _dot_nt(q_heads, k)
        p = jnp.concatenate([(sc[:C] * dmat0).astype(BF16), (sc[C:] * dmat1).astype(BF16)], axis=1)
        v_bd = jnp.concatenate([jnp.where(lane_h1, zero, v), jnp.where(lane_h1, v, zero)], axis=0)
        qf = q.astype(F32)
        q2 = jnp.concatenate([(qf * qdf).astype(BF16), (qf * qdb).astype(BF16)], axis=1)
        o = _dot(p, v_bd) + _dot(q2, sp_ref[n])
        o2 = o * o
        ss0 = jnp.sum(jnp.where(lane_h1, 0.0, o2), axis=-1, keepdims=True)
        ss1 = jnp.sum(jnp.where(lane_h1, o2, 0.0), axis=-1, keepdims=True)
        ms = jnp.where(lane_h1, ss1, ss0) * (1.0 / D_RET)
        o = o * lax.rsqrt(ms + EPS) * g * rg_ref[0, rows, ls].astype(F32)
        o_ref[0, rows, ls] = o.astype(BF16)
        return carry

    lax.fori_loop(0, n_chunks, phase_c, 0, unroll=unroll)


def _ret_kernel(has_s0, want_state, n_chunks, pairs, *refs):
    for pair in range(pairs):
        _ret_pair(has_s0, want_state, n_chunks, pair, pl.program_id(1) * pairs + pair, *refs)


def _retention(rq, rk, rv, rg, dec_f, dec_b, g, s0f, s0b, want_state):
    B, L, _ = rq.shape
    has_s0 = s0f is not None
    pairs = H_RET // 2 if L <= RET_SHORT_SEQ else 1
    seq = pl.BlockSpec((1, L, pairs * LANES), lambda b, hp: (b, 0, hp))
    st = pl.BlockSpec((1, 2 * pairs, D_RET, D_RET), lambda b, hp: (b, hp, 0, 0))
    cst = lambda shape: pl.BlockSpec(shape, lambda b, hp: (0,) * len(shape))
    in_specs = [seq] * 4 + [cst((H_RET, LANES)), cst((H_RET, LANES)), cst((1, LANES))]
    args = [rq, rk, rv, rg, dec_f, dec_b, g]
    if has_s0:
        in_specs += [st, st]
        args += [s0f, s0b]
    out_specs = [seq]
    out_shape = [jax.ShapeDtypeStruct((B, L, D_RET_ALL), BF16)]
    if want_state:
        out_specs += [st, st]
        out_shape += [jax.ShapeDtypeStruct((B, H_RET, D_RET, D_RET), F32)] * 2
    return pl.pallas_call(
        functools.partial(_ret_kernel, has_s0, want_state, L // RET_CHUNK, pairs),
        grid=(B, H_RET // 2 // pairs),
        in_specs=in_specs,
        out_specs=out_specs,
        out_shape=out_shape,
        scratch_shapes=[pltpu.VMEM((L // RET_CHUNK, 4 * D_RET, LANES), F32),
                        pltpu.VMEM((L // RET_CHUNK, 4 * D_RET, LANES), BF16)],
        compiler_params=_cparams("parallel", "parallel"),
        name="retention_s0" if has_s0 else "retention",
    )(*args)


def _pack_pairs(x):
    w = x.shape[1] // 2
    hi = lax.bitcast_convert_type(x[:, :w].astype(BF16).astype(F32), jnp.uint32)
    lo = lax.bitcast_convert_type(x[:, w:].astype(BF16).astype(F32), jnp.uint32)
    return hi | (lo >> 16)


def _unpack_pairs(p):
    hi = lax.bitcast_convert_type(p & jnp.uint32(0xFFFF0000), F32)
    lo = lax.bitcast_convert_type(p << 16, F32)
    return jnp.concatenate([hi, lo], axis=1)


D_PACKED = D_MODEL // 2


def _outproj_kernel(omla_ref, oret_ref, x_ref, mod_ref, wo_ref, n2g_ref, rwh_ref, rwl_ref, rb_ref, tri_ref,
                    x2_ref, h2p_ref, code_ref, rank_ref, wk_ref, cnt_ref):
    @pl.when(pl.program_id(0) == 0)
    def _():
        cnt_ref[...] = jnp.zeros_like(cnt_ref)

    mod = mod_ref[0]
    half = H_MLA * D_V
    a = _dot(omla_ref[...], wo_ref[:half, :]) + _dot(oret_ref[...], wo_ref[half:, :])
    x2 = x_ref[...] + mod[2:3] * a
    x2_ref[...] = x2
    h2 = _rms(x2, n2g_ref[...]) * (1.0 + mod[4:5]) + mod[3:4]
    h2p_ref[...] = _pack_pairs(h2)
    h2_hi = h2.astype(BF16)
    h2_lo = (h2 - h2_hi.astype(F32)).astype(BF16)
    rw_hi = rwh_ref[...]
    logits = _dot_nt(rw_hi, h2_hi) + _dot_nt(rw_hi, h2_lo) + _dot_nt(rwl_ref[...], h2_hi)
    scores = jax.nn.sigmoid(logits)
    tm = scores.shape[1]
    expert = lax.broadcasted_iota(jnp.int32, scores.shape, 0)
    cand = scores + rb_ref[...]
    code = jnp.zeros(scores.shape, jnp.int32)
    for k in range(TOP_K):
        mx = jnp.max(cand, axis=0, keepdims=True)
        first = jnp.min(jnp.where(cand == mx, expert, LANES), axis=0, keepdims=True)
        hit = expert == first
        code = jnp.where(hit, k + 1, code)
        cand = jnp.where(hit, -jnp.inf, cand)
    picked = code > 0
    sel = jnp.where(picked, scores, 0.0)
    gates = sel * (ROUTED_SCALE / jnp.sum(sel, axis=0, keepdims=True))
    code_ref[...] = code

    onehot = jnp.where(picked, 1.0, 0.0)
    carry = cnt_ref[...].astype(F32)
    rank = _dot(onehot.astype(BF16), tri_ref[...])
    rank_ref[...] = rank + jnp.concatenate([carry] * (tm // LANES), axis=1)
    cnt_ref[...] = (carry + jnp.sum(onehot, axis=1, keepdims=True)).astype(jnp.int32)

    row = lax.broadcasted_iota(jnp.int32, scores.shape, 0)
    wk_t = jnp.zeros(scores.shape, F32)
    for k in range(TOP_K):
        w = jnp.sum(jnp.where(code == k + 1, gates, 0.0), axis=0, keepdims=True)
        wk_t = jnp.where(row == k, w, wk_t)
    wk_ref[...] = wk_t.T


def _outproj(o_mla, o_ret, x, mod, mod_row0, tokens_per_row, wts, tm):
    T = x.shape[0]
    bpr = tokens_per_row // tm
    tok = lambda w: pl.BlockSpec((tm, w), lambda i: (i, 0))
    emaj = pl.BlockSpec((LANES, tm), lambda i: (0, i))
    tri = jnp.triu(jnp.ones((tm, tm), BF16), 1)
    bias = jnp.broadcast_to(wts["router_bias"].reshape(LANES, 1), (LANES, tm))
    return pl.pallas_call(
        _outproj_kernel,
        grid=(T // tm,),
        in_specs=[tok(H_MLA * D_V), tok(D_RET_ALL), tok(D_MODEL),
                  pl.BlockSpec((1, 6, D_MODEL), lambda i: (mod_row0 + i // bpr, 0, 0)),
                  _const_spec((D_MODEL, D_MODEL)), _const_spec((1, D_MODEL)),
                  _const_spec((LANES, D_MODEL)), _const_spec((LANES, D_MODEL)), _const_spec((LANES, tm)),
                  _const_spec((tm, tm))],
        out_specs=[tok(D_MODEL), tok(D_PACKED), emaj, emaj, tok(LANES), _const_spec((LANES, LANES))],
        out_shape=[jax.ShapeDtypeStruct((T, D_MODEL), F32),
                   jax.ShapeDtypeStruct((T, D_PACKED), jnp.uint32),
                   jax.ShapeDtypeStruct((LANES, T), jnp.int32),
                   jax.ShapeDtypeStruct((LANES, T), F32),
                   jax.ShapeDtypeStruct((T, LANES), F32),
                   jax.ShapeDtypeStruct((LANES, LANES), jnp.int32)],
        compiler_params=_cparams("arbitrary"),
        name="outproj_router",
    )(o_mla, o_ret, x, mod, wts["w_o"], wts["norm2_g"], wts["router_hi"], wts["router_lo"], bias, tri)


def _plan_kernel(code_ref, rank_ref, first_ref, pos_ref):
    code = code_ref[...]
    tm = code.shape[1]
    row = rank_ref[...] + jnp.concatenate([first_ref[...]] * (tm // LANES), axis=1)
    krow = lax.broadcasted_iota(jnp.int32, pos_ref.shape, 0)
    pos = jnp.zeros(pos_ref.shape, F32)
    for k in range(TOP_K):
        pos = jnp.where(krow == k, jnp.sum(jnp.where(code == k + 1, row, 0.0), axis=0, keepdims=True), pos)
    pos_ref[...] = pos.astype(jnp.int32)


def _plan(code, rank, first_rows, tm):
    T = code.shape[1]
    first = jnp.broadcast_to(jnp.pad(first_rows.astype(F32), (0, LANES - N_EXPERTS)).reshape(LANES, 1),
                             (LANES, LANES))
    emaj = pl.BlockSpec((LANES, tm), lambda i: (0, i))
    return pl.pallas_call(
        _plan_kernel,
        grid=(T // tm,),
        in_specs=[emaj, emaj, _const_spec((LANES, LANES))],
        out_specs=pl.BlockSpec((8, tm), lambda i: (0, i)),
        out_shape=jax.ShapeDtypeStruct((8, T), jnp.int32),
        compiler_params=_cparams("parallel"),
        name="moe_plan",
    )(code, rank, first)


SC_WORKERS = 32
SC_CHUNK = 64


def _sc_worker_id():
    return lax.axis_index("s") * 2 + lax.axis_index("c")


def _sc_mesh():
    return plsc.VectorSubcoreMesh(core_axis_name="c", subcore_axis_name="s")


def _sc_gather_rows(table, idx):
    N = idx.shape[0]
    W = table.shape[1]
    per_w = N // SC_WORKERS
    n_chunks = per_w // SC_CHUNK
    assert N == SC_WORKERS * n_chunks * SC_CHUNK and n_chunks % 2 == 0

    @functools.partial(
        pl.kernel, mesh=_sc_mesh(), out_type=jax.ShapeDtypeStruct((N, W), table.dtype),
        scratch_types=[pltpu.VMEM((n_chunks, SC_CHUNK), jnp.int32),
                       pltpu.VMEM((2, SC_CHUNK, W), table.dtype),
                       pltpu.SemaphoreType.DMA((2,)), pltpu.SemaphoreType.DMA((2,))],
        name="sc_gather_rows")
    def k(table_hbm, idx_hbm, out_hbm, idx_v, rows_v, gsem, wsem):
        wid = _sc_worker_id()
        base = wid * per_w
        pltpu.sync_copy(idx_hbm.at[wid], idx_v)

        def gather(j, b):
            return pltpu.make_async_copy(table_hbm.at[idx_v.at[j]], rows_v.at[b], gsem.at[b])

        def writeback(j, b):
            off = pl.multiple_of(base + j * SC_CHUNK, SC_CHUNK)
            return pltpu.make_async_copy(rows_v.at[b], out_hbm.at[pl.ds(off, SC_CHUNK)], wsem.at[b])

        gather(0, 0).start()

        @pl.loop(0, n_chunks, step=2)
        def _(j):
            for b in range(2):
                jj = j + b

                @pl.when(jj + 1 < n_chunks)
                def _():
                    @pl.when(jj >= 1)
                    def _():
                        writeback(jj - 1, 1 - b).wait()
                    gather(jj + 1, 1 - b).start()

                gather(jj, b).wait()
                writeback(jj, b).start()

        writeback(n_chunks - 2, 0).wait()
        writeback(n_chunks - 1, 1).wait()

    return k(table, idx.reshape(SC_WORKERS, n_chunks, SC_CHUNK))


def _sc_dispatch_rows(rows_list, pos_list, n_out):
    W = rows_list[0].shape[1]
    K = pos_list[0].shape[0]
    n_src = len(rows_list)
    plan = []
    idx_list = []
    for rows, pos in zip(rows_list, pos_list):
        T = rows.shape[0]
        per_w = T // SC_WORKERS
        n_chunks = per_w // SC_CHUNK
        assert T == SC_WORKERS * n_chunks * SC_CHUNK and n_chunks % 2 == 0
        plan.append((per_w, n_chunks))
        idx_list.append(pos.reshape(K, SC_WORKERS, n_chunks, SC_CHUNK).transpose(1, 2, 0, 3))
    max_chunks = max(n for _, n in plan)

    @functools.partial(
        pl.kernel, mesh=_sc_mesh(), out_type=jax.ShapeDtypeStruct((n_out, W), rows_list[0].dtype),
        scratch_types=[pltpu.VMEM((max_chunks, K, SC_CHUNK), jnp.int32),
                       pltpu.VMEM((2, SC_CHUNK, W), rows_list[0].dtype),
                       pltpu.SemaphoreType.DMA((2,)), pltpu.SemaphoreType.DMA((2,))],
        name="sc_dispatch_rows")
    def k(*refs):
        rows_refs, idx_refs = refs[:n_src], refs[n_src:2 * n_src]
        out_hbm, idx_v, rows_v, lsem, ssem = refs[2 * n_src:]
        wid = _sc_worker_id()
        for rows_hbm, idx_hbm, (per_w, n_chunks) in zip(rows_refs, idx_refs, plan):
            base = wid * per_w
            pltpu.sync_copy(idx_hbm.at[wid], idx_v.at[pl.ds(0, n_chunks)])

            def load(j, b):
                off = pl.multiple_of(base + j * SC_CHUNK, SC_CHUNK)
                return pltpu.make_async_copy(rows_hbm.at[pl.ds(off, SC_CHUNK)], rows_v.at[b], lsem.at[b])

            def scatter(j, b, kk):
                return pltpu.make_async_copy(rows_v.at[b], out_hbm.at[idx_v.at[j, kk]], ssem.at[b])

            load(0, 0).start()

            @pl.loop(0, n_chunks, step=2)
            def _(j):
                for b in range(2):
                    jj = j + b

                    @pl.when(jj + 1 < n_chunks)
                    def _():
                        @pl.when(jj >= 1)
                        def _():
                            for kk in range(K):
                                scatter(jj - 1, 1 - b, kk).wait()
                        load(jj + 1, 1 - b).start()

                    load(jj, b).wait()
                    for kk in range(K):
                        scatter(jj, b, kk).start()

            for kk in range(K):
                scatter(n_chunks - 2, 0, kk).wait()
            for kk in range(K):
                scatter(n_chunks - 1, 1, kk).wait()

    return k(*rows_list, *idx_list)


ROW_TILE = 1024
MOE_GATHER_GROUP = 8192


def _experts_kernel(te_ref, tv_ref, tb_ref, xs_ref, wg_ref, wu_ref, wd_ref, ys_ref):
    del te_ref, tb_ref
    valid = tv_ref[pl.program_id(0)]

    @pl.when(valid > 0)
    def _():
        row = lax.broadcasted_iota(jnp.int32, (ROW_TILE, 1), 0)
        p = jnp.where(row < valid, xs_ref[...], jnp.uint32(0))
        x = _unpack_pairs(p).astype(BF16)
        act = _silu(_dot(x, wg_ref[0])) * _dot(x, wu_ref[0])
        ys_ref[...] = _pack_pairs(_dot(act.astype(BF16), wd_ref[0]))


def _experts(xs, tile_expert, tile_valid, tile_block, wts):
    n_tiles = tile_expert.shape[0]
    ex = lambda a, b: pl.BlockSpec((1, a, b), lambda i, te, tv, tb: (te[i], 0, 0))
    row = pl.BlockSpec((ROW_TILE, D_PACKED), lambda i, te, tv, tb: (tb[i], 0))
    return pl.pallas_call(
        _experts_kernel,
        grid_spec=pltpu.PrefetchScalarGridSpec(
            num_scalar_prefetch=3, grid=(n_tiles,),
            in_specs=[row, ex(D_MODEL, D_EXPERT), ex(D_MODEL, D_EXPERT), ex(D_EXPERT, D_MODEL)],
            out_specs=row),
        out_shape=jax.ShapeDtypeStruct(xs.shape, jnp.uint32),
        compiler_params=_cparams("arbitrary"),
        name="experts",
    )(tile_expert, tile_valid, tile_block, xs, wts["exp_gate"], wts["exp_up"], wts["exp_down"])


def _combine_kernel(h2p_ref, ys_ref, wk_ref, x2_ref, mod_ref, shg_ref, shu_ref, shd_ref, *rest):
    y_ref = rest[-1]
    t = _unpack_pairs(h2p_ref[...]).astype(BF16)
    act = _silu(_dot(t, shg_ref[...])) * _dot(t, shu_ref[...])
    acc = _dot(act.astype(BF16), shd_ref[...])
    wk = wk_ref[...]
    for k in range(TOP_K):
        acc = acc + wk[:, k:k + 1] * _unpack_pairs(ys_ref[k])
    y_ref[...] = x2_ref[...] + mod_ref[0][5:6] * acc


def _combine(h2p, ys, row0, wk, x2, y_prev, mod, mod_row0, tokens_per_row, wts, tm):
    T = h2p.shape[0]
    G = ys.shape[1]
    bpr = tokens_per_row // tm
    blk0 = row0 // tm
    tok = lambda w: pl.BlockSpec((tm, w), lambda i: (blk0 + i, 0))
    in_specs = [tok(D_PACKED), pl.BlockSpec((TOP_K, tm, D_PACKED), lambda i: (0, i, 0)), tok(LANES),
                tok(D_MODEL), pl.BlockSpec((1, 6, D_MODEL), lambda i: (mod_row0 + (blk0 + i) // bpr, 0, 0)),
                _const_spec((D_MODEL, D_SHARED)), _const_spec((D_MODEL, D_SHARED)),
                _const_spec((D_SHARED, D_MODEL))]
    args = [h2p, ys, wk, x2, mod, wts["sh_gate"], wts["sh_up"], wts["sh_down"]]
    aliases = {}
    if y_prev is not None:
        aliases = {len(args): 0}
        in_specs.append(pl.BlockSpec(memory_space=pl.ANY))
        args.append(y_prev)
    return pl.pallas_call(
        _combine_kernel,
        grid=(G // tm,),
        in_specs=in_specs,
        out_specs=tok(D_MODEL),
        out_shape=jax.ShapeDtypeStruct((T, D_MODEL), F32),
        input_output_aliases=aliases,
        compiler_params=_cparams("parallel"),
        name="moe_combine",
    )(*args)


def _moe(halves, mod, wts):
    routed = [h["routed"] for h in halves]
    sizes = [r[1].shape[0] for r in routed]
    n_tiles = (TOP_K * sum(sizes)) // ROW_TILE + N_EXPERTS
    path_counts = [r[5][:N_EXPERTS, 0] for r in routed]
    counts = sum(path_counts)
    padded = ((counts + ROW_TILE - 1) // ROW_TILE) * ROW_TILE
    ends = jnp.cumsum(padded)
    starts = ends - padded
    tile_start = jnp.arange(n_tiles, dtype=jnp.int32) * ROW_TILE
    tile_expert = jnp.minimum(jnp.sum(tile_start[:, None] >= ends[None, :], axis=1), N_EXPERTS - 1).astype(jnp.int32)
    overlap = (jnp.minimum(tile_start[:, None] + ROW_TILE, (starts + counts)[None, :])
               - jnp.maximum(tile_start[:, None], starts[None, :]))
    tile_valid = jnp.sum(jnp.maximum(overlap, 0), axis=1).astype(jnp.int32)

    pos, first = [], starts
    for h, r, c in zip(halves, routed, path_counts):
        pos.append(_plan(r[2], r[3], first, h["tm"])[:TOP_K])
        first = first + c
    names = ("exp_gate", "exp_up", "exp_down")
    tied = lax.optimization_barrier((tuple(wts[n] for n in names), pos[-1]))
    pos[-1] = tied[1]
    expert_wts = {n: w.astype(BF16) for n, w in zip(names, tied[0])}
    xs = _sc_dispatch_rows([r[1] for r in routed], pos, n_tiles * ROW_TILE)
    tile_block = jnp.minimum(jnp.arange(n_tiles, dtype=jnp.int32), ends[-1] // ROW_TILE - 1).astype(jnp.int32)
    ys = _experts(xs, tile_expert, tile_valid, tile_block, expert_wts)
    outs = []
    for h, r, p in zip(halves, routed, pos):
        T = r[1].shape[0]
        G = MOE_GATHER_GROUP if T % MOE_GATHER_GROUP == 0 else T
        y = None
        for row0 in range(0, T, G):
            ysel = _sc_gather_rows(ys, p[:, row0:row0 + G].reshape(-1)).reshape(TOP_K, G, D_PACKED)
            y = _combine(r[1], ysel, row0, r[4], r[0], y, mod, h["mod_row0"], h["tokens_per_row"], wts, h["tm"])
        outs.append(y.reshape(h["shape"]))
    return outs


def _pad_heads(w, d_used):
    k = w.shape[0]
    w = w.reshape(k, H_MLA, d_used)
    return jnp.pad(w, ((0, 0), (0, 0), (0, HEAD_SLOT - d_used))).reshape(k, H_MLA * HEAD_SLOT)


def _lane_pad(v):
    return jnp.pad(v, (0, LANES - v.shape[0])).reshape(1, LANES)


def _rot_partners(w):
    rot = w[..., w.shape[-1] - D_ROPE:]
    return jnp.concatenate([jnp.zeros_like(w[..., :w.shape[-1] - D_ROPE]),
                            _swap_halves(rot, D_ROPE // 4)], axis=-1)


def _prep_weights(l, w_in, norm1_g, mla_q_norm_g, w_uq, mla_kv_norm_g, w_ukv, mla_q_gain, mla_k_gain,
                  w_o, norm2_g, router_w, router_bias, exp_w_gate, exp_w_up, exp_w_down,
                  sh_w_gate, sh_w_up, sh_w_down):
    w = w_in[l]
    a, b, c = Q_RANK, Q_RANK + KV_RANK, Q_RANK + KV_RANK + D_ROPE
    kr_slot = lambda wk: jnp.pad(wk, ((0, 0), (D_NOPE, LANES - D_QK)))
    w1 = jnp.concatenate([w[:, :b], kr_slot(w[:, b:c]), w[:, c:]], axis=1).astype(BF16)
    w_rq, w_rk = w[:, c:c + D_RET_ALL], w[:, c + D_RET_ALL:c + 2 * D_RET_ALL]
    w1_rope = jnp.concatenate([w1, kr_slot(_rot_partners(w[:, b:c])).astype(BF16),
                               _swap_halves(w_rq, D_RET // 4).astype(BF16),
                               _swap_halves(w_rk, D_RET // 4).astype(BF16)], axis=1)
    uq = w_uq[l].reshape(Q_RANK, H_MLA, D_QK)
    pad_slots = lambda u: jnp.pad(u, ((0, 0), (0, 0), (0, HEAD_SLOT - D_QK))).reshape(Q_RANK, H_MLA * HEAD_SLOT)
    gains = lambda g: jnp.concatenate([_lane_pad(g), _lane_pad(_rot_partners(g))], axis=0)
    ukv = w_ukv[l].reshape(KV_RANK, H_MLA, D_NOPE + D_V)
    w_k = _pad_heads(ukv[:, :, :D_NOPE].reshape(KV_RANK, H_MLA * D_NOPE), D_NOPE)
    w_v = _pad_heads(ukv[:, :, D_NOPE:].reshape(KV_RANK, H_MLA * D_V), D_V)
    rw = jnp.pad(router_w[l].T, ((0, LANES - N_EXPERTS), (0, 0)))
    rw_hi = rw.astype(BF16)
    rw_lo = (rw - rw_hi.astype(F32)).astype(BF16)
    rb = jnp.concatenate([router_bias[l].astype(F32), jnp.full((LANES - N_EXPERTS,), -jnp.inf, F32)])
    return dict(
        norm1_g=norm1_g[l].reshape(1, D_MODEL), w1=w1, w1_rope=w1_rope,
        q_norm_g=mla_q_norm_g[l].reshape(1, Q_RANK),
        w_uq=jnp.concatenate([pad_slots(uq), pad_slots(_rot_partners(uq))], axis=1).astype(BF16),
        kv_norm_g=mla_kv_norm_g[l].reshape(1, KV_RANK),
        w_kv=jnp.concatenate([w_k, w_v], axis=1).astype(BF16),
        q_gain=gains(mla_q_gain[l]), k_gain=gains(mla_k_gain[l]),
        w_o=w_o[l].astype(BF16), norm2_g=norm2_g[l].reshape(1, D_MODEL),
        router_hi=rw_hi, router_lo=rw_lo, router_bias=rb.reshape(1, LANES),
        exp_gate=exp_w_gate[l], exp_up=exp_w_up[l], exp_down=exp_w_down[l],
        sh_gate=sh_w_gate[l].astype(BF16), sh_up=sh_w_up[l].astype(BF16),
        sh_down=sh_w_down[l].astype(BF16))


def _pick(n, prefs):
    for p in prefs:
        if n % p == 0:
            return p
    return n


def _mixer_half(x, mod, mod_row0, wts, ret_wts, ctx):
    B, L, D = x.shape
    T = B * L
    tokens_per_row = L if ctx is not None else T
    xt = x.reshape(T, D)
    rope_tabs = None
    if ctx is not None:
        rope_tabs = _rope_tables(L)
    tm = _pick(tokens_per_row, (512, 256, 128))
    ckv_n, krp, q, k, v, rq, rk, rv, rg = _inproj(xt, mod, mod_row0, tokens_per_row, wts, rope_tabs, tm)

    seq = lambda a: a.reshape(B, L, a.shape[-1])
    kc = vc = s0f = s0b = None
    if ctx is not None:
        ckv_c, kr_c, s0f, s0b = ctx
        Lc = ckv_c.shape[1]
        krp_c = jnp.pad(kr_c, ((0, 0), (0, 0), (D_NOPE, LANES - D_NOPE - D_ROPE)))
        kc, vc = _ctx_kv(ckv_c.reshape(B * Lc, KV_RANK), krp_c.reshape(B * Lc, LANES), wts,
                         _pick(B * Lc, (512, 256, 128)))
        kc, vc = kc.reshape(B, Lc, -1), vc.reshape(B, Lc, -1)
    o_mla = _attention(seq(q), seq(k), seq(v), kc, vc, _pick(L, (1024, 512, 256, 128)))

    dec_f, dec_b, ret_g = ret_wts
    want_state = ctx is None
    ret = _retention(seq(rq), seq(rk), seq(rv), seq(rg), dec_f, dec_b, ret_g, s0f, s0b, want_state)
    o_ret = ret[0]

    routed = _outproj(o_mla.reshape(T, -1), o_ret.reshape(T, -1), xt, mod, mod_row0, tokens_per_row, wts,
                      _pick(tokens_per_row, (1024, 512, 256, 128)))
    new = None
    if want_state:
        new = (ckv_n.reshape(B, L, KV_RANK), krp[:, D_NOPE:D_NOPE + D_ROPE].reshape(B, L, D_ROPE),
               ret[1], ret[2])
    return dict(routed=routed, mod_row0=mod_row0, tokens_per_row=tokens_per_row, tm=tm, shape=(B, L, D)), new


def kernel(x_prompt, x_sample, cache_mla_ckv, cache_mla_krope, state_ret_fwd, state_ret_bwd, c, c_ctx,
           w_ada, b_ada, norm1_g, w_in, mla_q_norm_g, w_uq, mla_kv_norm_g, w_ukv, mla_q_gain, mla_k_gain,
           ret_decay_fwd, ret_decay_bwd, ret_norm_g, w_o, norm2_g, router_w, router_bias, exp_w_gate,
           exp_w_up, exp_w_down, sh_w_gate, sh_w_up, sh_w_down):
    depth = w_ada.shape[0]
    n_dec = c.shape[0]
    assert 1 + n_dec <= MOD_ROWS
    cond = jnp.concatenate([c_ctx[None], c, jnp.zeros((MOD_ROWS - 1 - n_dec, D_MODEL), F32)], axis=0)

    y_prompt, y_sample = x_prompt, x_sample
    ckv_l, kr_l, sf_l, sb_l = [], [], [], []
    for l in range(depth):
        wts = _prep_weights(l, w_in, norm1_g, mla_q_norm_g, w_uq, mla_kv_norm_g, w_ukv, mla_q_gain,
                            mla_k_gain, w_o, norm2_g, router_w, router_bias, exp_w_gate, exp_w_up,
                            exp_w_down, sh_w_gate, sh_w_up, sh_w_down)
        ret_wts = (jnp.broadcast_to(ret_decay_fwd[l].astype(F32)[:, None], (H_RET, LANES)),
                   jnp.broadcast_to(ret_decay_bwd[l].astype(F32)[:, None], (H_RET, LANES)),
                   jnp.tile(ret_norm_g[l].reshape(1, D_RET), (1, LANES // D_RET)))
        mod = _adaln(cond, w_ada[l], b_ada[l]).reshape(MOD_ROWS, 6, D_MODEL)
        half_p, new = _mixer_half(y_prompt, mod, 0, wts, ret_wts, None)
        ckv_l.append(new[0]); kr_l.append(new[1]); sf_l.append(new[2]); sb_l.append(new[3])
        ctx = (cache_mla_ckv[:, l], cache_mla_krope[:, l], state_ret_fwd[:, l], state_ret_bwd[:, l])
        half_s, _ = _mixer_half(y_sample, mod, 1, wts, ret_wts, ctx)
        y_prompt, y_sample = _moe([half_p, half_s], mod, wts)

    return (y_prompt, y_sample, jnp.stack(ckv_l, axis=1), jnp.stack(kr_l, axis=1),
            jnp.stack(sf_l, axis=1), jnp.stack(sb_l, axis=1))
```

```python
import functools
import math

import numpy as np
import jax
import jax.numpy as jnp
from jax import lax
from jax.experimental import pallas as pl
from jax.experimental.pallas import tpu as pltpu
from jax.experimental.pallas import tpu_sc as plsc

F32 = jnp.float32
BF16 = jnp.bfloat16

D_MODEL = 1024
GRID_W = 64
H_MLA = 8
D_NOPE = 64
D_ROPE = 32
D_QK = D_NOPE + D_ROPE
D_V = 64
Q_RANK = 256
KV_RANK = 128
H_RET = 8
D_RET = 64
RET_CHUNK = 256
D_RET_ALL = H_RET * D_RET
N_EXPERTS = 64
TOP_K = 6
D_EXPERT = 256
D_SHARED = 256
ROUTED_SCALE = 2.5
ROPE_BASE = 10000.0
EPS = 1e-6
LOG2_E = math.log2(math.e)

LANES = 128
HEAD_SLOT = LANES
D_IN_PAD = Q_RANK + KV_RANK + LANES + 4 * D_RET_ALL
MOD_ROWS = 16
VMEM_LIMIT = 56 * 1024 * 1024


def _cparams(*sem):
    return pltpu.CompilerParams(dimension_semantics=sem, vmem_limit_bytes=VMEM_LIMIT)


def _dot(a, b):
    return jnp.dot(a, b, preferred_element_type=F32)


def _dot_nt(a, b):
    return lax.dot_general(a, b, (((1,), (1,)), ((), ())), preferred_element_type=F32)


def _dot_tn(a, b):
    return lax.dot_general(a, b, (((0,), (0,)), ((), ())), preferred_element_type=F32)


def _rms(x, g):
    return x * lax.rsqrt(jnp.mean(x * x, axis=-1, keepdims=True) + EPS) * g


def _silu(x):
    return x * jax.nn.sigmoid(x)


def _adaln_kernel(c_ref, w_ref, b_ref, o_ref):
    s = _silu(c_ref[...])
    o_ref[...] = _dot(s.astype(BF16), w_ref[...].astype(BF16)) + b_ref[...]


def _adaln(cond, w_ada, b_ada):
    n_out = w_ada.shape[1]
    bn = 1536
    return pl.pallas_call(
        _adaln_kernel,
        grid=(n_out // bn,),
        in_specs=[pl.BlockSpec((MOD_ROWS, D_MODEL), lambda j: (0, 0)),
                  pl.BlockSpec((D_MODEL, bn), lambda j: (0, j)),
                  pl.BlockSpec((1, bn), lambda j: (0, j))],
        out_specs=pl.BlockSpec((MOD_ROWS, bn), lambda j: (0, j)),
        out_shape=jax.ShapeDtypeStruct((MOD_ROWS, n_out), F32),
        compiler_params=_cparams("arbitrary"),
        name="adaln",
    )(cond, w_ada, b_ada.reshape(1, n_out))


def _swap_halves(w, half_pair):
    g = w.reshape(*w.shape[:-1], w.shape[-1] // (2 * half_pair), 2, half_pair)
    return g[..., ::-1, :].reshape(w.shape)


def _rope_tables(n_tokens):
    t = np.arange(n_tokens)
    row = (t // GRID_W).astype(np.float64)
    col = (t % GRID_W).astype(np.float64)

    def axis_tables(width, lane0, period, fill):
        p = width // 2
        inv = 1.0 / (ROPE_BASE ** (np.arange(p, dtype=np.float64) / p))
        cos = np.full((n_tokens, LANES), fill)
        sin = np.zeros((n_tokens, LANES))
        starts = range(lane0, LANES, period) if period else (lane0,)
        for s0 in starts:
            for base, pos in ((s0, row), (s0 + width, col)):
                ang = pos[:, None] * inv[None, :]
                c, sn = np.cos(ang), np.sin(ang)
                cos[:, base:base + p] = c
                cos[:, base + p:base + 2 * p] = c
                sin[:, base:base + p] = -sn
                sin[:, base + p:base + 2 * p] = sn
        return cos, sin

    qk_cos, qk_sin = axis_tables(D_ROPE // 2, D_NOPE, 0, 1.0)
    ret_cos, ret_sin = axis_tables(D_RET // 2, 0, D_RET, 1.0)
    as_f32 = lambda *a: tuple(jnp.asarray(x, F32) for x in a)
    return as_f32(qk_cos, qk_sin), as_f32(ret_cos, ret_sin)


def _head_sums(x):
    r = lax.broadcasted_iota(jnp.int32, (2 * HEAD_SLOT, 2 * HEAD_SLOT), 0) // HEAD_SLOT
    c = lax.broadcasted_iota(jnp.int32, (2 * HEAD_SLOT, 2 * HEAD_SLOT), 1) // HEAD_SLOT
    ones_bd = jnp.where(r == c, 1.0, 0.0).astype(BF16)
    xb = x.astype(BF16)
    w = 2 * HEAD_SLOT
    return jnp.concatenate([_dot(xb[:, g * w:(g + 1) * w], ones_bd) for g in range(x.shape[1] // w)], axis=1)


def _tile_heads(v):
    return jnp.concatenate([v] * H_MLA, axis=1)


def _kv_heads(ckv_n, krp, krp_sw, wkv_ref, kgain, kgain_sw, tabs, k_ref, v_ref):
    kvp = _dot(ckv_n.astype(BF16), wkv_ref[...])
    kn = kvp[:, :H_MLA * HEAD_SLOT]
    ms = _head_sums(kn * kn + _tile_heads(krp * krp)) * (1.0 / D_QK)
    r = lax.rsqrt(ms + EPS)
    k = (kn + _tile_heads(krp)) * r * _tile_heads(kgain)
    if tabs is not None:
        cos, sin = tabs
        k = k * _tile_heads(cos) + r * _tile_heads(krp_sw * kgain_sw * sin)
    k_ref[...] = k.astype(BF16)
    v_lane = lax.broadcasted_iota(jnp.int32, (1, H_MLA * HEAD_SLOT), 1) % HEAD_SLOT
    v_ref[...] = (kvp[:, H_MLA * HEAD_SLOT:] + jnp.where(v_lane == D_V, 1.0, 0.0)).astype(BF16)


def _inproj_kernel(rope, *refs):
    (x_ref, mod_ref, n1g_ref, w1_ref, qng_ref, wuq_ref, kvg_ref, wkv_ref, qgain_ref,
     kgain_ref) = refs[:10]
    refs = refs[10:]
    if rope:
        qk_tabs = tuple(r[...] for r in refs[:2])
        ret_cos, ret_sin = (r[...] for r in refs[2:4])
        refs = refs[4:]
    else:
        qk_tabs = None
    ckvn_ref, krp_ref, q_ref, k_ref, v_ref, rq_ref, rk_ref, rv_ref, rg_ref = refs

    mod = mod_ref[0]
    h = _rms(x_ref[...], n1g_ref[...]) * (1.0 + mod[1:2]) + mod[0:1]
    z = _dot(h.astype(BF16), w1_ref[...])
    o = 0
    cq = z[:, o:o + Q_RANK]; o += Q_RANK
    ckv = z[:, o:o + KV_RANK]; o += KV_RANK
    krp = z[:, o:o + LANES]; o += LANES
    rq = z[:, o:o + D_RET_ALL]; o += D_RET_ALL
    rk = z[:, o:o + D_RET_ALL]; o += D_RET_ALL
    rv = z[:, o:o + D_RET_ALL]; o += D_RET_ALL
    rg = z[:, o:o + D_RET_ALL]; o += D_RET_ALL
    krp_sw = None
    if rope:
        krp_sw = z[:, o:o + LANES]; o += LANES
        rq_sw = z[:, o:o + D_RET_ALL]; o += D_RET_ALL
        rk_sw = z[:, o:o + D_RET_ALL]

    ckv_n = _rms(ckv, kvg_ref[...])
    ckvn_ref[...] = ckv_n
    krp_ref[...] = krp

    qn = _rms(cq, qng_ref[...]).astype(BF16)
    scale = D_QK ** -0.5 * LOG2_E
    q = _dot(qn, wuq_ref[:, :H_MLA * HEAD_SLOT])
    r = lax.rsqrt(_head_sums(q * q) * (1.0 / D_QK) + EPS)
    q = q * r * _tile_heads(qgain_ref[0:1] * scale)
    if rope:
        cos, sin = qk_tabs
        q_sw = _dot(qn, wuq_ref[:, H_MLA * HEAD_SLOT:])
        q = q * _tile_heads(cos) + q_sw * r * _tile_heads(qgain_ref[1:2] * scale * sin)
    q_ref[...] = q.astype(BF16)

    _kv_heads(ckv_n, krp, krp_sw, wkv_ref, kgain_ref[0:1], kgain_ref[1:2], qk_tabs, k_ref, v_ref)

    for t in range(D_RET_ALL // LANES):
        sl = slice(t * LANES, (t + 1) * LANES)
        rq_t, rk_t = rq[:, sl], rk[:, sl]
        if rope:
            rq_t = rq_t * ret_cos + rq_sw[:, sl] * ret_sin
            rk_t = rk_t * ret_cos + rk_sw[:, sl] * ret_sin
        rq_ref[:, sl] = rq_t.astype(BF16)
        rk_ref[:, sl] = (rk_t * (D_RET ** -0.5)).astype(BF16)
    rv_ref[...] = rv.astype(BF16)
    rg_ref[...] = _silu(rg).astype(BF16)


def _const_spec(shape):
    return pl.BlockSpec(shape, lambda i: (0,) * len(shape))


def _inproj(x, mod, mod_row0, tokens_per_row, wts, rope_tabs, tm):
    T = x.shape[0]
    bpr = tokens_per_row // tm
    rope = rope_tabs is not None
    tok = lambda w: pl.BlockSpec((tm, w), lambda i: (i, 0))
    in_specs = [tok(D_MODEL),
                pl.BlockSpec((1, 6, D_MODEL), lambda i: (mod_row0 + i // bpr, 0, 0)),
                _const_spec((1, D_MODEL)),
                _const_spec((D_MODEL, D_IN_PAD + (LANES + 2 * D_RET_ALL if rope else 0))),
                _const_spec((1, Q_RANK)),
                _const_spec((Q_RANK, 2 * H_MLA * HEAD_SLOT)),
                _const_spec((1, KV_RANK)),
                _const_spec((KV_RANK, 2 * H_MLA * HEAD_SLOT)),
                _const_spec((2, LANES)),
                _const_spec((2, LANES))]
    args = [x, mod, wts["norm1_g"], wts["w1_rope" if rope else "w1"], wts["q_norm_g"], wts["w_uq"],
            wts["kv_norm_g"], wts["w_kv"], wts["q_gain"], wts["k_gain"]]
    if rope:
        qk_tabs, ret_tabs = rope_tabs
        nb = qk_tabs[0].shape[0] // tm
        in_specs += [pl.BlockSpec((tm, LANES), lambda i: (i % nb, 0))] * 4
        args += list(qk_tabs) + list(ret_tabs)
    widths = [(KV_RANK, F32), (LANES, F32), (H_MLA * HEAD_SLOT, BF16), (H_MLA * HEAD_SLOT, BF16),
              (H_MLA * HEAD_SLOT, BF16)] + [(D_RET_ALL, BF16)] * 4
    return pl.pallas_call(
        functools.partial(_inproj_kernel, rope),
        grid=(T // tm,),
        in_specs=in_specs,
        out_specs=[tok(w) for w, _ in widths],
        out_shape=[jax.ShapeDtypeStruct((T, w), dt) for w, dt in widths],
        compiler_params=_cparams("parallel"),
        name="inproj_rope" if rope else "inproj",
    )(*args)


def _ctx_kv_kernel(ckv_ref, krp_ref, wkv_ref, kgain_ref, k_ref, v_ref):
    _kv_heads(ckv_ref[...], krp_ref[...], None, wkv_ref, kgain_ref[0:1], None, None, k_ref, v_ref)


def _ctx_kv(ckv, krp, wts, tm):
    T = ckv.shape[0]
    tok = lambda w: pl.BlockSpec((tm, w), lambda i: (i, 0))
    return pl.pallas_call(
        _ctx_kv_kernel,
        grid=(T // tm,),
        in_specs=[tok(KV_RANK), tok(LANES),
                  _const_spec((KV_RANK, 2 * H_MLA * HEAD_SLOT)), _const_spec((2, LANES))],
        out_specs=[tok(H_MLA * HEAD_SLOT), tok(H_MLA * HEAD_SLOT)],
        out_shape=[jax.ShapeDtypeStruct((T, H_MLA * HEAD_SLOT), BF16),
                   jax.ShapeDtypeStruct((T, H_MLA * HEAD_SLOT), BF16)],
        compiler_params=_cparams("parallel"),
        name="ctx_kv",
    )(ckv, krp, wts["w_kv"], wts["k_gain"])


ATTN_KEY_CHUNK = 2048
ATTN_HEADS_PER_STEP = 4


def _attn_kernel(has_ctx, *refs):
    if has_ctx:
        q_ref, k_ref, v_ref, kc_ref, vc_ref, o_ref = refs
    else:
        q_ref, k_ref, v_ref, o_ref = refs
    sources = [(k_ref, v_ref)] + ([(kc_ref, vc_ref)] if has_ctx else [])
    chunks = []
    for kr, vr in sources:
        n = kr.shape[1]
        kc = _pick(n, (ATTN_KEY_CHUNK, 256, 128))
        chunks += [(kr, vr, slice(c * kc, (c + 1) * kc)) for c in range(n // kc)]
    tq = q_ref.shape[1]
    outs = []
    for j in range(ATTN_HEADS_PER_STEP):
        hs = slice(j * HEAD_SLOT, (j + 1) * HEAD_SLOT)
        q = q_ref[0, :, hs]
        m = jnp.full((tq, 1), -jnp.inf, F32)
        acc = jnp.zeros((tq, HEAD_SLOT), F32)
        for kr, vr, rows in chunks:
            s = _dot_nt(q, kr[0, rows, hs])
            m_new = jnp.maximum(m, jnp.max(s, axis=-1, keepdims=True))
            p = jnp.exp2(s - m_new).astype(BF16)
            acc = acc * jnp.exp2(m - m_new) + _dot(p, vr[0, rows, hs])
            m = m_new
        outs.append(acc[:, :D_V] / acc[:, D_V:D_V + 1])
    o_ref[0] = jnp.concatenate(outs, axis=-1).astype(BF16)


def _attention(q, k, v, kc, vc, tq):
    B, L, _ = q.shape
    has_ctx = kc is not None
    hps = ATTN_HEADS_PER_STEP
    in_specs = [pl.BlockSpec((1, tq, hps * HEAD_SLOT), lambda b, hp, i: (b, i, hp)),
                pl.BlockSpec((1, L, hps * HEAD_SLOT), lambda b, hp, i: (b, 0, hp)),
                pl.BlockSpec((1, L, hps * HEAD_SLOT), lambda b, hp, i: (b, 0, hp))]
    args = [q, k, v]
    if has_ctx:
        Lc = kc.shape[1]
        in_specs += [pl.BlockSpec((1, Lc, hps * HEAD_SLOT), lambda b, hp, i: (b, 0, hp)),
                     pl.BlockSpec((1, Lc, hps * HEAD_SLOT), lambda b, hp, i: (b, 0, hp))]
        args += [kc, vc]
    return pl.pallas_call(
        functools.partial(_attn_kernel, has_ctx),
        grid=(B, H_MLA // hps, L // tq),
        in_specs=in_specs,
        out_specs=pl.BlockSpec((1, tq, hps * D_V), lambda b, hp, i: (b, i, hp)),
        out_shape=jax.ShapeDtypeStruct((B, L, H_MLA * D_V), BF16),
        compiler_params=_cparams("parallel", "parallel", "arbitrary"),
        name="attn_ctx" if has_ctx else "attn",
    )(*args)


RET_UNROLL = 8
RET_SHORT_SEQ = 1024


def _ret_pair(has_s0, want_state, n_chunks, pair, hp, *refs):
    rq_ref, rk_ref, rv_ref, rg_ref, decf_ref, decb_ref, g_ref = refs[:7]
    refs = refs[7:]
    if has_s0:
        s0f_ref, s0b_ref = refs[:2]
        refs = refs[2:]
    o_ref = refs[0]
    refs = refs[1:]
    if want_state:
        sf_ref, sb_ref = refs[:2]
        refs = refs[2:]
    kv_ref, sp_ref = refs

    C = RET_CHUNK
    W = 2 * D_RET
    ls = slice(pair * W, (pair + 1) * W)
    lane = lax.broadcasted_iota(jnp.int32, (1, W), 1)
    lane_h1 = lane >= D_RET
    row_h1 = lax.broadcasted_iota(jnp.int32, (W, 1), 0) >= D_RET
    blockdiag = row_h1 == lane_h1
    pos = lax.broadcasted_iota(jnp.int32, (C, 1), 0).astype(F32)
    diff = (lax.broadcasted_iota(jnp.int32, (C, C), 0)
            - lax.broadcasted_iota(jnp.int32, (C, C), 1)).astype(F32)

    lg = {}
    for name, ref in (("f", decf_ref), ("b", decb_ref)):
        a = -jnp.exp(ref[pl.ds(2 * hp, 1), :])
        b = -jnp.exp(ref[pl.ds(2 * hp + 1, 1), :])
        lg[name] = (a[:, :1], b[:, :1], jnp.where(lane_h1, b, a))
    lgf0, lgf1, lgf = lg["f"]
    lgb0, lgb1, lgb = lg["b"]
    qdf, kdf = jnp.exp(lgf * (pos + 1.0)), jnp.exp(lgf * (C - 1.0 - pos))
    qdb, kdb = jnp.exp(lgb * (C - pos)), jnp.exp(lgb * pos)
    row_h1_full = lax.broadcasted_iota(jnp.int32, (W, W), 0) >= D_RET
    cdf = jnp.where(row_h1_full, jnp.exp(lgf1 * C), jnp.exp(lgf0 * C))
    cdb = jnp.where(row_h1_full, jnp.exp(lgb1 * C), jnp.exp(lgb0 * C))

    def intra_decay(lf, lb):
        return jnp.where(diff >= 0, jnp.exp(lf * jnp.maximum(diff, 0.0)), jnp.exp(lb * jnp.maximum(-diff, 0.0)))

    dmat0, dmat1 = intra_decay(lgf0, lgb0), intra_decay(lgf1, lgb1)
    g = g_ref[...]

    def chunk_rows(n):
        return pl.ds(pl.multiple_of(n * C, C), C)

    def phase_a(n, carry):
        rows = chunk_rows(n)
        k = rk_ref[0, rows, ls].astype(F32)
        k2 = jnp.concatenate([(k * kdf).astype(BF16), (k * kdb).astype(BF16)], axis=1)
        kv_ref[n] = _dot_tn(k2, rv_ref[0, rows, ls])
        return carry

    unroll = min(RET_UNROLL, n_chunks)
    lax.fori_loop(0, n_chunks, phase_a, 0, unroll=unroll)

    def init_state(ref):
        if not has_s0:
            return jnp.zeros((W, W), F32)
        z = jnp.zeros((D_RET, D_RET), F32)
        return jnp.concatenate([jnp.concatenate([ref[0, 2 * pair], z], axis=1),
                                jnp.concatenate([z, ref[0, 2 * pair + 1]], axis=1)], axis=0)

    def scan_f(n, S):
        sp_ref[n, 0:W, :] = S.astype(BF16)
        return cdf * S + jnp.where(blockdiag, kv_ref[n, 0:W, :], 0.0)

    def scan_b(i, S):
        n = n_chunks - 1 - i
        sp_ref[n, W:2 * W, :] = S.astype(BF16)
        return cdb * S + jnp.where(blockdiag, kv_ref[n, W:2 * W, :], 0.0)

    Sf = lax.fori_loop(0, n_chunks, scan_f, init_state(s0f_ref if has_s0 else None))
    Sb = lax.fori_loop(0, n_chunks, scan_b, init_state(s0b_ref if has_s0 else None))
    if want_state:
        for S, ref in ((Sf, sf_ref), (Sb, sb_ref)):
            ref[0, 2 * pair] = S[0:D_RET, 0:D_RET]
            ref[0, 2 * pair + 1] = S[D_RET:W, D_RET:W]

    def phase_c(n, carry):
        rows = chunk_rows(n)
        q = rq_ref[0, rows, ls]
        k = rk_ref[0, rows, ls]
        v = rv_ref[0, rows, ls]
        zero = jnp.zeros_like(q)
        q_heads = jnp.concatenate([jnp.where(lane_h1, zero, q), jnp.where(lane_h1, q, zero)], axis=0)
        sc = _dot_nt(q_heads, k)
        p = jnp.concatenate([(sc[:C] * dmat0).astype(BF16), (sc[C:] * dmat1).astype(BF16)], axis=1)
        v_bd = jnp.concatenate([jnp.where(lane_h1, zero, v), jnp.where(lane_h1, v, zero)], axis=0)
        qf = q.astype(F32)
        q2 = jnp.concatenate([(qf * qdf).astype(BF16), (qf * qdb).astype(BF16)], axis=1)
        o = _dot(p, v_bd) + _dot(q2, sp_ref[n])
        o2 = o * o
        ss0 = jnp.sum(jnp.where(lane_h1, 0.0, o2), axis=-1, keepdims=True)
        ss1 = jnp.sum(jnp.where(lane_h1, o2, 0.0), axis=-1, keepdims=True)
        ms = jnp.where(lane_h1, ss1, ss0) * (1.0 / D_RET)
        o = o * lax.rsqrt(ms + EPS) * g * rg_ref[0, rows, ls].astype(F32)
        o_ref[0, rows, ls] = o.astype(BF16)
        return carry

    lax.fori_loop(0, n_chunks, phase_c, 0, unroll=unroll)


def _ret_kernel(has_s0, want_state, n_chunks, pairs, *refs):
    for pair in range(pairs):
        _ret_pair(has_s0, want_state, n_chunks, pair, pl.program_id(1) * pairs + pair, *refs)


def _retention(rq, rk, rv, rg, dec_f, dec_b, g, s0f, s0b, want_state):
    B, L, _ = rq.shape
    has_s0 = s0f is not None
    pairs = H_RET // 2 if L <= RET_SHORT_SEQ else 1
    seq = pl.BlockSpec((1, L, pairs * LANES), lambda b, hp: (b, 0, hp))
    st = pl.BlockSpec((1, 2 * pairs, D_RET, D_RET), lambda b, hp: (b, hp, 0, 0))
    cst = lambda shape: pl.BlockSpec(shape, lambda b, hp: (0,) * len(shape))
    in_specs = [seq] * 4 + [cst((H_RET, LANES)), cst((H_RET, LANES)), cst((1, LANES))]
    args = [rq, rk, rv, rg, dec_f, dec_b, g]
    if has_s0:
        in_specs += [st, st]
        args += [s0f, s0b]
    out_specs = [seq]
    out_shape = [jax.ShapeDtypeStruct((B, L, D_RET_ALL), BF16)]
    if want_state:
        out_specs += [st, st]
        out_shape += [jax.ShapeDtypeStruct((B, H_RET, D_RET, D_RET), F32)] * 2
    return pl.pallas_call(
        functools.partial(_ret_kernel, has_s0, want_state, L // RET_CHUNK, pairs),
        grid=(B, H_RET // 2 // pairs),
        in_specs=in_specs,
        out_specs=out_specs,
        out_shape=out_shape,
        scratch_shapes=[pltpu.VMEM((L // RET_CHUNK, 4 * D_RET, LANES), F32),
                        pltpu.VMEM((L // RET_CHUNK, 4 * D_RET, LANES), BF16)],
        compiler_params=_cparams("parallel", "parallel"),
        name="retention_s0" if has_s0 else "retention",
    )(*args)


def _pack_pairs(x):
    w = x.shape[1] // 2
    hi = lax.bitcast_convert_type(x[:, :w].astype(BF16).astype(F32), jnp.uint32)
    lo = lax.bitcast_convert_type(x[:, w:].astype(BF16).astype(F32), jnp.uint32)
    return hi | (lo >> 16)


def _unpack_pairs(p):
    hi = lax.bitcast_convert_type(p & jnp.uint32(0xFFFF0000), F32)
    lo = lax.bitcast_convert_type(p << 16, F32)
    return jnp.concatenate([hi, lo], axis=1)


D_PACKED = D_MODEL // 2


def _outproj_kernel(omla_ref, oret_ref, x_ref, mod_ref, wo_ref, n2g_ref, rwh_ref, rwl_ref, rb_ref, tri_ref,
                    x2_ref, h2p_ref, code_ref, rank_ref, wk_ref, cnt_ref):
    @pl.when(pl.program_id(0) == 0)
    def _():
        cnt_ref[...] = jnp.zeros_like(cnt_ref)

    mod = mod_ref[0]
    half = H_MLA * D_V
    a = _dot(omla_ref[...], wo_ref[:half, :]) + _dot(oret_ref[...], wo_ref[half:, :])
    x2 = x_ref[...] + mod[2:3] * a
    x2_ref[...] = x2
    h2 = _rms(x2, n2g_ref[...]) * (1.0 + mod[4:5]) + mod[3:4]
    h2p_ref[...] = _pack_pairs(h2)
    h2_hi = h2.astype(BF16)
    h2_lo = (h2 - h2_hi.astype(F32)).astype(BF16)
    rw_hi = rwh_ref[...]
    logits = _dot_nt(rw_hi, h2_hi) + _dot_nt(rw_hi, h2_lo) + _dot_nt(rwl_ref[...], h2_hi)
    scores = jax.nn.sigmoid(logits)
    tm = scores.shape[1]
    expert = lax.broadcasted_iota(jnp.int32, scores.shape, 0)
    cand = scores + rb_ref[...]
    code = jnp.zeros(scores.shape, jnp.int32)
    for k in range(TOP_K):
        mx = jnp.max(cand, axis=0, keepdims=True)
        first = jnp.min(jnp.where(cand == mx, expert, LANES), axis=0, keepdims=True)
        hit = expert == first
        code = jnp.where(hit, k + 1, code)
        cand = jnp.where(hit, -jnp.inf, cand)
    picked = code > 0
    sel = jnp.where(picked, scores, 0.0)
    gates = sel * (ROUTED_SCALE / jnp.sum(sel, axis=0, keepdims=True))
    code_ref[...] = code

    onehot = jnp.where(picked, 1.0, 0.0)
    carry = cnt_ref[...].astype(F32)
    rank = _dot(onehot.astype(BF16), tri_ref[...])
    rank_ref[...] = rank + jnp.concatenate([carry] * (tm // LANES), axis=1)
    cnt_ref[...] = (carry + jnp.sum(onehot, axis=1, keepdims=True)).astype(jnp.int32)

    row = lax.broadcasted_iota(jnp.int32, scores.shape, 0)
    wk_t = jnp.zeros(scores.shape, F32)
    for k in range(TOP_K):
        w = jnp.sum(jnp.where(code == k + 1, gates, 0.0), axis=0, keepdims=True)
        wk_t = jnp.where(row == k, w, wk_t)
    wk_ref[...] = wk_t.T


def _outproj(o_mla, o_ret, x, mod, mod_row0, tokens_per_row, wts, tm):
    T = x.shape[0]
    bpr = tokens_per_row // tm
    tok = lambda w: pl.BlockSpec((tm, w), lambda i: (i, 0))
    emaj = pl.BlockSpec((LANES, tm), lambda i: (0, i))
    tri = jnp.triu(jnp.ones((tm, tm), BF16), 1)
    bias = jnp.broadcast_to(wts["router_bias"].reshape(LANES, 1), (LANES, tm))
    return pl.pallas_call(
        _outproj_kernel,
        grid=(T // tm,),
        in_specs=[tok(H_MLA * D_V), tok(D_RET_ALL), tok(D_MODEL),
                  pl.BlockSpec((1, 6, D_MODEL), lambda i: (mod_row0 + i // bpr, 0, 0)),
                  _const_spec((D_MODEL, D_MODEL)), _const_spec((1, D_MODEL)),
                  _const_spec((LANES, D_MODEL)), _const_spec((LANES, D_MODEL)), _const_spec((LANES, tm)),
                  _const_spec((tm, tm))],
        out_specs=[tok(D_MODEL), tok(D_PACKED), emaj, emaj, tok(LANES), _const_spec((LANES, LANES))],
        out_shape=[jax.ShapeDtypeStruct((T, D_MODEL), F32),
                   jax.ShapeDtypeStruct((T, D_PACKED), jnp.uint32),
                   jax.ShapeDtypeStruct((LANES, T), jnp.int32),
                   jax.ShapeDtypeStruct((LANES, T), F32),
                   jax.ShapeDtypeStruct((T, LANES), F32),
                   jax.ShapeDtypeStruct((LANES, LANES), jnp.int32)],
        compiler_params=_cparams("arbitrary"),
        name="outproj_router",
    )(o_mla, o_ret, x, mod, wts["w_o"], wts["norm2_g"], wts["router_hi"], wts["router_lo"], bias, tri)


def _plan_kernel(code_ref, rank_ref, first_ref, pos_ref):
    code = code_ref[...]
    tm = code.shape[1]
    row = rank_ref[...] + jnp.concatenate([first_ref[...]] * (tm // LANES), axis=1)
    krow = lax.broadcasted_iota(jnp.int32, pos_ref.shape, 0)
    pos = jnp.zeros(pos_ref.shape, F32)
    for k in range(TOP_K):
        pos = jnp.where(krow == k, jnp.sum(jnp.where(code == k + 1, row, 0.0), axis=0, keepdims=True), pos)
    pos_ref[...] = pos.astype(jnp.int32)


def _plan(code, rank, first_rows, tm):
    T = code.shape[1]
    first = jnp.broadcast_to(jnp.pad(first_rows.astype(F32), (0, LANES - N_EXPERTS)).reshape(LANES, 1),
                             (LANES, LANES))
    emaj = pl.BlockSpec((LANES, tm), lambda i: (0, i))
    return pl.pallas_call(
        _plan_kernel,
        grid=(T // tm,),
        in_specs=[emaj, emaj, _const_spec((LANES, LANES))],
        out_specs=pl.BlockSpec((8, tm), lambda i: (0, i)),
        out_shape=jax.ShapeDtypeStruct((8, T), jnp.int32),
        compiler_params=_cparams("parallel"),
        name="moe_plan",
    )(code, rank, first)


SC_WORKERS = 32
SC_CHUNK = 64


def _sc_worker_id():
    return lax.axis_index("s") * 2 + lax.axis_index("c")


def _sc_mesh():
    return plsc.VectorSubcoreMesh(core_axis_name="c", subcore_axis_name="s")


def _sc_gather_rows(table, idx):
    N = idx.shape[0]
    W = table.shape[1]
    per_w = N // SC_WORKERS
    n_chunks = per_w // SC_CHUNK
    assert N == SC_WORKERS * n_chunks * SC_CHUNK and n_chunks % 2 == 0

    @functools.partial(
        pl.kernel, mesh=_sc_mesh(), out_type=jax.ShapeDtypeStruct((N, W), table.dtype),
        scratch_types=[pltpu.VMEM((n_chunks, SC_CHUNK), jnp.int32),
                       pltpu.VMEM((2, SC_CHUNK, W), table.dtype),
                       pltpu.SemaphoreType.DMA((2,)), pltpu.SemaphoreType.DMA((2,))],
        name="sc_gather_rows")
    def k(table_hbm, idx_hbm, out_hbm, idx_v, rows_v, gsem, wsem):
        wid = _sc_worker_id()
        base = wid * per_w
        pltpu.sync_copy(idx_hbm.at[wid], idx_v)

        def gather(j, b):
            return pltpu.make_async_copy(table_hbm.at[idx_v.at[j]], rows_v.at[b], gsem.at[b])

        def writeback(j, b):
            off = pl.multiple_of(base + j * SC_CHUNK, SC_CHUNK)
            return pltpu.make_async_copy(rows_v.at[b], out_hbm.at[pl.ds(off, SC_CHUNK)], wsem.at[b])

        gather(0, 0).start()

        @pl.loop(0, n_chunks, step=2)
        def _(j):
            for b in range(2):
                jj = j + b

                @pl.when(jj + 1 < n_chunks)
                def _():
                    @pl.when(jj >= 1)
                    def _():
                        writeback(jj - 1, 1 - b).wait()
                    gather(jj + 1, 1 - b).start()

                gather(jj, b).wait()
                writeback(jj, b).start()

        writeback(n_chunks - 2, 0).wait()
        writeback(n_chunks - 1, 1).wait()

    return k(table, idx.reshape(SC_WORKERS, n_chunks, SC_CHUNK))


def _sc_dispatch_rows(rows_list, pos_list, n_out):
    W = rows_list[0].shape[1]
    K = pos_list[0].shape[0]
    n_src = len(rows_list)
    plan = []
    idx_list = []
    for rows, pos in zip(rows_list, pos_list):
        T = rows.shape[0]
        per_w = T // SC_WORKERS
        n_chunks = per_w // SC_CHUNK
        assert T == SC_WORKERS * n_chunks * SC_CHUNK and n_chunks % 2 == 0
        plan.append((per_w, n_chunks))
        idx_list.append(pos.reshape(K, SC_WORKERS, n_chunks, SC_CHUNK).transpose(1, 2, 0, 3))
    max_chunks = max(n for _, n in plan)

    @functools.partial(
        pl.kernel, mesh=_sc_mesh(), out_type=jax.ShapeDtypeStruct((n_out, W), rows_list[0].dtype),
        scratch_types=[pltpu.VMEM((max_chunks, K, SC_CHUNK), jnp.int32),
                       pltpu.VMEM((2, SC_CHUNK, W), rows_list[0].dtype),
                       pltpu.SemaphoreType.DMA((2,)), pltpu.SemaphoreType.DMA((2,))],
        name="sc_dispatch_rows")
    def k(*refs):
        rows_refs, idx_refs = refs[:n_src], refs[n_src:2 * n_src]
        out_hbm, idx_v, rows_v, lsem, ssem = refs[2 * n_src:]
        wid = _sc_worker_id()
        for rows_hbm, idx_hbm, (per_w, n_chunks) in zip(rows_refs, idx_refs, plan):
            base = wid * per_w
            pltpu.sync_copy(idx_hbm.at[wid], idx_v.at[pl.ds(0, n_chunks)])

            def load(j, b):
                off = pl.multiple_of(base + j * SC_CHUNK, SC_CHUNK)
                return pltpu.make_async_copy(rows_hbm.at[pl.ds(off, SC_CHUNK)], rows_v.at[b], lsem.at[b])

            def scatter(j, b, kk):
                return pltpu.make_async_copy(rows_v.at[b], out_hbm.at[idx_v.at[j, kk]], ssem.at[b])

            load(0, 0).start()

            @pl.loop(0, n_chunks, step=2)
            def _(j):
                for b in range(2):
                    jj = j + b

                    @pl.when(jj + 1 < n_chunks)
                    def _():
                        @pl.when(jj >= 1)
                        def _():
                            for kk in range(K):
                                scatter(jj - 1, 1 - b, kk).wait()
                        load(jj + 1, 1 - b).start()

                    load(jj, b).wait()
                    for kk in range(K):
                        scatter(jj, b, kk).start()

            for kk in range(K):
                scatter(n_chunks - 2, 0, kk).wait()
            for kk in range(K):
                scatter(n_chunks - 1, 1, kk).wait()

    return k(*rows_list, *idx_list)


ROW_TILE = 1024
MOE_GATHER_GROUP = 8192


def _experts_kernel(te_ref, tv_ref, tb_ref, xs_ref, wg_ref, wu_ref, wd_ref, ys_ref, wg_s, wu_s, wd_s):
    del tb_ref
    i = pl.program_id(0)
    valid = tv_ref[i]

    @pl.when(jnp.logical_or(i == 0, te_ref[i] != te_ref[jnp.maximum(i - 1, 0)]))
    def _():
        wg_s[...] = wg_ref[0].astype(BF16)
        wu_s[...] = wu_ref[0].astype(BF16)
        wd_s[...] = wd_ref[0].astype(BF16)

    @pl.when(valid > 0)
    def _():
        row = lax.broadcasted_iota(jnp.int32, (ROW_TILE, 1), 0)
        p = jnp.where(row < valid, xs_ref[...], jnp.uint32(0))
        x = _unpack_pairs(p).astype(BF16)
        act = _silu(_dot(x, wg_s[...])) * _dot(x, wu_s[...])
        ys_ref[...] = _pack_pairs(_dot(act.astype(BF16), wd_s[...]))


def _experts(xs, tile_expert, tile_valid, tile_block, wts):
    n_tiles = tile_expert.shape[0]
    ex = lambda a, b: pl.BlockSpec((1, a, b), lambda i, te, tv, tb: (te[i], 0, 0))
    row = pl.BlockSpec((ROW_TILE, D_PACKED), lambda i, te, tv, tb: (tb[i], 0))
    return pl.pallas_call(
        _experts_kernel,
        grid_spec=pltpu.PrefetchScalarGridSpec(
            num_scalar_prefetch=3, grid=(n_tiles,),
            in_specs=[row, ex(D_MODEL, D_EXPERT), ex(D_MODEL, D_EXPERT), ex(D_EXPERT, D_MODEL)],
            out_specs=row,
            scratch_shapes=[pltpu.VMEM((D_MODEL, D_EXPERT), BF16), pltpu.VMEM((D_MODEL, D_EXPERT), BF16),
                            pltpu.VMEM((D_EXPERT, D_MODEL), BF16)]),
        out_shape=jax.ShapeDtypeStruct(xs.shape, jnp.uint32),
        compiler_params=_cparams("arbitrary"),
        name="experts",
    )(tile_expert, tile_valid, tile_block, xs, wts["exp_gate"], wts["exp_up"], wts["exp_down"])


def _combine_kernel(h2p_ref, ys_ref, wk_ref, x2_ref, mod_ref, shg_ref, shu_ref, shd_ref, *rest):
    y_ref = rest[-1]
    t = _unpack_pairs(h2p_ref[...]).astype(BF16)
    act = _silu(_dot(t, shg_ref[...])) * _dot(t, shu_ref[...])
    acc = _dot(act.astype(BF16), shd_ref[...])
    wk = wk_ref[...]
    for k in range(TOP_K):
        acc = acc + wk[:, k:k + 1] * _unpack_pairs(ys_ref[k])
    y_ref[...] = x2_ref[...] + mod_ref[0][5:6] * acc


def _combine(h2p, ys, row0, wk, x2, y_prev, mod, mod_row0, tokens_per_row, wts, tm):
    T = h2p.shape[0]
    G = ys.shape[1]
    bpr = tokens_per_row // tm
    blk0 = row0 // tm
    tok = lambda w: pl.BlockSpec((tm, w), lambda i: (blk0 + i, 0))
    in_specs = [tok(D_PACKED), pl.BlockSpec((TOP_K, tm, D_PACKED), lambda i: (0, i, 0)), tok(LANES),
                tok(D_MODEL), pl.BlockSpec((1, 6, D_MODEL), lambda i: (mod_row0 + (blk0 + i) // bpr, 0, 0)),
                _const_spec((D_MODEL, D_SHARED)), _const_spec((D_MODEL, D_SHARED)),
                _const_spec((D_SHARED, D_MODEL))]
    args = [h2p, ys, wk, x2, mod, wts["sh_gate"], wts["sh_up"], wts["sh_down"]]
    aliases = {}
    if y_prev is not None:
        aliases = {len(args): 0}
        in_specs.append(pl.BlockSpec(memory_space=pl.ANY))
        args.append(y_prev)
    return pl.pallas_call(
        _combine_kernel,
        grid=(G // tm,),
        in_specs=in_specs,
        out_specs=tok(D_MODEL),
        out_shape=jax.ShapeDtypeStruct((T, D_MODEL), F32),
        input_output_aliases=aliases,
        compiler_params=_cparams("parallel"),
        name="moe_combine",
    )(*args)


def _moe(halves, mod, wts):
    routed = [h["routed"] for h in halves]
    sizes = [r[1].shape[0] for r in routed]
    n_tiles = (TOP_K * sum(sizes)) // ROW_TILE + N_EXPERTS
    path_counts = [r[5][:N_EXPERTS, 0] for r in routed]
    counts = sum(path_counts)
    padded = ((counts + ROW_TILE - 1) // ROW_TILE) * ROW_TILE
    ends = jnp.cumsum(padded)
    starts = ends - padded
    tile_start = jnp.arange(n_tiles, dtype=jnp.int32) * ROW_TILE
    tile_expert = jnp.minimum(jnp.sum(tile_start[:, None] >= ends[None, :], axis=1), N_EXPERTS - 1).astype(jnp.int32)
    overlap = (jnp.minimum(tile_start[:, None] + ROW_TILE, (starts + counts)[None, :])
               - jnp.maximum(tile_start[:, None], starts[None, :]))
    tile_valid = jnp.sum(jnp.maximum(overlap, 0), axis=1).astype(jnp.int32)

    pos, first = [], starts
    for h, r, c in zip(halves, routed, path_counts):
        plan_tm = _pick(r[1].shape[0], (2048, 1024, 512, 256, 128))
        pos.append(_plan(r[2], r[3], first, plan_tm)[:TOP_K])
        first = first + c
    xs = _sc_dispatch_rows([r[1] for r in routed], pos, n_tiles * ROW_TILE)
    tile_block = jnp.minimum(jnp.arange(n_tiles, dtype=jnp.int32), ends[-1] // ROW_TILE - 1).astype(jnp.int32)
    ys = _experts(xs, tile_expert, tile_valid, tile_block, wts)
    outs = []
    for h, r, p in zip(halves, routed, pos):
        T = r[1].shape[0]
        G = MOE_GATHER_GROUP if T % MOE_GATHER_GROUP == 0 else T
        y = None
        for row0 in range(0, T, G):
            ysel = _sc_gather_rows(ys, p[:, row0:row0 + G].reshape(-1)).reshape(TOP_K, G, D_PACKED)
            y = _combine(r[1], ysel, row0, r[4], r[0], y, mod, h["mod_row0"], h["tokens_per_row"], wts, h["tm"])
        outs.append(y.reshape(h["shape"]))
    return outs


def _pad_heads(w, d_used):
    k = w.shape[0]
    w = w.reshape(k, H_MLA, d_used)
    return jnp.pad(w, ((0, 0), (0, 0), (0, HEAD_SLOT - d_used))).reshape(k, H_MLA * HEAD_SLOT)


def _lane_pad(v):
    return jnp.pad(v, (0, LANES - v.shape[0])).reshape(1, LANES)


def _rot_partners(w):
    rot = w[..., w.shape[-1] - D_ROPE:]
    return jnp.concatenate([jnp.zeros_like(w[..., :w.shape[-1] - D_ROPE]),
                            _swap_halves(rot, D_ROPE // 4)], axis=-1)


def _prep_weights(l, w_in, norm1_g, mla_q_norm_g, w_uq, mla_kv_norm_g, w_ukv, mla_q_gain, mla_k_gain,
                  w_o, norm2_g, router_w, router_bias, exp_w_gate, exp_w_up, exp_w_down,
                  sh_w_gate, sh_w_up, sh_w_down):
    w = w_in[l]
    a, b, c = Q_RANK, Q_RANK + KV_RANK, Q_RANK + KV_RANK + D_ROPE
    kr_slot = lambda wk: jnp.pad(wk, ((0, 0), (D_NOPE, LANES - D_QK)))
    w1 = jnp.concatenate([w[:, :b], kr_slot(w[:, b:c]), w[:, c:]], axis=1).astype(BF16)
    w_rq, w_rk = w[:, c:c + D_RET_ALL], w[:, c + D_RET_ALL:c + 2 * D_RET_ALL]
    w1_rope = jnp.concatenate([w1, kr_slot(_rot_partners(w[:, b:c])).astype(BF16),
                               _swap_halves(w_rq, D_RET // 4).astype(BF16),
                               _swap_halves(w_rk, D_RET // 4).astype(BF16)], axis=1)
    uq = w_uq[l].reshape(Q_RANK, H_MLA, D_QK)
    pad_slots = lambda u: jnp.pad(u, ((0, 0), (0, 0), (0, HEAD_SLOT - D_QK))).reshape(Q_RANK, H_MLA * HEAD_SLOT)
    gains = lambda g: jnp.concatenate([_lane_pad(g), _lane_pad(_rot_partners(g))], axis=0)
    ukv = w_ukv[l].reshape(KV_RANK, H_MLA, D_NOPE + D_V)
    w_k = _pad_heads(ukv[:, :, :D_NOPE].reshape(KV_RANK, H_MLA * D_NOPE), D_NOPE)
    w_v = _pad_heads(ukv[:, :, D_NOPE:].reshape(KV_RANK, H_MLA * D_V), D_V)
    rw = jnp.pad(router_w[l].T, ((0, LANES - N_EXPERTS), (0, 0)))
    rw_hi = rw.astype(BF16)
    rw_lo = (rw - rw_hi.astype(F32)).astype(BF16)
    rb = jnp.concatenate([router_bias[l].astype(F32), jnp.full((LANES - N_EXPERTS,), -jnp.inf, F32)])
    return dict(
        norm1_g=norm1_g[l].reshape(1, D_MODEL), w1=w1, w1_rope=w1_rope,
        q_norm_g=mla_q_norm_g[l].reshape(1, Q_RANK),
        w_uq=jnp.concatenate([pad_slots(uq), pad_slots(_rot_partners(uq))], axis=1).astype(BF16),
        kv_norm_g=mla_kv_norm_g[l].reshape(1, KV_RANK),
        w_kv=jnp.concatenate([w_k, w_v], axis=1).astype(BF16),
        q_gain=gains(mla_q_gain[l]), k_gain=gains(mla_k_gain[l]),
        w_o=w_o[l].astype(BF16), norm2_g=norm2_g[l].reshape(1, D_MODEL),
        router_hi=rw_hi, router_lo=rw_lo, router_bias=rb.reshape(1, LANES),
        exp_gate=exp_w_gate[l], exp_up=exp_w_up[l], exp_down=exp_w_down[l],
        sh_gate=sh_w_gate[l].astype(BF16), sh_up=sh_w_up[l].astype(BF16),
        sh_down=sh_w_down[l].astype(BF16))


def _pick(n, prefs):
    for p in prefs:
        if n % p == 0:
            return p
    return n


def _mixer_half(x, mod, mod_row0, wts, ret_wts, ctx):
    B, L, D = x.shape
    T = B * L
    tokens_per_row = L if ctx is not None else T
    xt = x.reshape(T, D)
    rope_tabs = None
    if ctx is not None:
        rope_tabs = _rope_tables(L)
    tm = _pick(tokens_per_row, (512, 256, 128))
    ckv_n, krp, q, k, v, rq, rk, rv, rg = _inproj(xt, mod, mod_row0, tokens_per_row, wts, rope_tabs, tm)

    seq = lambda a: a.reshape(B, L, a.shape[-1])
    kc = vc = s0f = s0b = None
    if ctx is not None:
        ckv_c, kr_c, s0f, s0b = ctx
        Lc = ckv_c.shape[1]
        krp_c = jnp.pad(kr_c, ((0, 0), (0, 0), (D_NOPE, LANES - D_NOPE - D_ROPE)))
        kc, vc = _ctx_kv(ckv_c.reshape(B * Lc, KV_RANK), krp_c.reshape(B * Lc, LANES), wts,
                         _pick(B * Lc, (512, 256, 128)))
        kc, vc = kc.reshape(B, Lc, -1), vc.reshape(B, Lc, -1)
    o_mla = _attention(seq(q), seq(k), seq(v), kc, vc, _pick(L, (1024, 512, 256, 128)))

    dec_f, dec_b, ret_g = ret_wts
    want_state = ctx is None
    ret = _retention(seq(rq), seq(rk), seq(rv), seq(rg), dec_f, dec_b, ret_g, s0f, s0b, want_state)
    o_ret = ret[0]

    routed = _outproj(o_mla.reshape(T, -1), o_ret.reshape(T, -1), xt, mod, mod_row0, tokens_per_row, wts,
                      _pick(tokens_per_row, (1024, 512, 256, 128)))
    new = None
    if want_state:
        new = (ckv_n.reshape(B, L, KV_RANK), krp[:, D_NOPE:D_NOPE + D_ROPE].reshape(B, L, D_ROPE),
               ret[1], ret[2])
    return dict(routed=routed, mod_row0=mod_row0, tokens_per_row=tokens_per_row, tm=tm, shape=(B, L, D)), new


def kernel(x_prompt, x_sample, cache_mla_ckv, cache_mla_krope, state_ret_fwd, state_ret_bwd, c, c_ctx,
           w_ada, b_ada, norm1_g, w_in, mla_q_norm_g, w_uq, mla_kv_norm_g, w_ukv, mla_q_gain, mla_k_gain,
           ret_decay_fwd, ret_decay_bwd, ret_norm_g, w_o, norm2_g, router_w, router_bias, exp_w_gate,
           exp_w_up, exp_w_down, sh_w_gate, sh_w_up, sh_w_down):
    depth = w_ada.shape[0]
    n_dec = c.shape[0]
    assert 1 + n_dec <= MOD_ROWS
    cond = jnp.concatenate([c_ctx[None], c, jnp.zeros((MOD_ROWS - 1 - n_dec, D_MODEL), F32)], axis=0)

    y_prompt, y_sample = x_prompt, x_sample
    ckv_l, kr_l, sf_l, sb_l = [], [], [], []
    for l in range(depth):
        wts = _prep_weights(l, w_in, norm1_g, mla_q_norm_g, w_uq, mla_kv_norm_g, w_ukv, mla_q_gain,
                            mla_k_gain, w_o, norm2_g, router_w, router_bias, exp_w_gate, exp_w_up,
                            exp_w_down, sh_w_gate, sh_w_up, sh_w_down)
        ret_wts = (jnp.broadcast_to(ret_decay_fwd[l].astype(F32)[:, None], (H_RET, LANES)),
                   jnp.broadcast_to(ret_decay_bwd[l].astype(F32)[:, None], (H_RET, LANES)),
                   jnp.tile(ret_norm_g[l].reshape(1, D_RET), (1, LANES // D_RET)))
        mod = _adaln(cond, w_ada[l], b_ada[l]).reshape(MOD_ROWS, 6, D_MODEL)
        half_p, new = _mixer_half(y_prompt, mod, 0, wts, ret_wts, None)
        ckv_l.append(new[0]); kr_l.append(new[1]); sf_l.append(new[2]); sb_l.append(new[3])
        ctx = (cache_mla_ckv[:, l], cache_mla_krope[:, l], state_ret_fwd[:, l], state_ret_bwd[:, l])
        half_s, _ = _mixer_half(y_sample, mod, 1, wts, ret_wts, ctx)
        y_prompt, y_sample = _moe([half_p, half_s], mod, wts)

    return (y_prompt, y_sample, jnp.stack(ckv_l, axis=1), jnp.stack(kr_l, axis=1),
            jnp.stack(sf_l, axis=1), jnp.stack(sb_l, axis=1))
```

```python
import functools
import math

import numpy as np
import jax
import jax.numpy as jnp
from jax import lax
from jax.experimental import pallas as pl
from jax.experimental.pallas import tpu as pltpu
from jax.experimental.pallas import tpu_sc as plsc

F32 = jnp.float32
BF16 = jnp.bfloat16

D_MODEL = 1024
GRID_W = 64
H_MLA = 8
D_NOPE = 64
D_ROPE = 32
D_QK = D_NOPE + D_ROPE
D_V = 64
Q_RANK = 256
KV_RANK = 128
H_RET = 8
D_RET = 64
RET_CHUNK = 256
D_RET_ALL = H_RET * D_RET
N_EXPERTS = 64
TOP_K = 6
D_EXPERT = 256
D_SHARED = 256
ROUTED_SCALE = 2.5
ROPE_BASE = 10000.0
EPS = 1e-6
LOG2_E = math.log2(math.e)

LANES = 128
HEAD_SLOT = LANES
D_IN_PAD = Q_RANK + KV_RANK + LANES + 4 * D_RET_ALL
MOD_ROWS = 16
VMEM_LIMIT = 56 * 1024 * 1024


def _cparams(*sem):
    return pltpu.CompilerParams(dimension_semantics=sem, vmem_limit_bytes=VMEM_LIMIT)


def _dot(a, b):
    return jnp.dot(a, b, preferred_element_type=F32)


def _dot_nt(a, b):
    return lax.dot_general(a, b, (((1,), (1,)), ((), ())), preferred_element_type=F32)


def _dot_tn(a, b):
    return lax.dot_general(a, b, (((0,), (0,)), ((), ())), preferred_element_type=F32)


def _rms(x, g):
    return x * lax.rsqrt(jnp.mean(x * x, axis=-1, keepdims=True) + EPS) * g


def _silu(x):
    return x * jax.nn.sigmoid(x)


def _adaln_kernel(c_ref, w_ref, b_ref, o_ref):
    s = _silu(c_ref[...])
    o_ref[...] = _dot(s.astype(BF16), w_ref[...].astype(BF16)) + b_ref[...]


def _adaln(cond, w_ada, b_ada):
    n_out = w_ada.shape[1]
    bn = 1536
    return pl.pallas_call(
        _adaln_kernel,
        grid=(n_out // bn,),
        in_specs=[pl.BlockSpec((MOD_ROWS, D_MODEL), lambda j: (0, 0)),
                  pl.BlockSpec((D_MODEL, bn), lambda j: (0, j)),
                  pl.BlockSpec((1, bn), lambda j: (0, j))],
        out_specs=pl.BlockSpec((MOD_ROWS, bn), lambda j: (0, j)),
        out_shape=jax.ShapeDtypeStruct((MOD_ROWS, n_out), F32),
        compiler_params=_cparams("arbitrary"),
        name="adaln",
    )(cond, w_ada, b_ada.reshape(1, n_out))


def _swap_halves(w, half_pair):
    g = w.reshape(*w.shape[:-1], w.shape[-1] // (2 * half_pair), 2, half_pair)
    return g[..., ::-1, :].reshape(w.shape)


def _rope_tables(n_tokens):
    t = np.arange(n_tokens)
    row = (t // GRID_W).astype(np.float64)
    col = (t % GRID_W).astype(np.float64)

    def axis_tables(width, lane0, period, fill):
        p = width // 2
        inv = 1.0 / (ROPE_BASE ** (np.arange(p, dtype=np.float64) / p))
        cos = np.full((n_tokens, LANES), fill)
        sin = np.zeros((n_tokens, LANES))
        starts = range(lane0, LANES, period) if period else (lane0,)
        for s0 in starts:
            for base, pos in ((s0, row), (s0 + width, col)):
                ang = pos[:, None] * inv[None, :]
                c, sn = np.cos(ang), np.sin(ang)
                cos[:, base:base + p] = c
                cos[:, base + p:base + 2 * p] = c
                sin[:, base:base + p] = -sn
                sin[:, base + p:base + 2 * p] = sn
        return cos, sin

    qk_cos, qk_sin = axis_tables(D_ROPE // 2, D_NOPE, 0, 1.0)
    ret_cos, ret_sin = axis_tables(D_RET // 2, 0, D_RET, 1.0)
    as_f32 = lambda *a: tuple(jnp.asarray(x, F32) for x in a)
    return as_f32(qk_cos, qk_sin), as_f32(ret_cos, ret_sin)


def _head_sums(x):
    r = lax.broadcasted_iota(jnp.int32, (2 * HEAD_SLOT, 2 * HEAD_SLOT), 0) // HEAD_SLOT
    c = lax.broadcasted_iota(jnp.int32, (2 * HEAD_SLOT, 2 * HEAD_SLOT), 1) // HEAD_SLOT
    ones_bd = jnp.where(r == c, 1.0, 0.0).astype(BF16)
    xb = x.astype(BF16)
    w = 2 * HEAD_SLOT
    return jnp.concatenate([_dot(xb[:, g * w:(g + 1) * w], ones_bd) for g in range(x.shape[1] // w)], axis=1)


def _tile_heads(v):
    return jnp.concatenate([v] * H_MLA, axis=1)


def _kv_heads(ckv_n, krp, krp_sw, wkv_ref, kgain, kgain_sw, tabs, k_ref, v_ref):
    kvp = _dot(ckv_n.astype(BF16), wkv_ref[...])
    kn = kvp[:, :H_MLA * HEAD_SLOT]
    ms = _head_sums(kn * kn + _tile_heads(krp * krp)) * (1.0 / D_QK)
    r = lax.rsqrt(ms + EPS)
    k = (kn + _tile_heads(krp)) * r * _tile_heads(kgain)
    if tabs is not None:
        cos, sin = tabs
        k = k * _tile_heads(cos) + r * _tile_heads(krp_sw * kgain_sw * sin)
    k_ref[...] = k.astype(BF16)
    v_lane = lax.broadcasted_iota(jnp.int32, (1, H_MLA * HEAD_SLOT), 1) % HEAD_SLOT
    v_ref[...] = (kvp[:, H_MLA * HEAD_SLOT:] + jnp.where(v_lane == D_V, 1.0, 0.0)).astype(BF16)


def _inproj_kernel(rope, *refs):
    (x_ref, mod_ref, n1g_ref, w1_ref, qng_ref, wuq_ref, kvg_ref, wkv_ref, qgain_ref,
     kgain_ref) = refs[:10]
    refs = refs[10:]
    if rope:
        qk_tabs = tuple(r[...] for r in refs[:2])
        ret_cos, ret_sin = (r[...] for r in refs[2:4])
        refs = refs[4:]
    else:
        qk_tabs = None
    ckvn_ref, krp_ref, q_ref, k_ref, v_ref, rq_ref, rk_ref, rv_ref, rg_ref = refs

    mod = mod_ref[0]
    h = _rms(x_ref[...], n1g_ref[...]) * (1.0 + mod[1:2]) + mod[0:1]
    z = _dot(h.astype(BF16), w1_ref[...])
    o = 0
    cq = z[:, o:o + Q_RANK]; o += Q_RANK
    ckv = z[:, o:o + KV_RANK]; o += KV_RANK
    krp = z[:, o:o + LANES]; o += LANES
    rq = z[:, o:o + D_RET_ALL]; o += D_RET_ALL
    rk = z[:, o:o + D_RET_ALL]; o += D_RET_ALL
    rv = z[:, o:o + D_RET_ALL]; o += D_RET_ALL
    rg = z[:, o:o + D_RET_ALL]; o += D_RET_ALL
    krp_sw = None
    if rope:
        krp_sw = z[:, o:o + LANES]; o += LANES
        rq_sw = z[:, o:o + D_RET_ALL]; o += D_RET_ALL
        rk_sw = z[:, o:o + D_RET_ALL]

    ckv_n = _rms(ckv, kvg_ref[...])
    ckvn_ref[...] = ckv_n
    krp_ref[...] = krp

    qn = _rms(cq, qng_ref[...]).astype(BF16)
    scale = D_QK ** -0.5 * LOG2_E
    q = _dot(qn, wuq_ref[:, :H_MLA * HEAD_SLOT])
    r = lax.rsqrt(_head_sums(q * q) * (1.0 / D_QK) + EPS)
    q = q * r * _tile_heads(qgain_ref[0:1] * scale)
    if rope:
        cos, sin = qk_tabs
        q_sw = _dot(qn, wuq_ref[:, H_MLA * HEAD_SLOT:])
        q = q * _tile_heads(cos) + q_sw * r * _tile_heads(qgain_ref[1:2] * scale * sin)
    q_ref[...] = q.astype(BF16)

    _kv_heads(ckv_n, krp, krp_sw, wkv_ref, kgain_ref[0:1], kgain_ref[1:2], qk_tabs, k_ref, v_ref)

    for t in range(D_RET_ALL // LANES):
        sl = slice(t * LANES, (t + 1) * LANES)
        rq_t, rk_t = rq[:, sl], rk[:, sl]
        if rope:
            rq_t = rq_t * ret_cos + rq_sw[:, sl] * ret_sin
            rk_t = rk_t * ret_cos + rk_sw[:, sl] * ret_sin
        rq_ref[:, sl] = rq_t.astype(BF16)
        rk_ref[:, sl] = (rk_t * (D_RET ** -0.5)).astype(BF16)
    rv_ref[...] = rv.astype(BF16)
    rg_ref[...] = _silu(rg).astype(BF16)


def _const_spec(shape):
    return pl.BlockSpec(shape, lambda i: (0,) * len(shape))


def _inproj(x, mod, mod_row0, tokens_per_row, wts, rope_tabs, tm):
    T = x.shape[0]
    bpr = tokens_per_row // tm
    rope = rope_tabs is not None
    tok = lambda w: pl.BlockSpec((tm, w), lambda i: (i, 0))
    in_specs = [tok(D_MODEL),
                pl.BlockSpec((1, 6, D_MODEL), lambda i: (mod_row0 + i // bpr, 0, 0)),
                _const_spec((1, D_MODEL)),
                _const_spec((D_MODEL, D_IN_PAD + (LANES + 2 * D_RET_ALL if rope else 0))),
                _const_spec((1, Q_RANK)),
                _const_spec((Q_RANK, 2 * H_MLA * HEAD_SLOT)),
                _const_spec((1, KV_RANK)),
                _const_spec((KV_RANK, 2 * H_MLA * HEAD_SLOT)),
                _const_spec((2, LANES)),
                _const_spec((2, LANES))]
    args = [x, mod, wts["norm1_g"], wts["w1_rope" if rope else "w1"], wts["q_norm_g"], wts["w_uq"],
            wts["kv_norm_g"], wts["w_kv"], wts["q_gain"], wts["k_gain"]]
    if rope:
        qk_tabs, ret_tabs = rope_tabs
        nb = qk_tabs[0].shape[0] // tm
        in_specs += [pl.BlockSpec((tm, LANES), lambda i: (i % nb, 0))] * 4
        args += list(qk_tabs) + list(ret_tabs)
    widths = [(KV_RANK, F32), (LANES, F32), (H_MLA * HEAD_SLOT, BF16), (H_MLA * HEAD_SLOT, BF16),
              (H_MLA * HEAD_SLOT, BF16)] + [(D_RET_ALL, BF16)] * 4
    return pl.pallas_call(
        functools.partial(_inproj_kernel, rope),
        grid=(T // tm,),
        in_specs=in_specs,
        out_specs=[tok(w) for w, _ in widths],
        out_shape=[jax.ShapeDtypeStruct((T, w), dt) for w, dt in widths],
        compiler_params=_cparams("parallel"),
        name="inproj_rope" if rope else "inproj",
    )(*args)


def _ctx_kv_kernel(ckv_ref, krp_ref, wkv_ref, kgain_ref, k_ref, v_ref):
    _kv_heads(ckv_ref[...], krp_ref[...], None, wkv_ref, kgain_ref[0:1], None, None, k_ref, v_ref)


def _ctx_kv(ckv, krp, wts, tm):
    T = ckv.shape[0]
    tok = lambda w: pl.BlockSpec((tm, w), lambda i: (i, 0))
    return pl.pallas_call(
        _ctx_kv_kernel,
        grid=(T // tm,),
        in_specs=[tok(KV_RANK), tok(LANES),
                  _const_spec((KV_RANK, 2 * H_MLA * HEAD_SLOT)), _const_spec((2, LANES))],
        out_specs=[tok(H_MLA * HEAD_SLOT), tok(H_MLA * HEAD_SLOT)],
        out_shape=[jax.ShapeDtypeStruct((T, H_MLA * HEAD_SLOT), BF16),
                   jax.ShapeDtypeStruct((T, H_MLA * HEAD_SLOT), BF16)],
        compiler_params=_cparams("parallel"),
        name="ctx_kv",
    )(ckv, krp, wts["w_kv"], wts["k_gain"])


ATTN_KEY_CHUNK = 2048
ATTN_HEADS_PER_STEP = 4


def _attn_kernel(has_ctx, *refs):
    if has_ctx:
        q_ref, k_ref, v_ref, kc_ref, vc_ref, o_ref = refs
    else:
        q_ref, k_ref, v_ref, o_ref = refs
    sources = [(k_ref, v_ref)] + ([(kc_ref, vc_ref)] if has_ctx else [])
    chunks = []
    for kr, vr in sources:
        n = kr.shape[1]
        kc = _pick(n, (ATTN_KEY_CHUNK, 256, 128))
        chunks += [(kr, vr, slice(c * kc, (c + 1) * kc)) for c in range(n // kc)]
    tq = q_ref.shape[1]
    outs = []
    for j in range(ATTN_HEADS_PER_STEP):
        hs = slice(j * HEAD_SLOT, (j + 1) * HEAD_SLOT)
        q = q_ref[0, :, hs]
        m = jnp.full((tq, 1), -jnp.inf, F32)
        acc = jnp.zeros((tq, HEAD_SLOT), F32)
        for kr, vr, rows in chunks:
            s = _dot_nt(q, kr[0, rows, hs])
            m_new = jnp.maximum(m, jnp.max(s, axis=-1, keepdims=True))
            p = jnp.exp2(s - m_new).astype(BF16)
            acc = acc * jnp.exp2(m - m_new) + _dot(p, vr[0, rows, hs])
            m = m_new
        outs.append(acc[:, :D_V] / acc[:, D_V:D_V + 1])
    o_ref[0] = jnp.concatenate(outs, axis=-1).astype(BF16)


def _attention(q, k, v, kc, vc, tq):
    B, L, _ = q.shape
    has_ctx = kc is not None
    hps = ATTN_HEADS_PER_STEP
    in_specs = [pl.BlockSpec((1, tq, hps * HEAD_SLOT), lambda b, hp, i: (b, i, hp)),
                pl.BlockSpec((1, L, hps * HEAD_SLOT), lambda b, hp, i: (b, 0, hp)),
                pl.BlockSpec((1, L, hps * HEAD_SLOT), lambda b, hp, i: (b, 0, hp))]
    args = [q, k, v]
    if has_ctx:
        Lc = kc.shape[1]
        in_specs += [pl.BlockSpec((1, Lc, hps * HEAD_SLOT), lambda b, hp, i: (b, 0, hp)),
                     pl.BlockSpec((1, Lc, hps * HEAD_SLOT), lambda b, hp, i: (b, 0, hp))]
        args += [kc, vc]
    return pl.pallas_call(
        functools.partial(_attn_kernel, has_ctx),
        grid=(B, H_MLA // hps, L // tq),
        in_specs=in_specs,
        out_specs=pl.BlockSpec((1, tq, hps * D_V), lambda b, hp, i: (b, i, hp)),
        out_shape=jax.ShapeDtypeStruct((B, L, H_MLA * D_V), BF16),
        compiler_params=_cparams("parallel", "parallel", "arbitrary"),
        name="attn_ctx" if has_ctx else "attn",
    )(*args)


RET_UNROLL = 8
RET_SHORT_SEQ = 1024


def _ret_pair(has_s0, want_state, n_chunks, pair, hp, *refs):
    rq_ref, rk_ref, rv_ref, rg_ref, decf_ref, decb_ref, g_ref = refs[:7]
    refs = refs[7:]
    if has_s0:
        s0f_ref, s0b_ref = refs[:2]
        refs = refs[2:]
    o_ref = refs[0]
    refs = refs[1:]
    if want_state:
        sf_ref, sb_ref = refs[:2]
        refs = refs[2:]
    kv_ref, sp_ref = refs

    C = RET_CHUNK
    W = 2 * D_RET
    ls = slice(pair * W, (pair + 1) * W)
    lane = lax.broadcasted_iota(jnp.int32, (1, W), 1)
    lane_h1 = lane >= D_RET
    row_h1 = lax.broadcasted_iota(jnp.int32, (W, 1), 0) >= D_RET
    blockdiag = row_h1 == lane_h1
    pos = lax.broadcasted_iota(jnp.int32, (C, 1), 0).astype(F32)
    diff = (lax.broadcasted_iota(jnp.int32, (C, C), 0)
            - lax.broadcasted_iota(jnp.int32, (C, C), 1)).astype(F32)

    lg = {}
    for name, ref in (("f", decf_ref), ("b", decb_ref)):
        a = -jnp.exp(ref[pl.ds(2 * hp, 1), :])
        b = -jnp.exp(ref[pl.ds(2 * hp + 1, 1), :])
        lg[name] = (a[:, :1], b[:, :1], jnp.where(lane_h1, b, a))
    lgf0, lgf1, lgf = lg["f"]
    lgb0, lgb1, lgb = lg["b"]
    qdf, kdf = jnp.exp(lgf * (pos + 1.0)), jnp.exp(lgf * (C - 1.0 - pos))
    qdb, kdb = jnp.exp(lgb * (C - pos)), jnp.exp(lgb * pos)
    row_h1_full = lax.broadcasted_iota(jnp.int32, (W, W), 0) >= D_RET
    cdf = jnp.where(row_h1_full, jnp.exp(lgf1 * C), jnp.exp(lgf0 * C))
    cdb = jnp.where(row_h1_full, jnp.exp(lgb1 * C), jnp.exp(lgb0 * C))

    def intra_decay(lf, lb):
        return jnp.where(diff >= 0, jnp.exp(lf * jnp.maximum(diff, 0.0)), jnp.exp(lb * jnp.maximum(-diff, 0.0)))

    dmat0, dmat1 = intra_decay(lgf0, lgb0), intra_decay(lgf1, lgb1)
    g = g_ref[...]

    def chunk_rows(n):
        return pl.ds(pl.multiple_of(n * C, C), C)

    def phase_a(n, carry):
        rows = chunk_rows(n)
        k = rk_ref[0, rows, ls].astype(F32)
        k2 = jnp.concatenate([(k * kdf).astype(BF16), (k * kdb).astype(BF16)], axis=1)
        kv_ref[n] = _dot_tn(k2, rv_ref[0, rows, ls])
        return carry

    unroll = min(RET_UNROLL, n_chunks)
    lax.fori_loop(0, n_chunks, phase_a, 0, unroll=unroll)

    def init_state(ref):
        if not has_s0:
            return jnp.zeros((W, W), F32)
        z = jnp.zeros((D_RET, D_RET), F32)
        return jnp.concatenate([jnp.concatenate([ref[0, 2 * pair], z], axis=1),
                                jnp.concatenate([z, ref[0, 2 * pair + 1]], axis=1)], axis=0)

    def scan_f(n, S):
        sp_ref[n, 0:W, :] = S.astype(BF16)
        return cdf * S + jnp.where(blockdiag, kv_ref[n, 0:W, :], 0.0)

    def scan_b(i, S):
        n = n_chunks - 1 - i
        sp_ref[n, W:2 * W, :] = S.astype(BF16)
        return cdb * S + jnp.where(blockdiag, kv_ref[n, W:2 * W, :], 0.0)

    Sf = lax.fori_loop(0, n_chunks, scan_f, init_state(s0f_ref if has_s0 else None))
    Sb = lax.fori_loop(0, n_chunks, scan_b, init_state(s0b_ref if has_s0 else None))
    if want_state:
        for S, ref in ((Sf, sf_ref), (Sb, sb_ref)):
            ref[0, 2 * pair] = S[0:D_RET, 0:D_RET]
            ref[0, 2 * pair + 1] = S[D_RET:W, D_RET:W]

    def phase_c(n, carry):
        rows = chunk_rows(n)
        q = rq_ref[0, rows, ls]
        k = rk_ref[0, rows, ls]
        v = rv_ref[0, rows, ls]
        zero = jnp.zeros_like(q)
        q_heads = jnp.concatenate([jnp.where(lane_h1, zero, q), jnp.where(lane_h1, q, zero)], axis=0)
        sc = _dot_nt(q_heads, k)
        p = jnp.concatenate([(sc[:C] * dmat0).astype(BF16), (sc[C:] * dmat1).astype(BF16)], axis=1)
        v_bd = jnp.concatenate([jnp.where(lane_h1, zero, v), jnp.where(lane_h1, v, zero)], axis=0)
        qf = q.astype(F32)
        q2 = jnp.concatenate([(qf * qdf).astype(BF16), (qf * qdb).astype(BF16)], axis=1)
        o = _dot(p, v_bd) + _dot(q2, sp_ref[n])
        o2 = o * o
        ss0 = jnp.sum(jnp.where(lane_h1, 0.0, o2), axis=-1, keepdims=True)
        ss1 = jnp.sum(jnp.where(lane_h1, o2, 0.0), axis=-1, keepdims=True)
        ms = jnp.where(lane_h1, ss1, ss0) * (1.0 / D_RET)
        o = o * lax.rsqrt(ms + EPS) * g * rg_ref[0, rows, ls].astype(F32)
        o_ref[0, rows, ls] = o.astype(BF16)
        return carry

    lax.fori_loop(0, n_chunks, phase_c, 0, unroll=unroll)


def _ret_kernel(has_s0, want_state, n_chunks, pairs, *refs):
    for pair in range(pairs):
        _ret_pair(has_s0, want_state, n_chunks, pair, pl.program_id(1) * pairs + pair, *refs)


def _retention(rq, rk, rv, rg, dec_f, dec_b, g, s0f, s0b, want_state):
    B, L, _ = rq.shape
    has_s0 = s0f is not None
    pairs = H_RET // 2 if L <= RET_SHORT_SEQ else 1
    seq = pl.BlockSpec((1, L, pairs * LANES), lambda b, hp: (b, 0, hp))
    st = pl.BlockSpec((1, 2 * pairs, D_RET, D_RET), lambda b, hp: (b, hp, 0, 0))
    cst = lambda shape: pl.BlockSpec(shape, lambda b, hp: (0,) * len(shape))
    in_specs = [seq] * 4 + [cst((H_RET, LANES)), cst((H_RET, LANES)), cst((1, LANES))]
    args = [rq, rk, rv, rg, dec_f, dec_b, g]
    if has_s0:
        in_specs += [st, st]
        args += [s0f, s0b]
    out_specs = [seq]
    out_shape = [jax.ShapeDtypeStruct((B, L, D_RET_ALL), BF16)]
    if want_state:
        out_specs += [st, st]
        out_shape += [jax.ShapeDtypeStruct((B, H_RET, D_RET, D_RET), F32)] * 2
    return pl.pallas_call(
        functools.partial(_ret_kernel, has_s0, want_state, L // RET_CHUNK, pairs),
        grid=(B, H_RET // 2 // pairs),
        in_specs=in_specs,
        out_specs=out_specs,
        out_shape=out_shape,
        scratch_shapes=[pltpu.VMEM((L // RET_CHUNK, 4 * D_RET, LANES), F32),
                        pltpu.VMEM((L // RET_CHUNK, 4 * D_RET, LANES), BF16)],
        compiler_params=_cparams("parallel", "parallel"),
        name="retention_s0" if has_s0 else "retention",
    )(*args)


def _pack_pairs(x):
    w = x.shape[1] // 2
    hi = lax.bitcast_convert_type(x[:, :w].astype(BF16).astype(F32), jnp.uint32)
    lo = lax.bitcast_convert_type(x[:, w:].astype(BF16).astype(F32), jnp.uint32)
    return hi | (lo >> 16)


def _unpack_pairs(p):
    hi = lax.bitcast_convert_type(p & jnp.uint32(0xFFFF0000), F32)
    lo = lax.bitcast_convert_type(p << 16, F32)
    return jnp.concatenate([hi, lo], axis=1)


D_PACKED = D_MODEL // 2


def _outproj_kernel(omla_ref, oret_ref, x_ref, mod_ref, wo_ref, n2g_ref, rwh_ref, rwl_ref, rb_ref, tri_ref,
                    x2_ref, h2p_ref, code_ref, rank_ref, wk_ref, cnt_ref):
    @pl.when(pl.program_id(0) == 0)
    def _():
        cnt_ref[...] = jnp.zeros_like(cnt_ref)

    mod = mod_ref[0]
    half = H_MLA * D_V
    a = _dot(omla_ref[...], wo_ref[:half, :]) + _dot(oret_ref[...], wo_ref[half:, :])
    x2 = x_ref[...] + mod[2:3] * a
    x2_ref[...] = x2
    h2 = _rms(x2, n2g_ref[...]) * (1.0 + mod[4:5]) + mod[3:4]
    h2p_ref[...] = _pack_pairs(h2)
    h2_hi = h2.astype(BF16)
    h2_lo = (h2 - h2_hi.astype(F32)).astype(BF16)
    rw_hi = rwh_ref[...]
    logits = _dot_nt(rw_hi, h2_hi) + _dot_nt(rw_hi, h2_lo) + _dot_nt(rwl_ref[...], h2_hi)
    scores = jax.nn.sigmoid(logits)
    tm = scores.shape[1]
    expert = lax.broadcasted_iota(jnp.int32, scores.shape, 0)
    cand = scores + rb_ref[...]
    code = jnp.zeros(scores.shape, jnp.int32)
    for k in range(TOP_K):
        mx = jnp.max(cand, axis=0, keepdims=True)
        first = jnp.min(jnp.where(cand == mx, expert, LANES), axis=0, keepdims=True)
        hit = expert == first
        code = jnp.where(hit, k + 1, code)
        cand = jnp.where(hit, -jnp.inf, cand)
    picked = code > 0
    sel = jnp.where(picked, scores, 0.0)
    gates = sel * (ROUTED_SCALE / jnp.sum(sel, axis=0, keepdims=True))
    code_ref[...] = code

    onehot = jnp.where(picked, 1.0, 0.0)
    carry = cnt_ref[...].astype(F32)
    rank = _dot(onehot.astype(BF16), tri_ref[...])
    rank_ref[...] = rank + jnp.concatenate([carry] * (tm // LANES), axis=1)
    cnt_ref[...] = (carry + jnp.sum(onehot, axis=1, keepdims=True)).astype(jnp.int32)

    row = lax.broadcasted_iota(jnp.int32, scores.shape, 0)
    wk_t = jnp.zeros(scores.shape, F32)
    for k in range(TOP_K):
        w = jnp.sum(jnp.where(code == k + 1, gates, 0.0), axis=0, keepdims=True)
        wk_t = jnp.where(row == k, w, wk_t)
    wk_ref[...] = wk_t.T


def _outproj(o_mla, o_ret, x, mod, mod_row0, tokens_per_row, wts, tm):
    T = x.shape[0]
    bpr = tokens_per_row // tm
    tok = lambda w: pl.BlockSpec((tm, w), lambda i: (i, 0))
    emaj = pl.BlockSpec((LANES, tm), lambda i: (0, i))
    tri = jnp.triu(jnp.ones((tm, tm), BF16), 1)
    bias = jnp.broadcast_to(wts["router_bias"].reshape(LANES, 1), (LANES, tm))
    return pl.pallas_call(
        _outproj_kernel,
        grid=(T // tm,),
        in_specs=[tok(H_MLA * D_V), tok(D_RET_ALL), tok(D_MODEL),
                  pl.BlockSpec((1, 6, D_MODEL), lambda i: (mod_row0 + i // bpr, 0, 0)),
                  _const_spec((D_MODEL, D_MODEL)), _const_spec((1, D_MODEL)),
                  _const_spec((LANES, D_MODEL)), _const_spec((LANES, D_MODEL)), _const_spec((LANES, tm)),
                  _const_spec((tm, tm))],
        out_specs=[tok(D_MODEL), tok(D_PACKED), emaj, emaj, tok(LANES), _const_spec((LANES, LANES))],
        out_shape=[jax.ShapeDtypeStruct((T, D_MODEL), F32),
                   jax.ShapeDtypeStruct((T, D_PACKED), jnp.uint32),
                   jax.ShapeDtypeStruct((LANES, T), jnp.int32),
                   jax.ShapeDtypeStruct((LANES, T), F32),
                   jax.ShapeDtypeStruct((T, LANES), F32),
                   jax.ShapeDtypeStruct((LANES, LANES), jnp.int32)],
        compiler_params=_cparams("arbitrary"),
        name="outproj_router",
    )(o_mla, o_ret, x, mod, wts["w_o"], wts["norm2_g"], wts["router_hi"], wts["router_lo"], bias, tri)


def _plan_kernel(code_ref, rank_ref, first_ref, pos_ref):
    code = code_ref[...]
    tm = code.shape[1]
    row = rank_ref[...] + jnp.concatenate([first_ref[...]] * (tm // LANES), axis=1)
    krow = lax.broadcasted_iota(jnp.int32, pos_ref.shape, 0)
    pos = jnp.zeros(pos_ref.shape, F32)
    for k in range(TOP_K):
        pos = jnp.where(krow == k, jnp.sum(jnp.where(code == k + 1, row, 0.0), axis=0, keepdims=True), pos)
    pos_ref[...] = pos.astype(jnp.int32)


def _plan(code, rank, first_rows, tm):
    T = code.shape[1]
    first = jnp.broadcast_to(jnp.pad(first_rows.astype(F32), (0, LANES - N_EXPERTS)).reshape(LANES, 1),
                             (LANES, LANES))
    emaj = pl.BlockSpec((LANES, tm), lambda i: (0, i))
    return pl.pallas_call(
        _plan_kernel,
        grid=(T // tm,),
        in_specs=[emaj, emaj, _const_spec((LANES, LANES))],
        out_specs=pl.BlockSpec((8, tm), lambda i: (0, i)),
        out_shape=jax.ShapeDtypeStruct((8, T), jnp.int32),
        compiler_params=_cparams("parallel"),
        name="moe_plan",
    )(code, rank, first)


SC_WORKERS = 32
SC_CHUNK = 64


def _sc_worker_id():
    return lax.axis_index("s") * 2 + lax.axis_index("c")


def _sc_mesh():
    return plsc.VectorSubcoreMesh(core_axis_name="c", subcore_axis_name="s")


def _sc_gather_rows(table, idx):
    N = idx.shape[0]
    W = table.shape[1]
    per_w = N // SC_WORKERS
    n_chunks = per_w // SC_CHUNK
    assert N == SC_WORKERS * n_chunks * SC_CHUNK and n_chunks % 2 == 0

    @functools.partial(
        pl.kernel, mesh=_sc_mesh(), out_type=jax.ShapeDtypeStruct((N, W), table.dtype),
        scratch_types=[pltpu.VMEM((n_chunks, SC_CHUNK), jnp.int32),
                       pltpu.VMEM((2, SC_CHUNK, W), table.dtype),
                       pltpu.SemaphoreType.DMA((2,)), pltpu.SemaphoreType.DMA((2,))],
        name="sc_gather_rows")
    def k(table_hbm, idx_hbm, out_hbm, idx_v, rows_v, gsem, wsem):
        wid = _sc_worker_id()
        base = wid * per_w
        pltpu.sync_copy(idx_hbm.at[wid], idx_v)

        def gather(j, b):
            return pltpu.make_async_copy(table_hbm.at[idx_v.at[j]], rows_v.at[b], gsem.at[b])

        def writeback(j, b):
            off = pl.multiple_of(base + j * SC_CHUNK, SC_CHUNK)
            return pltpu.make_async_copy(rows_v.at[b], out_hbm.at[pl.ds(off, SC_CHUNK)], wsem.at[b])

        gather(0, 0).start()

        @pl.loop(0, n_chunks, step=2)
        def _(j):
            for b in range(2):
                jj = j + b

                @pl.when(jj + 1 < n_chunks)
                def _():
                    @pl.when(jj >= 1)
                    def _():
                        writeback(jj - 1, 1 - b).wait()
                    gather(jj + 1, 1 - b).start()

                gather(jj, b).wait()
                writeback(jj, b).start()

        writeback(n_chunks - 2, 0).wait()
        writeback(n_chunks - 1, 1).wait()

    return k(table, idx.reshape(SC_WORKERS, n_chunks, SC_CHUNK))


def _sc_dispatch_rows(rows_list, pos_list, n_out):
    W = rows_list[0].shape[1]
    K = pos_list[0].shape[0]
    n_src = len(rows_list)
    plan = []
    idx_list = []
    for rows, pos in zip(rows_list, pos_list):
        T = rows.shape[0]
        per_w = T // SC_WORKERS
        n_chunks = per_w // SC_CHUNK
        assert T == SC_WORKERS * n_chunks * SC_CHUNK and n_chunks % 2 == 0
        plan.append((per_w, n_chunks))
        idx_list.append(pos.reshape(K, SC_WORKERS, n_chunks, SC_CHUNK).transpose(1, 2, 0, 3))
    max_chunks = max(n for _, n in plan)

    @functools.partial(
        pl.kernel, mesh=_sc_mesh(), out_type=jax.ShapeDtypeStruct((n_out, W), rows_list[0].dtype),
        scratch_types=[pltpu.VMEM((max_chunks, K, SC_CHUNK), jnp.int32),
                       pltpu.VMEM((2, SC_CHUNK, W), rows_list[0].dtype),
                       pltpu.SemaphoreType.DMA((2,)), pltpu.SemaphoreType.DMA((2,))],
        name="sc_dispatch_rows")
    def k(*refs):
        rows_refs, idx_refs = refs[:n_src], refs[n_src:2 * n_src]
        out_hbm, idx_v, rows_v, lsem, ssem = refs[2 * n_src:]
        wid = _sc_worker_id()
        for rows_hbm, idx_hbm, (per_w, n_chunks) in zip(rows_refs, idx_refs, plan):
            base = wid * per_w
            pltpu.sync_copy(idx_hbm.at[wid], idx_v.at[pl.ds(0, n_chunks)])

            def load(j, b):
                off = pl.multiple_of(base + j * SC_CHUNK, SC_CHUNK)
                return pltpu.make_async_copy(rows_hbm.at[pl.ds(off, SC_CHUNK)], rows_v.at[b], lsem.at[b])

            def scatter(j, b, kk):
                return pltpu.make_async_copy(rows_v.at[b], out_hbm.at[idx_v.at[j, kk]], ssem.at[b])

            load(0, 0).start()

            @pl.loop(0, n_chunks, step=2)
            def _(j):
                for b in range(2):
                    jj = j + b

                    @pl.when(jj + 1 < n_chunks)
                    def _():
                        @pl.when(jj >= 1)
                        def _():
                            for kk in range(K):
                                scatter(jj - 1, 1 - b, kk).wait()
                        load(jj + 1, 1 - b).start()

                    load(jj, b).wait()
                    for kk in range(K):
                        scatter(jj, b, kk).start()

            for kk in range(K):
                scatter(n_chunks - 2, 0, kk).wait()
            for kk in range(K):
                scatter(n_chunks - 1, 1, kk).wait()

    return k(*rows_list, *idx_list)


ROW_TILE = 1024
MOE_GATHER_GROUP = 8192


EXPERT_IN_BUFS = 3


def _experts_kernel(te_ref, tv_ref, tb_ref, xs_hbm, wg_ref, wu_ref, wd_ref, ys_ref, wg_s, wu_s, wd_s,
                    xbuf, xsem):
    i = pl.program_id(0)
    n = pl.num_programs(0)
    valid = tv_ref[i]
    ahead = EXPERT_IN_BUFS - 1

    def tile_copy(step):
        slot = step % EXPERT_IN_BUFS
        rows = pl.ds(pl.multiple_of(tb_ref[step] * ROW_TILE, ROW_TILE), ROW_TILE)
        return pltpu.make_async_copy(xs_hbm.at[rows, :], xbuf.at[slot], xsem.at[slot])

    @pl.when(i == 0)
    def _():
        for s in range(ahead):
            tile_copy(s).start()

    @pl.when(i + ahead < n)
    def _():
        tile_copy(i + ahead).start()

    tile_copy(i).wait()
    xs_ref = xbuf.at[i % EXPERT_IN_BUFS]

    @pl.when(jnp.logical_or(i == 0, te_ref[i] != te_ref[jnp.maximum(i - 1, 0)]))
    def _():
        wg_s[...] = wg_ref[0].astype(BF16)
        wu_s[...] = wu_ref[0].astype(BF16)
        wd_s[...] = wd_ref[0].astype(BF16)

    @pl.when(valid > 0)
    def _():
        row = lax.broadcasted_iota(jnp.int32, (ROW_TILE, 1), 0)
        p = jnp.where(row < valid, xs_ref[...], jnp.uint32(0))
        x = _unpack_pairs(p).astype(BF16)
        act = _silu(_dot(x, wg_s[...])) * _dot(x, wu_s[...])
        ys_ref[...] = _pack_pairs(_dot(act.astype(BF16), wd_s[...]))


def _experts(xs, tile_expert, tile_valid, tile_block, wts):
    n_tiles = tile_expert.shape[0]
    ex = lambda a, b: pl.BlockSpec((1, a, b), lambda i, te, tv, tb: (te[i], 0, 0))
    row = pl.BlockSpec((ROW_TILE, D_PACKED), lambda i, te, tv, tb: (tb[i], 0))
    return pl.pallas_call(
        _experts_kernel,
        grid_spec=pltpu.PrefetchScalarGridSpec(
            num_scalar_prefetch=3, grid=(n_tiles,),
            in_specs=[pl.BlockSpec(memory_space=pl.ANY),
                      ex(D_MODEL, D_EXPERT), ex(D_MODEL, D_EXPERT), ex(D_EXPERT, D_MODEL)],
            out_specs=row,
            scratch_shapes=[pltpu.VMEM((D_MODEL, D_EXPERT), BF16), pltpu.VMEM((D_MODEL, D_EXPERT), BF16),
                            pltpu.VMEM((D_EXPERT, D_MODEL), BF16),
                            pltpu.VMEM((EXPERT_IN_BUFS, ROW_TILE, D_PACKED), jnp.uint32),
                            pltpu.SemaphoreType.DMA((EXPERT_IN_BUFS,))]),
        out_shape=jax.ShapeDtypeStruct(xs.shape, jnp.uint32),
        compiler_params=_cparams("arbitrary"),
        name="experts",
    )(tile_expert, tile_valid, tile_block, xs, wts["exp_gate"], wts["exp_up"], wts["exp_down"])


def _combine_kernel(h2p_ref, ys_ref, wk_ref, x2_ref, mod_ref, shg_ref, shu_ref, shd_ref, *rest):
    y_ref = rest[-1]
    t = _unpack_pairs(h2p_ref[...]).astype(BF16)
    act = _silu(_dot(t, shg_ref[...])) * _dot(t, shu_ref[...])
    acc = _dot(act.astype(BF16), shd_ref[...])
    wk = wk_ref[...]
    for k in range(TOP_K):
        acc = acc + wk[:, k:k + 1] * _unpack_pairs(ys_ref[k])
    y_ref[...] = x2_ref[...] + mod_ref[0][5:6] * acc


def _combine(h2p, ys, row0, wk, x2, y_prev, mod, mod_row0, tokens_per_row, wts, tm):
    T = h2p.shape[0]
    G = ys.shape[1]
    bpr = tokens_per_row // tm
    blk0 = row0 // tm
    tok = lambda w: pl.BlockSpec((tm, w), lambda i: (blk0 + i, 0))
    in_specs = [tok(D_PACKED), pl.BlockSpec((TOP_K, tm, D_PACKED), lambda i: (0, i, 0)), tok(LANES),
                tok(D_MODEL), pl.BlockSpec((1, 6, D_MODEL), lambda i: (mod_row0 + (blk0 + i) // bpr, 0, 0)),
                _const_spec((D_MODEL, D_SHARED)), _const_spec((D_MODEL, D_SHARED)),
                _const_spec((D_SHARED, D_MODEL))]
    args = [h2p, ys, wk, x2, mod, wts["sh_gate"], wts["sh_up"], wts["sh_down"]]
    aliases = {}
    if y_prev is not None:
        aliases = {len(args): 0}
        in_specs.append(pl.BlockSpec(memory_space=pl.ANY))
        args.append(y_prev)
    return pl.pallas_call(
        _combine_kernel,
        grid=(G // tm,),
        in_specs=in_specs,
        out_specs=tok(D_MODEL),
        out_shape=jax.ShapeDtypeStruct((T, D_MODEL), F32),
        input_output_aliases=aliases,
        compiler_params=_cparams("parallel"),
        name="moe_combine",
    )(*args)


def _moe(halves, mod, wts):
    routed = [h["routed"] for h in halves]
    sizes = [r[1].shape[0] for r in routed]
    n_tiles = (TOP_K * sum(sizes)) // ROW_TILE + N_EXPERTS
    path_counts = [r[5][:N_EXPERTS, 0] for r in routed]
    counts = sum(path_counts)
    padded = ((counts + ROW_TILE - 1) // ROW_TILE) * ROW_TILE
    ends = jnp.cumsum(padded)
    starts = ends - padded
    tile_start = jnp.arange(n_tiles, dtype=jnp.int32) * ROW_TILE
    tile_expert = jnp.minimum(jnp.sum(tile_start[:, None] >= ends[None, :], axis=1), N_EXPERTS - 1).astype(jnp.int32)
    overlap = (jnp.minimum(tile_start[:, None] + ROW_TILE, (starts + counts)[None, :])
               - jnp.maximum(tile_start[:, None], starts[None, :]))
    tile_valid = jnp.sum(jnp.maximum(overlap, 0), axis=1).astype(jnp.int32)

    pos, first = [], starts
    for h, r, c in zip(halves, routed, path_counts):
        plan_tm = _pick(r[1].shape[0], (2048, 1024, 512, 256, 128))
        pos.append(_plan(r[2], r[3], first, plan_tm)[:TOP_K])
        first = first + c
    xs = _sc_dispatch_rows([r[1] for r in routed], pos, n_tiles * ROW_TILE)
    tile_block = jnp.minimum(jnp.arange(n_tiles, dtype=jnp.int32), ends[-1] // ROW_TILE - 1).astype(jnp.int32)
    ys = _experts(xs, tile_expert, tile_valid, tile_block, wts)
    outs = []
    for h, r, p in zip(halves, routed, pos):
        T = r[1].shape[0]
        G = MOE_GATHER_GROUP if T % MOE_GATHER_GROUP == 0 else T
        y = None
        for row0 in range(0, T, G):
            ysel = _sc_gather_rows(ys, p[:, row0:row0 + G].reshape(-1)).reshape(TOP_K, G, D_PACKED)
            y = _combine(r[1], ysel, row0, r[4], r[0], y, mod, h["mod_row0"], h["tokens_per_row"], wts, h["tm"])
        outs.append(y.reshape(h["shape"]))
    return outs


def _pad_heads(w, d_used):
    k = w.shape[0]
    w = w.reshape(k, H_MLA, d_used)
    return jnp.pad(w, ((0, 0), (0, 0), (0, HEAD_SLOT - d_used))).reshape(k, H_MLA * HEAD_SLOT)


def _lane_pad(v):
    return jnp.pad(v, (0, LANES - v.shape[0])).reshape(1, LANES)


def _rot_partners(w):
    rot = w[..., w.shape[-1] - D_ROPE:]
    return jnp.concatenate([jnp.zeros_like(w[..., :w.shape[-1] - D_ROPE]),
                            _swap_halves(rot, D_ROPE // 4)], axis=-1)


def _prep_weights(l, w_in, norm1_g, mla_q_norm_g, w_uq, mla_kv_norm_g, w_ukv, mla_q_gain, mla_k_gain,
                  w_o, norm2_g, router_w, router_bias, exp_w_gate, exp_w_up, exp_w_down,
                  sh_w_gate, sh_w_up, sh_w_down):
    w = w_in[l]
    a, b, c = Q_RANK, Q_RANK + KV_RANK, Q_RANK + KV_RANK + D_ROPE
    kr_slot = lambda wk: jnp.pad(wk, ((0, 0), (D_NOPE, LANES - D_QK)))
    w1 = jnp.concatenate([w[:, :b], kr_slot(w[:, b:c]), w[:, c:]], axis=1).astype(BF16)
    w_rq, w_rk = w[:, c:c + D_RET_ALL], w[:, c + D_RET_ALL:c + 2 * D_RET_ALL]
    w1_rope = jnp.concatenate([w1, kr_slot(_rot_partners(w[:, b:c])).astype(BF16),
                               _swap_halves(w_rq, D_RET // 4).astype(BF16),
                               _swap_halves(w_rk, D_RET // 4).astype(BF16)], axis=1)
    uq = w_uq[l].reshape(Q_RANK, H_MLA, D_QK)
    pad_slots = lambda u: jnp.pad(u, ((0, 0), (0, 0), (0, HEAD_SLOT - D_QK))).reshape(Q_RANK, H_MLA * HEAD_SLOT)
    gains = lambda g: jnp.concatenate([_lane_pad(g), _lane_pad(_rot_partners(g))], axis=0)
    ukv = w_ukv[l].reshape(KV_RANK, H_MLA, D_NOPE + D_V)
    w_k = _pad_heads(ukv[:, :, :D_NOPE].reshape(KV_RANK, H_MLA * D_NOPE), D_NOPE)
    w_v = _pad_heads(ukv[:, :, D_NOPE:].reshape(KV_RANK, H_MLA * D_V), D_V)
    rw = jnp.pad(router_w[l].T, ((0, LANES - N_EXPERTS), (0, 0)))
    rw_hi = rw.astype(BF16)
    rw_lo = (rw - rw_hi.astype(F32)).astype(BF16)
    rb = jnp.concatenate([router_bias[l].astype(F32), jnp.full((LANES - N_EXPERTS,), -jnp.inf, F32)])
    return dict(
        norm1_g=norm1_g[l].reshape(1, D_MODEL), w1=w1, w1_rope=w1_rope,
        q_norm_g=mla_q_norm_g[l].reshape(1, Q_RANK),
        w_uq=jnp.concatenate([pad_slots(uq), pad_slots(_rot_partners(uq))], axis=1).astype(BF16),
        kv_norm_g=mla_kv_norm_g[l].reshape(1, KV_RANK),
        w_kv=jnp.concatenate([w_k, w_v], axis=1).astype(BF16),
        q_gain=gains(mla_q_gain[l]), k_gain=gains(mla_k_gain[l]),
        w_o=w_o[l].astype(BF16), norm2_g=norm2_g[l].reshape(1, D_MODEL),
        router_hi=rw_hi, router_lo=rw_lo, router_bias=rb.reshape(1, LANES),
        exp_gate=exp_w_gate[l], exp_up=exp_w_up[l], exp_down=exp_w_down[l],
        sh_gate=sh_w_gate[l].astype(BF16), sh_up=sh_w_up[l].astype(BF16),
        sh_down=sh_w_down[l].astype(BF16))


def _pick(n, prefs):
    for p in prefs:
        if n % p == 0:
            return p
    return n


def _mixer_half(x, mod, mod_row0, wts, ret_wts, ctx):
    B, L, D = x.shape
    T = B * L
    tokens_per_row = L if ctx is not None else T
    xt = x.reshape(T, D)
    rope_tabs = None
    if ctx is not None:
        rope_tabs = _rope_tables(L)
    tm = _pick(tokens_per_row, (512, 256, 128))
    ckv_n, krp, q, k, v, rq, rk, rv, rg = _inproj(xt, mod, mod_row0, tokens_per_row, wts, rope_tabs, tm)

    seq = lambda a: a.reshape(B, L, a.shape[-1])
    kc = vc = s0f = s0b = None
    if ctx is not None:
        ckv_c, kr_c, s0f, s0b = ctx
        Lc = ckv_c.shape[1]
        krp_c = jnp.pad(kr_c, ((0, 0), (0, 0), (D_NOPE, LANES - D_NOPE - D_ROPE)))
        kc, vc = _ctx_kv(ckv_c.reshape(B * Lc, KV_RANK), krp_c.reshape(B * Lc, LANES), wts,
                         _pick(B * Lc, (512, 256, 128)))
        kc, vc = kc.reshape(B, Lc, -1), vc.reshape(B, Lc, -1)
    o_mla = _attention(seq(q), seq(k), seq(v), kc, vc, _pick(L, (1024, 512, 256, 128)))

    dec_f, dec_b, ret_g = ret_wts
    want_state = ctx is None
    ret = _retention(seq(rq), seq(rk), seq(rv), seq(rg), dec_f, dec_b, ret_g, s0f, s0b, want_state)
    o_ret = ret[0]

    routed = _outproj(o_mla.reshape(T, -1), o_ret.reshape(T, -1), xt, mod, mod_row0, tokens_per_row, wts,
                      _pick(tokens_per_row, (1024, 512, 256, 128)))
    new = None
    if want_state:
        new = (ckv_n.reshape(B, L, KV_RANK), krp[:, D_NOPE:D_NOPE + D_ROPE].reshape(B, L, D_ROPE),
               ret[1], ret[2])
    return dict(routed=routed, mod_row0=mod_row0, tokens_per_row=tokens_per_row, tm=tm, shape=(B, L, D)), new


def kernel(x_prompt, x_sample, cache_mla_ckv, cache_mla_krope, state_ret_fwd, state_ret_bwd, c, c_ctx,
           w_ada, b_ada, norm1_g, w_in, mla_q_norm_g, w_uq, mla_kv_norm_g, w_ukv, mla_q_gain, mla_k_gain,
           ret_decay_fwd, ret_decay_bwd, ret_norm_g, w_o, norm2_g, router_w, router_bias, exp_w_gate,
           exp_w_up, exp_w_down, sh_w_gate, sh_w_up, sh_w_down):
    depth = w_ada.shape[0]
    n_dec = c.shape[0]
    assert 1 + n_dec <= MOD_ROWS
    cond = jnp.concatenate([c_ctx[None], c, jnp.zeros((MOD_ROWS - 1 - n_dec, D_MODEL), F32)], axis=0)

    y_prompt, y_sample = x_prompt, x_sample
    ckv_l, kr_l, sf_l, sb_l = [], [], [], []
    for l in range(depth):
        wts = _prep_weights(l, w_in, norm1_g, mla_q_norm_g, w_uq, mla_kv_norm_g, w_ukv, mla_q_gain,
                            mla_k_gain, w_o, norm2_g, router_w, router_bias, exp_w_gate, exp_w_up,
                            exp_w_down, sh_w_gate, sh_w_up, sh_w_down)
        ret_wts = (jnp.broadcast_to(ret_decay_fwd[l].astype(F32)[:, None], (H_RET, LANES)),
                   jnp.broadcast_to(ret_decay_bwd[l].astype(F32)[:, None], (H_RET, LANES)),
                   jnp.tile(ret_norm_g[l].reshape(1, D_RET), (1, LANES // D_RET)))
        mod = _adaln(cond, w_ada[l], b_ada[l]).reshape(MOD_ROWS, 6, D_MODEL)
        half_p, new = _mixer_half(y_prompt, mod, 0, wts, ret_wts, None)
        ckv_l.append(new[0]); kr_l.append(new[1]); sf_l.append(new[2]); sb_l.append(new[3])
        ctx = (cache_mla_ckv[:, l], cache_mla_krope[:, l], state_ret_fwd[:, l], state_ret_bwd[:, l])
        half_s, _ = _mixer_half(y_sample, mod, 1, wts, ret_wts, ctx)
        y_prompt, y_sample = _moe([half_p, half_s], mod, wts)

    return (y_prompt, y_sample, jnp.stack(ckv_l, axis=1), jnp.stack(kr_l, axis=1),
            jnp.stack(sf_l, axis=1), jnp.stack(sb_l, axis=1))
```
